```python
import math
import jax, jax.numpy as jnp
from jax import lax
import numpy as np

D_MODEL = 1024
BATCH = 8
SEQ = 4096
DEPTH = 4

MEM_LEN = 256
A_HEADS = 4
A_HEAD_DIM = 64
A_WIDTH = A_HEADS * A_HEAD_DIM
CHUNK = 128
MLA_HEADS = 8
MLA_NOPE = 64
MLA_ROPE = 32
MLA_V = 64
Q_LORA = 256
KV_LORA = 128
MLA_WIDTH = MLA_HEADS * MLA_V
Q_BLOCK = 128
ROPE_BASE = 10000.0
C_WIDTH = 256
CONV_W = 3
D_WIDTH = 256
POOL_WINDOWS = (2, 4, 8, 16)
D_GROUP = D_WIDTH // len(POOL_WINDOWS)
MIX_WIDTH = A_WIDTH + MLA_WIDTH + C_WIDTH + D_WIDTH
P_A = 2 * A_WIDTH
P_B = Q_LORA + KV_LORA + MLA_ROPE
P_C = 3 * C_WIDTH
P_D = D_WIDTH
P_IN = P_A + P_B + P_C + P_D
X_HEADS = 4
X_HEAD_DIM = D_MODEL // X_HEADS
N_EXPERTS = 16
N_GROUPS = 4
EXPERTS_PER_GROUP = N_EXPERTS // N_GROUPS
TOP_K = 2
D_FF_EXPERT = 256
ALPHA = (2 * DEPTH) ** 0.25
BETA = (8 * DEPTH) ** -0.25
EPS = 1e-6

kernel_name = "hybrid_headgroup_deepnorm_moe_trunk"


def layer_norm(x, g, b):
    xf = x.astype(jnp.float32)
    mu = jnp.mean(xf, -1, keepdims=True)
    var = jnp.mean(jnp.square(xf - mu), -1, keepdims=True)
    y = (xf - mu) * lax.rsqrt(var + EPS)
    return (y * g.astype(jnp.float32) + b.astype(jnp.float32)).astype(x.dtype)


def rms_norm(x, g):
    xf = x.astype(jnp.float32)
    y = xf * lax.rsqrt(jnp.mean(jnp.square(xf), -1, keepdims=True) + EPS)
    return (y * g.astype(jnp.float32)).astype(x.dtype)


def apply_rope(t, cos, sin):
    half = t.shape[-1] // 2
    t1, t2 = t[..., :half], t[..., half:]
    return jnp.concatenate([t1 * cos - t2 * sin, t2 * cos + t1 * sin], axis=-1)


def chunked_spatial_gating(za, v_norm_g, w_s, b_s):
    bsz, s_len, _ = za.shape
    z = jax.nn.gelu(za)
    u, v = jnp.split(z, 2, axis=-1)
    v = rms_norm(v, v_norm_g)
    n_chunks = s_len // CHUNK
    v = v.reshape(bsz, n_chunks, CHUNK, A_HEADS, A_HEAD_DIM)
    causal = jnp.tril(jnp.ones((CHUNK, CHUNK), dtype=bool))
    w = jnp.where(causal[None], w_s, 0)
    mixed = jnp.einsum('hts,bcshd->bcthd', w, v) + b_s.T[None, None, :, :, None]
    return u * mixed.reshape(bsz, s_len, A_WIDTH)


def latent_attention(zb, cos, sin, q_norm_g, w_uq, kv_norm_g, w_ukv):
    bsz, s_len, _ = zb.shape
    c_q = rms_norm(zb[..., :Q_LORA], q_norm_g)
    c_kv = rms_norm(zb[..., Q_LORA:Q_LORA + KV_LORA], kv_norm_g)
    k_rope = apply_rope(zb[..., Q_LORA + KV_LORA:], cos, sin)
    q = (c_q @ w_uq).reshape(bsz, s_len, MLA_HEADS, MLA_NOPE + MLA_ROPE)
    q_nope = q[..., :MLA_NOPE]
    q_rope = apply_rope(q[..., MLA_NOPE:], cos[:, :, None, :], sin[:, :, None, :])
    kv = (c_kv @ w_ukv).reshape(bsz, s_len, MLA_HEADS, MLA_NOPE + MLA_V)
    k_nope, v = kv[..., :MLA_NOPE], kv[..., MLA_NOPE:]
    scale = (MLA_NOPE + MLA_ROPE) ** -0.5
    n_blocks = s_len // Q_BLOCK
    key_idx = jnp.arange(s_len)

    def to_blocks(t):
        return jnp.moveaxis(t.reshape(bsz, n_blocks, Q_BLOCK, *t.shape[2:]), 1, 0)

    def attend(args):
        qn, qr, blk = args
        s = (jnp.einsum('bqhd,bkhd->bhqk', qn, k_nope)
             + jnp.einsum('bqhr,bkr->bhqk', qr, k_rope)).astype(jnp.float32) * scale
        q_idx = blk * Q_BLOCK + jnp.arange(Q_BLOCK)
        s = jnp.where(key_idx[None, :] <= q_idx[:, None], s, -jnp.inf)
        p = jax.nn.softmax(s, axis=-1).astype(v.dtype)
        return jnp.einsum('bhqk,bkhd->bqhd', p, v)

    o = lax.map(attend, (to_blocks(q_nope), to_blocks(q_rope), jnp.arange(n_blocks)))
    return jnp.moveaxis(o, 0, 1).reshape(bsz, s_len, MLA_WIDTH)


def short_gated_conv(zc, conv_w):
    b_gate, c_gate, h = jnp.split(zc, 3, axis=-1)
    y = lax.conv_general_dilated(
        c_gate * h, conv_w[:, None, :], window_strides=(1,), padding=[(CONV_W - 1, 0)],
        dimension_numbers=('NWC', 'WIO', 'NWC'), feature_group_count=C_WIDTH)
    return b_gate * y


def multiscale_pool(zd, pool_w, pool_scale):
    bsz, s_len, _ = zd.shape
    zf = zd.astype(jnp.float32)
    cs = jnp.concatenate([jnp.zeros((bsz, 1, D_WIDTH), jnp.float32), jnp.cumsum(zf, axis=1)], axis=1)
    t1 = jnp.arange(1, s_len + 1)
    outs = []
    for g, w in enumerate(POOL_WINDOWS):
        cs_g = cs[..., g * D_GROUP:(g + 1) * D_GROUP]
        lo = jnp.maximum(t1 - w, 0)
        win_sum = cs_g[:, 1:] - jnp.take(cs_g, lo, axis=1)
        count = jnp.minimum(t1, w).astype(jnp.float32)[None, :, None]
        outs.append(win_sum / count - zf[..., g * D_GROUP:(g + 1) * D_GROUP])
    pooled = jnp.stack(outs, axis=2).astype(zd.dtype)
    y = jnp.einsum('bsgc,gcd->bsgd', pooled, pool_w).reshape(bsz, s_len, D_WIDTH)
    return y * pool_scale


def hybrid_mixer(x, cos, sin, w_in, gmlp_v_norm_g, gmlp_w_s, gmlp_b_s, mla_q_norm_g, mla_w_uq,
                 mla_kv_norm_g, mla_w_ukv, conv_w, pool_w, pool_scale, w_out):
    z = x @ w_in
    za, zb, zc, zd = jnp.split(z, [P_A, P_A + P_B, P_A + P_B + P_C], axis=-1)
    ya = chunked_spatial_gating(za, gmlp_v_norm_g, gmlp_w_s, gmlp_b_s)
    yb = latent_attention(zb, cos, sin, mla_q_norm_g, mla_w_uq, mla_kv_norm_g, mla_w_ukv)
    yc = short_gated_conv(zc, conv_w)
    yd = multiscale_pool(zd, pool_w, pool_scale)
    y = jnp.concatenate([ya, yb, yc, yd], axis=-1)
    return y @ w_out


def memory_cross_attention(x, mem, wq, wk, wv, wo):
    bsz, s_len, _ = x.shape
    q = (x @ wq).reshape(bsz, s_len, X_HEADS, X_HEAD_DIM)
    k = (mem @ wk).reshape(bsz, -1, X_HEADS, X_HEAD_DIM)
    v = (mem @ wv).reshape(bsz, -1, X_HEADS, X_HEAD_DIM)
    s = jnp.einsum('bqhd,bkhd->bhqk', q, k).astype(jnp.float32) * (X_HEAD_DIM ** -0.5)
    p = jax.nn.softmax(s, axis=-1).astype(v.dtype)
    o = jnp.einsum('bhqk,bkhd->bqhd', p, v).reshape(bsz, s_len, D_MODEL)
    return o @ wo


def route_gates(xt, router_w, router_b):
    n_tok = xt.shape[0]
    scores = jax.nn.sigmoid((xt @ router_w).astype(jnp.float32))
    sel = scores + router_b.astype(jnp.float32)
    group_score = jnp.sum(lax.top_k(sel.reshape(n_tok, N_GROUPS, EXPERTS_PER_GROUP), TOP_K)[0], axis=-1)
    best_group = jnp.argmax(group_score, axis=-1)
    in_group = (jnp.arange(N_EXPERTS) // EXPERTS_PER_GROUP)[None, :] == best_group[:, None]
    _, idx = lax.top_k(jnp.where(in_group, sel, -jnp.inf), TOP_K)
    w = jnp.take_along_axis(scores, idx, axis=-1)
    w = w / jnp.sum(w, axis=-1, keepdims=True)
    return jnp.sum(jax.nn.one_hot(idx, N_EXPERTS, dtype=jnp.float32) * w[..., None], axis=1)


def grouped_moe(x, router_w, router_b, w_gate, w_up, w_down):
    bsz, s_len, d = x.shape
    xt = x.reshape(-1, d)
    gates = route_gates(xt, router_w, router_b).astype(x.dtype)
    out = jnp.zeros_like(xt)
    for e in range(N_EXPERTS):
        h = jax.nn.silu(xt @ w_gate[e]) * (xt @ w_up[e])
        out = out + gates[:, e:e + 1] * (h @ w_down[e])
    return out.reshape(bsz, s_len, d)


def setup_inputs(seed: int = 0) -> dict:
    key = jax.random.key(seed)
    keys = iter(jax.random.split(key, 40))
    L = DEPTH

    def nrm(shape, scale):
        return jax.random.normal(next(keys), shape, jnp.float32) * scale

    def gain(shape):
        return 1.0 + 0.05 * jax.random.normal(next(keys), shape, jnp.float32)

    x = nrm((BATCH, SEQ, D_MODEL), 1.0)
    mem = nrm((BATCH, MEM_LEN, D_MODEL), 1.0)
    positions = (jax.random.randint(next(keys), (BATCH, 1), 0, SEQ, dtype=jnp.int32)
                 + jnp.arange(SEQ, dtype=jnp.int32)[None, :])
    return {
        'x': x,
        'mem': mem,
        'positions': positions,
        'w_in': nrm((L, D_MODEL, P_IN), D_MODEL ** -0.5),
        'gmlp_v_norm_g': gain((L, A_WIDTH)),
        'gmlp_w_s': nrm((L, A_HEADS, CHUNK, CHUNK), CHUNK ** -0.5),
        'gmlp_b_s': 1.0 + 0.1 * nrm((L, A_HEADS, CHUNK), 1.0),
        'mla_q_norm_g': gain((L, Q_LORA)),
        'mla_w_uq': nrm((L, Q_LORA, MLA_HEADS * (MLA_NOPE + MLA_ROPE)), Q_LORA ** -0.5),
        'mla_kv_norm_g': gain((L, KV_LORA)),
        'mla_w_ukv': nrm((L, KV_LORA, MLA_HEADS * (MLA_NOPE + MLA_V)), KV_LORA ** -0.5),
        'conv_w': nrm((L, CONV_W, C_WIDTH), CONV_W ** -0.5),
        'pool_w': nrm((L, len(POOL_WINDOWS), D_GROUP, D_GROUP), D_GROUP ** -0.5),
        'pool_scale': gain((L, D_WIDTH)),
        'w_out': nrm((L, MIX_WIDTH, D_MODEL), BETA * MIX_WIDTH ** -0.5),
        'ln1_g': gain((L, D_MODEL)),
        'ln1_b': nrm((L, D_MODEL), 0.02),
        'xattn_wq': nrm((L, D_MODEL, D_MODEL), D_MODEL ** -0.5),
        'xattn_wk': nrm((L, D_MODEL, D_MODEL), D_MODEL ** -0.5),
        'xattn_wv': nrm((L, D_MODEL, D_MODEL), BETA * D_MODEL ** -0.5),
        'xattn_wo': nrm((L, D_MODEL, D_MODEL), BETA * D_MODEL ** -0.5),
        'ln2_g': gain((L, D_MODEL)),
        'ln2_b': nrm((L, D_MODEL), 0.02),
        'router_w': nrm((D_MODEL, N_EXPERTS), D_MODEL ** -0.5),
        'router_b': nrm((N_EXPERTS,), 0.01),
        'moe_w_gate': nrm((L, N_EXPERTS, D_MODEL, D_FF_EXPERT), D_MODEL ** -0.5),
        'moe_w_up': nrm((L, N_EXPERTS, D_MODEL, D_FF_EXPERT), D_MODEL ** -0.5),
        'moe_w_down': nrm((L, N_EXPERTS, D_FF_EXPERT, D_MODEL), BETA * D_FF_EXPERT ** -0.5),
        'ln3_g': gain((L, D_MODEL)),
        'ln3_b': nrm((L, D_MODEL), 0.02),
    }


def reference(x, mem, positions, w_in, gmlp_v_norm_g, gmlp_w_s, gmlp_b_s, mla_q_norm_g, mla_w_uq,
              mla_kv_norm_g, mla_w_ukv, conv_w, pool_w, pool_scale, w_out, ln1_g, ln1_b,
              xattn_wq, xattn_wk, xattn_wv, xattn_wo, ln2_g, ln2_b, router_w, router_b,
              moe_w_gate, moe_w_up, moe_w_down, ln3_g, ln3_b):
    inv_freq = ROPE_BASE ** (-jnp.arange(0, MLA_ROPE, 2, dtype=jnp.float32) / MLA_ROPE)
    ang = positions.astype(jnp.float32)[..., None] * inv_freq
    cos = jnp.cos(ang).astype(x.dtype)
    sin = jnp.sin(ang).astype(x.dtype)
    for l in range(DEPTH):
        h = hybrid_mixer(x, cos, sin, w_in[l], gmlp_v_norm_g[l], gmlp_w_s[l], gmlp_b_s[l],
                         mla_q_norm_g[l], mla_w_uq[l], mla_kv_norm_g[l], mla_w_ukv[l],
                         conv_w[l], pool_w[l], pool_scale[l], w_out[l])
        x = layer_norm(ALPHA * x + h, ln1_g[l], ln1_b[l])
        h = memory_cross_attention(x, mem, xattn_wq[l], xattn_wk[l], xattn_wv[l], xattn_wo[l])
        x = layer_norm(ALPHA * x + h, ln2_g[l], ln2_b[l])
        h = grouped_moe(x, router_w, router_b, moe_w_gate[l], moe_w_up[l], moe_w_down[l])
        x = layer_norm(ALPHA * x + h, ln3_g[l], ln3_b[l])
    return x
```

```python
import functools

import jax
import jax.numpy as jnp
from jax import lax
from jax.experimental import pallas as pl
from jax.experimental.pallas import tpu as pltpu

F32 = jnp.float32
BF16 = jnp.bfloat16

D_MODEL = 1024
A_HEADS, A_HEAD_DIM, CHUNK = 4, 64, 128
A_WIDTH = A_HEADS * A_HEAD_DIM
MLA_HEADS, MLA_NOPE, MLA_ROPE, MLA_V = 8, 64, 32, 64
Q_LORA, KV_LORA = 256, 128
MLA_WIDTH = MLA_HEADS * MLA_V
ROPE_BASE = 10000.0
C_WIDTH, CONV_W = 256, 3
D_WIDTH = 256
POOL_WINDOWS = (2, 4, 8, 16)
D_GROUP = D_WIDTH // len(POOL_WINDOWS)
X_HEADS = 4
X_HEAD_DIM = D_MODEL // X_HEADS
N_EXPERTS, N_GROUPS, TOP_K = 16, 4, 2
EXPERTS_PER_GROUP = N_EXPERTS // N_GROUPS
D_FF = 256
EPS = 1e-6

LANES = 128
SUBLANES = 8
VMEM_LIMIT = 56 * 1024 * 1024

OFF_U, OFF_V, OFF_CQ, OFF_CKV = 0, 256, 512, 768
OFF_BG, OFF_CG, OFF_H, OFF_D, OFF_ROPE = 896, 1152, 1408, 1664, 1920
P_EXT = 2048
QK_DIM = 128
POOL_HALO = 16
NEG_BIG = -1e30


def _dot(a, b):
    return jnp.dot(a, b, preferred_element_type=F32)


def _dot_nt(a, b):
    return lax.dot_general(a, b, (((1,), (1,)), ((), ())), preferred_element_type=F32)


def _rms(x, g):
    return x * lax.rsqrt(jnp.mean(x * x, axis=-1, keepdims=True) + EPS) * g


def _layer_norm(x, g, b):
    mu = jnp.mean(x, axis=-1, keepdims=True)
    xc = x - mu
    var = jnp.mean(xc * xc, axis=-1, keepdims=True)
    return xc * lax.rsqrt(var + EPS) * g + b


def _params(n_grid):
    return pltpu.CompilerParams(dimension_semantics=("arbitrary",) * n_grid,
                                vmem_limit_bytes=VMEM_LIMIT)


def _const_spec(shape):
    zeros = (0,) * len(shape)
    return pl.BlockSpec(shape, lambda *_: zeros)


def _mem_kv_kernel(mem_ref, wk_ref, wv_ref, k_ref, v_ref):
    m = mem_ref[0].astype(BF16)
    k_ref[0, 0] = _dot(m, wk_ref[0]).astype(BF16)
    v_ref[0, 0] = _dot(m, wv_ref[0]).astype(BF16)


def _mem_kv(mem, wk, wv):
    depth = wk.shape[0]
    bsz, mlen, d = mem.shape
    out = jax.ShapeDtypeStruct((depth, bsz, mlen, d), BF16)
    return pl.pallas_call(
        _mem_kv_kernel,
        grid=(depth, bsz),
        in_specs=[pl.BlockSpec((1, mlen, d), lambda l, b: (b, 0, 0)),
                  pl.BlockSpec((1, d, d), lambda l, b: (l, 0, 0)),
                  pl.BlockSpec((1, d, d), lambda l, b: (l, 0, 0))],
        out_specs=[pl.BlockSpec((1, 1, mlen, d), lambda l, b: (l, b, 0, 0)),
                   pl.BlockSpec((1, 1, mlen, d), lambda l, b: (l, b, 0, 0))],
        out_shape=[out, out],
        compiler_params=_params(2),
        name="mem_kv",
    )(mem, wk, wv)


def _mixer_in_kernel(x_ref, csq_ref, csk_ref, w_in_ref, vng_ref, ws_ref, bs_ref, qng_ref,
                     wuq_ref, kvng_ref, wk_ref, wv_ref, convw_ref, poolw_ref, pscale_ref,
                     q_ref, k_ref, v_ref, y_ref,
                     conv_buf, pool_a, pool_b, pool_carry, *, tm):
    si = pl.program_id(1)
    xb = x_ref[0].astype(BF16)

    za = jax.nn.gelu(_dot(xb, w_in_ref[:, OFF_U:OFF_CQ]))
    u = za[:, :A_WIDTH]
    v = _rms(za[:, A_WIDTH:], vng_ref[...]).astype(BF16)
    row = lax.broadcasted_iota(jnp.int32, (CHUNK, CHUNK), 0)
    col = lax.broadcasted_iota(jnp.int32, (CHUNK, CHUNK), 1)
    w_causal = [jnp.where(row >= col, ws_ref[h], jnp.zeros((), BF16)) for h in range(A_HEADS)]
    lane_head = lax.broadcasted_iota(jnp.int32, (CHUNK, A_WIDTH), 1) // A_HEAD_DIM
    for c in range(tm // CHUNK):
        rows = slice(c * CHUNK, (c + 1) * CHUNK)
        vc = v[rows]
        mixed = _dot(w_causal[0], vc)
        for h in range(1, A_HEADS):
            mixed = jnp.where(lane_head == h, _dot(w_causal[h], vc), mixed)
        y_ref[0, rows, 0:A_WIDTH] = (u[rows] * (mixed + bs_ref[...])).astype(BF16)

    zb = _dot(xb, w_in_ref[:, OFF_CQ:OFF_BG])
    cq = _rms(zb[:, :Q_LORA], qng_ref[...]).astype(BF16)
    ckv = _rms(zb[:, Q_LORA:], kvng_ref[...]).astype(BF16)
    q = _dot(cq, wuq_ref[...])
    csq = csq_ref[0]
    for h in range(MLA_HEADS):
        q_ref[0, h] = (q[:, h * QK_DIM:(h + 1) * QK_DIM] * csq).astype(BF16)
    r4 = _dot(xb, w_in_ref[:, OFF_ROPE:P_EXT]) * csk_ref[0]
    kr = r4 + pltpu.roll(r4, MLA_ROPE, 1)
    lane = lax.broadcasted_iota(jnp.int32, (tm, QK_DIM), 1)
    kr = jnp.where(lane >= MLA_NOPE, kr, 0.0)
    kk = _dot(ckv, wk_ref[...])
    for h in range(MLA_HEADS):
        k_ref[0, h] = (kk[:, h * QK_DIM:(h + 1) * QK_DIM] + kr).astype(BF16)
    vv = _dot(ckv, wv_ref[...])
    for p in range(MLA_HEADS // 2):
        v_ref[0, p] = vv[:, p * LANES:(p + 1) * LANES].astype(BF16)

    zc = _dot(xb, w_in_ref[:, OFF_BG:OFF_D])
    gh = zc[:, C_WIDTH:2 * C_WIDTH] * zc[:, 2 * C_WIDTH:]

    @pl.when(si == 0)
    def _():
        conv_buf[0:SUBLANES, :] = jnp.zeros((SUBLANES, C_WIDTH), F32)

    conv_buf[SUBLANES:SUBLANES + tm, :] = gh
    conv = (convw_ref[2:3, :] * gh
            + convw_ref[1:2, :] * conv_buf[SUBLANES - 1:SUBLANES - 1 + tm, :]
            + convw_ref[0:1, :] * conv_buf[SUBLANES - 2:SUBLANES - 2 + tm, :])
    y_ref[0, :, A_WIDTH:A_WIDTH + C_WIDTH] = (zc[:, :C_WIDTH] * conv).astype(BF16)
    conv_buf[0:SUBLANES, :] = conv_buf[tm:tm + SUBLANES, :]

    zd = _dot(xb, w_in_ref[:, OFF_D:OFF_ROPE])

    @pl.when(si == 0)
    def _():
        pool_carry[...] = jnp.zeros((POOL_HALO, D_WIDTH), F32)

    base = SUBLANES
    n = tm + POOL_HALO
    pool_a[0:base, :] = jnp.zeros((base, D_WIDTH), F32)
    pool_b[0:base, :] = jnp.zeros((base, D_WIDTH), F32)
    pool_a[base:base + POOL_HALO, :] = pool_carry[...]
    pool_a[base + POOL_HALO:base + n, :] = zd
    pool_carry[...] = zd[tm - POOL_HALO:, :]
    lane_d = lax.broadcasted_iota(jnp.int32, (n, D_WIDTH), 1)
    pool_b[base:base + n, :] = pool_a[base:base + n, :] + pool_a[base - 1:base - 1 + n, :]
    pool_a[base:base + n, :] = pool_b[base:base + n, :] + jnp.where(
        lane_d >= D_GROUP, pool_b[base - 2:base - 2 + n, :], 0.0)
    pool_b[base:base + n, :] = pool_a[base:base + n, :] + jnp.where(
        lane_d >= 2 * D_GROUP, pool_a[base - 4:base - 4 + n, :], 0.0)
    t0 = base + POOL_HALO
    lane_t = lax.broadcasted_iota(jnp.int32, (tm, D_WIDTH), 1)
    win_sum = pool_b[t0:t0 + tm, :] + jnp.where(
        lane_t >= 3 * D_GROUP, pool_b[t0 - 8:t0 - 8 + tm, :], 0.0)
    pos1 = si * tm + lax.broadcasted_iota(jnp.int32, (tm, D_WIDTH), 0) + 1
    window = jnp.left_shift(2, lane_t // D_GROUP)
    count = jnp.minimum(pos1, window).astype(F32)
    pooled = (win_sum / count - zd).astype(BF16)
    yd = _dot(pooled, poolw_ref[...]) * pscale_ref[...]
    y_ref[0, :, A_WIDTH + C_WIDTH:] = yd.astype(BF16)


def _mixer_in(x, csq, csk, w_in, vng, ws, bs, qng, wuq, kvng, wk, wv, convw, poolw, pscale, *, tm):
    bsz, seq, d = x.shape
    tok = lambda w: pl.BlockSpec((1, tm, w), lambda b, s: (b, s, 0))
    head = lambda n, w: pl.BlockSpec((1, n, tm, w), lambda b, s: (b, 0, s, 0))
    return pl.pallas_call(
        functools.partial(_mixer_in_kernel, tm=tm),
        grid=(bsz, seq // tm),
        in_specs=[tok(d), tok(QK_DIM), tok(QK_DIM),
                  _const_spec(w_in.shape), _const_spec(vng.shape), _const_spec(ws.shape),
                  _const_spec(bs.shape), _const_spec(qng.shape), _const_spec(wuq.shape),
                  _const_spec(kvng.shape), _const_spec(wk.shape), _const_spec(wv.shape),
                  _const_spec(convw.shape), _const_spec(poolw.shape), _const_spec(pscale.shape)],
        out_specs=[head(MLA_HEADS, QK_DIM), head(MLA_HEADS, QK_DIM), head(MLA_HEADS // 2, LANES),
                   tok(A_WIDTH + C_WIDTH + D_WIDTH)],
        out_shape=[jax.ShapeDtypeStruct((bsz, MLA_HEADS, seq, QK_DIM), BF16),
                   jax.ShapeDtypeStruct((bsz, MLA_HEADS, seq, QK_DIM), BF16),
                   jax.ShapeDtypeStruct((bsz, MLA_HEADS // 2, seq, LANES), BF16),
                   jax.ShapeDtypeStruct((bsz, seq, A_WIDTH + C_WIDTH + D_WIDTH), BF16)],
        scratch_shapes=[pltpu.VMEM((tm + SUBLANES, C_WIDTH), F32),
                        pltpu.VMEM((tm + POOL_HALO + SUBLANES, D_WIDTH), F32),
                        pltpu.VMEM((tm + POOL_HALO + SUBLANES, D_WIDTH), F32),
                        pltpu.VMEM((POOL_HALO, D_WIDTH), F32)],
        compiler_params=_params(2),
        name="mixer_in",
    )(x, csq, csk, w_in, vng, ws, bs, qng, wuq, kvng, wk, wv, convw, poolw, pscale)


def _mla_attn_kernel(q_ref, k_ref, v_ref, o_ref, *, tq):
    qi = pl.program_id(2)
    row = lax.broadcasted_iota(jnp.int32, (tq, tq), 0)
    col = lax.broadcasted_iota(jnp.int32, (tq, tq), 1)
    outs = []
    for hh in range(2):
        q = q_ref[0, hh]

        def step(j, carry, masked):
            m, l, acc = carry
            start = pl.multiple_of(j * tq, tq)
            s = _dot_nt(q, k_ref[0, hh, pl.ds(start, tq), :])
            if masked:
                s = jnp.where(col <= row, s, NEG_BIG)
            m_new = jnp.maximum(m, jnp.max(s, axis=-1, keepdims=True))
            alpha = jnp.exp(m - m_new)
            p = jnp.exp(s - m_new)
            l = alpha * l + jnp.sum(p, axis=-1, keepdims=True)
            acc = alpha * acc + _dot(p.astype(BF16), v_ref[0, 0, pl.ds(start, tq), :])
            return m_new, l, acc

        init = (jnp.full((tq, 1), NEG_BIG, F32), jnp.zeros((tq, 1), F32),
                jnp.zeros((tq, LANES), F32))
        carry = lax.fori_loop(0, qi, functools.partial(step, masked=False), init)
        _, l, acc = step(qi, carry, True)
        outs.append(acc / l)
    lane = lax.broadcasted_iota(jnp.int32, (tq, LANES), 1)
    o_ref[0] = jnp.where(lane < MLA_V, outs[0], outs[1]).astype(BF16)


def _mla_attn(q, k, v, *, tq):
    bsz, heads, seq, _ = q.shape
    return pl.pallas_call(
        functools.partial(_mla_attn_kernel, tq=tq),
        grid=(bsz, heads // 2, seq // tq),
        in_specs=[pl.BlockSpec((1, 2, tq, QK_DIM), lambda b, p, i: (b, p, i, 0)),
                  pl.BlockSpec((1, 2, seq, QK_DIM), lambda b, p, i: (b, p, 0, 0)),
                  pl.BlockSpec((1, 1, seq, LANES), lambda b, p, i: (b, p, 0, 0))],
        out_specs=pl.BlockSpec((1, tq, LANES), lambda b, p, i: (b, i, p)),
        out_shape=jax.ShapeDtypeStruct((bsz, seq, MLA_WIDTH), BF16),
        compiler_params=_params(3),
        name="mla_attn",
    )(q, k, v)


def _lane_mate(x, d, period, pos):
    ahead = pltpu.roll(x, LANES - d, 1)
    behind = pltpu.roll(x, period - d, 1)
    return jnp.where(pos + d < period, ahead, behind)


def _route(logits, router_b, lane):
    scores = jax.nn.sigmoid(logits)
    sel = scores + router_b
    pos = lane & (EXPERTS_PER_GROUP - 1)
    mates = [_lane_mate(sel, d, EXPERTS_PER_GROUP, pos) for d in range(1, EXPERTS_PER_GROUP)]
    rank = jnp.zeros(sel.shape, jnp.int32)
    for d, m in enumerate(mates, start=1):
        tie_wins = (pos + d >= EXPERTS_PER_GROUP).astype(jnp.int32)
        rank = rank + jnp.where(m > sel, 1, jnp.where(m == sel, tie_wins, 0))
    vals = [sel] + mates
    top2 = None
    for i in range(EXPERTS_PER_GROUP):
        for j in range(i + 1, EXPERTS_PER_GROUP):
            pair = vals[i] + vals[j]
            top2 = pair if top2 is None else jnp.maximum(top2, pair)
    epos = lane & (N_EXPERTS - 1)
    beaten = jnp.zeros(sel.shape, jnp.int32)
    for d in range(1, N_GROUPS):
        other = _lane_mate(top2, d * EXPERTS_PER_GROUP, N_EXPERTS, epos)
        tie_wins = (epos + d * EXPERTS_PER_GROUP >= N_EXPERTS).astype(jnp.int32)
        beaten = beaten + jnp.where(other > top2, 1, jnp.where(other == top2, tie_wins, 0))
    rejected = beaten + jnp.where(rank < TOP_K, 0, 1) + jnp.where(lane < N_EXPERTS, 0, 1)
    w = jnp.where(rejected == 0, scores, 0.0)
    return w / jnp.sum(w, axis=-1, keepdims=True)


def _post_attn_kernel(yacd_ref, yb_ref, x_ref, wo_acd_ref, wo_b_ref, g1_ref, b1_ref, wq_ref,
                      km_ref, vm_ref, wo_ref, g2_ref, b2_ref, rw_hl_ref, rw_h_ref, rb_ref,
                      x2_ref, gates_ref, *, tm, alpha):
    h = _dot(yacd_ref[0], wo_acd_ref[...]) + _dot(yb_ref[0], wo_b_ref[...])
    x1 = _layer_norm(alpha * x_ref[0] + h, g1_ref[...], b1_ref[...])

    q = _dot(x1.astype(BF16), wq_ref[...])
    heads = []
    for hd in range(X_HEADS):
        cols = slice(hd * X_HEAD_DIM, (hd + 1) * X_HEAD_DIM)
        s = _dot_nt(q[:, cols].astype(BF16), km_ref[0, :, cols])
        e = jnp.exp(s - jnp.max(s, axis=-1, keepdims=True))
        p = e / jnp.sum(e, axis=-1, keepdims=True)
        heads.append(_dot(p.astype(BF16), vm_ref[0, :, cols]).astype(BF16))
    h2 = _dot(jnp.concatenate(heads, axis=1), wo_ref[...])
    x2 = _layer_norm(alpha * x1 + h2, g2_ref[...], b2_ref[...])
    x2_ref[0] = x2

    x_hi = x2.astype(BF16)
    x_lo = (x2 - x_hi.astype(F32)).astype(BF16)
    hl = _dot(x_hi, rw_hl_ref[...])
    logits = hl[:, :LANES] + hl[:, LANES:] + _dot(x_lo, rw_h_ref[...])
    lane = lax.broadcasted_iota(jnp.int32, (tm, LANES), 1)
    gates_ref[0] = _route(logits, rb_ref[...], lane)


def _post_attn(yacd, yb, x, wo_acd, wo_b, g1, b1, wq, km, vm, wo, g2, b2, rw_hl, rw_h, rb,
               *, tm, alpha):
    bsz, seq, d = x.shape
    tok = lambda w: pl.BlockSpec((1, tm, w), lambda b, s: (b, s, 0))
    memspec = pl.BlockSpec((1,) + km.shape[1:], lambda b, s: (b, 0, 0))
    return pl.pallas_call(
        functools.partial(_post_attn_kernel, tm=tm, alpha=alpha),
        grid=(bsz, seq // tm),
        in_specs=[tok(yacd.shape[-1]), tok(yb.shape[-1]), tok(d),
                  _const_spec(wo_acd.shape), _const_spec(wo_b.shape), _const_spec(g1.shape),
                  _const_spec(b1.shape), _const_spec(wq.shape), memspec, memspec,
                  _const_spec(wo.shape), _const_spec(g2.shape), _const_spec(b2.shape),
                  _const_spec(rw_hl.shape), _const_spec(rw_h.shape), _const_spec(rb.shape)],
        out_specs=[tok(d), tok(LANES)],
        out_shape=[jax.ShapeDtypeStruct((bsz, seq, d), F32),
                   jax.ShapeDtypeStruct((bsz, seq, LANES), F32)],
        compiler_params=_params(2),
        name="post_attn",
    )(yacd, yb, x, wo_acd, wo_b, g1, b1, wq, km, vm, wo, g2, b2, rw_hl, rw_h, rb)


def _moe_kernel(x_ref, gates_ref, wg_ref, wu_ref, wd_ref, g3_ref, b3_ref, o_ref, h_buf, *, alpha):
    x = x_ref[0]
    xb = x.astype(BF16)
    gates = gates_ref[0]
    for e in range(N_EXPERTS):
        cols = slice(e * D_FF, (e + 1) * D_FF)
        hg = _dot(xb, wg_ref[:, cols])
        hu = _dot(xb, wu_ref[:, cols])
        h_buf[:, cols] = (jax.nn.silu(hg) * hu * gates[:, e:e + 1]).astype(BF16)
    out = _dot(h_buf[...], wd_ref[...])
    o_ref[0] = _layer_norm(alpha * x + out, g3_ref[...], b3_ref[...])


def _moe(x, gates, wg, wu, wd, g3, b3, *, tm, alpha):
    bsz, seq, d = x.shape
    tok = lambda w: pl.BlockSpec((1, tm, w), lambda b, s: (b, s, 0))
    resident = lambda a: pl.BlockSpec(a.shape, lambda b, s: (0, 0), pipeline_mode=pl.Buffered(1))
    return pl.pallas_call(
        functools.partial(_moe_kernel, alpha=alpha),
        grid=(bsz, seq // tm),
        in_specs=[tok(d), tok(LANES), resident(wg), resident(wu), resident(wd),
                  _const_spec(g3.shape), _const_spec(b3.shape)],
        out_specs=tok(d),
        out_shape=jax.ShapeDtypeStruct((bsz, seq, d), F32),
        scratch_shapes=[pltpu.VMEM((tm, N_EXPERTS * D_FF), BF16)],
        compiler_params=_params(2),
        name="moe",
    )(x, gates, wg, wu, wd, g3, b3)


def _rot_cols(w):
    half = w.shape[-1] // 2
    return jnp.concatenate([-w[..., half:], w[..., :half]], axis=-1)


def _prep_w_in(w_in):
    u, v = w_in[..., 0:256], w_in[..., 256:512]
    cq, ckv, kr = w_in[..., 512:768], w_in[..., 768:896], w_in[..., 896:928]
    rest = w_in[..., 928:]
    rope4 = jnp.concatenate([kr, _rot_cols(kr), kr, _rot_cols(kr)], axis=-1)
    return jnp.concatenate([u, v, cq, ckv, rest, rope4], axis=-1).astype(BF16)


def _prep_w_uq(w_uq):
    depth = w_uq.shape[0]
    w = w_uq.reshape(depth, Q_LORA, MLA_HEADS, MLA_NOPE + MLA_ROPE)
    nope, rope = w[..., :MLA_NOPE], w[..., MLA_NOPE:]
    ext = jnp.concatenate([nope, rope, _rot_cols(rope)], axis=-1)
    return ext.reshape(depth, Q_LORA, MLA_HEADS * QK_DIM).astype(BF16)


def _prep_w_ukv(w_ukv):
    depth = w_ukv.shape[0]
    w = w_ukv.reshape(depth, KV_LORA, MLA_HEADS, MLA_NOPE + MLA_V)
    k_nope, v = w[..., :MLA_NOPE], w[..., MLA_NOPE:]
    wk = jnp.concatenate([k_nope, jnp.zeros_like(k_nope)], axis=-1)
    return (wk.reshape(depth, KV_LORA, MLA_HEADS * QK_DIM).astype(BF16),
            v.reshape(depth, KV_LORA, MLA_WIDTH).astype(BF16))


def _prep_pool_w(pool_w):
    depth, groups = pool_w.shape[:2]
    eye = jnp.eye(groups, dtype=pool_w.dtype)
    bd = jnp.einsum('lgcd,gh->lgchd', pool_w, eye)
    return bd.reshape(depth, D_WIDTH, D_WIDTH).astype(BF16)


def _pad_lanes(a):
    return jnp.pad(a, [(0, 0)] * (a.ndim - 1) + [(0, LANES - a.shape[-1])])


def kernel(x, mem, positions, w_in, gmlp_v_norm_g, gmlp_w_s, gmlp_b_s, mla_q_norm_g, mla_w_uq,
           mla_kv_norm_g, mla_w_ukv, conv_w, pool_w, pool_scale, w_out, ln1_g, ln1_b,
           xattn_wq, xattn_wk, xattn_wv, xattn_wo, ln2_g, ln2_b, router_w, router_b,
           moe_w_gate, moe_w_up, moe_w_down, ln3_g, ln3_b):
    depth = w_in.shape[0]
    alpha = (2 * depth) ** 0.25
    tm = 512
    tq = 256

    inv_freq = ROPE_BASE ** (-jnp.arange(0, MLA_ROPE, 2, dtype=F32) / MLA_ROPE)
    ang = positions.astype(F32)[..., None] * inv_freq
    cos2 = jnp.tile(jnp.cos(ang), (1, 1, 2))
    sin2 = jnp.tile(jnp.sin(ang), (1, 1, 2))
    scale = (MLA_NOPE + MLA_ROPE) ** -0.5
    csq = scale * jnp.concatenate([jnp.ones(cos2.shape[:2] + (MLA_NOPE,), F32), cos2, sin2], -1)
    csk = jnp.concatenate([cos2, sin2, cos2, sin2], axis=-1)

    row = lambda a: a[:, None, :]
    w_in_e = _prep_w_in(w_in)
    w_uq_e = _prep_w_uq(mla_w_uq)
    w_k_e, w_v_e = _prep_w_ukv(mla_w_ukv)
    ws = gmlp_w_s.astype(BF16)
    bs = jnp.repeat(jnp.swapaxes(gmlp_b_s, 1, 2), A_HEAD_DIM, axis=2)
    pool_bd = _prep_pool_w(pool_w)
    wo_acd = jnp.concatenate([w_out[:, :A_WIDTH], w_out[:, A_WIDTH + MLA_WIDTH:]], axis=1).astype(BF16)
    wo_b = w_out[:, A_WIDTH:A_WIDTH + MLA_WIDTH].astype(BF16)
    wq = (xattn_wq * (X_HEAD_DIM ** -0.5)).astype(BF16)
    wo = xattn_wo.astype(BF16)
    rw = _pad_lanes(router_w)
    rw_hi = rw.astype(BF16)
    rw_lo = (rw - rw_hi.astype(F32)).astype(BF16)
    rw_hl = jnp.concatenate([rw_hi, rw_lo], axis=1)
    rb = _pad_lanes(router_b[None, :])
    cat_experts = lambda w: jnp.swapaxes(w, 1, 2).reshape(depth, D_MODEL, N_EXPERTS * D_FF)
    wg = cat_experts(moe_w_gate).astype(BF16)
    wu = cat_experts(moe_w_up).astype(BF16)
    wd = moe_w_down.reshape(depth, N_EXPERTS * D_FF, D_MODEL).astype(BF16)

    km, vm = _mem_kv(mem, xattn_wk.astype(BF16), xattn_wv.astype(BF16))

    for l in range(depth):
        q, k, v, yacd = _mixer_in(
            x, csq, csk, w_in_e[l], row(gmlp_v_norm_g)[l], ws[l], bs[l], row(mla_q_norm_g)[l],
            w_uq_e[l], row(mla_kv_norm_g)[l], w_k_e[l], w_v_e[l], conv_w[l], pool_bd[l],
            row(pool_scale)[l], tm=tm)
        yb = _mla_attn(q, k, v, tq=tq)
        x2, gates = _post_attn(
            yacd, yb, x, wo_acd[l], wo_b[l], row(ln1_g)[l], row(ln1_b)[l], wq[l], km[l], vm[l],
            wo[l], row(ln2_g)[l], row(ln2_b)[l], rw_hl, rw_hi, rb, tm=tm, alpha=alpha)
        x = _moe(x2, gates, wg[l], wu[l], wd[l], row(ln3_g)[l], row(ln3_b)[l], tm=tm, alpha=alpha)
    return x
```

```python
import functools

import jax
import jax.numpy as jnp
from jax import lax
from jax.experimental import pallas as pl
from jax.experimental.pallas import tpu as pltpu

F32 = jnp.float32
BF16 = jnp.bfloat16

D_MODEL = 1024
A_HEADS, A_HEAD_DIM, CHUNK = 4, 64, 128
A_WIDTH = A_HEADS * A_HEAD_DIM
MLA_HEADS, MLA_NOPE, MLA_ROPE, MLA_V = 8, 64, 32, 64
Q_LORA, KV_LORA = 256, 128
MLA_WIDTH = MLA_HEADS * MLA_V
ROPE_BASE = 10000.0
C_WIDTH, CONV_W = 256, 3
D_WIDTH = 256
POOL_WINDOWS = (2, 4, 8, 16)
D_GROUP = D_WIDTH // len(POOL_WINDOWS)
X_HEADS = 4
X_HEAD_DIM = D_MODEL // X_HEADS
N_EXPERTS, N_GROUPS, TOP_K = 16, 4, 2
EXPERTS_PER_GROUP = N_EXPERTS // N_GROUPS
D_FF = 256
EPS = 1e-6

LANES = 128
SUBLANES = 8
VMEM_LIMIT = 56 * 1024 * 1024

OFF_U, OFF_V, OFF_CQ, OFF_CKV = 0, 256, 512, 768
OFF_BG, OFF_CG, OFF_H, OFF_D, OFF_ROPE = 896, 1152, 1408, 1664, 1920
P_EXT = 2048
QK_DIM = 128
POOL_HALO = 16
NEG_BIG = -1e30


def _dot(a, b):
    return jnp.dot(a, b, preferred_element_type=F32)


def _dot_nt(a, b):
    return lax.dot_general(a, b, (((1,), (1,)), ((), ())), preferred_element_type=F32)


def _rms(x, g):
    return x * lax.rsqrt(jnp.mean(x * x, axis=-1, keepdims=True) + EPS) * g


def _layer_norm(x, g, b):
    mu = jnp.mean(x, axis=-1, keepdims=True)
    xc = x - mu
    var = jnp.mean(xc * xc, axis=-1, keepdims=True)
    return xc * lax.rsqrt(var + EPS) * g + b


def _params(n_grid):
    return pltpu.CompilerParams(dimension_semantics=("arbitrary",) * n_grid,
                                vmem_limit_bytes=VMEM_LIMIT)


def _const_spec(shape):
    zeros = (0,) * len(shape)
    return pl.BlockSpec(shape, lambda *_: zeros)


def _mem_kv_kernel(mem_ref, wk_ref, wv_ref, k_ref, v_ref):
    m = mem_ref[0].astype(BF16)
    k_ref[0, 0] = _dot(m, wk_ref[0]).astype(BF16)
    v_ref[0, 0] = _dot(m, wv_ref[0]).astype(BF16)


def _mem_kv(mem, wk, wv):
    depth = wk.shape[0]
    bsz, mlen, d = mem.shape
    out = jax.ShapeDtypeStruct((depth, bsz, mlen, d), BF16)
    return pl.pallas_call(
        _mem_kv_kernel,
        grid=(depth, bsz),
        in_specs=[pl.BlockSpec((1, mlen, d), lambda l, b: (b, 0, 0)),
                  pl.BlockSpec((1, d, d), lambda l, b: (l, 0, 0)),
                  pl.BlockSpec((1, d, d), lambda l, b: (l, 0, 0))],
        out_specs=[pl.BlockSpec((1, 1, mlen, d), lambda l, b: (l, b, 0, 0)),
                   pl.BlockSpec((1, 1, mlen, d), lambda l, b: (l, b, 0, 0))],
        out_shape=[out, out],
        compiler_params=_params(2),
        name="mem_kv",
    )(mem, wk, wv)


def _mixer_in_kernel(x_ref, csq_ref, csk_ref, w_in_ref, vng_ref, ws_ref, bs_ref, qng_ref,
                     wuq_ref, kvng_ref, wk_ref, wv_ref, convw_ref, poolw_ref, pscale_ref,
                     q_ref, k_ref, v_ref, y_ref,
                     conv_buf, pool_a, pool_b, pool_carry, *, tm):
    si = pl.program_id(1)
    xb = x_ref[0].astype(BF16)

    za = jax.nn.gelu(_dot(xb, w_in_ref[:, OFF_U:OFF_CQ]))
    u = za[:, :A_WIDTH]
    v = _rms(za[:, A_WIDTH:], vng_ref[...]).astype(BF16)
    row = lax.broadcasted_iota(jnp.int32, (CHUNK, CHUNK), 0)
    col = lax.broadcasted_iota(jnp.int32, (CHUNK, CHUNK), 1)
    w_causal = [jnp.where(row >= col, ws_ref[h], jnp.zeros((), BF16)) for h in range(A_HEADS)]
    lane_head = lax.broadcasted_iota(jnp.int32, (CHUNK, A_WIDTH), 1) // A_HEAD_DIM
    for c in range(tm // CHUNK):
        rows = slice(c * CHUNK, (c + 1) * CHUNK)
        vc = v[rows]
        mixed = _dot(w_causal[0], vc)
        for h in range(1, A_HEADS):
            mixed = jnp.where(lane_head == h, _dot(w_causal[h], vc), mixed)
        y_ref[0, rows, 0:A_WIDTH] = (u[rows] * (mixed + bs_ref[...])).astype(BF16)

    zb = _dot(xb, w_in_ref[:, OFF_CQ:OFF_BG])
    cq = _rms(zb[:, :Q_LORA], qng_ref[...]).astype(BF16)
    ckv = _rms(zb[:, Q_LORA:], kvng_ref[...]).astype(BF16)
    q = _dot(cq, wuq_ref[...])
    csq = csq_ref[0]
    for h in range(MLA_HEADS):
        q_ref[0, h] = (q[:, h * QK_DIM:(h + 1) * QK_DIM] * csq).astype(BF16)
    r4 = _dot(xb, w_in_ref[:, OFF_ROPE:P_EXT]) * csk_ref[0]
    kr = r4 + pltpu.roll(r4, MLA_ROPE, 1)
    lane = lax.broadcasted_iota(jnp.int32, (tm, QK_DIM), 1)
    kr = jnp.where(lane >= MLA_NOPE, kr, 0.0)
    kk = _dot(ckv, wk_ref[...])
    for h in range(MLA_HEADS):
        k_ref[0, h] = (kk[:, h * QK_DIM:(h + 1) * QK_DIM] + kr).astype(BF16)
    ones_lane = jnp.where(lane == MLA_V, 1.0, 0.0)
    vv = _dot(ckv, wv_ref[...])
    for h in range(MLA_HEADS):
        v_ref[0, h] = (vv[:, h * LANES:(h + 1) * LANES] + ones_lane).astype(BF16)

    zc = _dot(xb, w_in_ref[:, OFF_BG:OFF_D])
    gh = zc[:, C_WIDTH:2 * C_WIDTH] * zc[:, 2 * C_WIDTH:]

    @pl.when(si == 0)
    def _():
        conv_buf[0:SUBLANES, :] = jnp.zeros((SUBLANES, C_WIDTH), F32)

    conv_buf[SUBLANES:SUBLANES + tm, :] = gh
    conv = (convw_ref[2:3, :] * gh
            + convw_ref[1:2, :] * conv_buf[SUBLANES - 1:SUBLANES - 1 + tm, :]
            + convw_ref[0:1, :] * conv_buf[SUBLANES - 2:SUBLANES - 2 + tm, :])
    y_ref[0, :, A_WIDTH:A_WIDTH + C_WIDTH] = (zc[:, :C_WIDTH] * conv).astype(BF16)
    conv_buf[0:SUBLANES, :] = conv_buf[tm:tm + SUBLANES, :]

    zd = _dot(xb, w_in_ref[:, OFF_D:OFF_ROPE])

    @pl.when(si == 0)
    def _():
        pool_carry[...] = jnp.zeros((POOL_HALO, D_WIDTH), F32)

    base = SUBLANES
    n = tm + POOL_HALO
    pool_a[0:base, :] = jnp.zeros((base, D_WIDTH), F32)
    pool_b[0:base, :] = jnp.zeros((base, D_WIDTH), F32)
    pool_a[base:base + POOL_HALO, :] = pool_carry[...]
    pool_a[base + POOL_HALO:base + n, :] = zd
    pool_carry[...] = zd[tm - POOL_HALO:, :]
    lane_d = lax.broadcasted_iota(jnp.int32, (n, D_WIDTH), 1)
    pool_b[base:base + n, :] = pool_a[base:base + n, :] + pool_a[base - 1:base - 1 + n, :]
    pool_a[base:base + n, :] = pool_b[base:base + n, :] + jnp.where(
        lane_d >= D_GROUP, pool_b[base - 2:base - 2 + n, :], 0.0)
    pool_b[base:base + n, :] = pool_a[base:base + n, :] + jnp.where(
        lane_d >= 2 * D_GROUP, pool_a[base - 4:base - 4 + n, :], 0.0)
    t0 = base + POOL_HALO
    lane_t = lax.broadcasted_iota(jnp.int32, (tm, D_WIDTH), 1)
    win_sum = pool_b[t0:t0 + tm, :] + jnp.where(
        lane_t >= 3 * D_GROUP, pool_b[t0 - 8:t0 - 8 + tm, :], 0.0)
    pos1 = si * tm + lax.broadcasted_iota(jnp.int32, (tm, D_WIDTH), 0) + 1
    window = jnp.left_shift(2, lane_t // D_GROUP)
    count = jnp.minimum(pos1, window).astype(F32)
    pooled = (win_sum / count - zd).astype(BF16)
    yd = _dot(pooled, poolw_ref[...]) * pscale_ref[...]
    y_ref[0, :, A_WIDTH + C_WIDTH:] = yd.astype(BF16)


def _mixer_in(x, csq, csk, w_in, vng, ws, bs, qng, wuq, kvng, wk, wv, convw, poolw, pscale, *, tm):
    bsz, seq, d = x.shape
    tok = lambda w: pl.BlockSpec((1, tm, w), lambda b, s: (b, s, 0))
    head = lambda n, w: pl.BlockSpec((1, n, tm, w), lambda b, s: (b, 0, s, 0))
    return pl.pallas_call(
        functools.partial(_mixer_in_kernel, tm=tm),
        grid=(bsz, seq // tm),
        in_specs=[tok(d), tok(QK_DIM), tok(QK_DIM),
                  _const_spec(w_in.shape), _const_spec(vng.shape), _const_spec(ws.shape),
                  _const_spec(bs.shape), _const_spec(qng.shape), _const_spec(wuq.shape),
                  _const_spec(kvng.shape), _const_spec(wk.shape), _const_spec(wv.shape),
                  _const_spec(convw.shape), _const_spec(poolw.shape), _const_spec(pscale.shape)],
        out_specs=[head(MLA_HEADS, QK_DIM), head(MLA_HEADS, QK_DIM), head(MLA_HEADS, LANES),
                   tok(A_WIDTH + C_WIDTH + D_WIDTH)],
        out_shape=[jax.ShapeDtypeStruct((bsz, MLA_HEADS, seq, QK_DIM), BF16),
                   jax.ShapeDtypeStruct((bsz, MLA_HEADS, seq, QK_DIM), BF16),
                   jax.ShapeDtypeStruct((bsz, MLA_HEADS, seq, LANES), BF16),
                   jax.ShapeDtypeStruct((bsz, seq, A_WIDTH + C_WIDTH + D_WIDTH), BF16)],
        scratch_shapes=[pltpu.VMEM((tm + SUBLANES, C_WIDTH), F32),
                        pltpu.VMEM((tm + POOL_HALO + SUBLANES, D_WIDTH), F32),
                        pltpu.VMEM((tm + POOL_HALO + SUBLANES, D_WIDTH), F32),
                        pltpu.VMEM((POOL_HALO, D_WIDTH), F32)],
        compiler_params=_params(2),
        name="mixer_in",
    )(x, csq, csk, w_in, vng, ws, bs, qng, wuq, kvng, wk, wv, convw, poolw, pscale)


def _mla_attn_kernel(q_ref, k_ref, v_ref, o_ref, *, tq):
    qi = pl.program_id(2)
    row = lax.broadcasted_iota(jnp.int32, (tq, tq), 0)
    col = lax.broadcasted_iota(jnp.int32, (tq, tq), 1)
    qs = [q_ref[0, hh] for hh in range(2)]

    def step(j, carry, masked):
        start = pl.multiple_of(j * tq, tq)
        out = []
        for hh in range(2):
            m, acc = carry[hh]
            s = _dot_nt(qs[hh], k_ref[0, hh, pl.ds(start, tq), :])
            if masked:
                s = jnp.where(col <= row, s, NEG_BIG)
            m_new = jnp.maximum(m, jnp.max(s, axis=-1, keepdims=True))
            p = jnp.exp(s - m_new).astype(BF16)
            pv = _dot(p, v_ref[0, hh, pl.ds(start, tq), :])
            out.append((m_new, jnp.exp(m - m_new) * acc + pv))
        return tuple(out)

    init = tuple((jnp.full((tq, 1), NEG_BIG, F32), jnp.zeros((tq, LANES), F32))
                 for _ in range(2))
    carry = lax.fori_loop(0, qi, functools.partial(step, masked=False), init)
    carry = step(qi, carry, True)
    outs = [acc / acc[:, MLA_V:MLA_V + 1] for _, acc in carry]
    lane = lax.broadcasted_iota(jnp.int32, (tq, LANES), 1)
    o_ref[0] = jnp.where(lane < MLA_V, outs[0], pltpu.roll(outs[1], MLA_V, 1)).astype(BF16)


def _mla_attn(q, k, v, *, tq):
    bsz, heads, seq, _ = q.shape
    pair = lambda rows: pl.BlockSpec((1, 2, rows, LANES), lambda b, p, i: (b, p, 0, 0))
    return pl.pallas_call(
        functools.partial(_mla_attn_kernel, tq=tq),
        grid=(bsz, heads // 2, seq // tq),
        in_specs=[pl.BlockSpec((1, 2, tq, QK_DIM), lambda b, p, i: (b, p, i, 0)),
                  pair(seq), pair(seq)],
        out_specs=pl.BlockSpec((1, tq, LANES), lambda b, p, i: (b, i, p)),
        out_shape=jax.ShapeDtypeStruct((bsz, seq, MLA_WIDTH), BF16),
        compiler_params=_params(3),
        name="mla_attn",
    )(q, k, v)


def _lane_mate(x, d, period, pos):
    ahead = pltpu.roll(x, LANES - d, 1)
    behind = pltpu.roll(x, period - d, 1)
    return jnp.where(pos + d < period, ahead, behind)


def _route(logits, router_b, lane):
    scores = jax.nn.sigmoid(logits)
    sel = scores + router_b
    pos = lane & (EXPERTS_PER_GROUP - 1)
    mates = [_lane_mate(sel, d, EXPERTS_PER_GROUP, pos) for d in range(1, EXPERTS_PER_GROUP)]
    rank = jnp.zeros(sel.shape, jnp.int32)
    for d, m in enumerate(mates, start=1):
        tie_wins = (pos + d >= EXPERTS_PER_GROUP).astype(jnp.int32)
        rank = rank + jnp.where(m > sel, 1, jnp.where(m == sel, tie_wins, 0))
    vals = [sel] + mates
    top2 = None
    for i in range(EXPERTS_PER_GROUP):
        for j in range(i + 1, EXPERTS_PER_GROUP):
            pair = vals[i] + vals[j]
            top2 = pair if top2 is None else jnp.maximum(top2, pair)
    epos = lane & (N_EXPERTS - 1)
    beaten = jnp.zeros(sel.shape, jnp.int32)
    for d in range(1, N_GROUPS):
        other = _lane_mate(top2, d * EXPERTS_PER_GROUP, N_EXPERTS, epos)
        tie_wins = (epos + d * EXPERTS_PER_GROUP >= N_EXPERTS).astype(jnp.int32)
        beaten = beaten + jnp.where(other > top2, 1, jnp.where(other == top2, tie_wins, 0))
    rejected = beaten + jnp.where(rank < TOP_K, 0, 1) + jnp.where(lane < N_EXPERTS, 0, 1)
    w = jnp.where(rejected == 0, scores, 0.0)
    return w / jnp.sum(w, axis=-1, keepdims=True)


def _post_attn_kernel(yacd_ref, yb_ref, x_ref, wo_acd_ref, wo_b_ref, g1_ref, b1_ref, wq_ref,
                      km_ref, vm_ref, wo_ref, g2_ref, b2_ref, rw_hl_ref, rw_h_ref, rb_ref,
                      x2_ref, gates_ref, *, tm, alpha):
    h = _dot(yacd_ref[0], wo_acd_ref[...]) + _dot(yb_ref[0], wo_b_ref[...])
    x1 = _layer_norm(alpha * x_ref[0] + h, g1_ref[...], b1_ref[...])

    q = _dot(x1.astype(BF16), wq_ref[...])
    heads = []
    for hd in range(X_HEADS):
        cols = slice(hd * X_HEAD_DIM, (hd + 1) * X_HEAD_DIM)
        s = _dot_nt(q[:, cols].astype(BF16), km_ref[0, :, cols])
        e = jnp.exp(s - jnp.max(s, axis=-1, keepdims=True))
        p = e / jnp.sum(e, axis=-1, keepdims=True)
        heads.append(_dot(p.astype(BF16), vm_ref[0, :, cols]).astype(BF16))
    h2 = _dot(jnp.concatenate(heads, axis=1), wo_ref[...])
    x2 = _layer_norm(alpha * x1 + h2, g2_ref[...], b2_ref[...])
    x2_ref[0] = x2

    x_hi = x2.astype(BF16)
    x_lo = (x2 - x_hi.astype(F32)).astype(BF16)
    hl = _dot(x_hi, rw_hl_ref[...])
    logits = hl[:, :LANES] + hl[:, LANES:] + _dot(x_lo, rw_h_ref[...])
    lane = lax.broadcasted_iota(jnp.int32, (tm, LANES), 1)
    gates_ref[0] = _route(logits, rb_ref[...], lane)


def _post_attn(yacd, yb, x, wo_acd, wo_b, g1, b1, wq, km, vm, wo, g2, b2, rw_hl, rw_h, rb,
               *, tm, alpha):
    bsz, seq, d = x.shape
    tok = lambda w: pl.BlockSpec((1, tm, w), lambda b, s: (b, s, 0))
    memspec = pl.BlockSpec((1,) + km.shape[1:], lambda b, s: (b, 0, 0))
    return pl.pallas_call(
        functools.partial(_post_attn_kernel, tm=tm, alpha=alpha),
        grid=(bsz, seq // tm),
        in_specs=[tok(yacd.shape[-1]), tok(yb.shape[-1]), tok(d),
                  _const_spec(wo_acd.shape), _const_spec(wo_b.shape), _const_spec(g1.shape),
                  _const_spec(b1.shape), _const_spec(wq.shape), memspec, memspec,
                  _const_spec(wo.shape), _const_spec(g2.shape), _const_spec(b2.shape),
                  _const_spec(rw_hl.shape), _const_spec(rw_h.shape), _const_spec(rb.shape)],
        out_specs=[tok(d), tok(LANES)],
        out_shape=[jax.ShapeDtypeStruct((bsz, seq, d), F32),
                   jax.ShapeDtypeStruct((bsz, seq, LANES), F32)],
        compiler_params=_params(2),
        name="post_attn",
    )(yacd, yb, x, wo_acd, wo_b, g1, b1, wq, km, vm, wo, g2, b2, rw_hl, rw_h, rb)


def _moe_kernel(x_ref, gates_ref, wg_ref, wu_ref, wd_ref, g3_ref, b3_ref, o_ref, h_buf, *, alpha):
    x = x_ref[0]
    xb = x.astype(BF16)
    gates = gates_ref[0]
    for e in range(N_EXPERTS):
        cols = slice(e * D_FF, (e + 1) * D_FF)
        hg = _dot(xb, wg_ref[:, cols])
        hu = _dot(xb, wu_ref[:, cols])
        h_buf[:, cols] = (jax.nn.silu(hg) * hu * gates[:, e:e + 1]).astype(BF16)
    out = _dot(h_buf[...], wd_ref[...])
    o_ref[0] = _layer_norm(alpha * x + out, g3_ref[...], b3_ref[...])


def _moe(x, gates, wg, wu, wd, g3, b3, *, tm, alpha):
    bsz, seq, d = x.shape
    tok = lambda w: pl.BlockSpec((1, tm, w), lambda b, s: (b, s, 0))
    resident = lambda a: pl.BlockSpec(a.shape, lambda b, s: (0, 0), pipeline_mode=pl.Buffered(1))
    return pl.pallas_call(
        functools.partial(_moe_kernel, alpha=alpha),
        grid=(bsz, seq // tm),
        in_specs=[tok(d), tok(LANES), resident(wg), resident(wu), resident(wd),
                  _const_spec(g3.shape), _const_spec(b3.shape)],
        out_specs=tok(d),
        out_shape=jax.ShapeDtypeStruct((bsz, seq, d), F32),
        scratch_shapes=[pltpu.VMEM((tm, N_EXPERTS * D_FF), BF16)],
        compiler_params=_params(2),
        name="moe",
    )(x, gates, wg, wu, wd, g3, b3)


def _rot_cols(w):
    half = w.shape[-1] // 2
    return jnp.concatenate([-w[..., half:], w[..., :half]], axis=-1)


def _prep_w_in(w_in):
    u, v = w_in[..., 0:256], w_in[..., 256:512]
    cq, ckv, kr = w_in[..., 512:768], w_in[..., 768:896], w_in[..., 896:928]
    rest = w_in[..., 928:]
    rope4 = jnp.concatenate([kr, _rot_cols(kr), kr, _rot_cols(kr)], axis=-1)
    return jnp.concatenate([u, v, cq, ckv, rest, rope4], axis=-1).astype(BF16)


def _prep_w_uq(w_uq):
    depth = w_uq.shape[0]
    w = w_uq.reshape(depth, Q_LORA, MLA_HEADS, MLA_NOPE + MLA_ROPE)
    nope, rope = w[..., :MLA_NOPE], w[..., MLA_NOPE:]
    ext = jnp.concatenate([nope, rope, _rot_cols(rope)], axis=-1)
    return ext.reshape(depth, Q_LORA, MLA_HEADS * QK_DIM).astype(BF16)


def _prep_w_ukv(w_ukv):
    depth = w_ukv.shape[0]
    w = w_ukv.reshape(depth, KV_LORA, MLA_HEADS, MLA_NOPE + MLA_V)
    k_nope, v = w[..., :MLA_NOPE], w[..., MLA_NOPE:]
    wk = jnp.concatenate([k_nope, jnp.zeros_like(k_nope)], axis=-1)
    wv = jnp.concatenate([v, jnp.zeros_like(v)], axis=-1)
    return (wk.reshape(depth, KV_LORA, MLA_HEADS * QK_DIM).astype(BF16),
            wv.reshape(depth, KV_LORA, MLA_HEADS * LANES).astype(BF16))


def _prep_pool_w(pool_w):
    depth, groups = pool_w.shape[:2]
    eye = jnp.eye(groups, dtype=pool_w.dtype)
    bd = jnp.einsum('lgcd,gh->lgchd', pool_w, eye)
    return bd.reshape(depth, D_WIDTH, D_WIDTH).astype(BF16)


def _pad_lanes(a):
    return jnp.pad(a, [(0, 0)] * (a.ndim - 1) + [(0, LANES - a.shape[-1])])


def kernel(x, mem, positions, w_in, gmlp_v_norm_g, gmlp_w_s, gmlp_b_s, mla_q_norm_g, mla_w_uq,
           mla_kv_norm_g, mla_w_ukv, conv_w, pool_w, pool_scale, w_out, ln1_g, ln1_b,
           xattn_wq, xattn_wk, xattn_wv, xattn_wo, ln2_g, ln2_b, router_w, router_b,
           moe_w_gate, moe_w_up, moe_w_down, ln3_g, ln3_b):
    depth = w_in.shape[0]
    alpha = (2 * depth) ** 0.25
    tm = 512
    tq = 512

    inv_freq = ROPE_BASE ** (-jnp.arange(0, MLA_ROPE, 2, dtype=F32) / MLA_ROPE)
    ang = positions.astype(F32)[..., None] * inv_freq
    cos2 = jnp.tile(jnp.cos(ang), (1, 1, 2))
    sin2 = jnp.tile(jnp.sin(ang), (1, 1, 2))
    scale = (MLA_NOPE + MLA_ROPE) ** -0.5
    csq = scale * jnp.concatenate([jnp.ones(cos2.shape[:2] + (MLA_NOPE,), F32), cos2, sin2], -1)
    csk = jnp.concatenate([cos2, sin2, cos2, sin2], axis=-1)

    row = lambda a: a[:, None, :]
    w_in_e = _prep_w_in(w_in)
    w_uq_e = _prep_w_uq(mla_w_uq)
    w_k_e, w_v_e = _prep_w_ukv(mla_w_ukv)
    ws = gmlp_w_s.astype(BF16)
    bs = jnp.repeat(jnp.swapaxes(gmlp_b_s, 1, 2), A_HEAD_DIM, axis=2)
    pool_bd = _prep_pool_w(pool_w)
    wo_acd = jnp.concatenate([w_out[:, :A_WIDTH], w_out[:, A_WIDTH + MLA_WIDTH:]], axis=1).astype(BF16)
    wo_b = w_out[:, A_WIDTH:A_WIDTH + MLA_WIDTH].astype(BF16)
    wq = (xattn_wq * (X_HEAD_DIM ** -0.5)).astype(BF16)
    wo = xattn_wo.astype(BF16)
    rw = _pad_lanes(router_w)
    rw_hi = rw.astype(BF16)
    rw_lo = (rw - rw_hi.astype(F32)).astype(BF16)
    rw_hl = jnp.concatenate([rw_hi, rw_lo], axis=1)
    rb = _pad_lanes(router_b[None, :])
    cat_experts = lambda w: jnp.swapaxes(w, 1, 2).reshape(depth, D_MODEL, N_EXPERTS * D_FF)
    wg = cat_experts(moe_w_gate).astype(BF16)
    wu = cat_experts(moe_w_up).astype(BF16)
    wd = moe_w_down.reshape(depth, N_EXPERTS * D_FF, D_MODEL).astype(BF16)

    km, vm = _mem_kv(mem, xattn_wk.astype(BF16), xattn_wv.astype(BF16))

    for l in range(depth):
        q, k, v, yacd = _mixer_in(
            x, csq, csk, w_in_e[l], row(gmlp_v_norm_g)[l], ws[l], bs[l], row(mla_q_norm_g)[l],
            w_uq_e[l], row(mla_kv_norm_g)[l], w_k_e[l], w_v_e[l], conv_w[l], pool_bd[l],
            row(pool_scale)[l], tm=tm)
        yb = _mla_attn(q, k, v, tq=tq)
        x2, gates = _post_attn(
            yacd, yb, x, wo_acd[l], wo_b[l], row(ln1_g)[l], row(ln1_b)[l], wq[l], km[l], vm[l],
            wo[l], row(ln2_g)[l], row(ln2_b)[l], rw_hl, rw_hi, rb, tm=tm, alpha=alpha)
        x = _moe(x2, gates, wg[l], wu[l], wd[l], row(ln3_g)[l], row(ln3_b)[l], tm=tm, alpha=alpha)
    return x
```

```python
import functools

import jax
import jax.numpy as jnp
from jax import lax
from jax.experimental import pallas as pl
from jax.experimental.pallas import tpu as pltpu

F32 = jnp.float32
BF16 = jnp.bfloat16

D_MODEL = 1024
A_HEADS, A_HEAD_DIM, CHUNK = 4, 64, 128
A_WIDTH = A_HEADS * A_HEAD_DIM
MLA_HEADS, MLA_NOPE, MLA_ROPE, MLA_V = 8, 64, 32, 64
Q_LORA, KV_LORA = 256, 128
MLA_WIDTH = MLA_HEADS * MLA_V
ROPE_BASE = 10000.0
C_WIDTH, CONV_W = 256, 3
D_WIDTH = 256
POOL_WINDOWS = (2, 4, 8, 16)
D_GROUP = D_WIDTH // len(POOL_WINDOWS)
X_HEADS = 4
X_HEAD_DIM = D_MODEL // X_HEADS
N_EXPERTS, N_GROUPS, TOP_K = 16, 4, 2
EXPERTS_PER_GROUP = N_EXPERTS // N_GROUPS
D_FF = 256
EPS = 1e-6

LANES = 128
SUBLANES = 8
VMEM_LIMIT = 56 * 1024 * 1024

OFF_U, OFF_V, OFF_CQ, OFF_CKV = 0, 256, 512, 768
OFF_BG, OFF_CG, OFF_H, OFF_D, OFF_ROPE = 896, 1152, 1408, 1664, 1920
P_EXT = 2048
QK_DIM = 128
POOL_HALO = 16
NEG_BIG = -1e30
LOG2_E = 1.4426950408889634


def _dot(a, b):
    return jnp.dot(a, b, preferred_element_type=F32)


def _dot_nt(a, b):
    return lax.dot_general(a, b, (((1,), (1,)), ((), ())), preferred_element_type=F32)


def _rms(x, g):
    return x * lax.rsqrt(jnp.mean(x * x, axis=-1, keepdims=True) + EPS) * g


def _layer_norm(x, g, b):
    mu = jnp.mean(x, axis=-1, keepdims=True)
    xc = x - mu
    var = jnp.mean(xc * xc, axis=-1, keepdims=True)
    return xc * lax.rsqrt(var + EPS) * g + b


def _params(n_grid):
    return pltpu.CompilerParams(dimension_semantics=("arbitrary",) * n_grid,
                                vmem_limit_bytes=VMEM_LIMIT)


def _const_spec(shape):
    zeros = (0,) * len(shape)
    return pl.BlockSpec(shape, lambda *_: zeros)


def _mem_kv_kernel(mem_ref, wk_ref, wv_ref, k_ref, v_ref):
    m = mem_ref[0].astype(BF16)
    k_ref[0, 0] = _dot(m, wk_ref[0]).astype(BF16)
    v_ref[0, 0] = _dot(m, wv_ref[0]).astype(BF16)


def _mem_kv(mem, wk, wv):
    depth = wk.shape[0]
    bsz, mlen, d = mem.shape
    out = jax.ShapeDtypeStruct((depth, bsz, mlen, d), BF16)
    return pl.pallas_call(
        _mem_kv_kernel,
        grid=(depth, bsz),
        in_specs=[pl.BlockSpec((1, mlen, d), lambda l, b: (b, 0, 0)),
                  pl.BlockSpec((1, d, d), lambda l, b: (l, 0, 0)),
                  pl.BlockSpec((1, d, d), lambda l, b: (l, 0, 0))],
        out_specs=[pl.BlockSpec((1, 1, mlen, d), lambda l, b: (l, b, 0, 0)),
                   pl.BlockSpec((1, 1, mlen, d), lambda l, b: (l, b, 0, 0))],
        out_shape=[out, out],
        compiler_params=_params(2),
        name="mem_kv",
    )(mem, wk, wv)


def _mixer_in_kernel(x_ref, csq_ref, csk_ref, w_in_ref, vng_ref, ws_ref, bs_ref, qng_ref,
                     wuq_ref, kvng_ref, wk_ref, wv_ref, convw_ref, poolw_ref, pscale_ref,
                     q_ref, k_ref, v_ref, y_ref,
                     conv_buf, pool_a, pool_b, pool_carry, *, tm):
    si = pl.program_id(1)
    xb = x_ref[0].astype(BF16)

    za = jax.nn.gelu(_dot(xb, w_in_ref[:, OFF_U:OFF_CQ]))
    u = za[:, :A_WIDTH]
    v = _rms(za[:, A_WIDTH:], vng_ref[...]).astype(BF16)
    row = lax.broadcasted_iota(jnp.int32, (CHUNK, CHUNK), 0)
    col = lax.broadcasted_iota(jnp.int32, (CHUNK, CHUNK), 1)
    w_causal = [jnp.where(row >= col, ws_ref[h], jnp.zeros((), BF16)) for h in range(A_HEADS)]
    lane_head = lax.broadcasted_iota(jnp.int32, (CHUNK, A_WIDTH), 1) // A_HEAD_DIM
    for c in range(tm // CHUNK):
        rows = slice(c * CHUNK, (c + 1) * CHUNK)
        vc = v[rows]
        mixed = _dot(w_causal[0], vc)
        for h in range(1, A_HEADS):
            mixed = jnp.where(lane_head == h, _dot(w_causal[h], vc), mixed)
        y_ref[0, rows, 0:A_WIDTH] = (u[rows] * (mixed + bs_ref[...])).astype(BF16)

    zb = _dot(xb, w_in_ref[:, OFF_CQ:OFF_BG])
    cq = _rms(zb[:, :Q_LORA], qng_ref[...]).astype(BF16)
    ckv = _rms(zb[:, Q_LORA:], kvng_ref[...]).astype(BF16)
    qt = _dot_nt(wuq_ref[...], cq)
    csq = csq_ref[0, 0]
    for h in range(MLA_HEADS):
        q_ref[0, h, 0] = (qt[h * QK_DIM:(h + 1) * QK_DIM, :] * csq).astype(BF16)
    r4 = _dot(xb, w_in_ref[:, OFF_ROPE:P_EXT]) * csk_ref[0]
    kr = r4 + pltpu.roll(r4, MLA_ROPE, 1)
    lane = lax.broadcasted_iota(jnp.int32, (tm, QK_DIM), 1)
    kr = jnp.where(lane >= MLA_NOPE, kr, 0.0)
    kk = _dot(ckv, wk_ref[...])
    for h in range(MLA_HEADS):
        k_ref[0, h] = (kk[:, h * QK_DIM:(h + 1) * QK_DIM] + kr).astype(BF16)
    ones_row = jnp.where(lax.broadcasted_iota(jnp.int32, (LANES, tm), 0) == MLA_V, 1.0, 0.0)
    vt = _dot_nt(wv_ref[...], ckv)
    for h in range(MLA_HEADS):
        v_ref[0, h, 0] = (vt[h * LANES:(h + 1) * LANES, :] + ones_row).astype(BF16)

    zc = _dot(xb, w_in_ref[:, OFF_BG:OFF_D])
    gh = zc[:, C_WIDTH:2 * C_WIDTH] * zc[:, 2 * C_WIDTH:]

    @pl.when(si == 0)
    def _():
        conv_buf[0:SUBLANES, :] = jnp.zeros((SUBLANES, C_WIDTH), F32)

    conv_buf[SUBLANES:SUBLANES + tm, :] = gh
    conv = (convw_ref[2:3, :] * gh
            + convw_ref[1:2, :] * conv_buf[SUBLANES - 1:SUBLANES - 1 + tm, :]
            + convw_ref[0:1, :] * conv_buf[SUBLANES - 2:SUBLANES - 2 + tm, :])
    y_ref[0, :, A_WIDTH:A_WIDTH + C_WIDTH] = (zc[:, :C_WIDTH] * conv).astype(BF16)
    conv_buf[0:SUBLANES, :] = conv_buf[tm:tm + SUBLANES, :]

    zd = _dot(xb, w_in_ref[:, OFF_D:OFF_ROPE])

    @pl.when(si == 0)
    def _():
        pool_carry[...] = jnp.zeros((POOL_HALO, D_WIDTH), F32)

    base = SUBLANES
    n = tm + POOL_HALO
    pool_a[0:base, :] = jnp.zeros((base, D_WIDTH), F32)
    pool_b[0:base, :] = jnp.zeros((base, D_WIDTH), F32)
    pool_a[base:base + POOL_HALO, :] = pool_carry[...]
    pool_a[base + POOL_HALO:base + n, :] = zd
    pool_carry[...] = zd[tm - POOL_HALO:, :]
    lane_d = lax.broadcasted_iota(jnp.int32, (n, D_WIDTH), 1)
    pool_b[base:base + n, :] = pool_a[base:base + n, :] + pool_a[base - 1:base - 1 + n, :]
    pool_a[base:base + n, :] = pool_b[base:base + n, :] + jnp.where(
        lane_d >= D_GROUP, pool_b[base - 2:base - 2 + n, :], 0.0)
    pool_b[base:base + n, :] = pool_a[base:base + n, :] + jnp.where(
        lane_d >= 2 * D_GROUP, pool_a[base - 4:base - 4 + n, :], 0.0)
    t0 = base + POOL_HALO
    lane_t = lax.broadcasted_iota(jnp.int32, (tm, D_WIDTH), 1)
    win_sum = pool_b[t0:t0 + tm, :] + jnp.where(
        lane_t >= 3 * D_GROUP, pool_b[t0 - 8:t0 - 8 + tm, :], 0.0)
    pos1 = si * tm + lax.broadcasted_iota(jnp.int32, (tm, D_WIDTH), 0) + 1
    window = jnp.left_shift(2, lane_t // D_GROUP)
    count = jnp.minimum(pos1, window).astype(F32)
    pooled = (win_sum / count - zd).astype(BF16)
    yd = _dot(pooled, poolw_ref[...]) * pscale_ref[...]
    y_ref[0, :, A_WIDTH + C_WIDTH:] = yd.astype(BF16)


def _mixer_in(x, csq, csk, w_in, vng, ws, bs, qng, wuq, kvng, wk, wv, convw, poolw, pscale, *, tm):
    bsz, seq, d = x.shape
    tok = lambda w: pl.BlockSpec((1, tm, w), lambda b, s: (b, s, 0))
    head = lambda n, w: pl.BlockSpec((1, n, tm, w), lambda b, s: (b, 0, s, 0))
    head_t = pl.BlockSpec((1, MLA_HEADS, 1, LANES, tm), lambda b, s: (b, 0, s, 0, 0))
    shape_t = jax.ShapeDtypeStruct((bsz, MLA_HEADS, seq // tm, LANES, tm), BF16)
    return pl.pallas_call(
        functools.partial(_mixer_in_kernel, tm=tm),
        grid=(bsz, seq // tm),
        in_specs=[tok(d), pl.BlockSpec((1, 1, QK_DIM, tm), lambda b, s: (b, s, 0, 0)), tok(QK_DIM),
                  _const_spec(w_in.shape), _const_spec(vng.shape), _const_spec(ws.shape),
                  _const_spec(bs.shape), _const_spec(qng.shape), _const_spec(wuq.shape),
                  _const_spec(kvng.shape), _const_spec(wk.shape), _const_spec(wv.shape),
                  _const_spec(convw.shape), _const_spec(poolw.shape), _const_spec(pscale.shape)],
        out_specs=[head_t, head(MLA_HEADS, QK_DIM), head_t, tok(A_WIDTH + C_WIDTH + D_WIDTH)],
        out_shape=[shape_t,
                   jax.ShapeDtypeStruct((bsz, MLA_HEADS, seq, QK_DIM), BF16),
                   shape_t,
                   jax.ShapeDtypeStruct((bsz, seq, A_WIDTH + C_WIDTH + D_WIDTH), BF16)],
        scratch_shapes=[pltpu.VMEM((tm + SUBLANES, C_WIDTH), F32),
                        pltpu.VMEM((tm + POOL_HALO + SUBLANES, D_WIDTH), F32),
                        pltpu.VMEM((tm + POOL_HALO + SUBLANES, D_WIDTH), F32),
                        pltpu.VMEM((POOL_HALO, D_WIDTH), F32)],
        compiler_params=_params(2),
        name="mixer_in",
    )(x, csq, csk, w_in, vng, ws, bs, qng, wuq, kvng, wk, wv, convw, poolw, pscale)


def _mla_attn_kernel(qt_ref, k_ref, vt_ref, o_ref, m_ref, acc_ref, *, tq, hg):
    qi = pl.program_id(2)
    key = lax.broadcasted_iota(jnp.int32, (tq, tq), 0)
    qry = lax.broadcasted_iota(jnp.int32, (tq, tq), 1)
    m_ref[...] = jnp.full(m_ref.shape, NEG_BIG, F32)
    acc_ref[...] = jnp.zeros(acc_ref.shape, F32)

    def step(j, masked):
        def scores(hh):
            return _dot(k_ref[0, hh, pl.ds(pl.multiple_of(j * tq, tq), tq), :], qt_ref[0, hh, 0])

        st_next = scores(0)
        for hh in range(hg):
            st = st_next
            if hh + 1 < hg:
                st_next = scores(hh + 1)
            if masked:
                st = jnp.where(key <= qry, st, NEG_BIG)
            m_old = m_ref[hh]
            m_new = jnp.maximum(m_old, jnp.max(st, axis=0, keepdims=True))
            pt = jnp.exp2(st - m_new).astype(BF16)
            pv = _dot(vt_ref[0, hh, j], pt)
            acc_ref[hh] = jnp.exp2(m_old - m_new) * acc_ref[hh] + pv
            m_ref[hh] = m_new

    def body(j, carry):
        step(j, False)
        return carry

    lax.fori_loop(0, qi, body, 0)
    step(qi, True)
    for pr in range(hg // 2):
        halves = []
        for hh in (2 * pr, 2 * pr + 1):
            acc = acc_ref[hh]
            halves.append(acc[:MLA_V] / acc[MLA_V:MLA_V + 1])
        o_ref[0, :, pr * LANES:(pr + 1) * LANES] = jnp.concatenate(halves, axis=0).T.astype(BF16)


def _mla_attn(qt, k, vt, *, hg):
    bsz, heads, n_tiles, _, tq = qt.shape
    seq = n_tiles * tq
    return pl.pallas_call(
        functools.partial(_mla_attn_kernel, tq=tq, hg=hg),
        grid=(bsz, heads // hg, n_tiles),
        in_specs=[pl.BlockSpec((1, hg, 1, QK_DIM, tq), lambda b, g, i: (b, g, i, 0, 0)),
                  pl.BlockSpec((1, hg, seq, QK_DIM), lambda b, g, i: (b, g, 0, 0)),
                  pl.BlockSpec((1, hg, n_tiles, LANES, tq), lambda b, g, i: (b, g, 0, 0, 0))],
        out_specs=pl.BlockSpec((1, tq, hg * MLA_V), lambda b, g, i: (b, i, g)),
        out_shape=jax.ShapeDtypeStruct((bsz, seq, MLA_WIDTH), BF16),
        scratch_shapes=[pltpu.VMEM((hg, 1, tq), F32), pltpu.VMEM((hg, LANES, tq), F32)],
        compiler_params=_params(3),
        name="mla_attn",
    )(qt, k, vt)


def _lane_mate(x, d, period, pos):
    ahead = pltpu.roll(x, LANES - d, 1)
    behind = pltpu.roll(x, period - d, 1)
    return jnp.where(pos + d < period, ahead, behind)


def _route(logits, router_b, lane):
    scores = jax.nn.sigmoid(logits)
    sel = scores + router_b
    pos = lane & (EXPERTS_PER_GROUP - 1)
    mates = [_lane_mate(sel, d, EXPERTS_PER_GROUP, pos) for d in range(1, EXPERTS_PER_GROUP)]
    rank = jnp.zeros(sel.shape, jnp.int32)
    for d, m in enumerate(mates, start=1):
        tie_wins = (pos + d >= EXPERTS_PER_GROUP).astype(jnp.int32)
        rank = rank + jnp.where(m > sel, 1, jnp.where(m == sel, tie_wins, 0))
    vals = [sel] + mates
    top2 = None
    for i in range(EXPERTS_PER_GROUP):
        for j in range(i + 1, EXPERTS_PER_GROUP):
            pair = vals[i] + vals[j]
            top2 = pair if top2 is None else jnp.maximum(top2, pair)
    epos = lane & (N_EXPERTS - 1)
    beaten = jnp.zeros(sel.shape, jnp.int32)
    for d in range(1, N_GROUPS):
        other = _lane_mate(top2, d * EXPERTS_PER_GROUP, N_EXPERTS, epos)
        tie_wins = (epos + d * EXPERTS_PER_GROUP >= N_EXPERTS).astype(jnp.int32)
        beaten = beaten + jnp.where(other > top2, 1, jnp.where(other == top2, tie_wins, 0))
    rejected = beaten + jnp.where(rank < TOP_K, 0, 1) + jnp.where(lane < N_EXPERTS, 0, 1)
    w = jnp.where(rejected == 0, scores, 0.0)
    return w / jnp.sum(w, axis=-1, keepdims=True)


def _post_attn_kernel(yacd_ref, yb_ref, x_ref, wo_acd_ref, wo_b_ref, g1_ref, b1_ref, wq_ref,
                      km_ref, vm_ref, wo_ref, g2_ref, b2_ref, rw_hl_ref, rw_h_ref, rb_ref,
                      x2_ref, gates_ref, *, tm, alpha):
    h = _dot(yacd_ref[0], wo_acd_ref[...]) + _dot(yb_ref[0], wo_b_ref[...])
    x1 = _layer_norm(alpha * x_ref[0] + h, g1_ref[...], b1_ref[...])

    q = _dot(x1.astype(BF16), wq_ref[...])
    heads = []
    for hd in range(X_HEADS):
        cols = slice(hd * X_HEAD_DIM, (hd + 1) * X_HEAD_DIM)
        s = _dot_nt(q[:, cols].astype(BF16), km_ref[0, :, cols])
        e = jnp.exp(s - jnp.max(s, axis=-1, keepdims=True))
        p = e / jnp.sum(e, axis=-1, keepdims=True)
        heads.append(_dot(p.astype(BF16), vm_ref[0, :, cols]).astype(BF16))
    h2 = _dot(jnp.concatenate(heads, axis=1), wo_ref[...])
    x2 = _layer_norm(alpha * x1 + h2, g2_ref[...], b2_ref[...])
    x2_ref[0] = x2

    x_hi = x2.astype(BF16)
    x_lo = (x2 - x_hi.astype(F32)).astype(BF16)
    hl = _dot(x_hi, rw_hl_ref[...])
    logits = hl[:, :LANES] + hl[:, LANES:] + _dot(x_lo, rw_h_ref[...])
    lane = lax.broadcasted_iota(jnp.int32, (tm, LANES), 1)
    gates_ref[0] = _route(logits, rb_ref[...], lane)


def _post_attn(yacd, yb, x, wo_acd, wo_b, g1, b1, wq, km, vm, wo, g2, b2, rw_hl, rw_h, rb,
               *, tm, alpha):
    bsz, seq, d = x.shape
    tok = lambda w: pl.BlockSpec((1, tm, w), lambda b, s: (b, s, 0))
    memspec = pl.BlockSpec((1,) + km.shape[1:], lambda b, s: (b, 0, 0))
    return pl.pallas_call(
        functools.partial(_post_attn_kernel, tm=tm, alpha=alpha),
        grid=(bsz, seq // tm),
        in_specs=[tok(yacd.shape[-1]), tok(yb.shape[-1]), tok(d),
                  _const_spec(wo_acd.shape), _const_spec(wo_b.shape), _const_spec(g1.shape),
                  _const_spec(b1.shape), _const_spec(wq.shape), memspec, memspec,
                  _const_spec(wo.shape), _const_spec(g2.shape), _const_spec(b2.shape),
                  _const_spec(rw_hl.shape), _const_spec(rw_h.shape), _const_spec(rb.shape)],
        out_specs=[tok(d), tok(LANES)],
        out_shape=[jax.ShapeDtypeStruct((bsz, seq, d), F32),
                   jax.ShapeDtypeStruct((bsz, seq, LANES), F32)],
        compiler_params=_params(2),
        name="post_attn",
    )(yacd, yb, x, wo_acd, wo_b, g1, b1, wq, km, vm, wo, g2, b2, rw_hl, rw_h, rb)


def _moe_kernel(x_ref, gates_ref, wg_ref, wu_ref, wd_ref, g3_ref, b3_ref, o_ref, h_buf, *, alpha):
    x = x_ref[0]
    xb = x.astype(BF16)
    gates = gates_ref[0]
    for e in range(N_EXPERTS):
        cols = slice(e * D_FF, (e + 1) * D_FF)
        hg = _dot(xb, wg_ref[:, cols])
        hu = _dot(xb, wu_ref[:, cols])
        h_buf[:, cols] = (jax.nn.silu(hg) * hu * gates[:, e:e + 1]).astype(BF16)
    out = _dot(h_buf[...], wd_ref[...])
    o_ref[0] = _layer_norm(alpha * x + out, g3_ref[...], b3_ref[...])


def _moe(x, gates, wg, wu, wd, g3, b3, *, tm, alpha):
    bsz, seq, d = x.shape
    tok = lambda w: pl.BlockSpec((1, tm, w), lambda b, s: (b, s, 0))
    resident = lambda a: pl.BlockSpec(a.shape, lambda b, s: (0, 0), pipeline_mode=pl.Buffered(1))
    return pl.pallas_call(
        functools.partial(_moe_kernel, alpha=alpha),
        grid=(bsz, seq // tm),
        in_specs=[tok(d), tok(LANES), resident(wg), resident(wu), resident(wd),
                  _const_spec(g3.shape), _const_spec(b3.shape)],
        out_specs=tok(d),
        out_shape=jax.ShapeDtypeStruct((bsz, seq, d), F32),
        scratch_shapes=[pltpu.VMEM((tm, N_EXPERTS * D_FF), BF16)],
        compiler_params=_params(2),
        name="moe",
    )(x, gates, wg, wu, wd, g3, b3)


def _rot_cols(w):
    half = w.shape[-1] // 2
    return jnp.concatenate([-w[..., half:], w[..., :half]], axis=-1)


def _prep_w_in(w_in):
    u, v = w_in[..., 0:256], w_in[..., 256:512]
    cq, ckv, kr = w_in[..., 512:768], w_in[..., 768:896], w_in[..., 896:928]
    rest = w_in[..., 928:]
    rope4 = jnp.concatenate([kr, _rot_cols(kr), kr, _rot_cols(kr)], axis=-1)
    return jnp.concatenate([u, v, cq, ckv, rest, rope4], axis=-1).astype(BF16)


def _prep_w_uq(w_uq):
    depth = w_uq.shape[0]
    w = w_uq.reshape(depth, Q_LORA, MLA_HEADS, MLA_NOPE + MLA_ROPE)
    nope, rope = w[..., :MLA_NOPE], w[..., MLA_NOPE:]
    ext = jnp.concatenate([nope, rope, _rot_cols(rope)], axis=-1)
    return ext.reshape(depth, Q_LORA, MLA_HEADS * QK_DIM).astype(BF16)


def _prep_w_ukv(w_ukv):
    depth = w_ukv.shape[0]
    w = w_ukv.reshape(depth, KV_LORA, MLA_HEADS, MLA_NOPE + MLA_V)
    k_nope, v = w[..., :MLA_NOPE], w[..., MLA_NOPE:]
    wk = jnp.concatenate([k_nope, jnp.zeros_like(k_nope)], axis=-1)
    wv = jnp.concatenate([v, jnp.zeros_like(v)], axis=-1)
    return (wk.reshape(depth, KV_LORA, MLA_HEADS * QK_DIM).astype(BF16),
            wv.reshape(depth, KV_LORA, MLA_HEADS * LANES).astype(BF16))


def _prep_pool_w(pool_w):
    depth, groups = pool_w.shape[:2]
    eye = jnp.eye(groups, dtype=pool_w.dtype)
    bd = jnp.einsum('lgcd,gh->lgchd', pool_w, eye)
    return bd.reshape(depth, D_WIDTH, D_WIDTH).astype(BF16)


def _pad_lanes(a):
    return jnp.pad(a, [(0, 0)] * (a.ndim - 1) + [(0, LANES - a.shape[-1])])


def kernel(x, mem, positions, w_in, gmlp_v_norm_g, gmlp_w_s, gmlp_b_s, mla_q_norm_g, mla_w_uq,
           mla_kv_norm_g, mla_w_ukv, conv_w, pool_w, pool_scale, w_out, ln1_g, ln1_b,
           xattn_wq, xattn_wk, xattn_wv, xattn_wo, ln2_g, ln2_b, router_w, router_b,
           moe_w_gate, moe_w_up, moe_w_down, ln3_g, ln3_b):
    depth = w_in.shape[0]
    alpha = (2 * depth) ** 0.25
    tm = 512
    bsz, seq, _ = x.shape

    inv_freq = ROPE_BASE ** (-jnp.arange(0, MLA_ROPE, 2, dtype=F32) / MLA_ROPE)
    ang = positions.astype(F32)[..., None] * inv_freq
    cos2 = jnp.tile(jnp.cos(ang), (1, 1, 2))
    sin2 = jnp.tile(jnp.sin(ang), (1, 1, 2))
    scale = (MLA_NOPE + MLA_ROPE) ** -0.5 * LOG2_E
    csq = scale * jnp.concatenate([jnp.ones(cos2.shape[:2] + (MLA_NOPE,), F32), cos2, sin2], -1)
    csq = jnp.swapaxes(csq.reshape(bsz, seq // tm, tm, QK_DIM), 2, 3)
    csk = jnp.concatenate([cos2, sin2, cos2, sin2], axis=-1)

    row = lambda a: a[:, None, :]
    w_in_e = _prep_w_in(w_in)
    w_uq_e = jnp.swapaxes(_prep_w_uq(mla_w_uq), 1, 2)
    w_k_e, w_v_e = _prep_w_ukv(mla_w_ukv)
    w_v_e = jnp.swapaxes(w_v_e, 1, 2)
    ws = gmlp_w_s.astype(BF16)
    bs = jnp.repeat(jnp.swapaxes(gmlp_b_s, 1, 2), A_HEAD_DIM, axis=2)
    pool_bd = _prep_pool_w(pool_w)
    wo_acd = jnp.concatenate([w_out[:, :A_WIDTH], w_out[:, A_WIDTH + MLA_WIDTH:]], axis=1).astype(BF16)
    wo_b = w_out[:, A_WIDTH:A_WIDTH + MLA_WIDTH].astype(BF16)
    wq = (xattn_wq * (X_HEAD_DIM ** -0.5)).astype(BF16)
    wo = xattn_wo.astype(BF16)
    rw = _pad_lanes(router_w)
    rw_hi = rw.astype(BF16)
    rw_lo = (rw - rw_hi.astype(F32)).astype(BF16)
    rw_hl = jnp.concatenate([rw_hi, rw_lo], axis=1)
    rb = _pad_lanes(router_b[None, :])
    cat_experts = lambda w: jnp.swapaxes(w, 1, 2).reshape(depth, D_MODEL, N_EXPERTS * D_FF)
    wg = cat_experts(moe_w_gate).astype(BF16)
    wu = cat_experts(moe_w_up).astype(BF16)
    wd = moe_w_down.reshape(depth, N_EXPERTS * D_FF, D_MODEL).astype(BF16)

    km, vm = _mem_kv(mem, xattn_wk.astype(BF16), xattn_wv.astype(BF16))

    for l in range(depth):
        q, k, v, yacd = _mixer_in(
            x, csq, csk, w_in_e[l], row(gmlp_v_norm_g)[l], ws[l], bs[l], row(mla_q_norm_g)[l],
            w_uq_e[l], row(mla_kv_norm_g)[l], w_k_e[l], w_v_e[l], conv_w[l], pool_bd[l],
            row(pool_scale)[l], tm=tm)
        yb = _mla_attn(q, k, v, hg=MLA_HEADS)
        x2, gates = _post_attn(
            yacd, yb, x, wo_acd[l], wo_b[l], row(ln1_g)[l], row(ln1_b)[l], wq[l], km[l], vm[l],
            wo[l], row(ln2_g)[l], row(ln2_b)[l], rw_hl, rw_hi, rb, tm=tm, alpha=alpha)
        x = _moe(x2, gates, wg[l], wu[l], wd[l], row(ln3_g)[l], row(ln3_b)[l], tm=tm, alpha=alpha)
    return x
```

```python
import functools

import jax
import jax.numpy as jnp
from jax import lax
from jax.experimental import pallas as pl
from jax.experimental.pallas import tpu as pltpu

F32 = jnp.float32
BF16 = jnp.bfloat16

D_MODEL = 1024
A_HEADS, A_HEAD_DIM, CHUNK = 4, 64, 128
A_WIDTH = A_HEADS * A_HEAD_DIM
MLA_HEADS, MLA_NOPE, MLA_ROPE, MLA_V = 8, 64, 32, 64
Q_LORA, KV_LORA = 256, 128
MLA_WIDTH = MLA_HEADS * MLA_V
ROPE_BASE = 10000.0
C_WIDTH, CONV_W = 256, 3
D_WIDTH = 256
POOL_WINDOWS = (2, 4, 8, 16)
D_GROUP = D_WIDTH // len(POOL_WINDOWS)
X_HEADS = 4
X_HEAD_DIM = D_MODEL // X_HEADS
N_EXPERTS, N_GROUPS, TOP_K = 16, 4, 2
EXPERTS_PER_GROUP = N_EXPERTS // N_GROUPS
D_FF = 256
EPS = 1e-6

LANES = 128
SUBLANES = 8
VMEM_LIMIT = 56 * 1024 * 1024

OFF_U, OFF_V, OFF_CQ, OFF_CKV = 0, 256, 512, 768
OFF_BG, OFF_CG, OFF_H, OFF_D, OFF_ROPE = 896, 1152, 1408, 1664, 1920
P_EXT = 2048
QK_DIM = 128
POOL_HALO = 16
NEG_BIG = -1e30
LOG2_E = 1.4426950408889634


def _dot(a, b):
    return jnp.dot(a, b, preferred_element_type=F32)


def _dot_nt(a, b):
    return lax.dot_general(a, b, (((1,), (1,)), ((), ())), preferred_element_type=F32)


def _rms(x, g):
    return x * lax.rsqrt(jnp.mean(x * x, axis=-1, keepdims=True) + EPS) * g


def _layer_norm(x, g, b):
    mu = jnp.mean(x, axis=-1, keepdims=True)
    xc = x - mu
    var = jnp.mean(xc * xc, axis=-1, keepdims=True)
    return xc * lax.rsqrt(var + EPS) * g + b


def _params(n_grid):
    return pltpu.CompilerParams(dimension_semantics=("arbitrary",) * n_grid,
                                vmem_limit_bytes=VMEM_LIMIT)


def _const_spec(shape):
    zeros = (0,) * len(shape)
    return pl.BlockSpec(shape, lambda *_: zeros)


def _mem_kv_kernel(mem_ref, wk_ref, wv_ref, k_ref, v_ref):
    m = mem_ref[0].astype(BF16)
    k_ref[0, 0] = _dot(m, wk_ref[0]).astype(BF16)
    v_ref[0, 0] = _dot(m, wv_ref[0]).astype(BF16)


def _mem_kv(mem, wk, wv):
    depth = wk.shape[0]
    bsz, mlen, d = mem.shape
    out = jax.ShapeDtypeStruct((depth, bsz, mlen, d), BF16)
    return pl.pallas_call(
        _mem_kv_kernel,
        grid=(depth, bsz),
        in_specs=[pl.BlockSpec((1, mlen, d), lambda l, b: (b, 0, 0)),
                  pl.BlockSpec((1, d, d), lambda l, b: (l, 0, 0)),
                  pl.BlockSpec((1, d, d), lambda l, b: (l, 0, 0))],
        out_specs=[pl.BlockSpec((1, 1, mlen, d), lambda l, b: (l, b, 0, 0)),
                   pl.BlockSpec((1, 1, mlen, d), lambda l, b: (l, b, 0, 0))],
        out_shape=[out, out],
        compiler_params=_params(2),
        name="mem_kv",
    )(mem, wk, wv)


def _mixer_in_kernel(x_ref, csq_ref, csk_ref, w_in_ref, vng_ref, ws_ref, bs_ref, qng_ref,
                     wuq_ref, kvng_ref, wk_ref, wv_ref, convw_ref, poolw_ref, pscale_ref,
                     q_ref, k_ref, v_ref, y_ref,
                     conv_buf, pool_a, pool_b, pool_carry, *, tm):
    si = pl.program_id(1)
    xb = x_ref[0].astype(BF16)

    za = jax.nn.gelu(_dot(xb, w_in_ref[:, OFF_U:OFF_CQ]))
    u = za[:, :A_WIDTH]
    v = _rms(za[:, A_WIDTH:], vng_ref[...]).astype(BF16)
    row = lax.broadcasted_iota(jnp.int32, (CHUNK, CHUNK), 0)
    col = lax.broadcasted_iota(jnp.int32, (CHUNK, CHUNK), 1)
    w_causal = [jnp.where(row >= col, ws_ref[h], jnp.zeros((), BF16)) for h in range(A_HEADS)]
    lane_head = lax.broadcasted_iota(jnp.int32, (CHUNK, A_WIDTH), 1) // A_HEAD_DIM
    for c in range(tm // CHUNK):
        rows = slice(c * CHUNK, (c + 1) * CHUNK)
        vc = v[rows]
        mixed = _dot(w_causal[0], vc)
        for h in range(1, A_HEADS):
            mixed = jnp.where(lane_head == h, _dot(w_causal[h], vc), mixed)
        y_ref[0, rows, 0:A_WIDTH] = (u[rows] * (mixed + bs_ref[...])).astype(BF16)

    zb = _dot(xb, w_in_ref[:, OFF_CQ:OFF_BG])
    cq = _rms(zb[:, :Q_LORA], qng_ref[...]).astype(BF16)
    ckv = _rms(zb[:, Q_LORA:], kvng_ref[...]).astype(BF16)
    qt = _dot_nt(wuq_ref[...], cq)
    csq = csq_ref[0, 0]
    for h in range(MLA_HEADS):
        q_ref[0, h, 0] = (qt[h * QK_DIM:(h + 1) * QK_DIM, :] * csq).astype(BF16)
    r4 = _dot(xb, w_in_ref[:, OFF_ROPE:P_EXT]) * csk_ref[0]
    kr = r4 + pltpu.roll(r4, MLA_ROPE, 1)
    lane = lax.broadcasted_iota(jnp.int32, (tm, QK_DIM), 1)
    kr = jnp.where(lane >= MLA_NOPE, kr, 0.0)
    kk = _dot(ckv, wk_ref[...])
    for h in range(MLA_HEADS):
        k_ref[0, h] = (kk[:, h * QK_DIM:(h + 1) * QK_DIM] + kr).astype(BF16)
    ones_row = jnp.where(lax.broadcasted_iota(jnp.int32, (LANES, tm), 0) == MLA_V, 1.0, 0.0)
    vt = _dot_nt(wv_ref[...], ckv)
    for h in range(MLA_HEADS):
        v_ref[0, h, 0] = (vt[h * LANES:(h + 1) * LANES, :] + ones_row).astype(BF16)

    zc = _dot(xb, w_in_ref[:, OFF_BG:OFF_D])
    gh = zc[:, C_WIDTH:2 * C_WIDTH] * zc[:, 2 * C_WIDTH:]

    @pl.when(si == 0)
    def _():
        conv_buf[0:SUBLANES, :] = jnp.zeros((SUBLANES, C_WIDTH), F32)

    conv_buf[SUBLANES:SUBLANES + tm, :] = gh
    conv = (convw_ref[2:3, :] * gh
            + convw_ref[1:2, :] * conv_buf[SUBLANES - 1:SUBLANES - 1 + tm, :]
            + convw_ref[0:1, :] * conv_buf[SUBLANES - 2:SUBLANES - 2 + tm, :])
    y_ref[0, :, A_WIDTH:A_WIDTH + C_WIDTH] = (zc[:, :C_WIDTH] * conv).astype(BF16)
    conv_buf[0:SUBLANES, :] = conv_buf[tm:tm + SUBLANES, :]

    zd = _dot(xb, w_in_ref[:, OFF_D:OFF_ROPE])

    @pl.when(si == 0)
    def _():
        pool_carry[...] = jnp.zeros((POOL_HALO, D_WIDTH), F32)

    base = SUBLANES
    n = tm + POOL_HALO
    pool_a[0:base, :] = jnp.zeros((base, D_WIDTH), F32)
    pool_b[0:base, :] = jnp.zeros((base, D_WIDTH), F32)
    pool_a[base:base + POOL_HALO, :] = pool_carry[...]
    pool_a[base + POOL_HALO:base + n, :] = zd
    pool_carry[...] = zd[tm - POOL_HALO:, :]
    lane_d = lax.broadcasted_iota(jnp.int32, (n, D_WIDTH), 1)
    pool_b[base:base + n, :] = pool_a[base:base + n, :] + pool_a[base - 1:base - 1 + n, :]
    pool_a[base:base + n, :] = pool_b[base:base + n, :] + jnp.where(
        lane_d >= D_GROUP, pool_b[base - 2:base - 2 + n, :], 0.0)
    pool_b[base:base + n, :] = pool_a[base:base + n, :] + jnp.where(
        lane_d >= 2 * D_GROUP, pool_a[base - 4:base - 4 + n, :], 0.0)
    t0 = base + POOL_HALO
    lane_t = lax.broadcasted_iota(jnp.int32, (tm, D_WIDTH), 1)
    win_sum = pool_b[t0:t0 + tm, :] + jnp.where(
        lane_t >= 3 * D_GROUP, pool_b[t0 - 8:t0 - 8 + tm, :], 0.0)
    pos1 = si * tm + lax.broadcasted_iota(jnp.int32, (tm, D_WIDTH), 0) + 1
    window = jnp.left_shift(2, lane_t // D_GROUP)
    count = jnp.minimum(pos1, window).astype(F32)
    pooled = (win_sum / count - zd).astype(BF16)
    yd = _dot(pooled, poolw_ref[...]) * pscale_ref[...]
    y_ref[0, :, A_WIDTH + C_WIDTH:] = yd.astype(BF16)


def _mixer_in(x, csq, csk, w_in, vng, ws, bs, qng, wuq, kvng, wk, wv, convw, poolw, pscale, *, tm):
    bsz, seq, d = x.shape
    tok = lambda w: pl.BlockSpec((1, tm, w), lambda b, s: (b, s, 0))
    head = lambda n, w: pl.BlockSpec((1, n, tm, w), lambda b, s: (b, 0, s, 0))
    head_t = pl.BlockSpec((1, MLA_HEADS, 1, LANES, tm), lambda b, s: (b, 0, s, 0, 0))
    shape_t = jax.ShapeDtypeStruct((bsz, MLA_HEADS, seq // tm, LANES, tm), BF16)
    return pl.pallas_call(
        functools.partial(_mixer_in_kernel, tm=tm),
        grid=(bsz, seq // tm),
        in_specs=[tok(d), pl.BlockSpec((1, 1, QK_DIM, tm), lambda b, s: (b, s, 0, 0)), tok(QK_DIM),
                  _const_spec(w_in.shape), _const_spec(vng.shape), _const_spec(ws.shape),
                  _const_spec(bs.shape), _const_spec(qng.shape), _const_spec(wuq.shape),
                  _const_spec(kvng.shape), _const_spec(wk.shape), _const_spec(wv.shape),
                  _const_spec(convw.shape), _const_spec(poolw.shape), _const_spec(pscale.shape)],
        out_specs=[head_t, head(MLA_HEADS, QK_DIM), head_t, tok(A_WIDTH + C_WIDTH + D_WIDTH)],
        out_shape=[shape_t,
                   jax.ShapeDtypeStruct((bsz, MLA_HEADS, seq, QK_DIM), BF16),
                   shape_t,
                   jax.ShapeDtypeStruct((bsz, seq, A_WIDTH + C_WIDTH + D_WIDTH), BF16)],
        scratch_shapes=[pltpu.VMEM((tm + SUBLANES, C_WIDTH), F32),
                        pltpu.VMEM((tm + POOL_HALO + SUBLANES, D_WIDTH), F32),
                        pltpu.VMEM((tm + POOL_HALO + SUBLANES, D_WIDTH), F32),
                        pltpu.VMEM((POOL_HALO, D_WIDTH), F32)],
        compiler_params=_params(2),
        name="mixer_in",
    )(x, csq, csk, w_in, vng, ws, bs, qng, wuq, kvng, wk, wv, convw, poolw, pscale)


def _mla_attn_kernel(qt_ref, k_ref, vt_ref, o_ref, m_ref, acc_ref, *, tq, hg):
    qi = pl.program_id(2)
    key = lax.broadcasted_iota(jnp.int32, (tq, tq), 0)
    qry = lax.broadcasted_iota(jnp.int32, (tq, tq), 1)
    m_ref[...] = jnp.full(m_ref.shape, NEG_BIG, F32)
    acc_ref[...] = jnp.zeros(acc_ref.shape, F32)

    def step(j, masked):
        def scores(hh):
            return _dot(k_ref[0, hh, pl.ds(pl.multiple_of(j * tq, tq), tq), :], qt_ref[0, hh, 0])

        def weighted_values(hh, pt, rescale):
            pv = _dot(vt_ref[0, hh, j], pt)
            acc_ref[hh] = rescale * acc_ref[hh] + pv

        st_next = scores(0)
        pending = None
        for hh in range(hg):
            st = st_next
            if hh + 1 < hg:
                st_next = scores(hh + 1)
            if pending is not None:
                weighted_values(*pending)
            if masked:
                st = jnp.where(key <= qry, st, NEG_BIG)
            m_old = m_ref[hh]
            m_new = jnp.maximum(m_old, jnp.max(st, axis=0, keepdims=True))
            m_ref[hh] = m_new
            pending = (hh, jnp.exp2(st - m_new).astype(BF16), jnp.exp2(m_old - m_new))
        weighted_values(*pending)

    def body(j, carry):
        step(j, False)
        return carry

    lax.fori_loop(0, qi, body, 0)
    step(qi, True)
    for pr in range(hg // 2):
        halves = []
        for hh in (2 * pr, 2 * pr + 1):
            acc = acc_ref[hh]
            halves.append(acc[:MLA_V] / acc[MLA_V:MLA_V + 1])
        o_ref[0, :, pr * LANES:(pr + 1) * LANES] = jnp.concatenate(halves, axis=0).T.astype(BF16)


def _mla_attn(qt, k, vt, *, hg):
    bsz, heads, n_tiles, _, tq = qt.shape
    seq = n_tiles * tq
    return pl.pallas_call(
        functools.partial(_mla_attn_kernel, tq=tq, hg=hg),
        grid=(bsz, heads // hg, n_tiles),
        in_specs=[pl.BlockSpec((1, hg, 1, QK_DIM, tq), lambda b, g, i: (b, g, i, 0, 0)),
                  pl.BlockSpec((1, hg, seq, QK_DIM), lambda b, g, i: (b, g, 0, 0)),
                  pl.BlockSpec((1, hg, n_tiles, LANES, tq), lambda b, g, i: (b, g, 0, 0, 0))],
        out_specs=pl.BlockSpec((1, tq, hg * MLA_V), lambda b, g, i: (b, i, g)),
        out_shape=jax.ShapeDtypeStruct((bsz, seq, MLA_WIDTH), BF16),
        scratch_shapes=[pltpu.VMEM((hg, 1, tq), F32), pltpu.VMEM((hg, LANES, tq), F32)],
        compiler_params=_params(3),
        name="mla_attn",
    )(qt, k, vt)


def _lane_mate(x, d, period, pos):
    ahead = pltpu.roll(x, LANES - d, 1)
    behind = pltpu.roll(x, period - d, 1)
    return jnp.where(pos + d < period, ahead, behind)


def _route(logits, router_b, lane):
    scores = jax.nn.sigmoid(logits)
    sel = scores + router_b
    pos = lane & (EXPERTS_PER_GROUP - 1)
    mates = [_lane_mate(sel, d, EXPERTS_PER_GROUP, pos) for d in range(1, EXPERTS_PER_GROUP)]
    rank = jnp.zeros(sel.shape, jnp.int32)
    for d, m in enumerate(mates, start=1):
        tie_wins = (pos + d >= EXPERTS_PER_GROUP).astype(jnp.int32)
        rank = rank + jnp.where(m > sel, 1, jnp.where(m == sel, tie_wins, 0))
    vals = [sel] + mates
    top2 = None
    for i in range(EXPERTS_PER_GROUP):
        for j in range(i + 1, EXPERTS_PER_GROUP):
            pair = vals[i] + vals[j]
            top2 = pair if top2 is None else jnp.maximum(top2, pair)
    epos = lane & (N_EXPERTS - 1)
    beaten = jnp.zeros(sel.shape, jnp.int32)
    for d in range(1, N_GROUPS):
        other = _lane_mate(top2, d * EXPERTS_PER_GROUP, N_EXPERTS, epos)
        tie_wins = (epos + d * EXPERTS_PER_GROUP >= N_EXPERTS).astype(jnp.int32)
        beaten = beaten + jnp.where(other > top2, 1, jnp.where(other == top2, tie_wins, 0))
    rejected = beaten + jnp.where(rank < TOP_K, 0, 1) + jnp.where(lane < N_EXPERTS, 0, 1)
    w = jnp.where(rejected == 0, scores, 0.0)
    return w / jnp.sum(w, axis=-1, keepdims=True)


def _post_attn_kernel(yacd_ref, yb_ref, x_ref, wo_acd_ref, wo_b_ref, g1_ref, b1_ref, wq_ref,
                      km_ref, vm_ref, wo_ref, g2_ref, b2_ref, rw_hl_ref, rw_h_ref, rb_ref,
                      x2_ref, gates_ref, *, tm, alpha):
    h = _dot(yacd_ref[0], wo_acd_ref[...]) + _dot(yb_ref[0], wo_b_ref[...])
    x1 = _layer_norm(alpha * x_ref[0] + h, g1_ref[...], b1_ref[...])

    q = _dot(x1.astype(BF16), wq_ref[...])
    heads = []
    for hd in range(X_HEADS):
        cols = slice(hd * X_HEAD_DIM, (hd + 1) * X_HEAD_DIM)
        s = _dot_nt(q[:, cols].astype(BF16), km_ref[0, :, cols])
        e = jnp.exp(s - jnp.max(s, axis=-1, keepdims=True))
        p = e / jnp.sum(e, axis=-1, keepdims=True)
        heads.append(_dot(p.astype(BF16), vm_ref[0, :, cols]).astype(BF16))
    h2 = _dot(jnp.concatenate(heads, axis=1), wo_ref[...])
    x2 = _layer_norm(alpha * x1 + h2, g2_ref[...], b2_ref[...])
    x2_ref[0] = x2

    x_hi = x2.astype(BF16)
    x_lo = (x2 - x_hi.astype(F32)).astype(BF16)
    hl = _dot(x_hi, rw_hl_ref[...])
    logits = hl[:, :LANES] + hl[:, LANES:] + _dot(x_lo, rw_h_ref[...])
    lane = lax.broadcasted_iota(jnp.int32, (tm, LANES), 1)
    gates_ref[0] = _route(logits, rb_ref[...], lane)


def _post_attn(yacd, yb, x, wo_acd, wo_b, g1, b1, wq, km, vm, wo, g2, b2, rw_hl, rw_h, rb,
               *, tm, alpha):
    bsz, seq, d = x.shape
    tok = lambda w: pl.BlockSpec((1, tm, w), lambda b, s: (b, s, 0))
    memspec = pl.BlockSpec((1,) + km.shape[1:], lambda b, s: (b, 0, 0))
    return pl.pallas_call(
        functools.partial(_post_attn_kernel, tm=tm, alpha=alpha),
        grid=(bsz, seq // tm),
        in_specs=[tok(yacd.shape[-1]), tok(yb.shape[-1]), tok(d),
                  _const_spec(wo_acd.shape), _const_spec(wo_b.shape), _const_spec(g1.shape),
                  _const_spec(b1.shape), _const_spec(wq.shape), memspec, memspec,
                  _const_spec(wo.shape), _const_spec(g2.shape), _const_spec(b2.shape),
                  _const_spec(rw_hl.shape), _const_spec(rw_h.shape), _const_spec(rb.shape)],
        out_specs=[tok(d), tok(LANES)],
        out_shape=[jax.ShapeDtypeStruct((bsz, seq, d), F32),
                   jax.ShapeDtypeStruct((bsz, seq, LANES), F32)],
        compiler_params=_params(2),
        name="post_attn",
    )(yacd, yb, x, wo_acd, wo_b, g1, b1, wq, km, vm, wo, g2, b2, rw_hl, rw_h, rb)


def _moe_kernel(x_ref, gates_ref, wg_ref, wu_ref, wd_ref, g3_ref, b3_ref, o_ref, h_buf, *, alpha):
    x = x_ref[0]
    xb = x.astype(BF16)
    gates = gates_ref[0]
    for e in range(N_EXPERTS):
        cols = slice(e * D_FF, (e + 1) * D_FF)
        hg = _dot(xb, wg_ref[:, cols])
        hu = _dot(xb, wu_ref[:, cols])
        h_buf[:, cols] = (jax.nn.silu(hg) * hu * gates[:, e:e + 1]).astype(BF16)
    out = _dot(h_buf[...], wd_ref[...])
    o_ref[0] = _layer_norm(alpha * x + out, g3_ref[...], b3_ref[...])


def _moe(x, gates, wg, wu, wd, g3, b3, *, tm, alpha):
    bsz, seq, d = x.shape
    tok = lambda w: pl.BlockSpec((1, tm, w), lambda b, s: (b, s, 0))
    resident = lambda a: pl.BlockSpec(a.shape, lambda b, s: (0, 0), pipeline_mode=pl.Buffered(1))
    return pl.pallas_call(
        functools.partial(_moe_kernel, alpha=alpha),
        grid=(bsz, seq // tm),
        in_specs=[tok(d), tok(LANES), resident(wg), resident(wu), resident(wd),
                  _const_spec(g3.shape), _const_spec(b3.shape)],
        out_specs=tok(d),
        out_shape=jax.ShapeDtypeStruct((bsz, seq, d), F32),
        scratch_shapes=[pltpu.VMEM((tm, N_EXPERTS * D_FF), BF16)],
        compiler_params=_params(2),
        name="moe",
    )(x, gates, wg, wu, wd, g3, b3)


def _rot_cols(w):
    half = w.shape[-1] // 2
    return jnp.concatenate([-w[..., half:], w[..., :half]], axis=-1)


def _prep_w_in(w_in):
    u, v = w_in[..., 0:256], w_in[..., 256:512]
    cq, ckv, kr = w_in[..., 512:768], w_in[..., 768:896], w_in[..., 896:928]
    rest = w_in[..., 928:]
    rope4 = jnp.concatenate([kr, _rot_cols(kr), kr, _rot_cols(kr)], axis=-1)
    return jnp.concatenate([u, v, cq, ckv, rest, rope4], axis=-1).astype(BF16)


def _prep_w_uq(w_uq):
    depth = w_uq.shape[0]
    w = w_uq.reshape(depth, Q_LORA, MLA_HEADS, MLA_NOPE + MLA_ROPE)
    nope, rope = w[..., :MLA_NOPE], w[..., MLA_NOPE:]
    ext = jnp.concatenate([nope, rope, _rot_cols(rope)], axis=-1)
    return ext.reshape(depth, Q_LORA, MLA_HEADS * QK_DIM).astype(BF16)


def _prep_w_ukv(w_ukv):
    depth = w_ukv.shape[0]
    w = w_ukv.reshape(depth, KV_LORA, MLA_HEADS, MLA_NOPE + MLA_V)
    k_nope, v = w[..., :MLA_NOPE], w[..., MLA_NOPE:]
    wk = jnp.concatenate([k_nope, jnp.zeros_like(k_nope)], axis=-1)
    wv = jnp.concatenate([v, jnp.zeros_like(v)], axis=-1)
    return (wk.reshape(depth, KV_LORA, MLA_HEADS * QK_DIM).astype(BF16),
            wv.reshape(depth, KV_LORA, MLA_HEADS * LANES).astype(BF16))


def _prep_pool_w(pool_w):
    depth, groups = pool_w.shape[:2]
    eye = jnp.eye(groups, dtype=pool_w.dtype)
    bd = jnp.einsum('lgcd,gh->lgchd', pool_w, eye)
    return bd.reshape(depth, D_WIDTH, D_WIDTH).astype(BF16)


def _pad_lanes(a):
    return jnp.pad(a, [(0, 0)] * (a.ndim - 1) + [(0, LANES - a.shape[-1])])


def kernel(x, mem, positions, w_in, gmlp_v_norm_g, gmlp_w_s, gmlp_b_s, mla_q_norm_g, mla_w_uq,
           mla_kv_norm_g, mla_w_ukv, conv_w, pool_w, pool_scale, w_out, ln1_g, ln1_b,
           xattn_wq, xattn_wk, xattn_wv, xattn_wo, ln2_g, ln2_b, router_w, router_b,
           moe_w_gate, moe_w_up, moe_w_down, ln3_g, ln3_b):
    depth = w_in.shape[0]
    alpha = (2 * depth) ** 0.25
    tm = 512
    bsz, seq, _ = x.shape

    inv_freq = ROPE_BASE ** (-jnp.arange(0, MLA_ROPE, 2, dtype=F32) / MLA_ROPE)
    ang = positions.astype(F32)[..., None] * inv_freq
    cos2 = jnp.tile(jnp.cos(ang), (1, 1, 2))
    sin2 = jnp.tile(jnp.sin(ang), (1, 1, 2))
    scale = (MLA_NOPE + MLA_ROPE) ** -0.5 * LOG2_E
    csq = scale * jnp.concatenate([jnp.ones(cos2.shape[:2] + (MLA_NOPE,), F32), cos2, sin2], -1)
    csq = jnp.swapaxes(csq.reshape(bsz, seq // tm, tm, QK_DIM), 2, 3)
    csk = jnp.concatenate([cos2, sin2, cos2, sin2], axis=-1)

    row = lambda a: a[:, None, :]
    w_in_e = _prep_w_in(w_in)
    w_uq_e = jnp.swapaxes(_prep_w_uq(mla_w_uq), 1, 2)
    w_k_e, w_v_e = _prep_w_ukv(mla_w_ukv)
    w_v_e = jnp.swapaxes(w_v_e, 1, 2)
    ws = gmlp_w_s.astype(BF16)
    bs = jnp.repeat(jnp.swapaxes(gmlp_b_s, 1, 2), A_HEAD_DIM, axis=2)
    pool_bd = _prep_pool_w(pool_w)
    wo_acd = jnp.concatenate([w_out[:, :A_WIDTH], w_out[:, A_WIDTH + MLA_WIDTH:]], axis=1).astype(BF16)
    wo_b = w_out[:, A_WIDTH:A_WIDTH + MLA_WIDTH].astype(BF16)
    wq = (xattn_wq * (X_HEAD_DIM ** -0.5)).astype(BF16)
    wo = xattn_wo.astype(BF16)
    rw = _pad_lanes(router_w)
    rw_hi = rw.astype(BF16)
    rw_lo = (rw - rw_hi.astype(F32)).astype(BF16)
    rw_hl = jnp.concatenate([rw_hi, rw_lo], axis=1)
    rb = _pad_lanes(router_b[None, :])
    cat_experts = lambda w: jnp.swapaxes(w, 1, 2).reshape(depth, D_MODEL, N_EXPERTS * D_FF)
    wg = cat_experts(moe_w_gate).astype(BF16)
    wu = cat_experts(moe_w_up).astype(BF16)
    wd = moe_w_down.reshape(depth, N_EXPERTS * D_FF, D_MODEL).astype(BF16)

    km, vm = _mem_kv(mem, xattn_wk.astype(BF16), xattn_wv.astype(BF16))

    for l in range(depth):
        q, k, v, yacd = _mixer_in(
            x, csq, csk, w_in_e[l], row(gmlp_v_norm_g)[l], ws[l], bs[l], row(mla_q_norm_g)[l],
            w_uq_e[l], row(mla_kv_norm_g)[l], w_k_e[l], w_v_e[l], conv_w[l], pool_bd[l],
            row(pool_scale)[l], tm=tm)
        yb = _mla_attn(q, k, v, hg=MLA_HEADS)
        x2, gates = _post_attn(
            yacd, yb, x, wo_acd[l], wo_b[l], row(ln1_g)[l], row(ln1_b)[l], wq[l], km[l], vm[l],
            wo[l], row(ln2_g)[l], row(ln2_b)[l], rw_hl, rw_hi, rb, tm=tm, alpha=alpha)
        x = _moe(x2, gates, wg[l], wu[l], wd[l], row(ln3_g)[l], row(ln3_b)[l], tm=tm, alpha=alpha)
    return x
```

```python
import functools

import jax
import jax.numpy as jnp
from jax import lax
from jax.experimental import pallas as pl
from jax.experimental.pallas import tpu as pltpu

F32 = jnp.float32
BF16 = jnp.bfloat16

D_MODEL = 1024
A_HEADS, A_HEAD_DIM, CHUNK = 4, 64, 128
A_WIDTH = A_HEADS * A_HEAD_DIM
MLA_HEADS, MLA_NOPE, MLA_ROPE, MLA_V = 8, 64, 32, 64
Q_LORA, KV_LORA = 256, 128
MLA_WIDTH = MLA_HEADS * MLA_V
ROPE_BASE = 10000.0
C_WIDTH, CONV_W = 256, 3
D_WIDTH = 256
POOL_WINDOWS = (2, 4, 8, 16)
D_GROUP = D_WIDTH // len(POOL_WINDOWS)
X_HEADS = 4
X_HEAD_DIM = D_MODEL // X_HEADS
N_EXPERTS, N_GROUPS, TOP_K = 16, 4, 2
EXPERTS_PER_GROUP = N_EXPERTS // N_GROUPS
D_FF = 256
EPS = 1e-6

LANES = 128
SUBLANES = 8
VMEM_LIMIT = 56 * 1024 * 1024

OFF_U, OFF_CQ, OFF_BG, P_EXT = 0, 512, 1024, 2048
QK_DIM = 128
POOL_HALO = 16
NEG_BIG = -1e30
LOG2_E = 1.4426950408889634


def _dot(a, b):
    return jnp.dot(a, b, preferred_element_type=F32)


def _dot_nt(a, b):
    return lax.dot_general(a, b, (((1,), (1,)), ((), ())), preferred_element_type=F32)


def _rms(x, g):
    return x * lax.rsqrt(jnp.mean(x * x, axis=-1, keepdims=True) + EPS) * g


def _layer_norm(x, g, b):
    mu = jnp.mean(x, axis=-1, keepdims=True)
    xc = x - mu
    var = jnp.mean(xc * xc, axis=-1, keepdims=True)
    return xc * lax.rsqrt(var + EPS) * g + b


def _params(n_grid):
    return pltpu.CompilerParams(dimension_semantics=("arbitrary",) * n_grid,
                                vmem_limit_bytes=VMEM_LIMIT)


def _run_interleaved(chains):
    live = list(chains)
    while live:
        for c in list(live):
            try:
                next(c)
            except StopIteration:
                live.remove(c)


def _const_spec(shape):
    zeros = (0,) * len(shape)
    return pl.BlockSpec(shape, lambda *_: zeros)


def _mem_kv_kernel(mem_ref, wk_ref, wv_ref, k_ref, v_ref):
    m = mem_ref[0].astype(BF16)
    k_ref[0, 0] = _dot(m, wk_ref[0]).astype(BF16)
    v_ref[0, 0] = _dot(m, wv_ref[0]).astype(BF16)


def _mem_kv(mem, wk, wv):
    depth = wk.shape[0]
    bsz, mlen, d = mem.shape
    out = jax.ShapeDtypeStruct((depth, bsz, mlen, d), BF16)
    return pl.pallas_call(
        _mem_kv_kernel,
        grid=(depth, bsz),
        in_specs=[pl.BlockSpec((1, mlen, d), lambda l, b: (b, 0, 0)),
                  pl.BlockSpec((1, d, d), lambda l, b: (l, 0, 0)),
                  pl.BlockSpec((1, d, d), lambda l, b: (l, 0, 0))],
        out_specs=[pl.BlockSpec((1, 1, mlen, d), lambda l, b: (l, b, 0, 0)),
                   pl.BlockSpec((1, 1, mlen, d), lambda l, b: (l, b, 0, 0))],
        out_shape=[out, out],
        compiler_params=_params(2),
        name="mem_kv",
    )(mem, wk, wv)


def _mixer_in_kernel(x_ref, csq_ref, csk_ref, w_in_ref, vng_ref, ws_ref, bs_ref, qng_ref,
                     wuq_ref, kvng_ref, wk_ref, wv_ref, convw_ref, poolw_ref, pscale_ref,
                     q_ref, k_ref, v_ref, y_ref,
                     conv_buf, pool_a, pool_b, pool_carry, *, tm):
    si = pl.program_id(1)

    @pl.when(si == 0)
    def _():
        pool_carry[...] = jnp.zeros((POOL_HALO, D_WIDTH), F32)
        conv_buf[0:SUBLANES, :] = jnp.zeros((SUBLANES, C_WIDTH), F32)

    xb = x_ref[0].astype(BF16)

    za = _dot(xb, w_in_ref[:, OFF_U:OFF_CQ])
    zb = _dot(xb, w_in_ref[:, OFF_CQ:OFF_BG])
    zcd = _dot(xb, w_in_ref[:, OFF_BG:P_EXT])

    za = jax.nn.gelu(za)
    u = za[:, :A_WIDTH]
    v = _rms(za[:, A_WIDTH:], vng_ref[...]).astype(BF16)
    cq = _rms(zb[:, :Q_LORA], qng_ref[...]).astype(BF16)
    ckv = _rms(zb[:, Q_LORA:Q_LORA + KV_LORA], kvng_ref[...]).astype(BF16)

    row = lax.broadcasted_iota(jnp.int32, (CHUNK, CHUNK), 0)
    col = lax.broadcasted_iota(jnp.int32, (CHUNK, CHUNK), 1)
    w_causal = [jnp.where(row >= col, ws_ref[h], jnp.zeros((), BF16)) for h in range(A_HEADS)]
    lane_head = lax.broadcasted_iota(jnp.int32, (CHUNK, A_WIDTH), 1) // A_HEAD_DIM
    for c in range(tm // CHUNK):
        rows = slice(c * CHUNK, (c + 1) * CHUNK)
        vc = v[rows]
        mixed = _dot(w_causal[0], vc)
        for h in range(1, A_HEADS):
            mixed = jnp.where(lane_head == h, _dot(w_causal[h], vc), mixed)
        y_ref[0, rows, 0:A_WIDTH] = (u[rows] * (mixed + bs_ref[...])).astype(BF16)

    qt = _dot_nt(wuq_ref[...], cq)
    kk = _dot(ckv, wk_ref[...])
    vt = _dot_nt(wv_ref[...], ckv)

    zd = zcd[:, 3 * C_WIDTH:]
    base = SUBLANES
    n = tm + POOL_HALO
    pool_a[0:base, :] = jnp.zeros((base, D_WIDTH), F32)
    pool_b[0:base, :] = jnp.zeros((base, D_WIDTH), F32)
    pool_a[base:base + POOL_HALO, :] = pool_carry[...]
    pool_a[base + POOL_HALO:base + n, :] = zd
    pool_carry[...] = zd[tm - POOL_HALO:, :]
    lane_d = lax.broadcasted_iota(jnp.int32, (n, D_WIDTH), 1)
    pool_b[base:base + n, :] = pool_a[base:base + n, :] + pool_a[base - 1:base - 1 + n, :]
    pool_a[base:base + n, :] = pool_b[base:base + n, :] + jnp.where(
        lane_d >= D_GROUP, pool_b[base - 2:base - 2 + n, :], 0.0)
    pool_b[base:base + n, :] = pool_a[base:base + n, :] + jnp.where(
        lane_d >= 2 * D_GROUP, pool_a[base - 4:base - 4 + n, :], 0.0)
    t0 = base + POOL_HALO
    lane_t = lax.broadcasted_iota(jnp.int32, (tm, D_WIDTH), 1)
    win_sum = pool_b[t0:t0 + tm, :] + jnp.where(
        lane_t >= 3 * D_GROUP, pool_b[t0 - 8:t0 - 8 + tm, :], 0.0)
    pos1 = si * tm + lax.broadcasted_iota(jnp.int32, (tm, D_WIDTH), 0) + 1
    window = jnp.left_shift(2, lane_t // D_GROUP)
    count = jnp.minimum(pos1, window).astype(F32)
    pooled = (win_sum / count - zd).astype(BF16)
    yd = _dot(pooled, poolw_ref[...]) * pscale_ref[...]
    y_ref[0, :, A_WIDTH + C_WIDTH:] = yd.astype(BF16)

    gh = zcd[:, C_WIDTH:2 * C_WIDTH] * zcd[:, 2 * C_WIDTH:3 * C_WIDTH]
    conv_buf[SUBLANES:SUBLANES + tm, :] = gh
    conv = (convw_ref[2:3, :] * gh
            + convw_ref[1:2, :] * conv_buf[SUBLANES - 1:SUBLANES - 1 + tm, :]
            + convw_ref[0:1, :] * conv_buf[SUBLANES - 2:SUBLANES - 2 + tm, :])
    y_ref[0, :, A_WIDTH:A_WIDTH + C_WIDTH] = (zcd[:, :C_WIDTH] * conv).astype(BF16)
    conv_buf[0:SUBLANES, :] = conv_buf[tm:tm + SUBLANES, :]

    csq = csq_ref[0, 0]
    for h in range(MLA_HEADS):
        q_ref[0, h, 0] = (qt[h * QK_DIM:(h + 1) * QK_DIM, :] * csq).astype(BF16)
    r4 = zb[:, Q_LORA + KV_LORA:] * csk_ref[0]
    kr = r4 + pltpu.roll(r4, MLA_ROPE, 1)
    lane = lax.broadcasted_iota(jnp.int32, (tm, QK_DIM), 1)
    kr = jnp.where(lane >= MLA_NOPE, kr, 0.0)
    for h in range(MLA_HEADS):
        k_ref[0, h] = (kk[:, h * QK_DIM:(h + 1) * QK_DIM] + kr).astype(BF16)
    ones_row = jnp.where(lax.broadcasted_iota(jnp.int32, (LANES, tm), 0) == MLA_V, 1.0, 0.0)
    for h in range(MLA_HEADS):
        v_ref[0, h, 0] = (vt[h * LANES:(h + 1) * LANES, :] + ones_row).astype(BF16)


def _mixer_in(x, csq, csk, w_in, vng, ws, bs, qng, wuq, kvng, wk, wv, convw, poolw, pscale, *, tm):
    bsz, seq, d = x.shape
    tok = lambda w: pl.BlockSpec((1, tm, w), lambda b, s: (b, s, 0))
    head = lambda n, w: pl.BlockSpec((1, n, tm, w), lambda b, s: (b, 0, s, 0))
    head_t = pl.BlockSpec((1, MLA_HEADS, 1, LANES, tm), lambda b, s: (b, 0, s, 0, 0))
    shape_t = jax.ShapeDtypeStruct((bsz, MLA_HEADS, seq // tm, LANES, tm), BF16)
    return pl.pallas_call(
        functools.partial(_mixer_in_kernel, tm=tm),
        grid=(bsz, seq // tm),
        in_specs=[tok(d), pl.BlockSpec((1, 1, QK_DIM, tm), lambda b, s: (b, s, 0, 0)), tok(QK_DIM),
                  _const_spec(w_in.shape), _const_spec(vng.shape), _const_spec(ws.shape),
                  _const_spec(bs.shape), _const_spec(qng.shape), _const_spec(wuq.shape),
                  _const_spec(kvng.shape), _const_spec(wk.shape), _const_spec(wv.shape),
                  _const_spec(convw.shape), _const_spec(poolw.shape), _const_spec(pscale.shape)],
        out_specs=[head_t, head(MLA_HEADS, QK_DIM), head_t, tok(A_WIDTH + C_WIDTH + D_WIDTH)],
        out_shape=[shape_t,
                   jax.ShapeDtypeStruct((bsz, MLA_HEADS, seq, QK_DIM), BF16),
                   shape_t,
                   jax.ShapeDtypeStruct((bsz, seq, A_WIDTH + C_WIDTH + D_WIDTH), BF16)],
        scratch_shapes=[pltpu.VMEM((tm + SUBLANES, C_WIDTH), F32),
                        pltpu.VMEM((tm + POOL_HALO + SUBLANES, D_WIDTH), F32),
                        pltpu.VMEM((tm + POOL_HALO + SUBLANES, D_WIDTH), F32),
                        pltpu.VMEM((POOL_HALO, D_WIDTH), F32)],
        compiler_params=_params(2),
        name="mixer_in",
    )(x, csq, csk, w_in, vng, ws, bs, qng, wuq, kvng, wk, wv, convw, poolw, pscale)


def _mla_attn_kernel(qt_ref, k_ref, vt_ref, o_ref, m_ref, acc_ref, *, tq, hg):
    qi = pl.program_id(2)
    key = lax.broadcasted_iota(jnp.int32, (tq, tq), 0)
    qry = lax.broadcasted_iota(jnp.int32, (tq, tq), 1)
    m_ref[...] = jnp.full(m_ref.shape, NEG_BIG, F32)
    acc_ref[...] = jnp.zeros(acc_ref.shape, F32)

    def step(j, masked):
        def scores(hh):
            return _dot(k_ref[0, hh, pl.ds(pl.multiple_of(j * tq, tq), tq), :], qt_ref[0, hh, 0])

        def weighted_values(hh, pt, rescale):
            pv = _dot(vt_ref[0, hh, j], pt)
            acc_ref[hh] = rescale * acc_ref[hh] + pv

        st_next = scores(0)
        pending = None
        for hh in range(hg):
            st = st_next
            if hh + 1 < hg:
                st_next = scores(hh + 1)
            if pending is not None:
                weighted_values(*pending)
            if masked:
                st = jnp.where(key <= qry, st, NEG_BIG)
            m_old = m_ref[hh]
            m_new = jnp.maximum(m_old, jnp.max(st, axis=0, keepdims=True))
            m_ref[hh] = m_new
            pending = (hh, jnp.exp2(st - m_new).astype(BF16), jnp.exp2(m_old - m_new))
        weighted_values(*pending)

    def body(j, carry):
        step(j, False)
        return carry

    lax.fori_loop(0, qi, body, 0)
    step(qi, True)
    for pr in range(hg // 2):
        halves = []
        for hh in (2 * pr, 2 * pr + 1):
            acc = acc_ref[hh]
            halves.append(acc[:MLA_V] / acc[MLA_V:MLA_V + 1])
        o_ref[0, :, pr * LANES:(pr + 1) * LANES] = jnp.concatenate(halves, axis=0).T.astype(BF16)


def _mla_attn(qt, k, vt, *, hg):
    bsz, heads, n_tiles, _, tq = qt.shape
    seq = n_tiles * tq
    return pl.pallas_call(
        functools.partial(_mla_attn_kernel, tq=tq, hg=hg),
        grid=(bsz, heads // hg, n_tiles),
        in_specs=[pl.BlockSpec((1, hg, 1, QK_DIM, tq), lambda b, g, i: (b, g, i, 0, 0)),
                  pl.BlockSpec((1, hg, seq, QK_DIM), lambda b, g, i: (b, g, 0, 0)),
                  pl.BlockSpec((1, hg, n_tiles, LANES, tq), lambda b, g, i: (b, g, 0, 0, 0))],
        out_specs=pl.BlockSpec((1, tq, hg * MLA_V), lambda b, g, i: (b, i, g)),
        out_shape=jax.ShapeDtypeStruct((bsz, seq, MLA_WIDTH), BF16),
        scratch_shapes=[pltpu.VMEM((hg, 1, tq), F32), pltpu.VMEM((hg, LANES, tq), F32)],
        compiler_params=_params(3),
        name="mla_attn",
    )(qt, k, vt)


def _lane_mate(x, d, period, pos):
    ahead = pltpu.roll(x, LANES - d, 1)
    behind = pltpu.roll(x, period - d, 1)
    return jnp.where(pos + d < period, ahead, behind)


def _route(logits, router_b, lane):
    scores = jax.nn.sigmoid(logits)
    sel = scores + router_b
    pos = lane & (EXPERTS_PER_GROUP - 1)
    mates = [_lane_mate(sel, d, EXPERTS_PER_GROUP, pos) for d in range(1, EXPERTS_PER_GROUP)]
    rank = jnp.zeros(sel.shape, jnp.int32)
    for d, m in enumerate(mates, start=1):
        tie_wins = (pos + d >= EXPERTS_PER_GROUP).astype(jnp.int32)
        rank = rank + jnp.where(m > sel, 1, jnp.where(m == sel, tie_wins, 0))
    vals = [sel] + mates
    top2 = None
    for i in range(EXPERTS_PER_GROUP):
        for j in range(i + 1, EXPERTS_PER_GROUP):
            pair = vals[i] + vals[j]
            top2 = pair if top2 is None else jnp.maximum(top2, pair)
    epos = lane & (N_EXPERTS - 1)
    beaten = jnp.zeros(sel.shape, jnp.int32)
    for d in range(1, N_GROUPS):
        other = _lane_mate(top2, d * EXPERTS_PER_GROUP, N_EXPERTS, epos)
        tie_wins = (epos + d * EXPERTS_PER_GROUP >= N_EXPERTS).astype(jnp.int32)
        beaten = beaten + jnp.where(other > top2, 1, jnp.where(other == top2, tie_wins, 0))
    rejected = beaten + jnp.where(rank < TOP_K, 0, 1) + jnp.where(lane < N_EXPERTS, 0, 1)
    w = jnp.where(rejected == 0, scores, 0.0)
    return w / jnp.sum(w, axis=-1, keepdims=True)


def _post_attn_kernel(yacd_ref, yb_ref, x_ref, wo_acd_ref, wo_b_ref, g1_ref, b1_ref, wq_ref,
                      km_ref, vm_ref, wo_ref, g2_ref, b2_ref, rw_hl_ref, rw_h_ref,
                      x2_ref, logits_ref, *, tm, alpha, n_chains):
    def chain(rows):
        h = _dot(yacd_ref[0, rows], wo_acd_ref[...]) + _dot(yb_ref[0, rows], wo_b_ref[...])
        yield
        x1 = _layer_norm(alpha * x_ref[0, rows] + h, g1_ref[...], b1_ref[...])
        q = _dot(x1.astype(BF16), wq_ref[...])
        yield
        heads = []
        for hd in range(X_HEADS):
            cols = slice(hd * X_HEAD_DIM, (hd + 1) * X_HEAD_DIM)
            s = _dot_nt(q[:, cols].astype(BF16), km_ref[0, :, cols])
            e = jnp.exp(s - jnp.max(s, axis=-1, keepdims=True))
            p = e / jnp.sum(e, axis=-1, keepdims=True)
            heads.append(_dot(p.astype(BF16), vm_ref[0, :, cols]).astype(BF16))
            yield
        h2 = _dot(jnp.concatenate(heads, axis=1), wo_ref[...])
        yield
        x2 = _layer_norm(alpha * x1 + h2, g2_ref[...], b2_ref[...])
        x2_ref[0, rows] = x2
        x_hi = x2.astype(BF16)
        x_lo = (x2 - x_hi.astype(F32)).astype(BF16)
        hl = _dot(x_hi, rw_hl_ref[...])
        logits_ref[0, rows] = hl[:, :LANES] + hl[:, LANES:] + _dot(x_lo, rw_h_ref[...])

    rows_per = tm // n_chains
    _run_interleaved([chain(slice(c * rows_per, (c + 1) * rows_per)) for c in range(n_chains)])


def _post_attn(yacd, yb, x, wo_acd, wo_b, g1, b1, wq, km, vm, wo, g2, b2, rw_hl, rw_h,
               *, tm, alpha):
    bsz, seq, d = x.shape
    tok = lambda w: pl.BlockSpec((1, tm, w), lambda b, s: (b, s, 0))
    memspec = pl.BlockSpec((1,) + km.shape[1:], lambda b, s: (b, 0, 0))
    return pl.pallas_call(
        functools.partial(_post_attn_kernel, tm=tm, alpha=alpha, n_chains=2),
        grid=(bsz, seq // tm),
        in_specs=[tok(yacd.shape[-1]), tok(yb.shape[-1]), tok(d),
                  _const_spec(wo_acd.shape), _const_spec(wo_b.shape), _const_spec(g1.shape),
                  _const_spec(b1.shape), _const_spec(wq.shape), memspec, memspec,
                  _const_spec(wo.shape), _const_spec(g2.shape), _const_spec(b2.shape),
                  _const_spec(rw_hl.shape), _const_spec(rw_h.shape)],
        out_specs=[tok(d), tok(LANES)],
        out_shape=[jax.ShapeDtypeStruct((bsz, seq, d), F32),
                   jax.ShapeDtypeStruct((bsz, seq, LANES), F32)],
        compiler_params=_params(2),
        name="post_attn",
    )(yacd, yb, x, wo_acd, wo_b, g1, b1, wq, km, vm, wo, g2, b2, rw_hl, rw_h)


def _moe_kernel(x_ref, logits_ref, rb_ref, wg_ref, wu_ref, wd_ref, g3_ref, b3_ref, o_ref, h_buf,
                *, alpha):
    x = x_ref[0]
    xb = x.astype(BF16)
    logits = logits_ref[0]
    gates = _route(logits, rb_ref[...], lax.broadcasted_iota(jnp.int32, logits.shape, 1))
    for e in range(N_EXPERTS):
        cols = slice(e * D_FF, (e + 1) * D_FF)
        hg = _dot(xb, wg_ref[:, cols])
        hu = _dot(xb, wu_ref[:, cols])
        h_buf[:, cols] = (jax.nn.silu(hg) * hu * gates[:, e:e + 1]).astype(BF16)
    out = _dot(h_buf[...], wd_ref[...])
    o_ref[0] = _layer_norm(alpha * x + out, g3_ref[...], b3_ref[...])


def _moe(x, logits, rb, wg, wu, wd, g3, b3, *, tm, alpha):
    bsz, seq, d = x.shape
    tok = lambda w: pl.BlockSpec((1, tm, w), lambda b, s: (b, s, 0))
    resident = lambda a: pl.BlockSpec(a.shape, lambda b, s: (0, 0), pipeline_mode=pl.Buffered(1))
    return pl.pallas_call(
        functools.partial(_moe_kernel, alpha=alpha),
        grid=(bsz, seq // tm),
        in_specs=[tok(d), tok(LANES), _const_spec(rb.shape),
                  resident(wg), resident(wu), resident(wd),
                  _const_spec(g3.shape), _const_spec(b3.shape)],
        out_specs=tok(d),
        out_shape=jax.ShapeDtypeStruct((bsz, seq, d), F32),
        scratch_shapes=[pltpu.VMEM((tm, N_EXPERTS * D_FF), BF16)],
        compiler_params=_params(2),
        name="moe",
    )(x, logits, rb, wg, wu, wd, g3, b3)


def _rot_cols(w):
    half = w.shape[-1] // 2
    return jnp.concatenate([-w[..., half:], w[..., :half]], axis=-1)


def _prep_w_in(w_in):
    u, v = w_in[..., 0:256], w_in[..., 256:512]
    cq, ckv, kr = w_in[..., 512:768], w_in[..., 768:896], w_in[..., 896:928]
    rest = w_in[..., 928:]
    rope4 = jnp.concatenate([kr, _rot_cols(kr), kr, _rot_cols(kr)], axis=-1)
    return jnp.concatenate([u, v, cq, ckv, rope4, rest], axis=-1).astype(BF16)


def _prep_w_uq(w_uq):
    depth = w_uq.shape[0]
    w = w_uq.reshape(depth, Q_LORA, MLA_HEADS, MLA_NOPE + MLA_ROPE)
    nope, rope = w[..., :MLA_NOPE], w[..., MLA_NOPE:]
    ext = jnp.concatenate([nope, rope, _rot_cols(rope)], axis=-1)
    return ext.reshape(depth, Q_LORA, MLA_HEADS * QK_DIM).astype(BF16)


def _prep_w_ukv(w_ukv):
    depth = w_ukv.shape[0]
    w = w_ukv.reshape(depth, KV_LORA, MLA_HEADS, MLA_NOPE + MLA_V)
    k_nope, v = w[..., :MLA_NOPE], w[..., MLA_NOPE:]
    wk = jnp.concatenate([k_nope, jnp.zeros_like(k_nope)], axis=-1)
    wv = jnp.concatenate([v, jnp.zeros_like(v)], axis=-1)
    return (wk.reshape(depth, KV_LORA, MLA_HEADS * QK_DIM).astype(BF16),
            wv.reshape(depth, KV_LORA, MLA_HEADS * LANES).astype(BF16))


def _prep_pool_w(pool_w):
    depth, groups = pool_w.shape[:2]
    eye = jnp.eye(groups, dtype=pool_w.dtype)
    bd = jnp.einsum('lgcd,gh->lgchd', pool_w, eye)
    return bd.reshape(depth, D_WIDTH, D_WIDTH).astype(BF16)


def _pad_lanes(a):
    return jnp.pad(a, [(0, 0)] * (a.ndim - 1) + [(0, LANES - a.shape[-1])])


def kernel(x, mem, positions, w_in, gmlp_v_norm_g, gmlp_w_s, gmlp_b_s, mla_q_norm_g, mla_w_uq,
           mla_kv_norm_g, mla_w_ukv, conv_w, pool_w, pool_scale, w_out, ln1_g, ln1_b,
           xattn_wq, xattn_wk, xattn_wv, xattn_wo, ln2_g, ln2_b, router_w, router_b,
           moe_w_gate, moe_w_up, moe_w_down, ln3_g, ln3_b):
    depth = w_in.shape[0]
    alpha = (2 * depth) ** 0.25
    tm = 512
    bsz, seq, _ = x.shape

    inv_freq = ROPE_BASE ** (-jnp.arange(0, MLA_ROPE, 2, dtype=F32) / MLA_ROPE)
    ang = positions.astype(F32)[..., None] * inv_freq
    cos2 = jnp.tile(jnp.cos(ang), (1, 1, 2))
    sin2 = jnp.tile(jnp.sin(ang), (1, 1, 2))
    scale = (MLA_NOPE + MLA_ROPE) ** -0.5 * LOG2_E
    csq = scale * jnp.concatenate([jnp.ones(cos2.shape[:2] + (MLA_NOPE,), F32), cos2, sin2], -1)
    csq = jnp.swapaxes(csq.reshape(bsz, seq // tm, tm, QK_DIM), 2, 3)
    csk = jnp.concatenate([cos2, sin2, cos2, sin2], axis=-1)

    row = lambda a: a[:, None, :]
    w_in_e = _prep_w_in(w_in)
    w_uq_e = jnp.swapaxes(_prep_w_uq(mla_w_uq), 1, 2)
    w_k_e, w_v_e = _prep_w_ukv(mla_w_ukv)
    w_v_e = jnp.swapaxes(w_v_e, 1, 2)
    ws = gmlp_w_s.astype(BF16)
    bs = jnp.repeat(jnp.swapaxes(gmlp_b_s, 1, 2), A_HEAD_DIM, axis=2)
    pool_bd = _prep_pool_w(pool_w)
    wo_acd = jnp.concatenate([w_out[:, :A_WIDTH], w_out[:, A_WIDTH + MLA_WIDTH:]], axis=1).astype(BF16)
    wo_b = w_out[:, A_WIDTH:A_WIDTH + MLA_WIDTH].astype(BF16)
    wq = (xattn_wq * (X_HEAD_DIM ** -0.5)).astype(BF16)
    wo = xattn_wo.astype(BF16)
    rw = _pad_lanes(router_w)
    rw_hi = rw.astype(BF16)
    rw_lo = (rw - rw_hi.astype(F32)).astype(BF16)
    rw_hl = jnp.concatenate([rw_hi, rw_lo], axis=1)
    rb = _pad_lanes(router_b[None, :])
    cat_experts = lambda w: jnp.swapaxes(w, 1, 2).reshape(depth, D_MODEL, N_EXPERTS * D_FF)
    wg = cat_experts(moe_w_gate).astype(BF16)
    wu = cat_experts(moe_w_up).astype(BF16)
    wd = moe_w_down.reshape(depth, N_EXPERTS * D_FF, D_MODEL).astype(BF16)

    km, vm = _mem_kv(mem, xattn_wk.astype(BF16), xattn_wv.astype(BF16))

    for l in range(depth):
        q, k, v, yacd = _mixer_in(
            x, csq, csk, w_in_e[l], row(gmlp_v_norm_g)[l], ws[l], bs[l], row(mla_q_norm_g)[l],
            w_uq_e[l], row(mla_kv_norm_g)[l], w_k_e[l], w_v_e[l], conv_w[l], pool_bd[l],
            row(pool_scale)[l], tm=tm)
        yb = _mla_attn(q, k, v, hg=MLA_HEADS)
        x2, logits = _post_attn(
            yacd, yb, x, wo_acd[l], wo_b[l], row(ln1_g)[l], row(ln1_b)[l], wq[l], km[l], vm[l],
            wo[l], row(ln2_g)[l], row(ln2_b)[l], rw_hl, rw_hi, tm=tm, alpha=alpha)
        x = _moe(x2, logits, rb, wg[l], wu[l], wd[l], row(ln3_g)[l], row(ln3_b)[l], tm=tm, alpha=alpha)
    return x
```

```python
import functools

import jax
import jax.numpy as jnp
from jax import lax
from jax.experimental import pallas as pl
from jax.experimental.pallas import tpu as pltpu

F32 = jnp.float32
BF16 = jnp.bfloat16

D_MODEL = 1024
A_HEADS, A_HEAD_DIM, CHUNK = 4, 64, 128
A_WIDTH = A_HEADS * A_HEAD_DIM
MLA_HEADS, MLA_NOPE, MLA_ROPE, MLA_V = 8, 64, 32, 64
Q_LORA, KV_LORA = 256, 128
MLA_WIDTH = MLA_HEADS * MLA_V
ROPE_BASE = 10000.0
C_WIDTH, CONV_W = 256, 3
D_WIDTH = 256
POOL_WINDOWS = (2, 4, 8, 16)
D_GROUP = D_WIDTH // len(POOL_WINDOWS)
X_HEADS = 4
X_HEAD_DIM = D_MODEL // X_HEADS
N_EXPERTS, N_GROUPS, TOP_K = 16, 4, 2
EXPERTS_PER_GROUP = N_EXPERTS // N_GROUPS
D_FF = 256
EPS = 1e-6

LANES = 128
SUBLANES = 8
VMEM_LIMIT = 56 * 1024 * 1024

OFF_U, OFF_CQ, OFF_BG, P_EXT = 0, 512, 1024, 2048
QK_DIM = 128
POOL_HALO = 16
MOE_ROWS = 160
ROUTE_ROWS = 16
NEG_BIG = -1e30
LOG2_E = 1.4426950408889634


def _dot(a, b):
    return jnp.dot(a, b, preferred_element_type=F32)


def _dot_nt(a, b):
    return lax.dot_general(a, b, (((1,), (1,)), ((), ())), preferred_element_type=F32)


def _rms(x, g):
    return x * lax.rsqrt(jnp.mean(x * x, axis=-1, keepdims=True) + EPS) * g


def _layer_norm(x, g, b):
    mu = jnp.mean(x, axis=-1, keepdims=True)
    xc = x - mu
    var = jnp.mean(xc * xc, axis=-1, keepdims=True)
    return xc * lax.rsqrt(var + EPS) * g + b


def _params(n_grid):
    return pltpu.CompilerParams(dimension_semantics=("arbitrary",) * n_grid,
                                vmem_limit_bytes=VMEM_LIMIT)


def _run_interleaved(chains):
    live = list(chains)
    while live:
        for c in list(live):
            try:
                next(c)
            except StopIteration:
                live.remove(c)


def _const_spec(shape):
    zeros = (0,) * len(shape)
    return pl.BlockSpec(shape, lambda *_: zeros)


def _mem_kv_kernel(mem_ref, wk_ref, wv_ref, k_ref, v_ref):
    m = mem_ref[0].astype(BF16)
    k_ref[0, 0] = _dot(m, wk_ref[0]).astype(BF16)
    v_ref[0, 0] = _dot(m, wv_ref[0]).astype(BF16)


def _mem_kv(mem, wk, wv):
    depth = wk.shape[0]
    bsz, mlen, d = mem.shape
    out = jax.ShapeDtypeStruct((depth, bsz, mlen, d), BF16)
    return pl.pallas_call(
        _mem_kv_kernel,
        grid=(depth, bsz),
        in_specs=[pl.BlockSpec((1, mlen, d), lambda l, b: (b, 0, 0)),
                  pl.BlockSpec((1, d, d), lambda l, b: (l, 0, 0)),
                  pl.BlockSpec((1, d, d), lambda l, b: (l, 0, 0))],
        out_specs=[pl.BlockSpec((1, 1, mlen, d), lambda l, b: (l, b, 0, 0)),
                   pl.BlockSpec((1, 1, mlen, d), lambda l, b: (l, b, 0, 0))],
        out_shape=[out, out],
        compiler_params=_params(2),
        name="mem_kv",
    )(mem, wk, wv)


def _mixer_in_kernel(x_ref, csq_ref, csk_ref, w_in_ref, vng_ref, ws_ref, bs_ref, qng_ref,
                     wuq_ref, kvng_ref, wk_ref, wv_ref, convw_ref, poolw_ref, pscale_ref,
                     q_ref, k_ref, v_ref, y_ref,
                     conv_buf, pool_a, pool_b, pool_carry, *, tm):
    si = pl.program_id(1)

    @pl.when(si == 0)
    def _():
        pool_carry[...] = jnp.zeros((POOL_HALO, D_WIDTH), F32)
        conv_buf[0:SUBLANES, :] = jnp.zeros((SUBLANES, C_WIDTH), F32)

    xb = x_ref[0].astype(BF16)

    za = _dot(xb, w_in_ref[:, OFF_U:OFF_CQ])
    zb = _dot(xb, w_in_ref[:, OFF_CQ:OFF_BG])
    zcd = _dot(xb, w_in_ref[:, OFF_BG:P_EXT])

    za = jax.nn.gelu(za)
    u = za[:, :A_WIDTH]
    v = _rms(za[:, A_WIDTH:], vng_ref[...]).astype(BF16)
    cq = _rms(zb[:, :Q_LORA], qng_ref[...]).astype(BF16)
    ckv = _rms(zb[:, Q_LORA:Q_LORA + KV_LORA], kvng_ref[...]).astype(BF16)

    row = lax.broadcasted_iota(jnp.int32, (CHUNK, CHUNK), 0)
    col = lax.broadcasted_iota(jnp.int32, (CHUNK, CHUNK), 1)
    w_causal = [jnp.where(row >= col, ws_ref[h], jnp.zeros((), BF16)) for h in range(A_HEADS)]
    lane_head = lax.broadcasted_iota(jnp.int32, (CHUNK, A_WIDTH), 1) // A_HEAD_DIM
    for c in range(tm // CHUNK):
        rows = slice(c * CHUNK, (c + 1) * CHUNK)
        vc = v[rows]
        mixed = _dot(w_causal[0], vc)
        for h in range(1, A_HEADS):
            mixed = jnp.where(lane_head == h, _dot(w_causal[h], vc), mixed)
        y_ref[0, rows, 0:A_WIDTH] = (u[rows] * (mixed + bs_ref[...])).astype(BF16)

    qt = _dot_nt(wuq_ref[...], cq)
    kk = _dot(ckv, wk_ref[...])
    vt = _dot_nt(wv_ref[...], ckv)

    zd = zcd[:, 3 * C_WIDTH:]
    base = SUBLANES
    n = tm + POOL_HALO
    pool_a[0:base, :] = jnp.zeros((base, D_WIDTH), F32)
    pool_b[0:base, :] = jnp.zeros((base, D_WIDTH), F32)
    pool_a[base:base + POOL_HALO, :] = pool_carry[...]
    pool_a[base + POOL_HALO:base + n, :] = zd
    pool_carry[...] = zd[tm - POOL_HALO:, :]
    lane_d = lax.broadcasted_iota(jnp.int32, (n, D_WIDTH), 1)
    pool_b[base:base + n, :] = pool_a[base:base + n, :] + pool_a[base - 1:base - 1 + n, :]
    pool_a[base:base + n, :] = pool_b[base:base + n, :] + jnp.where(
        lane_d >= D_GROUP, pool_b[base - 2:base - 2 + n, :], 0.0)
    pool_b[base:base + n, :] = pool_a[base:base + n, :] + jnp.where(
        lane_d >= 2 * D_GROUP, pool_a[base - 4:base - 4 + n, :], 0.0)
    t0 = base + POOL_HALO
    lane_t = lax.broadcasted_iota(jnp.int32, (tm, D_WIDTH), 1)
    win_sum = pool_b[t0:t0 + tm, :] + jnp.where(
        lane_t >= 3 * D_GROUP, pool_b[t0 - 8:t0 - 8 + tm, :], 0.0)
    pos1 = si * tm + lax.broadcasted_iota(jnp.int32, (tm, D_WIDTH), 0) + 1
    window = jnp.left_shift(2, lane_t // D_GROUP)
    count = jnp.minimum(pos1, window).astype(F32)
    pooled = (win_sum / count - zd).astype(BF16)
    yd = _dot(pooled, poolw_ref[...]) * pscale_ref[...]
    y_ref[0, :, A_WIDTH + C_WIDTH:] = yd.astype(BF16)

    gh = zcd[:, C_WIDTH:2 * C_WIDTH] * zcd[:, 2 * C_WIDTH:3 * C_WIDTH]
    conv_buf[SUBLANES:SUBLANES + tm, :] = gh
    conv = (convw_ref[2:3, :] * gh
            + convw_ref[1:2, :] * conv_buf[SUBLANES - 1:SUBLANES - 1 + tm, :]
            + convw_ref[0:1, :] * conv_buf[SUBLANES - 2:SUBLANES - 2 + tm, :])
    y_ref[0, :, A_WIDTH:A_WIDTH + C_WIDTH] = (zcd[:, :C_WIDTH] * conv).astype(BF16)
    conv_buf[0:SUBLANES, :] = conv_buf[tm:tm + SUBLANES, :]

    csq = csq_ref[0, 0]
    for h in range(MLA_HEADS):
        q_ref[0, h, 0] = (qt[h * QK_DIM:(h + 1) * QK_DIM, :] * csq).astype(BF16)
    r4 = zb[:, Q_LORA + KV_LORA:] * csk_ref[0]
    kr = r4 + pltpu.roll(r4, MLA_ROPE, 1)
    lane = lax.broadcasted_iota(jnp.int32, (tm, QK_DIM), 1)
    kr = jnp.where(lane >= MLA_NOPE, kr, 0.0)
    for h in range(MLA_HEADS):
        k_ref[0, h] = (kk[:, h * QK_DIM:(h + 1) * QK_DIM] + kr).astype(BF16)
    ones_row = jnp.where(lax.broadcasted_iota(jnp.int32, (LANES, tm), 0) == MLA_V, 1.0, 0.0)
    for h in range(MLA_HEADS):
        v_ref[0, h, 0] = (vt[h * LANES:(h + 1) * LANES, :] + ones_row).astype(BF16)


def _mixer_in(x, csq, csk, w_in, vng, ws, bs, qng, wuq, kvng, wk, wv, convw, poolw, pscale, *, tm):
    bsz, seq, d = x.shape
    tok = lambda w: pl.BlockSpec((1, tm, w), lambda b, s: (b, s, 0))
    head = lambda n, w: pl.BlockSpec((1, n, tm, w), lambda b, s: (b, 0, s, 0))
    head_t = pl.BlockSpec((1, MLA_HEADS, 1, LANES, tm), lambda b, s: (b, 0, s, 0, 0))
    shape_t = jax.ShapeDtypeStruct((bsz, MLA_HEADS, seq // tm, LANES, tm), BF16)
    return pl.pallas_call(
        functools.partial(_mixer_in_kernel, tm=tm),
        grid=(bsz, seq // tm),
        in_specs=[tok(d), pl.BlockSpec((1, 1, QK_DIM, tm), lambda b, s: (b, s, 0, 0)), tok(QK_DIM),
                  _const_spec(w_in.shape), _const_spec(vng.shape), _const_spec(ws.shape),
                  _const_spec(bs.shape), _const_spec(qng.shape), _const_spec(wuq.shape),
                  _const_spec(kvng.shape), _const_spec(wk.shape), _const_spec(wv.shape),
                  _const_spec(convw.shape), _const_spec(poolw.shape), _const_spec(pscale.shape)],
        out_specs=[head_t, head(MLA_HEADS, QK_DIM), head_t, tok(A_WIDTH + C_WIDTH + D_WIDTH)],
        out_shape=[shape_t,
                   jax.ShapeDtypeStruct((bsz, MLA_HEADS, seq, QK_DIM), BF16),
                   shape_t,
                   jax.ShapeDtypeStruct((bsz, seq, A_WIDTH + C_WIDTH + D_WIDTH), BF16)],
        scratch_shapes=[pltpu.VMEM((tm + SUBLANES, C_WIDTH), F32),
                        pltpu.VMEM((tm + POOL_HALO + SUBLANES, D_WIDTH), F32),
                        pltpu.VMEM((tm + POOL_HALO + SUBLANES, D_WIDTH), F32),
                        pltpu.VMEM((POOL_HALO, D_WIDTH), F32)],
        compiler_params=_params(2),
        name="mixer_in",
    )(x, csq, csk, w_in, vng, ws, bs, qng, wuq, kvng, wk, wv, convw, poolw, pscale)


def _mla_attn_kernel(qt_ref, k_ref, vt_ref, o_ref, m_ref, acc_ref, *, tq, hg):
    qi = pl.program_id(2)
    key = lax.broadcasted_iota(jnp.int32, (tq, tq), 0)
    qry = lax.broadcasted_iota(jnp.int32, (tq, tq), 1)
    m_ref[...] = jnp.full(m_ref.shape, NEG_BIG, F32)
    acc_ref[...] = jnp.zeros(acc_ref.shape, F32)

    def step(j, masked):
        def scores(hh):
            return _dot(k_ref[0, hh, pl.ds(pl.multiple_of(j * tq, tq), tq), :], qt_ref[0, hh, 0])

        def weighted_values(hh, pt, rescale):
            pv = _dot(vt_ref[0, hh, j], pt)
            acc_ref[hh] = rescale * acc_ref[hh] + pv

        st_next = scores(0)
        pending = None
        for hh in range(hg):
            st = st_next
            if hh + 1 < hg:
                st_next = scores(hh + 1)
            if pending is not None:
                weighted_values(*pending)
            if masked:
                st = jnp.where(key <= qry, st, NEG_BIG)
            m_old = m_ref[hh]
            m_new = jnp.maximum(m_old, jnp.max(st, axis=0, keepdims=True))
            m_ref[hh] = m_new
            pending = (hh, jnp.exp2(st - m_new).astype(BF16), jnp.exp2(m_old - m_new))
        weighted_values(*pending)

    def body(j, carry):
        step(j, False)
        return carry

    lax.fori_loop(0, qi, body, 0)
    step(qi, True)
    for pr in range(hg // 2):
        halves = []
        for hh in (2 * pr, 2 * pr + 1):
            acc = acc_ref[hh]
            halves.append(acc[:MLA_V] / acc[MLA_V:MLA_V + 1])
        o_ref[0, :, pr * LANES:(pr + 1) * LANES] = jnp.concatenate(halves, axis=0).T.astype(BF16)


def _mla_attn(qt, k, vt, *, hg):
    bsz, heads, n_tiles, _, tq = qt.shape
    seq = n_tiles * tq
    return pl.pallas_call(
        functools.partial(_mla_attn_kernel, tq=tq, hg=hg),
        grid=(bsz, heads // hg, n_tiles),
        in_specs=[pl.BlockSpec((1, hg, 1, QK_DIM, tq), lambda b, g, i: (b, g, i, 0, 0)),
                  pl.BlockSpec((1, hg, seq, QK_DIM), lambda b, g, i: (b, g, 0, 0)),
                  pl.BlockSpec((1, hg, n_tiles, LANES, tq), lambda b, g, i: (b, g, 0, 0, 0))],
        out_specs=pl.BlockSpec((1, tq, hg * MLA_V), lambda b, g, i: (b, i, g)),
        out_shape=jax.ShapeDtypeStruct((bsz, seq, MLA_WIDTH), BF16),
        scratch_shapes=[pltpu.VMEM((hg, 1, tq), F32), pltpu.VMEM((hg, LANES, tq), F32)],
        compiler_params=_params(3),
        name="mla_attn",
    )(qt, k, vt)


def _route_t(logits_t, bias_t):
    scores = jax.nn.sigmoid(logits_t)
    sel_all = scores + bias_t
    sel = [sel_all[e:e + 1] for e in range(N_EXPERTS)]
    group_score = []
    for g in range(N_GROUPS):
        v = sel[g * EXPERTS_PER_GROUP:(g + 1) * EXPERTS_PER_GROUP]
        best = None
        for i in range(EXPERTS_PER_GROUP):
            for j in range(i + 1, EXPERTS_PER_GROUP):
                best = v[i] + v[j] if best is None else jnp.maximum(best, v[i] + v[j])
        group_score.append(best)
    top, top_idx = group_score[0], jnp.zeros(group_score[0].shape, jnp.int32)
    for g in range(1, N_GROUPS):
        better = group_score[g] > top
        top = jnp.where(better, group_score[g], top)
        top_idx = jnp.where(better, g, top_idx)
    member = [jnp.where(top_idx == g, 1.0, 0.0) for g in range(N_GROUPS)]
    weights = []
    for e in range(N_EXPERTS):
        g = e // EXPERTS_PER_GROUP
        rank = jnp.zeros(sel[e].shape, jnp.int32)
        for j in range(g * EXPERTS_PER_GROUP, (g + 1) * EXPERTS_PER_GROUP):
            if j != e:
                rank = rank + jnp.where(sel[j] > sel[e], 1,
                                        jnp.where(sel[j] == sel[e], 1 if j < e else 0, 0))
        weights.append(jnp.where(rank < TOP_K, member[g], 0.0) * scores[e:e + 1])
    total = weights[0]
    for w in weights[1:]:
        total = total + w
    return [w / total for w in weights], member


def _post_attn_kernel(yacd_ref, yb_ref, x_ref, wo_acd_ref, wo_b_ref, g1_ref, b1_ref, wq_ref,
                      km_ref, vm_ref, wo_ref, g2_ref, b2_ref, rw_ref,
                      x2_ref, logits_ref, *, tm, alpha, n_chains):
    def chain(rows):
        h = _dot(yacd_ref[0, rows], wo_acd_ref[...]) + _dot(yb_ref[0, rows], wo_b_ref[...])
        yield
        x1 = _layer_norm(alpha * x_ref[0, rows] + h, g1_ref[...], b1_ref[...])
        q = _dot(x1.astype(BF16), wq_ref[...])
        yield
        heads = []
        for hd in range(X_HEADS):
            cols = slice(hd * X_HEAD_DIM, (hd + 1) * X_HEAD_DIM)
            s = _dot_nt(q[:, cols].astype(BF16), km_ref[0, :, cols])
            e = jnp.exp(s - jnp.max(s, axis=-1, keepdims=True))
            p = e / jnp.sum(e, axis=-1, keepdims=True)
            heads.append(_dot(p.astype(BF16), vm_ref[0, :, cols]).astype(BF16))
            yield
        h2 = _dot(jnp.concatenate(heads, axis=1), wo_ref[...])
        yield
        x2 = _layer_norm(alpha * x1 + h2, g2_ref[...], b2_ref[...])
        x2_ref[0, rows] = x2
        x_hi = x2.astype(BF16)
        x_lo = (x2 - x_hi.astype(F32)).astype(BF16)
        hl = _dot_nt(rw_ref[...], x_hi)
        lo = _dot_nt(rw_ref[0:N_EXPERTS, :], x_lo)
        logits_ref[0, :, rows] = hl[:N_EXPERTS] + hl[N_EXPERTS:] + lo

    rows_per = tm // n_chains
    _run_interleaved([chain(slice(c * rows_per, (c + 1) * rows_per)) for c in range(n_chains)])


def _post_attn(yacd, yb, x, wo_acd, wo_b, g1, b1, wq, km, vm, wo, g2, b2, rw, *, tm, alpha):
    bsz, seq, d = x.shape
    tok = lambda w: pl.BlockSpec((1, tm, w), lambda b, s: (b, s, 0))
    memspec = pl.BlockSpec((1,) + km.shape[1:], lambda b, s: (b, 0, 0))
    return pl.pallas_call(
        functools.partial(_post_attn_kernel, tm=tm, alpha=alpha, n_chains=2),
        grid=(bsz, seq // tm),
        in_specs=[tok(yacd.shape[-1]), tok(yb.shape[-1]), tok(d),
                  _const_spec(wo_acd.shape), _const_spec(wo_b.shape), _const_spec(g1.shape),
                  _const_spec(b1.shape), _const_spec(wq.shape), memspec, memspec,
                  _const_spec(wo.shape), _const_spec(g2.shape), _const_spec(b2.shape),
                  _const_spec(rw.shape)],
        out_specs=[tok(d), pl.BlockSpec((1, N_EXPERTS, tm), lambda b, s: (b, 0, s))],
        out_shape=[jax.ShapeDtypeStruct((bsz, seq, d), F32),
                   jax.ShapeDtypeStruct((bsz, N_EXPERTS, seq), F32)],
        compiler_params=_params(2),
        name="post_attn",
    )(yacd, yb, x, wo_acd, wo_b, g1, b1, wq, km, vm, wo, g2, b2, rw)


def _dot_tn(a, b):
    return lax.dot_general(a, b, (((0,), (0,)), ((), ())), preferred_element_type=F32)


def _moe_kernel(x_ref, logits_ref, rb_ref, before_ref, wg_ref, wu_ref, wd_ref, g3_ref, b3_ref, o_ref,
                xb_ref, pos_ref, memb_ref, gsplit_ref, acc_ref, *, alpha, tm, rows):
    group_w = EXPERTS_PER_GROUP * D_FF
    x = x_ref[0]
    xb_ref[...] = x.astype(BF16)
    gates, member = _route_t(logits_ref[0], rb_ref[...])
    memb = jnp.concatenate(member + [jnp.zeros((ROUTE_ROWS - N_GROUPS, tm), F32)], axis=0)
    memb_ref[...] = memb
    pos_ref[...] = _dot(memb.astype(BF16), before_ref[...])
    gate_rows = jnp.concatenate(gates, axis=0)
    g_hi = gate_rows.astype(BF16)
    gsplit_ref[...] = jnp.concatenate([g_hi, (gate_rows - g_hi.astype(F32)).astype(BF16)], axis=0)
    acc_ref[...] = jnp.zeros(acc_ref.shape, F32)

    for g in range(N_GROUPS):
        n_tok = jnp.sum(member[g]).astype(jnp.int32)

        def chunk(ci, carry, g=g):
            slot = (lax.broadcasted_iota(jnp.int32, (rows, tm), 0) + ci * rows).astype(F32)
            p = jnp.where(pos_ref[g:g + 1, :] == slot, memb_ref[g:g + 1, :], 0.0).astype(BF16)
            xg = _dot(p, xb_ref[...]).astype(BF16)
            gg = _dot_nt(p, gsplit_ref[...])
            hs = []
            for j in range(EXPERTS_PER_GROUP):
                e = g * EXPERTS_PER_GROUP + j
                cols = slice(e * D_FF, (e + 1) * D_FF)
                gate = gg[:, e:e + 1] + gg[:, N_EXPERTS + e:N_EXPERTS + e + 1]
                hg = _dot(xg, wg_ref[:, cols])
                hu = _dot(xg, wu_ref[:, cols])
                hs.append((jax.nn.silu(hg) * hu * gate).astype(BF16))
            y = _dot(jnp.concatenate(hs, axis=1), wd_ref[g * group_w:(g + 1) * group_w, :])
            acc_ref[...] += _dot_tn(p, y.astype(BF16))
            return carry

        lax.fori_loop(0, (n_tok + rows - 1) // rows, chunk, 0)

    o_ref[0] = _layer_norm(alpha * x + acc_ref[...], g3_ref[...], b3_ref[...])


def _moe(x, logits_t, rb_t, before, wg, wu, wd, g3, b3, *, tm, alpha):
    bsz, seq, d = x.shape
    tok = lambda w: pl.BlockSpec((1, tm, w), lambda b, s: (b, s, 0))
    resident = lambda a: pl.BlockSpec(a.shape, lambda b, s: (0, 0), pipeline_mode=pl.Buffered(1))
    return pl.pallas_call(
        functools.partial(_moe_kernel, alpha=alpha, tm=tm, rows=MOE_ROWS),
        grid=(bsz, seq // tm),
        in_specs=[tok(d), pl.BlockSpec((1, N_EXPERTS, tm), lambda b, s: (b, 0, s)),
                  _const_spec(rb_t.shape), _const_spec(before.shape),
                  resident(wg), resident(wu), resident(wd),
                  _const_spec(g3.shape), _const_spec(b3.shape)],
        out_specs=tok(d),
        out_shape=jax.ShapeDtypeStruct((bsz, seq, d), F32),
        scratch_shapes=[pltpu.VMEM((tm, d), BF16),
                        pltpu.VMEM((ROUTE_ROWS, tm), F32),
                        pltpu.VMEM((ROUTE_ROWS, tm), F32),
                        pltpu.VMEM((2 * N_EXPERTS, tm), BF16),
                        pltpu.VMEM((tm, d), F32)],
        compiler_params=_params(2),
        name="moe",
    )(x, logits_t, rb_t, before, wg, wu, wd, g3, b3)


def _rot_cols(w):
    half = w.shape[-1] // 2
    return jnp.concatenate([-w[..., half:], w[..., :half]], axis=-1)


def _prep_w_in(w_in):
    u, v = w_in[..., 0:256], w_in[..., 256:512]
    cq, ckv, kr = w_in[..., 512:768], w_in[..., 768:896], w_in[..., 896:928]
    rest = w_in[..., 928:]
    rope4 = jnp.concatenate([kr, _rot_cols(kr), kr, _rot_cols(kr)], axis=-1)
    return jnp.concatenate([u, v, cq, ckv, rope4, rest], axis=-1).astype(BF16)


def _prep_w_uq(w_uq):
    depth = w_uq.shape[0]
    w = w_uq.reshape(depth, Q_LORA, MLA_HEADS, MLA_NOPE + MLA_ROPE)
    nope, rope = w[..., :MLA_NOPE], w[..., MLA_NOPE:]
    ext = jnp.concatenate([nope, rope, _rot_cols(rope)], axis=-1)
    return ext.reshape(depth, Q_LORA, MLA_HEADS * QK_DIM).astype(BF16)


def _prep_w_ukv(w_ukv):
    depth = w_ukv.shape[0]
    w = w_ukv.reshape(depth, KV_LORA, MLA_HEADS, MLA_NOPE + MLA_V)
    k_nope, v = w[..., :MLA_NOPE], w[..., MLA_NOPE:]
    wk = jnp.concatenate([k_nope, jnp.zeros_like(k_nope)], axis=-1)
    wv = jnp.concatenate([v, jnp.zeros_like(v)], axis=-1)
    return (wk.reshape(depth, KV_LORA, MLA_HEADS * QK_DIM).astype(BF16),
            wv.reshape(depth, KV_LORA, MLA_HEADS * LANES).astype(BF16))


def _prep_pool_w(pool_w):
    depth, groups = pool_w.shape[:2]
    eye = jnp.eye(groups, dtype=pool_w.dtype)
    bd = jnp.einsum('lgcd,gh->lgchd', pool_w, eye)
    return bd.reshape(depth, D_WIDTH, D_WIDTH).astype(BF16)


def kernel(x, mem, positions, w_in, gmlp_v_norm_g, gmlp_w_s, gmlp_b_s, mla_q_norm_g, mla_w_uq,
           mla_kv_norm_g, mla_w_ukv, conv_w, pool_w, pool_scale, w_out, ln1_g, ln1_b,
           xattn_wq, xattn_wk, xattn_wv, xattn_wo, ln2_g, ln2_b, router_w, router_b,
           moe_w_gate, moe_w_up, moe_w_down, ln3_g, ln3_b):
    depth = w_in.shape[0]
    alpha = (2 * depth) ** 0.25
    tm = 512
    bsz, seq, _ = x.shape

    inv_freq = ROPE_BASE ** (-jnp.arange(0, MLA_ROPE, 2, dtype=F32) / MLA_ROPE)
    ang = positions.astype(F32)[..., None] * inv_freq
    cos2 = jnp.tile(jnp.cos(ang), (1, 1, 2))
    sin2 = jnp.tile(jnp.sin(ang), (1, 1, 2))
    scale = (MLA_NOPE + MLA_ROPE) ** -0.5 * LOG2_E
    csq = scale * jnp.concatenate([jnp.ones(cos2.shape[:2] + (MLA_NOPE,), F32), cos2, sin2], -1)
    csq = jnp.swapaxes(csq.reshape(bsz, seq // tm, tm, QK_DIM), 2, 3)
    csk = jnp.concatenate([cos2, sin2, cos2, sin2], axis=-1)

    row = lambda a: a[:, None, :]
    w_in_e = _prep_w_in(w_in)
    w_uq_e = jnp.swapaxes(_prep_w_uq(mla_w_uq), 1, 2)
    w_k_e, w_v_e = _prep_w_ukv(mla_w_ukv)
    w_v_e = jnp.swapaxes(w_v_e, 1, 2)
    ws = gmlp_w_s.astype(BF16)
    bs = jnp.repeat(jnp.swapaxes(gmlp_b_s, 1, 2), A_HEAD_DIM, axis=2)
    pool_bd = _prep_pool_w(pool_w)
    wo_acd = jnp.concatenate([w_out[:, :A_WIDTH], w_out[:, A_WIDTH + MLA_WIDTH:]], axis=1).astype(BF16)
    wo_b = w_out[:, A_WIDTH:A_WIDTH + MLA_WIDTH].astype(BF16)
    wq = (xattn_wq * (X_HEAD_DIM ** -0.5)).astype(BF16)
    wo = xattn_wo.astype(BF16)
    rw_t = router_w.T
    rw_hi = rw_t.astype(BF16)
    rw = jnp.concatenate([rw_hi, (rw_t - rw_hi.astype(F32)).astype(BF16)], axis=0)
    rb_t = jnp.broadcast_to(router_b[:, None], (N_EXPERTS, tm))
    before = jnp.triu(jnp.ones((tm, tm), BF16), 1)
    cat_experts = lambda w: jnp.swapaxes(w, 1, 2).reshape(depth, D_MODEL, N_EXPERTS * D_FF)
    wg = cat_experts(moe_w_gate).astype(BF16)
    wu = cat_experts(moe_w_up).astype(BF16)
    wd = moe_w_down.reshape(depth, N_EXPERTS * D_FF, D_MODEL).astype(BF16)

    km, vm = _mem_kv(mem, xattn_wk.astype(BF16), xattn_wv.astype(BF16))

    for l in range(depth):
        q, k, v, yacd = _mixer_in(
            x, csq, csk, w_in_e[l], row(gmlp_v_norm_g)[l], ws[l], bs[l], row(mla_q_norm_g)[l],
            w_uq_e[l], row(mla_kv_norm_g)[l], w_k_e[l], w_v_e[l], conv_w[l], pool_bd[l],
            row(pool_scale)[l], tm=tm)
        yb = _mla_attn(q, k, v, hg=MLA_HEADS)
        x2, logits = _post_attn(
            yacd, yb, x, wo_acd[l], wo_b[l], row(ln1_g)[l], row(ln1_b)[l], wq[l], km[l], vm[l],
            wo[l], row(ln2_g)[l], row(ln2_b)[l], rw, tm=tm, alpha=alpha)
        x = _moe(x2, logits, rb_t, before, wg[l], wu[l], wd[l], row(ln3_g)[l], row(ln3_b)[l], tm=tm, alpha=alpha)
    return x
```

```python
import functools

import jax
import jax.numpy as jnp
from jax import lax
from jax.experimental import pallas as pl
from jax.experimental.pallas import tpu as pltpu

F32 = jnp.float32
BF16 = jnp.bfloat16

D_MODEL = 1024
A_HEADS, A_HEAD_DIM, CHUNK = 4, 64, 128
A_WIDTH = A_HEADS * A_HEAD_DIM
MLA_HEADS, MLA_NOPE, MLA_ROPE, MLA_V = 8, 64, 32, 64
Q_LORA, KV_LORA = 256, 128
MLA_WIDTH = MLA_HEADS * MLA_V
ROPE_BASE = 10000.0
C_WIDTH, CONV_W = 256, 3
D_WIDTH = 256
POOL_WINDOWS = (2, 4, 8, 16)
D_GROUP = D_WIDTH // len(POOL_WINDOWS)
X_HEADS = 4
X_HEAD_DIM = D_MODEL // X_HEADS
N_EXPERTS, N_GROUPS, TOP_K = 16, 4, 2
EXPERTS_PER_GROUP = N_EXPERTS // N_GROUPS
D_FF = 256
EPS = 1e-6

LANES = 128
SUBLANES = 8
VMEM_LIMIT = 56 * 1024 * 1024

OFF_U, OFF_CQ, OFF_BG, P_EXT = 0, 512, 1024, 2048
QK_DIM = 128
POOL_HALO = 16
MOE_ROWS = 160
ROUTE_ROWS = 16
NEG_BIG = -1e30
LOG2_E = 1.4426950408889634


def _dot(a, b):
    return jnp.dot(a, b, preferred_element_type=F32)


def _dot_nt(a, b):
    return lax.dot_general(a, b, (((1,), (1,)), ((), ())), preferred_element_type=F32)


def _rms(x, g):
    return x * lax.rsqrt(jnp.mean(x * x, axis=-1, keepdims=True) + EPS) * g


def _layer_norm(x, g, b):
    mu = jnp.mean(x, axis=-1, keepdims=True)
    xc = x - mu
    var = jnp.mean(xc * xc, axis=-1, keepdims=True)
    return xc * lax.rsqrt(var + EPS) * g + b


def _params(n_grid):
    return pltpu.CompilerParams(dimension_semantics=("arbitrary",) * n_grid,
                                vmem_limit_bytes=VMEM_LIMIT)


def _run_interleaved(chains):
    live = list(chains)
    while live:
        for c in list(live):
            try:
                next(c)
            except StopIteration:
                live.remove(c)


def _const_spec(shape):
    zeros = (0,) * len(shape)
    return pl.BlockSpec(shape, lambda *_: zeros)


def _layer_spec(stacked, layer, single_buffer=False):
    index = (layer,) + (0,) * (stacked.ndim - 1)
    mode = dict(pipeline_mode=pl.Buffered(1)) if single_buffer else {}
    return pl.BlockSpec((None,) + stacked.shape[1:], lambda *_: index, **mode)


def _mem_kv_kernel(mem_ref, wk_ref, wv_ref, k_ref, v_ref):
    m = mem_ref[0].astype(BF16)
    k_ref[0, 0] = _dot(m, wk_ref[0]).astype(BF16)
    v_ref[0, 0] = _dot(m, wv_ref[0]).astype(BF16)


def _mem_kv(mem, wk, wv):
    depth = wk.shape[0]
    bsz, mlen, d = mem.shape
    out = jax.ShapeDtypeStruct((depth, bsz, mlen, d), BF16)
    return pl.pallas_call(
        _mem_kv_kernel,
        grid=(depth, bsz),
        in_specs=[pl.BlockSpec((1, mlen, d), lambda l, b: (b, 0, 0)),
                  pl.BlockSpec((1, d, d), lambda l, b: (l, 0, 0)),
                  pl.BlockSpec((1, d, d), lambda l, b: (l, 0, 0))],
        out_specs=[pl.BlockSpec((1, 1, mlen, d), lambda l, b: (l, b, 0, 0)),
                   pl.BlockSpec((1, 1, mlen, d), lambda l, b: (l, b, 0, 0))],
        out_shape=[out, out],
        compiler_params=_params(2),
        name="mem_kv",
    )(mem, wk, wv)


def _mixer_in_kernel(x_ref, csq_ref, csk_ref, w_in_ref, vng_ref, ws_ref, bs_ref, qng_ref,
                     wuq_ref, kvng_ref, wk_ref, wv_ref, convw_ref, poolw_ref, pscale_ref,
                     q_ref, k_ref, v_ref, y_ref,
                     conv_buf, pool_a, pool_b, pool_carry, *, tm):
    si = pl.program_id(1)

    @pl.when(si == 0)
    def _():
        pool_carry[...] = jnp.zeros((POOL_HALO, D_WIDTH), F32)
        conv_buf[0:SUBLANES, :] = jnp.zeros((SUBLANES, C_WIDTH), F32)

    xb = x_ref[0].astype(BF16)

    za = _dot(xb, w_in_ref[:, OFF_U:OFF_CQ])
    zb = _dot(xb, w_in_ref[:, OFF_CQ:OFF_BG])
    zcd = _dot(xb, w_in_ref[:, OFF_BG:P_EXT])

    za = jax.nn.gelu(za)
    u = za[:, :A_WIDTH]
    v = _rms(za[:, A_WIDTH:], vng_ref[...]).astype(BF16)
    cq = _rms(zb[:, :Q_LORA], qng_ref[...]).astype(BF16)
    ckv = _rms(zb[:, Q_LORA:Q_LORA + KV_LORA], kvng_ref[...]).astype(BF16)

    row = lax.broadcasted_iota(jnp.int32, (CHUNK, CHUNK), 0)
    col = lax.broadcasted_iota(jnp.int32, (CHUNK, CHUNK), 1)
    w_causal = [jnp.where(row >= col, ws_ref[h], jnp.zeros((), BF16)) for h in range(A_HEADS)]
    lane_head = lax.broadcasted_iota(jnp.int32, (CHUNK, A_WIDTH), 1) // A_HEAD_DIM
    for c in range(tm // CHUNK):
        rows = slice(c * CHUNK, (c + 1) * CHUNK)
        vc = v[rows]
        mixed = _dot(w_causal[0], vc)
        for h in range(1, A_HEADS):
            mixed = jnp.where(lane_head == h, _dot(w_causal[h], vc), mixed)
        y_ref[0, rows, 0:A_WIDTH] = (u[rows] * (mixed + bs_ref[...])).astype(BF16)

    qt = _dot_nt(wuq_ref[...], cq)
    kk = _dot(ckv, wk_ref[...])
    vt = _dot_nt(wv_ref[...], ckv)

    zd = zcd[:, 3 * C_WIDTH:]
    base = SUBLANES
    n = tm + POOL_HALO
    pool_a[0:base, :] = jnp.zeros((base, D_WIDTH), F32)
    pool_b[0:base, :] = jnp.zeros((base, D_WIDTH), F32)
    pool_a[base:base + POOL_HALO, :] = pool_carry[...]
    pool_a[base + POOL_HALO:base + n, :] = zd
    pool_carry[...] = zd[tm - POOL_HALO:, :]
    lane_d = lax.broadcasted_iota(jnp.int32, (n, D_WIDTH), 1)
    pool_b[base:base + n, :] = pool_a[base:base + n, :] + pool_a[base - 1:base - 1 + n, :]
    pool_a[base:base + n, :] = pool_b[base:base + n, :] + jnp.where(
        lane_d >= D_GROUP, pool_b[base - 2:base - 2 + n, :], 0.0)
    pool_b[base:base + n, :] = pool_a[base:base + n, :] + jnp.where(
        lane_d >= 2 * D_GROUP, pool_a[base - 4:base - 4 + n, :], 0.0)
    t0 = base + POOL_HALO
    lane_t = lax.broadcasted_iota(jnp.int32, (tm, D_WIDTH), 1)
    win_sum = pool_b[t0:t0 + tm, :] + jnp.where(
        lane_t >= 3 * D_GROUP, pool_b[t0 - 8:t0 - 8 + tm, :], 0.0)
    pos1 = si * tm + lax.broadcasted_iota(jnp.int32, (tm, D_WIDTH), 0) + 1
    window = jnp.left_shift(2, lane_t // D_GROUP)
    count = jnp.minimum(pos1, window).astype(F32)
    pooled = (win_sum / count - zd).astype(BF16)
    yd = _dot(pooled, poolw_ref[...]) * pscale_ref[...]
    y_ref[0, :, A_WIDTH + C_WIDTH:] = yd.astype(BF16)

    gh = zcd[:, C_WIDTH:2 * C_WIDTH] * zcd[:, 2 * C_WIDTH:3 * C_WIDTH]
    conv_buf[SUBLANES:SUBLANES + tm, :] = gh
    conv = (convw_ref[2:3, :] * gh
            + convw_ref[1:2, :] * conv_buf[SUBLANES - 1:SUBLANES - 1 + tm, :]
            + convw_ref[0:1, :] * conv_buf[SUBLANES - 2:SUBLANES - 2 + tm, :])
    y_ref[0, :, A_WIDTH:A_WIDTH + C_WIDTH] = (zcd[:, :C_WIDTH] * conv).astype(BF16)
    conv_buf[0:SUBLANES, :] = conv_buf[tm:tm + SUBLANES, :]

    csq = csq_ref[0, 0]
    for h in range(MLA_HEADS):
        q_ref[0, h, 0] = (qt[h * QK_DIM:(h + 1) * QK_DIM, :] * csq).astype(BF16)
    r4 = zb[:, Q_LORA + KV_LORA:] * csk_ref[0]
    kr = r4 + pltpu.roll(r4, MLA_ROPE, 1)
    lane = lax.broadcasted_iota(jnp.int32, (tm, QK_DIM), 1)
    kr = jnp.where(lane >= MLA_NOPE, kr, 0.0)
    for h in range(MLA_HEADS):
        k_ref[0, h] = (kk[:, h * QK_DIM:(h + 1) * QK_DIM] + kr).astype(BF16)
    ones_row = jnp.where(lax.broadcasted_iota(jnp.int32, (LANES, tm), 0) == MLA_V, 1.0, 0.0)
    for h in range(MLA_HEADS):
        v_ref[0, h, 0] = (vt[h * LANES:(h + 1) * LANES, :] + ones_row).astype(BF16)


def _mixer_in(x, csq, csk, w_in, vng, ws, bs, qng, wuq, kvng, wk, wv, convw, poolw, pscale,
              *, layer, tm):
    bsz, seq, d = x.shape
    tok = lambda w: pl.BlockSpec((1, tm, w), lambda b, s: (b, s, 0))
    head = lambda n, w: pl.BlockSpec((1, n, tm, w), lambda b, s: (b, 0, s, 0))
    head_t = pl.BlockSpec((1, MLA_HEADS, 1, LANES, tm), lambda b, s: (b, 0, s, 0, 0))
    shape_t = jax.ShapeDtypeStruct((bsz, MLA_HEADS, seq // tm, LANES, tm), BF16)
    return pl.pallas_call(
        functools.partial(_mixer_in_kernel, tm=tm),
        grid=(bsz, seq // tm),
        in_specs=[tok(d), pl.BlockSpec((1, 1, QK_DIM, tm), lambda b, s: (b, s, 0, 0)), tok(QK_DIM),
                  *[_layer_spec(w, layer) for w in (w_in, vng, ws, bs, qng, wuq, kvng, wk, wv,
                                                     convw, poolw, pscale)]],
        out_specs=[head_t, head(MLA_HEADS, QK_DIM), head_t, tok(A_WIDTH + C_WIDTH + D_WIDTH)],
        out_shape=[shape_t,
                   jax.ShapeDtypeStruct((bsz, MLA_HEADS, seq, QK_DIM), BF16),
                   shape_t,
                   jax.ShapeDtypeStruct((bsz, seq, A_WIDTH + C_WIDTH + D_WIDTH), BF16)],
        scratch_shapes=[pltpu.VMEM((tm + SUBLANES, C_WIDTH), F32),
                        pltpu.VMEM((tm + POOL_HALO + SUBLANES, D_WIDTH), F32),
                        pltpu.VMEM((tm + POOL_HALO + SUBLANES, D_WIDTH), F32),
                        pltpu.VMEM((POOL_HALO, D_WIDTH), F32)],
        compiler_params=_params(2),
        name="mixer_in",
    )(x, csq, csk, w_in, vng, ws, bs, qng, wuq, kvng, wk, wv, convw, poolw, pscale)


def _mla_attn_kernel(qt_ref, k_ref, vt_ref, o_ref, m_ref, acc_ref, *, tq, hg):
    qi = pl.program_id(2)
    key = lax.broadcasted_iota(jnp.int32, (tq, tq), 0)
    qry = lax.broadcasted_iota(jnp.int32, (tq, tq), 1)
    m_ref[...] = jnp.full(m_ref.shape, NEG_BIG, F32)
    acc_ref[...] = jnp.zeros(acc_ref.shape, F32)

    def step(j, masked):
        def scores(hh):
            return _dot(k_ref[0, hh, pl.ds(pl.multiple_of(j * tq, tq), tq), :], qt_ref[0, hh, 0])

        def weighted_values(hh, pt, rescale):
            pv = _dot(vt_ref[0, hh, j], pt)
            acc_ref[hh] = rescale * acc_ref[hh] + pv

        st_next = scores(0)
        pending = None
        for hh in range(hg):
            st = st_next
            if hh + 1 < hg:
                st_next = scores(hh + 1)
            if pending is not None:
                weighted_values(*pending)
            if masked:
                st = jnp.where(key <= qry, st, NEG_BIG)
            m_old = m_ref[hh]
            m_new = jnp.maximum(m_old, jnp.max(st, axis=0, keepdims=True))
            m_ref[hh] = m_new
            pending = (hh, jnp.exp2(st - m_new).astype(BF16), jnp.exp2(m_old - m_new))
        weighted_values(*pending)

    def body(j, carry):
        step(j, False)
        return carry

    lax.fori_loop(0, qi, body, 0)
    step(qi, True)
    for pr in range(hg // 2):
        halves = []
        for hh in (2 * pr, 2 * pr + 1):
            acc = acc_ref[hh]
            halves.append(acc[:MLA_V] / acc[MLA_V:MLA_V + 1])
        o_ref[0, :, pr * LANES:(pr + 1) * LANES] = jnp.concatenate(halves, axis=0).T.astype(BF16)


def _mla_attn(qt, k, vt, *, hg):
    bsz, heads, n_tiles, _, tq = qt.shape
    seq = n_tiles * tq
    return pl.pallas_call(
        functools.partial(_mla_attn_kernel, tq=tq, hg=hg),
        grid=(bsz, heads // hg, n_tiles),
        in_specs=[pl.BlockSpec((1, hg, 1, QK_DIM, tq), lambda b, g, i: (b, g, i, 0, 0)),
                  pl.BlockSpec((1, hg, seq, QK_DIM), lambda b, g, i: (b, g, 0, 0)),
                  pl.BlockSpec((1, hg, n_tiles, LANES, tq), lambda b, g, i: (b, g, 0, 0, 0))],
        out_specs=pl.BlockSpec((1, tq, hg * MLA_V), lambda b, g, i: (b, i, g)),
        out_shape=jax.ShapeDtypeStruct((bsz, seq, MLA_WIDTH), BF16),
        scratch_shapes=[pltpu.VMEM((hg, 1, tq), F32), pltpu.VMEM((hg, LANES, tq), F32)],
        compiler_params=_params(3),
        name="mla_attn",
    )(qt, k, vt)


def _route_t(logits_t, bias_t):
    scores = jax.nn.sigmoid(logits_t)
    sel_all = scores + bias_t
    sel = [sel_all[e:e + 1] for e in range(N_EXPERTS)]
    group_score = []
    for g in range(N_GROUPS):
        v = sel[g * EXPERTS_PER_GROUP:(g + 1) * EXPERTS_PER_GROUP]
        best = None
        for i in range(EXPERTS_PER_GROUP):
            for j in range(i + 1, EXPERTS_PER_GROUP):
                best = v[i] + v[j] if best is None else jnp.maximum(best, v[i] + v[j])
        group_score.append(best)
    top, top_idx = group_score[0], jnp.zeros(group_score[0].shape, jnp.int32)
    for g in range(1, N_GROUPS):
        better = group_score[g] > top
        top = jnp.where(better, group_score[g], top)
        top_idx = jnp.where(better, g, top_idx)
    member = [jnp.where(top_idx == g, 1.0, 0.0) for g in range(N_GROUPS)]
    weights = []
    for e in range(N_EXPERTS):
        g = e // EXPERTS_PER_GROUP
        rank = jnp.zeros(sel[e].shape, jnp.int32)
        for j in range(g * EXPERTS_PER_GROUP, (g + 1) * EXPERTS_PER_GROUP):
            if j != e:
                rank = rank + jnp.where(sel[j] > sel[e], 1,
                                        jnp.where(sel[j] == sel[e], 1 if j < e else 0, 0))
        weights.append(jnp.where(rank < TOP_K, member[g], 0.0) * scores[e:e + 1])
    total = weights[0]
    for w in weights[1:]:
        total = total + w
    return [w / total for w in weights], member


def _post_attn_kernel(yacd_ref, yb_ref, x_ref, wo_acd_ref, wo_b_ref, g1_ref, b1_ref, wq_ref,
                      km_ref, vm_ref, wo_ref, g2_ref, b2_ref, rw_ref,
                      x2_ref, logits_ref, *, tm, alpha, n_chains):
    def chain(rows):
        h = _dot(yacd_ref[0, rows], wo_acd_ref[...]) + _dot(yb_ref[0, rows], wo_b_ref[...])
        yield
        x1 = _layer_norm(alpha * x_ref[0, rows] + h, g1_ref[...], b1_ref[...])
        q = _dot(x1.astype(BF16), wq_ref[...])
        yield
        heads = []
        for hd in range(X_HEADS):
            cols = slice(hd * X_HEAD_DIM, (hd + 1) * X_HEAD_DIM)
            s = _dot_nt(q[:, cols].astype(BF16), km_ref[0, :, cols])
            e = jnp.exp(s - jnp.max(s, axis=-1, keepdims=True))
            p = e / jnp.sum(e, axis=-1, keepdims=True)
            heads.append(_dot(p.astype(BF16), vm_ref[0, :, cols]).astype(BF16))
            yield
        h2 = _dot(jnp.concatenate(heads, axis=1), wo_ref[...])
        yield
        x2 = _layer_norm(alpha * x1 + h2, g2_ref[...], b2_ref[...])
        x2_ref[0, rows] = x2
        x_hi = x2.astype(BF16)
        x_lo = (x2 - x_hi.astype(F32)).astype(BF16)
        hl = _dot_nt(rw_ref[...], x_hi)
        lo = _dot_nt(rw_ref[0:N_EXPERTS, :], x_lo)
        logits_ref[0, :, rows] = hl[:N_EXPERTS] + hl[N_EXPERTS:] + lo

    rows_per = tm // n_chains
    _run_interleaved([chain(slice(c * rows_per, (c + 1) * rows_per)) for c in range(n_chains)])


def _post_attn(yacd, yb, x, wo_acd, wo_b, g1, b1, wq, km, vm, wo, g2, b2, rw,
               *, layer, tm, alpha):
    bsz, seq, d = x.shape
    tok = lambda w: pl.BlockSpec((1, tm, w), lambda b, s: (b, s, 0))
    memspec = pl.BlockSpec((None, 1) + km.shape[2:], lambda b, s: (layer, b, 0, 0))
    per_layer = lambda w: _layer_spec(w, layer)
    return pl.pallas_call(
        functools.partial(_post_attn_kernel, tm=tm, alpha=alpha, n_chains=2),
        grid=(bsz, seq // tm),
        in_specs=[tok(yacd.shape[-1]), tok(yb.shape[-1]), tok(d),
                  per_layer(wo_acd), per_layer(wo_b), per_layer(g1), per_layer(b1), per_layer(wq),
                  memspec, memspec, per_layer(wo), per_layer(g2), per_layer(b2),
                  _const_spec(rw.shape)],
        out_specs=[tok(d), pl.BlockSpec((1, N_EXPERTS, tm), lambda b, s: (b, 0, s))],
        out_shape=[jax.ShapeDtypeStruct((bsz, seq, d), F32),
                   jax.ShapeDtypeStruct((bsz, N_EXPERTS, seq), F32)],
        compiler_params=_params(2),
        name="post_attn",
    )(yacd, yb, x, wo_acd, wo_b, g1, b1, wq, km, vm, wo, g2, b2, rw)


def _dot_tn(a, b):
    return lax.dot_general(a, b, (((0,), (0,)), ((), ())), preferred_element_type=F32)


def _moe_kernel(x_ref, logits_ref, rb_ref, before_ref, wg_ref, wu_ref, wd_ref, g3_ref, b3_ref, o_ref,
                xb_ref, pos_ref, memb_ref, gsplit_ref, acc_ref, *, alpha, tm, rows):
    group_w = EXPERTS_PER_GROUP * D_FF
    x = x_ref[0]
    xb_ref[...] = x.astype(BF16)
    gates, member = _route_t(logits_ref[0], rb_ref[...])
    memb = jnp.concatenate(member + [jnp.zeros((ROUTE_ROWS - N_GROUPS, tm), F32)], axis=0)
    memb_ref[...] = memb
    pos_ref[...] = _dot(memb.astype(BF16), before_ref[...])
    gate_rows = jnp.concatenate(gates, axis=0)
    g_hi = gate_rows.astype(BF16)
    gsplit_ref[...] = jnp.concatenate([g_hi, (gate_rows - g_hi.astype(F32)).astype(BF16)], axis=0)
    acc_ref[...] = jnp.zeros(acc_ref.shape, F32)

    for g in range(N_GROUPS):
        n_tok = jnp.sum(member[g]).astype(jnp.int32)

        def chunk(ci, carry, g=g):
            slot = (lax.broadcasted_iota(jnp.int32, (rows, tm), 0) + ci * rows).astype(F32)
            p = jnp.where(pos_ref[g:g + 1, :] == slot, memb_ref[g:g + 1, :], 0.0).astype(BF16)
            xg = _dot(p, xb_ref[...]).astype(BF16)
            gg = _dot_nt(p, gsplit_ref[...])
            hs = []
            for j in range(EXPERTS_PER_GROUP):
                e = g * EXPERTS_PER_GROUP + j
                cols = slice(e * D_FF, (e + 1) * D_FF)
                gate = gg[:, e:e + 1] + gg[:, N_EXPERTS + e:N_EXPERTS + e + 1]
                hg = _dot(xg, wg_ref[:, cols])
                hu = _dot(xg, wu_ref[:, cols])
                hs.append((jax.nn.silu(hg) * hu * gate).astype(BF16))
            y = _dot(jnp.concatenate(hs, axis=1), wd_ref[g * group_w:(g + 1) * group_w, :])
            acc_ref[...] += _dot_tn(p, y.astype(BF16))
            return carry

        lax.fori_loop(0, (n_tok + rows - 1) // rows, chunk, 0)

    o_ref[0] = _layer_norm(alpha * x + acc_ref[...], g3_ref[...], b3_ref[...])


def _moe(x, logits_t, rb_t, before, wg, wu, wd, g3, b3, *, layer, tm, alpha):
    bsz, seq, d = x.shape
    tok = lambda w: pl.BlockSpec((1, tm, w), lambda b, s: (b, s, 0))
    resident = lambda w: _layer_spec(w, layer, single_buffer=True)
    return pl.pallas_call(
        functools.partial(_moe_kernel, alpha=alpha, tm=tm, rows=MOE_ROWS),
        grid=(bsz, seq // tm),
        in_specs=[tok(d), pl.BlockSpec((1, N_EXPERTS, tm), lambda b, s: (b, 0, s)),
                  _const_spec(rb_t.shape), _const_spec(before.shape),
                  resident(wg), resident(wu), resident(wd),
                  _layer_spec(g3, layer), _layer_spec(b3, layer)],
        out_specs=tok(d),
        out_shape=jax.ShapeDtypeStruct((bsz, seq, d), F32),
        scratch_shapes=[pltpu.VMEM((tm, d), BF16),
                        pltpu.VMEM((ROUTE_ROWS, tm), F32),
                        pltpu.VMEM((ROUTE_ROWS, tm), F32),
                        pltpu.VMEM((2 * N_EXPERTS, tm), BF16),
                        pltpu.VMEM((tm, d), F32)],
        compiler_params=_params(2),
        name="moe",
    )(x, logits_t, rb_t, before, wg, wu, wd, g3, b3)


def _rot_cols(w):
    half = w.shape[-1] // 2
    return jnp.concatenate([-w[..., half:], w[..., :half]], axis=-1)


def _prep_w_in(w_in):
    u, v = w_in[..., 0:256], w_in[..., 256:512]
    cq, ckv, kr = w_in[..., 512:768], w_in[..., 768:896], w_in[..., 896:928]
    rest = w_in[..., 928:]
    rope4 = jnp.concatenate([kr, _rot_cols(kr), kr, _rot_cols(kr)], axis=-1)
    return jnp.concatenate([u, v, cq, ckv, rope4, rest], axis=-1).astype(BF16)


def _prep_w_uq(w_uq):
    depth = w_uq.shape[0]
    w = w_uq.reshape(depth, Q_LORA, MLA_HEADS, MLA_NOPE + MLA_ROPE)
    nope, rope = w[..., :MLA_NOPE], w[..., MLA_NOPE:]
    ext = jnp.concatenate([nope, rope, _rot_cols(rope)], axis=-1)
    return ext.reshape(depth, Q_LORA, MLA_HEADS * QK_DIM).astype(BF16)


def _prep_w_ukv(w_ukv):
    depth = w_ukv.shape[0]
    w = w_ukv.reshape(depth, KV_LORA, MLA_HEADS, MLA_NOPE + MLA_V)
    k_nope, v = w[..., :MLA_NOPE], w[..., MLA_NOPE:]
    wk = jnp.concatenate([k_nope, jnp.zeros_like(k_nope)], axis=-1)
    wv = jnp.concatenate([v, jnp.zeros_like(v)], axis=-1)
    return (wk.reshape(depth, KV_LORA, MLA_HEADS * QK_DIM).astype(BF16),
            wv.reshape(depth, KV_LORA, MLA_HEADS * LANES).astype(BF16))


def _prep_pool_w(pool_w):
    depth, groups = pool_w.shape[:2]
    eye = jnp.eye(groups, dtype=pool_w.dtype)
    bd = jnp.einsum('lgcd,gh->lgchd', pool_w, eye)
    return bd.reshape(depth, D_WIDTH, D_WIDTH).astype(BF16)


def kernel(x, mem, positions, w_in, gmlp_v_norm_g, gmlp_w_s, gmlp_b_s, mla_q_norm_g, mla_w_uq,
           mla_kv_norm_g, mla_w_ukv, conv_w, pool_w, pool_scale, w_out, ln1_g, ln1_b,
           xattn_wq, xattn_wk, xattn_wv, xattn_wo, ln2_g, ln2_b, router_w, router_b,
           moe_w_gate, moe_w_up, moe_w_down, ln3_g, ln3_b):
    depth = w_in.shape[0]
    alpha = (2 * depth) ** 0.25
    tm = 512
    bsz, seq, _ = x.shape

    inv_freq = ROPE_BASE ** (-jnp.arange(0, MLA_ROPE, 2, dtype=F32) / MLA_ROPE)
    ang = positions.astype(F32)[..., None] * inv_freq
    cos2 = jnp.tile(jnp.cos(ang), (1, 1, 2))
    sin2 = jnp.tile(jnp.sin(ang), (1, 1, 2))
    scale = (MLA_NOPE + MLA_ROPE) ** -0.5 * LOG2_E
    csq = scale * jnp.concatenate([jnp.ones(cos2.shape[:2] + (MLA_NOPE,), F32), cos2, sin2], -1)
    csq = jnp.swapaxes(csq.reshape(bsz, seq // tm, tm, QK_DIM), 2, 3)
    csk = jnp.concatenate([cos2, sin2, cos2, sin2], axis=-1)

    row = lambda a: a[:, None, :]
    w_in_e = _prep_w_in(w_in)
    w_uq_e = jnp.swapaxes(_prep_w_uq(mla_w_uq), 1, 2)
    w_k_e, w_v_e = _prep_w_ukv(mla_w_ukv)
    w_v_e = jnp.swapaxes(w_v_e, 1, 2)
    ws = gmlp_w_s.astype(BF16)
    bs = jnp.repeat(jnp.swapaxes(gmlp_b_s, 1, 2), A_HEAD_DIM, axis=2)
    pool_bd = _prep_pool_w(pool_w)
    wo_acd = jnp.concatenate([w_out[:, :A_WIDTH], w_out[:, A_WIDTH + MLA_WIDTH:]], axis=1).astype(BF16)
    wo_b = w_out[:, A_WIDTH:A_WIDTH + MLA_WIDTH].astype(BF16)
    wq = (xattn_wq * (X_HEAD_DIM ** -0.5)).astype(BF16)
    wo = xattn_wo.astype(BF16)
    rw_t = router_w.T
    rw_hi = rw_t.astype(BF16)
    rw = jnp.concatenate([rw_hi, (rw_t - rw_hi.astype(F32)).astype(BF16)], axis=0)
    rb_t = jnp.broadcast_to(router_b[:, None], (N_EXPERTS, tm))
    before = jnp.triu(jnp.ones((tm, tm), BF16), 1)
    cat_experts = lambda w: jnp.swapaxes(w, 1, 2).reshape(depth, D_MODEL, N_EXPERTS * D_FF)
    wg = cat_experts(moe_w_gate).astype(BF16)
    wu = cat_experts(moe_w_up).astype(BF16)
    wd = moe_w_down.reshape(depth, N_EXPERTS * D_FF, D_MODEL).astype(BF16)

    km, vm = _mem_kv(mem, xattn_wk.astype(BF16), xattn_wv.astype(BF16))

    for l in range(depth):
        q, k, v, yacd = _mixer_in(
            x, csq, csk, w_in_e, row(gmlp_v_norm_g), ws, bs, row(mla_q_norm_g),
            w_uq_e, row(mla_kv_norm_g), w_k_e, w_v_e, conv_w, pool_bd,
            row(pool_scale), layer=l, tm=tm)
        yb = _mla_attn(q, k, v, hg=MLA_HEADS)
        x2, logits = _post_attn(
            yacd, yb, x, wo_acd, wo_b, row(ln1_g), row(ln1_b), wq, km, vm,
            wo, row(ln2_g), row(ln2_b), rw, layer=l, tm=tm, alpha=alpha)
        x = _moe(x2, logits, rb_t, before, wg, wu, wd, row(ln3_g), row(ln3_b),
                 layer=l, tm=tm, alpha=alpha)
    return x
```

```python
import functools

import jax
import jax.numpy as jnp
from jax import lax
from jax.experimental import pallas as pl
from jax.experimental.pallas import tpu as pltpu

F32 = jnp.float32
BF16 = jnp.bfloat16

D_MODEL = 1024
A_HEADS, A_HEAD_DIM, CHUNK = 4, 64, 128
A_WIDTH = A_HEADS * A_HEAD_DIM
MLA_HEADS, MLA_NOPE, MLA_ROPE, MLA_V = 8, 64, 32, 64
Q_LORA, KV_LORA = 256, 128
MLA_WIDTH = MLA_HEADS * MLA_V
ROPE_BASE = 10000.0
C_WIDTH, CONV_W = 256, 3
D_WIDTH = 256
POOL_WINDOWS = (2, 4, 8, 16)
D_GROUP = D_WIDTH // len(POOL_WINDOWS)
X_HEADS = 4
X_HEAD_DIM = D_MODEL // X_HEADS
N_EXPERTS, N_GROUPS, TOP_K = 16, 4, 2
EXPERTS_PER_GROUP = N_EXPERTS // N_GROUPS
D_FF = 256
EPS = 1e-6

LANES = 128
SUBLANES = 8
VMEM_LIMIT = 56 * 1024 * 1024

OFF_U, OFF_CQ, OFF_BG, P_EXT = 0, 512, 1024, 2048
QK_DIM = 128
POOL_HALO = 16
MOE_ROWS = 160
ROUTE_ROWS = 16
NEG_BIG = -1e30
LOG2_E = 1.4426950408889634


def _dot(a, b):
    return jnp.dot(a, b, preferred_element_type=F32)


def _dot_nt(a, b):
    return lax.dot_general(a, b, (((1,), (1,)), ((), ())), preferred_element_type=F32)


def _rms(x, g):
    return x * lax.rsqrt(jnp.mean(x * x, axis=-1, keepdims=True) + EPS) * g


def _layer_norm(x, g, b):
    mu = jnp.mean(x, axis=-1, keepdims=True)
    xc = x - mu
    var = jnp.mean(xc * xc, axis=-1, keepdims=True)
    return xc * lax.rsqrt(var + EPS) * g + b


def _params(n_grid):
    return pltpu.CompilerParams(dimension_semantics=("arbitrary",) * n_grid,
                                vmem_limit_bytes=VMEM_LIMIT)


def _run_interleaved(chains):
    live = list(chains)
    while live:
        for c in list(live):
            try:
                next(c)
            except StopIteration:
                live.remove(c)


def _const_spec(shape):
    zeros = (0,) * len(shape)
    return pl.BlockSpec(shape, lambda *_: zeros)


def _layer_spec(stacked, layer, single_buffer=False):
    index = (layer,) + (0,) * (stacked.ndim - 1)
    mode = dict(pipeline_mode=pl.Buffered(1)) if single_buffer else {}
    return pl.BlockSpec((None,) + stacked.shape[1:], lambda *_: index, **mode)


def _mem_kv_kernel(mem_ref, wk_ref, wv_ref, k_ref, v_ref):
    m = mem_ref[0].astype(BF16)
    k_ref[0, 0] = _dot(m, wk_ref[0]).astype(BF16)
    v_ref[0, 0] = _dot(m, wv_ref[0]).astype(BF16)


def _mem_kv(mem, wk, wv):
    depth = wk.shape[0]
    bsz, mlen, d = mem.shape
    out = jax.ShapeDtypeStruct((depth, bsz, mlen, d), BF16)
    return pl.pallas_call(
        _mem_kv_kernel,
        grid=(depth, bsz),
        in_specs=[pl.BlockSpec((1, mlen, d), lambda l, b: (b, 0, 0)),
                  pl.BlockSpec((1, d, d), lambda l, b: (l, 0, 0)),
                  pl.BlockSpec((1, d, d), lambda l, b: (l, 0, 0))],
        out_specs=[pl.BlockSpec((1, 1, mlen, d), lambda l, b: (l, b, 0, 0)),
                   pl.BlockSpec((1, 1, mlen, d), lambda l, b: (l, b, 0, 0))],
        out_shape=[out, out],
        compiler_params=_params(2),
        name="mem_kv",
    )(mem, wk, wv)


def _mixer_in_kernel(x_ref, csq_ref, csk_ref, w_in_ref, vng_ref, ws_ref, bs_ref, qng_ref,
                     wuq_ref, kvng_ref, wk_ref, wv_ref, convw_ref, poolw_ref, pscale_ref,
                     q_ref, k_ref, v_ref, y_ref,
                     conv_buf, pool_a, pool_b, pool_carry, *, tm):
    si = pl.program_id(1)

    @pl.when(si == 0)
    def _():
        pool_carry[...] = jnp.zeros((POOL_HALO, D_WIDTH), F32)
        conv_buf[0:SUBLANES, :] = jnp.zeros((SUBLANES, C_WIDTH), F32)

    xb = x_ref[0].astype(BF16)

    za = _dot(xb, w_in_ref[:, OFF_U:OFF_CQ])
    zb = _dot(xb, w_in_ref[:, OFF_CQ:OFF_BG])
    zcd = _dot(xb, w_in_ref[:, OFF_BG:P_EXT])

    za = jax.nn.gelu(za)
    u = za[:, :A_WIDTH]
    v = _rms(za[:, A_WIDTH:], vng_ref[...]).astype(BF16)
    cq = _rms(zb[:, :Q_LORA], qng_ref[...]).astype(BF16)
    ckv = _rms(zb[:, Q_LORA:Q_LORA + KV_LORA], kvng_ref[...]).astype(BF16)

    row = lax.broadcasted_iota(jnp.int32, (CHUNK, CHUNK), 0)
    col = lax.broadcasted_iota(jnp.int32, (CHUNK, CHUNK), 1)
    w_causal = [jnp.where(row >= col, ws_ref[h], jnp.zeros((), BF16)) for h in range(A_HEADS)]
    lane_head = lax.broadcasted_iota(jnp.int32, (CHUNK, A_WIDTH), 1) // A_HEAD_DIM
    for c in range(tm // CHUNK):
        rows = slice(c * CHUNK, (c + 1) * CHUNK)
        vc = v[rows]
        mixed = _dot(w_causal[0], vc)
        for h in range(1, A_HEADS):
            mixed = jnp.where(lane_head == h, _dot(w_causal[h], vc), mixed)
        y_ref[0, rows, 0:A_WIDTH] = (u[rows] * (mixed + bs_ref[...])).astype(BF16)

    qt = _dot_nt(wuq_ref[...], cq)
    kk = _dot(ckv, wk_ref[...])
    vt = _dot_nt(wv_ref[...], ckv)

    zd = zcd[:, 3 * C_WIDTH:]
    base = SUBLANES
    n = tm + POOL_HALO
    pool_a[0:base, :] = jnp.zeros((base, D_WIDTH), F32)
    pool_b[0:base, :] = jnp.zeros((base, D_WIDTH), F32)
    pool_a[base:base + POOL_HALO, :] = pool_carry[...]
    pool_a[base + POOL_HALO:base + n, :] = zd
    pool_carry[...] = zd[tm - POOL_HALO:, :]
    lane_d = lax.broadcasted_iota(jnp.int32, (n, D_WIDTH), 1)
    pool_b[base:base + n, :] = pool_a[base:base + n, :] + pool_a[base - 1:base - 1 + n, :]
    pool_a[base:base + n, :] = pool_b[base:base + n, :] + jnp.where(
        lane_d >= D_GROUP, pool_b[base - 2:base - 2 + n, :], 0.0)
    pool_b[base:base + n, :] = pool_a[base:base + n, :] + jnp.where(
        lane_d >= 2 * D_GROUP, pool_a[base - 4:base - 4 + n, :], 0.0)
    t0 = base + POOL_HALO
    lane_t = lax.broadcasted_iota(jnp.int32, (tm, D_WIDTH), 1)
    win_sum = pool_b[t0:t0 + tm, :] + jnp.where(
        lane_t >= 3 * D_GROUP, pool_b[t0 - 8:t0 - 8 + tm, :], 0.0)
    pos1 = si * tm + lax.broadcasted_iota(jnp.int32, (tm, D_WIDTH), 0) + 1
    window = jnp.left_shift(2, lane_t // D_GROUP)
    count = jnp.minimum(pos1, window).astype(F32)
    pooled = (win_sum / count - zd).astype(BF16)
    yd = _dot(pooled, poolw_ref[...]) * pscale_ref[...]
    y_ref[0, :, A_WIDTH + C_WIDTH:] = yd.astype(BF16)

    gh = zcd[:, C_WIDTH:2 * C_WIDTH] * zcd[:, 2 * C_WIDTH:3 * C_WIDTH]
    conv_buf[SUBLANES:SUBLANES + tm, :] = gh
    conv = (convw_ref[2:3, :] * gh
            + convw_ref[1:2, :] * conv_buf[SUBLANES - 1:SUBLANES - 1 + tm, :]
            + convw_ref[0:1, :] * conv_buf[SUBLANES - 2:SUBLANES - 2 + tm, :])
    y_ref[0, :, A_WIDTH:A_WIDTH + C_WIDTH] = (zcd[:, :C_WIDTH] * conv).astype(BF16)
    conv_buf[0:SUBLANES, :] = conv_buf[tm:tm + SUBLANES, :]

    csq = csq_ref[0, 0]
    for h in range(MLA_HEADS):
        q_ref[0, h, 0] = (qt[h * QK_DIM:(h + 1) * QK_DIM, :] * csq).astype(BF16)
    r4 = zb[:, Q_LORA + KV_LORA:] * csk_ref[0]
    kr = r4 + pltpu.roll(r4, MLA_ROPE, 1)
    lane = lax.broadcasted_iota(jnp.int32, (tm, QK_DIM), 1)
    kr = jnp.where(lane >= MLA_NOPE, kr, 0.0)
    for h in range(MLA_HEADS):
        k_ref[0, h] = (kk[:, h * QK_DIM:(h + 1) * QK_DIM] + kr).astype(BF16)
    ones_row = jnp.where(lax.broadcasted_iota(jnp.int32, (LANES, tm), 0) == MLA_V, 1.0, 0.0)
    for h in range(MLA_HEADS):
        v_ref[0, h, 0] = (vt[h * LANES:(h + 1) * LANES, :] + ones_row).astype(BF16)


def _mixer_in(x, csq, csk, w_in, vng, ws, bs, qng, wuq, kvng, wk, wv, convw, poolw, pscale,
              *, layer, tm):
    bsz, seq, d = x.shape
    tok = lambda w: pl.BlockSpec((1, tm, w), lambda b, s: (b, s, 0))
    head = lambda n, w: pl.BlockSpec((1, n, tm, w), lambda b, s: (b, 0, s, 0))
    head_t = pl.BlockSpec((1, MLA_HEADS, 1, LANES, tm), lambda b, s: (b, 0, s, 0, 0))
    shape_t = jax.ShapeDtypeStruct((bsz, MLA_HEADS, seq // tm, LANES, tm), BF16)
    return pl.pallas_call(
        functools.partial(_mixer_in_kernel, tm=tm),
        grid=(bsz, seq // tm),
        in_specs=[tok(d), pl.BlockSpec((1, 1, QK_DIM, tm), lambda b, s: (b, s, 0, 0)), tok(QK_DIM),
                  *[_layer_spec(w, layer) for w in (w_in, vng, ws, bs, qng, wuq, kvng, wk, wv,
                                                     convw, poolw, pscale)]],
        out_specs=[head_t, head(MLA_HEADS, QK_DIM), head_t, tok(A_WIDTH + C_WIDTH + D_WIDTH)],
        out_shape=[shape_t,
                   jax.ShapeDtypeStruct((bsz, MLA_HEADS, seq, QK_DIM), BF16),
                   shape_t,
                   jax.ShapeDtypeStruct((bsz, seq, A_WIDTH + C_WIDTH + D_WIDTH), BF16)],
        scratch_shapes=[pltpu.VMEM((tm + SUBLANES, C_WIDTH), F32),
                        pltpu.VMEM((tm + POOL_HALO + SUBLANES, D_WIDTH), F32),
                        pltpu.VMEM((tm + POOL_HALO + SUBLANES, D_WIDTH), F32),
                        pltpu.VMEM((POOL_HALO, D_WIDTH), F32)],
        compiler_params=_params(2),
        name="mixer_in",
    )(x, csq, csk, w_in, vng, ws, bs, qng, wuq, kvng, wk, wv, convw, poolw, pscale)


def _mla_attn_kernel(qt_ref, k_ref, vt_ref, o_ref, m_ref, acc_ref, *, tq, hg):
    qi = pl.program_id(2)
    half = tq // 2
    m_ref[...] = jnp.full(m_ref.shape, NEG_BIG, F32)
    acc_ref[...] = jnp.zeros(acc_ref.shape, F32)

    def step(j, diagonal):
        base = pl.multiple_of(j * tq, tq)

        def scores(hh):
            if not diagonal:
                return (_dot(k_ref[0, hh, pl.ds(base, tq), :], qt_ref[0, hh, 0]),)
            return (_dot(k_ref[0, hh, pl.ds(base, half), :], qt_ref[0, hh, 0]),
                    _dot(k_ref[0, hh, pl.ds(base + half, half), :], qt_ref[0, hh, 0, :, half:]))

        def softmax(hh, st):
            m_old = m_ref[hh]
            if not diagonal:
                m_new = jnp.maximum(m_old, jnp.max(st[0], axis=0, keepdims=True))
                probs = (jnp.exp2(st[0] - m_new).astype(BF16),)
            else:
                def causal(s):
                    key = lax.broadcasted_iota(jnp.int32, s.shape, 0)
                    qry = lax.broadcasted_iota(jnp.int32, s.shape, 1)
                    return jnp.where(key <= qry, s, NEG_BIG)
                top, bottom = causal(st[0]), causal(st[1])
                later = jnp.concatenate([jnp.full((1, half), NEG_BIG, F32),
                                         jnp.max(bottom, axis=0, keepdims=True)], axis=1)
                m_new = jnp.maximum(m_old, jnp.maximum(jnp.max(top, axis=0, keepdims=True), later))
                probs = (jnp.exp2(top - m_new).astype(BF16),
                         jnp.exp2(bottom - m_new[:, half:]).astype(BF16))
            m_ref[hh] = m_new
            return hh, probs, jnp.exp2(m_old - m_new)

        def weighted_values(hh, probs, rescale):
            if not diagonal:
                pv = _dot(vt_ref[0, hh, j], probs[0])
            else:
                later = _dot(vt_ref[0, hh, j, :, half:], probs[1])
                pv = (_dot(vt_ref[0, hh, j, :, :half], probs[0])
                      + jnp.concatenate([jnp.zeros((LANES, half), F32), later], axis=1))
            acc_ref[hh] = rescale * acc_ref[hh] + pv

        st_next = scores(0)
        pending = None
        for hh in range(hg):
            st = st_next
            if hh + 1 < hg:
                st_next = scores(hh + 1)
            if pending is not None:
                weighted_values(*pending)
            pending = softmax(hh, st)
        weighted_values(*pending)

    def body(j, carry):
        step(j, False)
        return carry

    lax.fori_loop(0, qi, body, 0)
    step(qi, True)
    for pr in range(hg // 2):
        halves = []
        for hh in (2 * pr, 2 * pr + 1):
            acc = acc_ref[hh]
            halves.append(acc[:MLA_V] / acc[MLA_V:MLA_V + 1])
        o_ref[0, :, pr * LANES:(pr + 1) * LANES] = jnp.concatenate(halves, axis=0).T.astype(BF16)


def _mla_attn(qt, k, vt, *, hg):
    bsz, heads, n_tiles, _, tq = qt.shape
    seq = n_tiles * tq
    return pl.pallas_call(
        functools.partial(_mla_attn_kernel, tq=tq, hg=hg),
        grid=(bsz, heads // hg, n_tiles),
        in_specs=[pl.BlockSpec((1, hg, 1, QK_DIM, tq), lambda b, g, i: (b, g, i, 0, 0)),
                  pl.BlockSpec((1, hg, seq, QK_DIM), lambda b, g, i: (b, g, 0, 0)),
                  pl.BlockSpec((1, hg, n_tiles, LANES, tq), lambda b, g, i: (b, g, 0, 0, 0))],
        out_specs=pl.BlockSpec((1, tq, hg * MLA_V), lambda b, g, i: (b, i, g)),
        out_shape=jax.ShapeDtypeStruct((bsz, seq, MLA_WIDTH), BF16),
        scratch_shapes=[pltpu.VMEM((hg, 1, tq), F32), pltpu.VMEM((hg, LANES, tq), F32)],
        compiler_params=_params(3),
        name="mla_attn",
    )(qt, k, vt)


def _route_t(logits_t, bias_t):
    scores = jax.nn.sigmoid(logits_t)
    sel_all = scores + bias_t
    sel = [sel_all[e:e + 1] for e in range(N_EXPERTS)]
    group_score = []
    for g in range(N_GROUPS):
        v = sel[g * EXPERTS_PER_GROUP:(g + 1) * EXPERTS_PER_GROUP]
        best = None
        for i in range(EXPERTS_PER_GROUP):
            for j in range(i + 1, EXPERTS_PER_GROUP):
                best = v[i] + v[j] if best is None else jnp.maximum(best, v[i] + v[j])
        group_score.append(best)
    top, top_idx = group_score[0], jnp.zeros(group_score[0].shape, jnp.int32)
    for g in range(1, N_GROUPS):
        better = group_score[g] > top
        top = jnp.where(better, group_score[g], top)
        top_idx = jnp.where(better, g, top_idx)
    member = [jnp.where(top_idx == g, 1.0, 0.0) for g in range(N_GROUPS)]
    weights = []
    for e in range(N_EXPERTS):
        g = e // EXPERTS_PER_GROUP
        rank = jnp.zeros(sel[e].shape, jnp.int32)
        for j in range(g * EXPERTS_PER_GROUP, (g + 1) * EXPERTS_PER_GROUP):
            if j != e:
                rank = rank + jnp.where(sel[j] > sel[e], 1,
                                        jnp.where(sel[j] == sel[e], 1 if j < e else 0, 0))
        weights.append(jnp.where(rank < TOP_K, member[g], 0.0) * scores[e:e + 1])
    total = weights[0]
    for w in weights[1:]:
        total = total + w
    return [w / total for w in weights], member


def _post_attn_kernel(yacd_ref, yb_ref, x_ref, wo_acd_ref, wo_b_ref, g1_ref, b1_ref, wq_ref,
                      km_ref, vm_ref, wo_ref, g2_ref, b2_ref, rw_ref,
                      x2_ref, logits_ref, *, tm, alpha, n_chains):
    def chain(rows):
        h = _dot(yacd_ref[0, rows], wo_acd_ref[...]) + _dot(yb_ref[0, rows], wo_b_ref[...])
        yield
        x1 = _layer_norm(alpha * x_ref[0, rows] + h, g1_ref[...], b1_ref[...])
        q = _dot(x1.astype(BF16), wq_ref[...])
        yield
        heads = []
        for hd in range(X_HEADS):
            cols = slice(hd * X_HEAD_DIM, (hd + 1) * X_HEAD_DIM)
            s = _dot_nt(q[:, cols].astype(BF16), km_ref[0, :, cols])
            e = jnp.exp(s - jnp.max(s, axis=-1, keepdims=True))
            p = e / jnp.sum(e, axis=-1, keepdims=True)
            heads.append(_dot(p.astype(BF16), vm_ref[0, :, cols]).astype(BF16))
            yield
        h2 = _dot(jnp.concatenate(heads, axis=1), wo_ref[...])
        yield
        x2 = _layer_norm(alpha * x1 + h2, g2_ref[...], b2_ref[...])
        x2_ref[0, rows] = x2
        x_hi = x2.astype(BF16)
        x_lo = (x2 - x_hi.astype(F32)).astype(BF16)
        hl = _dot_nt(rw_ref[...], x_hi)
        lo = _dot_nt(rw_ref[0:N_EXPERTS, :], x_lo)
        logits_ref[0, :, rows] = hl[:N_EXPERTS] + hl[N_EXPERTS:] + lo

    rows_per = tm // n_chains
    _run_interleaved([chain(slice(c * rows_per, (c + 1) * rows_per)) for c in range(n_chains)])


def _post_attn(yacd, yb, x, wo_acd, wo_b, g1, b1, wq, km, vm, wo, g2, b2, rw,
               *, layer, tm, alpha):
    bsz, seq, d = x.shape
    tok = lambda w: pl.BlockSpec((1, tm, w), lambda b, s: (b, s, 0))
    memspec = pl.BlockSpec((None, 1) + km.shape[2:], lambda b, s: (layer, b, 0, 0))
    per_layer = lambda w: _layer_spec(w, layer)
    return pl.pallas_call(
        functools.partial(_post_attn_kernel, tm=tm, alpha=alpha, n_chains=2),
        grid=(bsz, seq // tm),
        in_specs=[tok(yacd.shape[-1]), tok(yb.shape[-1]), tok(d),
                  per_layer(wo_acd), per_layer(wo_b), per_layer(g1), per_layer(b1), per_layer(wq),
                  memspec, memspec, per_layer(wo), per_layer(g2), per_layer(b2),
                  _const_spec(rw.shape)],
        out_specs=[tok(d), pl.BlockSpec((1, N_EXPERTS, tm), lambda b, s: (b, 0, s))],
        out_shape=[jax.ShapeDtypeStruct((bsz, seq, d), F32),
                   jax.ShapeDtypeStruct((bsz, N_EXPERTS, seq), F32)],
        compiler_params=_params(2),
        name="post_attn",
    )(yacd, yb, x, wo_acd, wo_b, g1, b1, wq, km, vm, wo, g2, b2, rw)


def _dot_tn(a, b):
    return lax.dot_general(a, b, (((0,), (0,)), ((), ())), preferred_element_type=F32)


def _moe_kernel(x_ref, logits_ref, rb_ref, before_ref, wg_ref, wu_ref, wd_ref, g3_ref, b3_ref, o_ref,
                xb_ref, pos_ref, memb_ref, gsplit_ref, acc_ref, *, alpha, tm, rows):
    group_w = EXPERTS_PER_GROUP * D_FF
    x = x_ref[0]
    xb_ref[...] = x.astype(BF16)
    gates, member = _route_t(logits_ref[0], rb_ref[...])
    memb = jnp.concatenate(member + [jnp.zeros((ROUTE_ROWS - N_GROUPS, tm), F32)], axis=0)
    memb_ref[...] = memb
    pos_ref[...] = _dot(memb.astype(BF16), before_ref[...])
    gate_rows = jnp.concatenate(gates, axis=0)
    g_hi = gate_rows.astype(BF16)
    gsplit_ref[...] = jnp.concatenate([g_hi, (gate_rows - g_hi.astype(F32)).astype(BF16)], axis=0)

    def expert_chunk(g, ci):
        slot = (lax.broadcasted_iota(jnp.int32, (rows, tm), 0) + ci * rows).astype(F32)
        p = jnp.where(pos_ref[g:g + 1, :] == slot, memb_ref[g:g + 1, :], 0.0).astype(BF16)
        xg = _dot(p, xb_ref[...]).astype(BF16)
        gg = _dot_nt(p, gsplit_ref[...])
        hs = []
        for j in range(EXPERTS_PER_GROUP):
            e = g * EXPERTS_PER_GROUP + j
            cols = slice(e * D_FF, (e + 1) * D_FF)
            gate = gg[:, e:e + 1] + gg[:, N_EXPERTS + e:N_EXPERTS + e + 1]
            hg = _dot(xg, wg_ref[:, cols])
            hu = _dot(xg, wu_ref[:, cols])
            hs.append((jax.nn.silu(hg) * hu * gate).astype(BF16))
        y = _dot(jnp.concatenate(hs, axis=1), wd_ref[g * group_w:(g + 1) * group_w, :])
        return p, y.astype(BF16)

    first = [expert_chunk(g, 0) for g in range(N_GROUPS)]
    acc_ref[...] = _dot_tn(jnp.concatenate([p for p, _ in first], axis=0),
                           jnp.concatenate([y for _, y in first], axis=0))
    for g in range(N_GROUPS):
        n_tok = jnp.sum(memb_ref[g:g + 1, :]).astype(jnp.int32)

        def extra(ci, carry, g=g):
            p, y = expert_chunk(g, ci)
            acc_ref[...] += _dot_tn(p, y)
            return carry

        lax.fori_loop(1, (n_tok + rows - 1) // rows, extra, 0)

    o_ref[0] = _layer_norm(alpha * x_ref[0] + acc_ref[...], g3_ref[...], b3_ref[...])


def _moe(x, logits_t, rb_t, before, wg, wu, wd, g3, b3, *, layer, tm, alpha):
    bsz, seq, d = x.shape
    tok = lambda w: pl.BlockSpec((1, tm, w), lambda b, s: (b, s, 0))
    resident = lambda w: _layer_spec(w, layer, single_buffer=True)
    return pl.pallas_call(
        functools.partial(_moe_kernel, alpha=alpha, tm=tm, rows=MOE_ROWS),
        grid=(bsz, seq // tm),
        in_specs=[tok(d), pl.BlockSpec((1, N_EXPERTS, tm), lambda b, s: (b, 0, s)),
                  _const_spec(rb_t.shape), _const_spec(before.shape),
                  resident(wg), resident(wu), resident(wd),
                  _layer_spec(g3, layer), _layer_spec(b3, layer)],
        out_specs=tok(d),
        out_shape=jax.ShapeDtypeStruct((bsz, seq, d), F32),
        scratch_shapes=[pltpu.VMEM((tm, d), BF16),
                        pltpu.VMEM((ROUTE_ROWS, tm), F32),
                        pltpu.VMEM((ROUTE_ROWS, tm), F32),
                        pltpu.VMEM((2 * N_EXPERTS, tm), BF16),
                        pltpu.VMEM((tm, d), F32)],
        compiler_params=_params(2),
        name="moe",
    )(x, logits_t, rb_t, before, wg, wu, wd, g3, b3)


def _rot_cols(w):
    half = w.shape[-1] // 2
    return jnp.concatenate([-w[..., half:], w[..., :half]], axis=-1)


def _prep_w_in(w_in):
    u, v = w_in[..., 0:256], w_in[..., 256:512]
    cq, ckv, kr = w_in[..., 512:768], w_in[..., 768:896], w_in[..., 896:928]
    rest = w_in[..., 928:]
    rope4 = jnp.concatenate([kr, _rot_cols(kr), kr, _rot_cols(kr)], axis=-1)
    return jnp.concatenate([u, v, cq, ckv, rope4, rest], axis=-1).astype(BF16)


def _prep_w_uq(w_uq):
    depth = w_uq.shape[0]
    w = w_uq.reshape(depth, Q_LORA, MLA_HEADS, MLA_NOPE + MLA_ROPE)
    nope, rope = w[..., :MLA_NOPE], w[..., MLA_NOPE:]
    ext = jnp.concatenate([nope, rope, _rot_cols(rope)], axis=-1)
    return ext.reshape(depth, Q_LORA, MLA_HEADS * QK_DIM).astype(BF16)


def _prep_w_ukv(w_ukv):
    depth = w_ukv.shape[0]
    w = w_ukv.reshape(depth, KV_LORA, MLA_HEADS, MLA_NOPE + MLA_V)
    k_nope, v = w[..., :MLA_NOPE], w[..., MLA_NOPE:]
    wk = jnp.concatenate([k_nope, jnp.zeros_like(k_nope)], axis=-1)
    wv = jnp.concatenate([v, jnp.zeros_like(v)], axis=-1)
    return (wk.reshape(depth, KV_LORA, MLA_HEADS * QK_DIM).astype(BF16),
            wv.reshape(depth, KV_LORA, MLA_HEADS * LANES).astype(BF16))


def _prep_pool_w(pool_w):
    depth, groups = pool_w.shape[:2]
    eye = jnp.eye(groups, dtype=pool_w.dtype)
    bd = jnp.einsum('lgcd,gh->lgchd', pool_w, eye)
    return bd.reshape(depth, D_WIDTH, D_WIDTH).astype(BF16)


def kernel(x, mem, positions, w_in, gmlp_v_norm_g, gmlp_w_s, gmlp_b_s, mla_q_norm_g, mla_w_uq,
           mla_kv_norm_g, mla_w_ukv, conv_w, pool_w, pool_scale, w_out, ln1_g, ln1_b,
           xattn_wq, xattn_wk, xattn_wv, xattn_wo, ln2_g, ln2_b, router_w, router_b,
           moe_w_gate, moe_w_up, moe_w_down, ln3_g, ln3_b):
    depth = w_in.shape[0]
    alpha = (2 * depth) ** 0.25
    tm = 512
    bsz, seq, _ = x.shape

    inv_freq = ROPE_BASE ** (-jnp.arange(0, MLA_ROPE, 2, dtype=F32) / MLA_ROPE)
    ang = positions.astype(F32)[..., None] * inv_freq
    cos2 = jnp.tile(jnp.cos(ang), (1, 1, 2))
    sin2 = jnp.tile(jnp.sin(ang), (1, 1, 2))
    scale = (MLA_NOPE + MLA_ROPE) ** -0.5 * LOG2_E
    csq = scale * jnp.concatenate([jnp.ones(cos2.shape[:2] + (MLA_NOPE,), F32), cos2, sin2], -1)
    csq = jnp.swapaxes(csq.reshape(bsz, seq // tm, tm, QK_DIM), 2, 3)
    csk = jnp.concatenate([cos2, sin2, cos2, sin2], axis=-1)

    row = lambda a: a[:, None, :]
    w_in_e = _prep_w_in(w_in)
    w_uq_e = jnp.swapaxes(_prep_w_uq(mla_w_uq), 1, 2)
    w_k_e, w_v_e = _prep_w_ukv(mla_w_ukv)
    w_v_e = jnp.swapaxes(w_v_e, 1, 2)
    ws = gmlp_w_s.astype(BF16)
    bs = jnp.repeat(jnp.swapaxes(gmlp_b_s, 1, 2), A_HEAD_DIM, axis=2)
    pool_bd = _prep_pool_w(pool_w)
    wo_acd = jnp.concatenate([w_out[:, :A_WIDTH], w_out[:, A_WIDTH + MLA_WIDTH:]], axis=1).astype(BF16)
    wo_b = w_out[:, A_WIDTH:A_WIDTH + MLA_WIDTH].astype(BF16)
    wq = (xattn_wq * (X_HEAD_DIM ** -0.5)).astype(BF16)
    wo = xattn_wo.astype(BF16)
    rw_t = router_w.T
    rw_hi = rw_t.astype(BF16)
    rw = jnp.concatenate([rw_hi, (rw_t - rw_hi.astype(F32)).astype(BF16)], axis=0)
    rb_t = jnp.broadcast_to(router_b[:, None], (N_EXPERTS, tm))
    before = jnp.triu(jnp.ones((tm, tm), BF16), 1)
    cat_experts = lambda w: jnp.swapaxes(w, 1, 2).reshape(depth, D_MODEL, N_EXPERTS * D_FF)
    wg = cat_experts(moe_w_gate).astype(BF16)
    wu = cat_experts(moe_w_up).astype(BF16)
    wd = moe_w_down.reshape(depth, N_EXPERTS * D_FF, D_MODEL).astype(BF16)

    km, vm = _mem_kv(mem, xattn_wk.astype(BF16), xattn_wv.astype(BF16))

    for l in range(depth):
        q, k, v, yacd = _mixer_in(
            x, csq, csk, w_in_e, row(gmlp_v_norm_g), ws, bs, row(mla_q_norm_g),
            w_uq_e, row(mla_kv_norm_g), w_k_e, w_v_e, conv_w, pool_bd,
            row(pool_scale), layer=l, tm=tm)
        yb = _mla_attn(q, k, v, hg=MLA_HEADS)
        x2, logits = _post_attn(
            yacd, yb, x, wo_acd, wo_b, row(ln1_g), row(ln1_b), wq, km, vm,
            wo, row(ln2_g), row(ln2_b), rw, layer=l, tm=tm, alpha=alpha)
        x = _moe(x2, logits, rb_t, before, wg, wu, wd, row(ln3_g), row(ln3_b),
                 layer=l, tm=tm, alpha=alpha)
    return x
```

```python
import functools

import jax
import jax.numpy as jnp
from jax import lax
from jax.experimental import pallas as pl
from jax.experimental.pallas import tpu as pltpu

F32 = jnp.float32
BF16 = jnp.bfloat16

D_MODEL = 1024
A_HEADS, A_HEAD_DIM, CHUNK = 4, 64, 128
A_WIDTH = A_HEADS * A_HEAD_DIM
MLA_HEADS, MLA_NOPE, MLA_ROPE, MLA_V = 8, 64, 32, 64
Q_LORA, KV_LORA = 256, 128
MLA_WIDTH = MLA_HEADS * MLA_V
ROPE_BASE = 10000.0
C_WIDTH, CONV_W = 256, 3
D_WIDTH = 256
POOL_WINDOWS = (2, 4, 8, 16)
D_GROUP = D_WIDTH // len(POOL_WINDOWS)
X_HEADS = 4
X_HEAD_DIM = D_MODEL // X_HEADS
N_EXPERTS, N_GROUPS, TOP_K = 16, 4, 2
EXPERTS_PER_GROUP = N_EXPERTS // N_GROUPS
D_FF = 256
EPS = 1e-6

LANES = 128
SUBLANES = 8
VMEM_LIMIT = 56 * 1024 * 1024

OFF_U, OFF_CQ, OFF_BG, P_EXT = 0, 512, 1024, 2048
QK_DIM = 128
POOL_HALO = 16
MOE_ROWS = 160
ROUTE_ROWS = 16
NEG_BIG = -1e30
LOG2_E = 1.4426950408889634


def _dot(a, b):
    return jnp.dot(a, b, preferred_element_type=F32)


def _dot_nt(a, b):
    return lax.dot_general(a, b, (((1,), (1,)), ((), ())), preferred_element_type=F32)


def _rms(x, g):
    return x * lax.rsqrt(jnp.mean(x * x, axis=-1, keepdims=True) + EPS) * g


def _layer_norm(x, g, b):
    mu = jnp.mean(x, axis=-1, keepdims=True)
    xc = x - mu
    var = jnp.mean(xc * xc, axis=-1, keepdims=True)
    return xc * lax.rsqrt(var + EPS) * g + b


def _params(n_grid):
    return pltpu.CompilerParams(dimension_semantics=("arbitrary",) * n_grid,
                                vmem_limit_bytes=VMEM_LIMIT)


def _run_interleaved(chains):
    live = list(chains)
    while live:
        for c in list(live):
            try:
                next(c)
            except StopIteration:
                live.remove(c)


def _const_spec(shape):
    zeros = (0,) * len(shape)
    return pl.BlockSpec(shape, lambda *_: zeros)


def _layer_spec(stacked, layer, single_buffer=False):
    index = (layer,) + (0,) * (stacked.ndim - 1)
    mode = dict(pipeline_mode=pl.Buffered(1)) if single_buffer else {}
    return pl.BlockSpec((None,) + stacked.shape[1:], lambda *_: index, **mode)


def _mem_kv_kernel(mem_ref, wk_ref, wv_ref, k_ref, v_ref):
    m = mem_ref[0].astype(BF16)
    k_ref[0, 0] = _dot(m, wk_ref[0]).astype(BF16)
    v_ref[0, 0] = _dot(m, wv_ref[0]).astype(BF16)


def _mem_kv(mem, wk, wv):
    depth = wk.shape[0]
    bsz, mlen, d = mem.shape
    out = jax.ShapeDtypeStruct((depth, bsz, mlen, d), BF16)
    return pl.pallas_call(
        _mem_kv_kernel,
        grid=(depth, bsz),
        in_specs=[pl.BlockSpec((1, mlen, d), lambda l, b: (b, 0, 0)),
                  pl.BlockSpec((1, d, d), lambda l, b: (l, 0, 0)),
                  pl.BlockSpec((1, d, d), lambda l, b: (l, 0, 0))],
        out_specs=[pl.BlockSpec((1, 1, mlen, d), lambda l, b: (l, b, 0, 0)),
                   pl.BlockSpec((1, 1, mlen, d), lambda l, b: (l, b, 0, 0))],
        out_shape=[out, out],
        compiler_params=_params(2),
        name="mem_kv",
    )(mem, wk, wv)


def _mixer_in_kernel(x_ref, csq_ref, csk_ref, w_in_ref, vng_ref, ws_ref, bs_ref, qng_ref,
                     wuq_ref, kvng_ref, wk_ref, wv_ref, convw_ref, poolw_ref, pscale_ref,
                     q_ref, k_ref, v_ref, y_ref,
                     conv_buf, pool_a, pool_b, pool_carry, *, tm):
    si = pl.program_id(1)

    @pl.when(si == 0)
    def _():
        pool_carry[...] = jnp.zeros((POOL_HALO, D_WIDTH), F32)
        conv_buf[0:SUBLANES, :] = jnp.zeros((SUBLANES, C_WIDTH), F32)

    xb = x_ref[0].astype(BF16)

    za = _dot(xb, w_in_ref[:, OFF_U:OFF_CQ])
    zb = _dot(xb, w_in_ref[:, OFF_CQ:OFF_BG])
    zcd = _dot(xb, w_in_ref[:, OFF_BG:P_EXT])

    za = jax.nn.gelu(za)
    u = za[:, :A_WIDTH]
    v = _rms(za[:, A_WIDTH:], vng_ref[...]).astype(BF16)
    cq = _rms(zb[:, :Q_LORA], qng_ref[...]).astype(BF16)
    ckv = _rms(zb[:, Q_LORA:Q_LORA + KV_LORA], kvng_ref[...]).astype(BF16)

    row = lax.broadcasted_iota(jnp.int32, (CHUNK, CHUNK), 0)
    col = lax.broadcasted_iota(jnp.int32, (CHUNK, CHUNK), 1)
    w_causal = [jnp.where(row >= col, ws_ref[h], jnp.zeros((), BF16)) for h in range(A_HEADS)]
    lane_head = lax.broadcasted_iota(jnp.int32, (CHUNK, A_WIDTH), 1) // A_HEAD_DIM
    for c in range(tm // CHUNK):
        rows = slice(c * CHUNK, (c + 1) * CHUNK)
        vc = v[rows]
        mixed = _dot(w_causal[0], vc)
        for h in range(1, A_HEADS):
            mixed = jnp.where(lane_head == h, _dot(w_causal[h], vc), mixed)
        y_ref[0, rows, 0:A_WIDTH] = (u[rows] * (mixed + bs_ref[...])).astype(BF16)

    qt = _dot_nt(wuq_ref[...], cq)
    kk = _dot(ckv, wk_ref[...])
    vt = _dot_nt(wv_ref[...], ckv)

    zd = zcd[:, 3 * C_WIDTH:]
    base = SUBLANES
    n = tm + POOL_HALO
    pool_a[0:base, :] = jnp.zeros((base, D_WIDTH), F32)
    pool_b[0:base, :] = jnp.zeros((base, D_WIDTH), F32)
    pool_a[base:base + POOL_HALO, :] = pool_carry[...]
    pool_a[base + POOL_HALO:base + n, :] = zd
    pool_carry[...] = zd[tm - POOL_HALO:, :]
    lane_d = lax.broadcasted_iota(jnp.int32, (n, D_WIDTH), 1)
    pool_b[base:base + n, :] = pool_a[base:base + n, :] + pool_a[base - 1:base - 1 + n, :]
    pool_a[base:base + n, :] = pool_b[base:base + n, :] + jnp.where(
        lane_d >= D_GROUP, pool_b[base - 2:base - 2 + n, :], 0.0)
    pool_b[base:base + n, :] = pool_a[base:base + n, :] + jnp.where(
        lane_d >= 2 * D_GROUP, pool_a[base - 4:base - 4 + n, :], 0.0)
    t0 = base + POOL_HALO
    lane_t = lax.broadcasted_iota(jnp.int32, (tm, D_WIDTH), 1)
    win_sum = pool_b[t0:t0 + tm, :] + jnp.where(
        lane_t >= 3 * D_GROUP, pool_b[t0 - 8:t0 - 8 + tm, :], 0.0)
    pos1 = si * tm + lax.broadcasted_iota(jnp.int32, (tm, D_WIDTH), 0) + 1
    window = jnp.left_shift(2, lane_t // D_GROUP)
    count = jnp.minimum(pos1, window).astype(F32)
    pooled = (win_sum / count - zd).astype(BF16)
    yd = _dot(pooled, poolw_ref[...]) * pscale_ref[...]
    y_ref[0, :, A_WIDTH + C_WIDTH:] = yd.astype(BF16)

    gh = zcd[:, C_WIDTH:2 * C_WIDTH] * zcd[:, 2 * C_WIDTH:3 * C_WIDTH]
    conv_buf[SUBLANES:SUBLANES + tm, :] = gh
    conv = (convw_ref[2:3, :] * gh
            + convw_ref[1:2, :] * conv_buf[SUBLANES - 1:SUBLANES - 1 + tm, :]
            + convw_ref[0:1, :] * conv_buf[SUBLANES - 2:SUBLANES - 2 + tm, :])
    y_ref[0, :, A_WIDTH:A_WIDTH + C_WIDTH] = (zcd[:, :C_WIDTH] * conv).astype(BF16)
    conv_buf[0:SUBLANES, :] = conv_buf[tm:tm + SUBLANES, :]

    csq = csq_ref[0, 0]
    for h in range(MLA_HEADS):
        q_ref[0, h, 0] = (qt[h * QK_DIM:(h + 1) * QK_DIM, :] * csq).astype(BF16)
    r4 = zb[:, Q_LORA + KV_LORA:] * csk_ref[0]
    kr = r4 + pltpu.roll(r4, MLA_ROPE, 1)
    lane = lax.broadcasted_iota(jnp.int32, (tm, QK_DIM), 1)
    kr = jnp.where(lane >= MLA_NOPE, kr, 0.0)
    for h in range(MLA_HEADS):
        k_ref[0, h] = (kk[:, h * QK_DIM:(h + 1) * QK_DIM] + kr).astype(BF16)
    ones_row = jnp.where(lax.broadcasted_iota(jnp.int32, (LANES, tm), 0) == MLA_V, 1.0, 0.0)
    for h in range(MLA_HEADS):
        v_ref[0, h, 0] = (vt[h * LANES:(h + 1) * LANES, :] + ones_row).astype(BF16)


def _mixer_in(x, csq, csk, w_in, vng, ws, bs, qng, wuq, kvng, wk, wv, convw, poolw, pscale,
              *, layer, tm):
    bsz, seq, d = x.shape
    tok = lambda w: pl.BlockSpec((1, tm, w), lambda b, s: (b, s, 0))
    head = lambda n, w: pl.BlockSpec((1, n, tm, w), lambda b, s: (b, 0, s, 0))
    head_t = pl.BlockSpec((1, MLA_HEADS, 1, LANES, tm), lambda b, s: (b, 0, s, 0, 0))
    shape_t = jax.ShapeDtypeStruct((bsz, MLA_HEADS, seq // tm, LANES, tm), BF16)
    return pl.pallas_call(
        functools.partial(_mixer_in_kernel, tm=tm),
        grid=(bsz, seq // tm),
        in_specs=[tok(d), pl.BlockSpec((1, 1, QK_DIM, tm), lambda b, s: (b, s, 0, 0)), tok(QK_DIM),
                  *[_layer_spec(w, layer) for w in (w_in, vng, ws, bs, qng, wuq, kvng, wk, wv,
                                                     convw, poolw, pscale)]],
        out_specs=[head_t, head(MLA_HEADS, QK_DIM), head_t, tok(A_WIDTH + C_WIDTH + D_WIDTH)],
        out_shape=[shape_t,
                   jax.ShapeDtypeStruct((bsz, MLA_HEADS, seq, QK_DIM), BF16),
                   shape_t,
                   jax.ShapeDtypeStruct((bsz, seq, A_WIDTH + C_WIDTH + D_WIDTH), BF16)],
        scratch_shapes=[pltpu.VMEM((tm + SUBLANES, C_WIDTH), F32),
                        pltpu.VMEM((tm + POOL_HALO + SUBLANES, D_WIDTH), F32),
                        pltpu.VMEM((tm + POOL_HALO + SUBLANES, D_WIDTH), F32),
                        pltpu.VMEM((POOL_HALO, D_WIDTH), F32)],
        compiler_params=_params(2),
        name="mixer_in",
    )(x, csq, csk, w_in, vng, ws, bs, qng, wuq, kvng, wk, wv, convw, poolw, pscale)


def _mla_attn_kernel(qt_ref, k_ref, vt_ref, o_ref, m_ref, acc_ref, *, tq, hg):
    qi = pl.program_id(2)
    half = tq // 2
    m_ref[...] = jnp.full(m_ref.shape, NEG_BIG, F32)
    acc_ref[...] = jnp.zeros(acc_ref.shape, F32)

    def step(j, diagonal):
        base = pl.multiple_of(j * tq, tq)

        def scores(hh):
            if not diagonal:
                return (_dot(k_ref[0, hh, pl.ds(base, tq), :], qt_ref[0, hh, 0]),)
            return (_dot(k_ref[0, hh, pl.ds(base, half), :], qt_ref[0, hh, 0]),
                    _dot(k_ref[0, hh, pl.ds(base + half, half), :], qt_ref[0, hh, 0, :, half:]))

        def softmax(hh, st):
            m_old = m_ref[hh]
            if not diagonal:
                m_new = jnp.maximum(m_old, jnp.max(st[0], axis=0, keepdims=True))
                probs = (jnp.exp2(st[0] - m_new).astype(BF16),)
            else:
                def causal(s):
                    key = lax.broadcasted_iota(jnp.int32, s.shape, 0)
                    qry = lax.broadcasted_iota(jnp.int32, s.shape, 1)
                    return jnp.where(key <= qry, s, NEG_BIG)
                top, bottom = causal(st[0]), causal(st[1])
                later = jnp.concatenate([jnp.full((1, half), NEG_BIG, F32),
                                         jnp.max(bottom, axis=0, keepdims=True)], axis=1)
                m_new = jnp.maximum(m_old, jnp.maximum(jnp.max(top, axis=0, keepdims=True), later))
                probs = (jnp.exp2(top - m_new).astype(BF16),
                         jnp.exp2(bottom - m_new[:, half:]).astype(BF16))
            m_ref[hh] = m_new
            return hh, probs, jnp.exp2(m_old - m_new)

        def weighted_values(hh, probs, rescale):
            if not diagonal:
                pv = _dot(vt_ref[0, hh, j], probs[0])
            else:
                later = _dot(vt_ref[0, hh, j, :, half:], probs[1])
                pv = (_dot(vt_ref[0, hh, j, :, :half], probs[0])
                      + jnp.concatenate([jnp.zeros((LANES, half), F32), later], axis=1))
            acc_ref[hh] = rescale * acc_ref[hh] + pv

        st_next = scores(0)
        pending = None
        for hh in range(hg):
            st = st_next
            if hh + 1 < hg:
                st_next = scores(hh + 1)
            if pending is not None:
                weighted_values(*pending)
            pending = softmax(hh, st)
        weighted_values(*pending)

    def body(j, carry):
        step(j, False)
        return carry

    lax.fori_loop(0, qi, body, 0)
    step(qi, True)
    for pr in range(hg // 2):
        halves = []
        for hh in (2 * pr, 2 * pr + 1):
            acc = acc_ref[hh]
            halves.append(acc[:MLA_V] / acc[MLA_V:MLA_V + 1])
        o_ref[0, :, pr * LANES:(pr + 1) * LANES] = jnp.concatenate(halves, axis=0).T.astype(BF16)


def _mla_attn(qt, k, vt, *, hg):
    bsz, heads, n_tiles, _, tq = qt.shape
    seq = n_tiles * tq
    return pl.pallas_call(
        functools.partial(_mla_attn_kernel, tq=tq, hg=hg),
        grid=(bsz, heads // hg, n_tiles),
        in_specs=[pl.BlockSpec((1, hg, 1, QK_DIM, tq), lambda b, g, i: (b, g, i, 0, 0)),
                  pl.BlockSpec((1, hg, seq, QK_DIM), lambda b, g, i: (b, g, 0, 0)),
                  pl.BlockSpec((1, hg, n_tiles, LANES, tq), lambda b, g, i: (b, g, 0, 0, 0))],
        out_specs=pl.BlockSpec((1, tq, hg * MLA_V), lambda b, g, i: (b, i, g)),
        out_shape=jax.ShapeDtypeStruct((bsz, seq, MLA_WIDTH), BF16),
        scratch_shapes=[pltpu.VMEM((hg, 1, tq), F32), pltpu.VMEM((hg, LANES, tq), F32)],
        compiler_params=_params(3),
        name="mla_attn",
    )(qt, k, vt)


def _route_t(logits_t, bias_t):
    scores = jax.nn.sigmoid(logits_t)
    sel_all = scores + bias_t
    sel = [sel_all[e:e + 1] for e in range(N_EXPERTS)]
    group_score = []
    for g in range(N_GROUPS):
        v = sel[g * EXPERTS_PER_GROUP:(g + 1) * EXPERTS_PER_GROUP]
        best = None
        for i in range(EXPERTS_PER_GROUP):
            for j in range(i + 1, EXPERTS_PER_GROUP):
                best = v[i] + v[j] if best is None else jnp.maximum(best, v[i] + v[j])
        group_score.append(best)
    top, top_idx = group_score[0], jnp.zeros(group_score[0].shape, jnp.int32)
    for g in range(1, N_GROUPS):
        better = group_score[g] > top
        top = jnp.where(better, group_score[g], top)
        top_idx = jnp.where(better, g, top_idx)
    member = [jnp.where(top_idx == g, 1.0, 0.0) for g in range(N_GROUPS)]
    weights = []
    for e in range(N_EXPERTS):
        g = e // EXPERTS_PER_GROUP
        rank = jnp.zeros(sel[e].shape, jnp.int32)
        for j in range(g * EXPERTS_PER_GROUP, (g + 1) * EXPERTS_PER_GROUP):
            if j != e:
                rank = rank + jnp.where(sel[j] > sel[e], 1,
                                        jnp.where(sel[j] == sel[e], 1 if j < e else 0, 0))
        weights.append(jnp.where(rank < TOP_K, member[g], 0.0) * scores[e:e + 1])
    total = weights[0]
    for w in weights[1:]:
        total = total + w
    return [w / total for w in weights], member


def _post_attn_kernel(yacd_ref, yb_ref, x_ref, wo_acd_ref, wo_b_ref, g1_ref, b1_ref, wq_ref,
                      km_ref, vm_ref, wo_ref, g2_ref, b2_ref, rw_ref,
                      x2_ref, logits_ref, *, tm, alpha, n_chains):
    def chain(rows):
        h = _dot(yacd_ref[0, rows], wo_acd_ref[...]) + _dot(yb_ref[0, rows], wo_b_ref[...])
        yield
        x1 = _layer_norm(alpha * x_ref[0, rows] + h, g1_ref[...], b1_ref[...])
        q = _dot(x1.astype(BF16), wq_ref[...])
        yield
        heads = []
        for hd in range(X_HEADS):
            cols = slice(hd * X_HEAD_DIM, (hd + 1) * X_HEAD_DIM)
            s = _dot_nt(q[:, cols].astype(BF16), km_ref[0, :, cols])
            e = jnp.exp(s - jnp.max(s, axis=-1, keepdims=True))
            p = e / jnp.sum(e, axis=-1, keepdims=True)
            heads.append(_dot(p.astype(BF16), vm_ref[0, :, cols]).astype(BF16))
            yield
        h2 = _dot(jnp.concatenate(heads, axis=1), wo_ref[...])
        yield
        x2 = _layer_norm(alpha * x1 + h2, g2_ref[...], b2_ref[...])
        x2_ref[0, rows] = x2
        x_hi = x2.astype(BF16)
        x_lo = (x2 - x_hi.astype(F32)).astype(BF16)
        hl = _dot_nt(rw_ref[...], x_hi)
        lo = _dot_nt(rw_ref[0:N_EXPERTS, :], x_lo)
        logits_ref[0, :, rows] = hl[:N_EXPERTS] + hl[N_EXPERTS:] + lo

    rows_per = tm // n_chains
    _run_interleaved([chain(slice(c * rows_per, (c + 1) * rows_per)) for c in range(n_chains)])


def _post_attn(yacd, yb, x, wo_acd, wo_b, g1, b1, wq, km, vm, wo, g2, b2, rw,
               *, layer, tm, alpha):
    bsz, seq, d = x.shape
    tok = lambda w: pl.BlockSpec((1, tm, w), lambda b, s: (b, s, 0))
    memspec = pl.BlockSpec((None, 1) + km.shape[2:], lambda b, s: (layer, b, 0, 0))
    per_layer = lambda w: _layer_spec(w, layer)
    return pl.pallas_call(
        functools.partial(_post_attn_kernel, tm=tm, alpha=alpha, n_chains=2),
        grid=(bsz, seq // tm),
        in_specs=[tok(yacd.shape[-1]), tok(yb.shape[-1]), tok(d),
                  per_layer(wo_acd), per_layer(wo_b), per_layer(g1), per_layer(b1), per_layer(wq),
                  memspec, memspec, per_layer(wo), per_layer(g2), per_layer(b2),
                  _const_spec(rw.shape)],
        out_specs=[tok(d), pl.BlockSpec((1, N_EXPERTS, tm), lambda b, s: (b, 0, s))],
        out_shape=[jax.ShapeDtypeStruct((bsz, seq, d), F32),
                   jax.ShapeDtypeStruct((bsz, N_EXPERTS, seq), F32)],
        compiler_params=_params(2),
        name="post_attn",
    )(yacd, yb, x, wo_acd, wo_b, g1, b1, wq, km, vm, wo, g2, b2, rw)


def _dot_tn(a, b):
    return lax.dot_general(a, b, (((0,), (0,)), ((), ())), preferred_element_type=F32)


def _moe_kernel(x_ref, logits_ref, rb_ref, before_ref, wg_ref, wu_ref, wd_ref, g3_ref, b3_ref, o_ref,
                xb_ref, pos_ref, memb_ref, gsplit_ref, acc_ref, *, alpha, tm, rows):
    group_w = EXPERTS_PER_GROUP * D_FF
    x = x_ref[0]
    xb_ref[...] = x.astype(BF16)
    gates, member = _route_t(logits_ref[0], rb_ref[...])
    memb = jnp.concatenate(member + [jnp.zeros((ROUTE_ROWS - N_GROUPS, tm), F32)], axis=0)
    memb_ref[...] = memb
    pos_ref[...] = _dot(memb.astype(BF16), before_ref[...])
    gate_rows = jnp.concatenate(gates, axis=0)
    g_hi = gate_rows.astype(BF16)
    gsplit_ref[...] = jnp.concatenate([g_hi, (gate_rows - g_hi.astype(F32)).astype(BF16)], axis=0)

    def expert_chunk(g, ci):
        slot = (lax.broadcasted_iota(jnp.int32, (rows, tm), 0) + ci * rows).astype(F32)
        p = jnp.where(pos_ref[g:g + 1, :] == slot, memb_ref[g:g + 1, :], 0.0).astype(BF16)
        xg = _dot(p, xb_ref[...]).astype(BF16)
        gg = _dot_nt(p, gsplit_ref[...])
        hs = []
        for j in range(EXPERTS_PER_GROUP):
            e = g * EXPERTS_PER_GROUP + j
            cols = slice(e * D_FF, (e + 1) * D_FF)
            gate = gg[:, e:e + 1] + gg[:, N_EXPERTS + e:N_EXPERTS + e + 1]
            hg = _dot(xg, wg_ref[:, cols])
            hu = _dot(xg, wu_ref[:, cols])
            hs.append((jax.nn.silu(hg) * hu * gate).astype(BF16))
        y = _dot(jnp.concatenate(hs, axis=1), wd_ref[g * group_w:(g + 1) * group_w, :])
        return p, y.astype(BF16)

    first = [expert_chunk(g, 0) for g in range(N_GROUPS)]
    acc_ref[...] = _dot_tn(jnp.concatenate([p for p, _ in first], axis=0),
                           jnp.concatenate([y for _, y in first], axis=0))
    largest = jnp.max(jnp.sum(memb_ref[0:SUBLANES, :], axis=1, keepdims=True))

    @pl.when(largest > rows)
    def _():
        for g in range(N_GROUPS):
            n_tok = jnp.sum(memb_ref[g:g + 1, :]).astype(jnp.int32)

            def extra(ci, carry, g=g):
                p, y = expert_chunk(g, ci)
                acc_ref[...] += _dot_tn(p, y)
                return carry

            lax.fori_loop(1, (n_tok + rows - 1) // rows, extra, 0)

    o_ref[0] = _layer_norm(alpha * x_ref[0] + acc_ref[...], g3_ref[...], b3_ref[...])


def _moe(x, logits_t, rb_t, before, wg, wu, wd, g3, b3, *, layer, tm, alpha):
    bsz, seq, d = x.shape
    tok = lambda w: pl.BlockSpec((1, tm, w), lambda b, s: (b, s, 0))
    resident = lambda w: _layer_spec(w, layer, single_buffer=True)
    return pl.pallas_call(
        functools.partial(_moe_kernel, alpha=alpha, tm=tm, rows=MOE_ROWS),
        grid=(bsz, seq // tm),
        in_specs=[tok(d), pl.BlockSpec((1, N_EXPERTS, tm), lambda b, s: (b, 0, s)),
                  _const_spec(rb_t.shape), _const_spec(before.shape),
                  resident(wg), resident(wu), resident(wd),
                  _layer_spec(g3, layer), _layer_spec(b3, layer)],
        out_specs=tok(d),
        out_shape=jax.ShapeDtypeStruct((bsz, seq, d), F32),
        scratch_shapes=[pltpu.VMEM((tm, d), BF16),
                        pltpu.VMEM((ROUTE_ROWS, tm), F32),
                        pltpu.VMEM((ROUTE_ROWS, tm), F32),
                        pltpu.VMEM((2 * N_EXPERTS, tm), BF16),
                        pltpu.VMEM((tm, d), F32)],
        compiler_params=_params(2),
        name="moe",
    )(x, logits_t, rb_t, before, wg, wu, wd, g3, b3)


def _rot_cols(w):
    half = w.shape[-1] // 2
    return jnp.concatenate([-w[..., half:], w[..., :half]], axis=-1)


def _prep_w_in(w_in):
    u, v = w_in[..., 0:256], w_in[..., 256:512]
    cq, ckv, kr = w_in[..., 512:768], w_in[..., 768:896], w_in[..., 896:928]
    rest = w_in[..., 928:]
    rope4 = jnp.concatenate([kr, _rot_cols(kr), kr, _rot_cols(kr)], axis=-1)
    return jnp.concatenate([u, v, cq, ckv, rope4, rest], axis=-1).astype(BF16)


def _prep_w_uq(w_uq):
    depth = w_uq.shape[0]
    w = w_uq.reshape(depth, Q_LORA, MLA_HEADS, MLA_NOPE + MLA_ROPE)
    nope, rope = w[..., :MLA_NOPE], w[..., MLA_NOPE:]
    ext = jnp.concatenate([nope, rope, _rot_cols(rope)], axis=-1)
    return ext.reshape(depth, Q_LORA, MLA_HEADS * QK_DIM).astype(BF16)


def _prep_w_ukv(w_ukv):
    depth = w_ukv.shape[0]
    w = w_ukv.reshape(depth, KV_LORA, MLA_HEADS, MLA_NOPE + MLA_V)
    k_nope, v = w[..., :MLA_NOPE], w[..., MLA_NOPE:]
    wk = jnp.concatenate([k_nope, jnp.zeros_like(k_nope)], axis=-1)
    wv = jnp.concatenate([v, jnp.zeros_like(v)], axis=-1)
    return (wk.reshape(depth, KV_LORA, MLA_HEADS * QK_DIM).astype(BF16),
            wv.reshape(depth, KV_LORA, MLA_HEADS * LANES).astype(BF16))


def _prep_pool_w(pool_w):
    depth, groups = pool_w.shape[:2]
    eye = jnp.eye(groups, dtype=pool_w.dtype)
    bd = jnp.einsum('lgcd,gh->lgchd', pool_w, eye)
    return bd.reshape(depth, D_WIDTH, D_WIDTH).astype(BF16)


def kernel(x, mem, positions, w_in, gmlp_v_norm_g, gmlp_w_s, gmlp_b_s, mla_q_norm_g, mla_w_uq,
           mla_kv_norm_g, mla_w_ukv, conv_w, pool_w, pool_scale, w_out, ln1_g, ln1_b,
           xattn_wq, xattn_wk, xattn_wv, xattn_wo, ln2_g, ln2_b, router_w, router_b,
           moe_w_gate, moe_w_up, moe_w_down, ln3_g, ln3_b):
    depth = w_in.shape[0]
    alpha = (2 * depth) ** 0.25
    tm = 512
    bsz, seq, _ = x.shape

    inv_freq = ROPE_BASE ** (-jnp.arange(0, MLA_ROPE, 2, dtype=F32) / MLA_ROPE)
    ang = positions.astype(F32)[..., None] * inv_freq
    cos2 = jnp.tile(jnp.cos(ang), (1, 1, 2))
    sin2 = jnp.tile(jnp.sin(ang), (1, 1, 2))
    scale = (MLA_NOPE + MLA_ROPE) ** -0.5 * LOG2_E
    csq = scale * jnp.concatenate([jnp.ones(cos2.shape[:2] + (MLA_NOPE,), F32), cos2, sin2], -1)
    csq = jnp.swapaxes(csq.reshape(bsz, seq // tm, tm, QK_DIM), 2, 3)
    csk = jnp.concatenate([cos2, sin2, cos2, sin2], axis=-1)

    row = lambda a: a[:, None, :]
    w_in_e = _prep_w_in(w_in)
    w_uq_e = jnp.swapaxes(_prep_w_uq(mla_w_uq), 1, 2)
    w_k_e, w_v_e = _prep_w_ukv(mla_w_ukv)
    w_v_e = jnp.swapaxes(w_v_e, 1, 2)
    ws = gmlp_w_s.astype(BF16)
    bs = jnp.repeat(jnp.swapaxes(gmlp_b_s, 1, 2), A_HEAD_DIM, axis=2)
    pool_bd = _prep_pool_w(pool_w)
    wo_acd = jnp.concatenate([w_out[:, :A_WIDTH], w_out[:, A_WIDTH + MLA_WIDTH:]], axis=1).astype(BF16)
    wo_b = w_out[:, A_WIDTH:A_WIDTH + MLA_WIDTH].astype(BF16)
    wq = (xattn_wq * (X_HEAD_DIM ** -0.5)).astype(BF16)
    wo = xattn_wo.astype(BF16)
    rw_t = router_w.T
    rw_hi = rw_t.astype(BF16)
    rw = jnp.concatenate([rw_hi, (rw_t - rw_hi.astype(F32)).astype(BF16)], axis=0)
    rb_t = jnp.broadcast_to(router_b[:, None], (N_EXPERTS, tm))
    before = jnp.triu(jnp.ones((tm, tm), BF16), 1)
    cat_experts = lambda w: jnp.swapaxes(w, 1, 2).reshape(depth, D_MODEL, N_EXPERTS * D_FF)
    wg = cat_experts(moe_w_gate).astype(BF16)
    wu = cat_experts(moe_w_up).astype(BF16)
    wd = moe_w_down.reshape(depth, N_EXPERTS * D_FF, D_MODEL).astype(BF16)

    km, vm = _mem_kv(mem, xattn_wk.astype(BF16), xattn_wv.astype(BF16))

    for l in range(depth):
        q, k, v, yacd = _mixer_in(
            x, csq, csk, w_in_e, row(gmlp_v_norm_g), ws, bs, row(mla_q_norm_g),
            w_uq_e, row(mla_kv_norm_g), w_k_e, w_v_e, conv_w, pool_bd,
            row(pool_scale), layer=l, tm=tm)
        yb = _mla_attn(q, k, v, hg=MLA_HEADS)
        x2, logits = _post_attn(
            yacd, yb, x, wo_acd, wo_b, row(ln1_g), row(ln1_b), wq, km, vm,
            wo, row(ln2_g), row(ln2_b), rw, layer=l, tm=tm, alpha=alpha)
        x = _moe(x2, logits, rb_t, before, wg, wu, wd, row(ln3_g), row(ln3_b),
                 layer=l, tm=tm, alpha=alpha)
    return x
```

```python
import functools

import jax
import jax.numpy as jnp
from jax import lax
from jax.experimental import pallas as pl
from jax.experimental.pallas import tpu as pltpu

F32 = jnp.float32
BF16 = jnp.bfloat16

D_MODEL = 1024
A_HEADS, A_HEAD_DIM, CHUNK = 4, 64, 128
A_WIDTH = A_HEADS * A_HEAD_DIM
MLA_HEADS, MLA_NOPE, MLA_ROPE, MLA_V = 8, 64, 32, 64
Q_LORA, KV_LORA = 256, 128
MLA_WIDTH = MLA_HEADS * MLA_V
ROPE_BASE = 10000.0
C_WIDTH, CONV_W = 256, 3
D_WIDTH = 256
POOL_WINDOWS = (2, 4, 8, 16)
D_GROUP = D_WIDTH // len(POOL_WINDOWS)
X_HEADS = 4
X_HEAD_DIM = D_MODEL // X_HEADS
N_EXPERTS, N_GROUPS, TOP_K = 16, 4, 2
EXPERTS_PER_GROUP = N_EXPERTS // N_GROUPS
D_FF = 256
EPS = 1e-6

LANES = 128
SUBLANES = 8
VMEM_LIMIT = 56 * 1024 * 1024

OFF_U, OFF_CQ, OFF_BG, P_EXT = 0, 512, 1024, 2048
QK_DIM = 128
POOL_HALO = 16
MOE_ROWS = 160
ROUTE_ROWS = 16
NEG_BIG = -1e30
LOG2_E = 1.4426950408889634


def _dot(a, b):
    return jnp.dot(a, b, preferred_element_type=F32)


def _dot_nt(a, b):
    return lax.dot_general(a, b, (((1,), (1,)), ((), ())), preferred_element_type=F32)


def _rms(x, g):
    return x * lax.rsqrt(jnp.mean(x * x, axis=-1, keepdims=True) + EPS) * g


def _layer_norm(x, g, b):
    mu = jnp.mean(x, axis=-1, keepdims=True)
    xc = x - mu
    var = jnp.mean(xc * xc, axis=-1, keepdims=True)
    return xc * lax.rsqrt(var + EPS) * g + b


def _params(n_grid):
    return pltpu.CompilerParams(dimension_semantics=("arbitrary",) * n_grid,
                                vmem_limit_bytes=VMEM_LIMIT)


def _run_interleaved(chains):
    live = list(chains)
    while live:
        for c in list(live):
            try:
                next(c)
            except StopIteration:
                live.remove(c)


def _const_spec(shape):
    zeros = (0,) * len(shape)
    return pl.BlockSpec(shape, lambda *_: zeros)


def _layer_spec(stacked, layer, single_buffer=False):
    index = (layer,) + (0,) * (stacked.ndim - 1)
    mode = dict(pipeline_mode=pl.Buffered(1)) if single_buffer else {}
    return pl.BlockSpec((None,) + stacked.shape[1:], lambda *_: index, **mode)


def _mem_kv_kernel(mem_ref, wk_ref, wv_ref, k_ref, v_ref):
    m = mem_ref[0].astype(BF16)
    k_ref[0, 0] = _dot(m, wk_ref[0]).astype(BF16)
    v_ref[0, 0] = _dot(m, wv_ref[0]).astype(BF16)


def _mem_kv(mem, wk, wv):
    depth = wk.shape[0]
    bsz, mlen, d = mem.shape
    out = jax.ShapeDtypeStruct((depth, bsz, mlen, d), BF16)
    return pl.pallas_call(
        _mem_kv_kernel,
        grid=(depth, bsz),
        in_specs=[pl.BlockSpec((1, mlen, d), lambda l, b: (b, 0, 0)),
                  pl.BlockSpec((1, d, d), lambda l, b: (l, 0, 0)),
                  pl.BlockSpec((1, d, d), lambda l, b: (l, 0, 0))],
        out_specs=[pl.BlockSpec((1, 1, mlen, d), lambda l, b: (l, b, 0, 0)),
                   pl.BlockSpec((1, 1, mlen, d), lambda l, b: (l, b, 0, 0))],
        out_shape=[out, out],
        compiler_params=_params(2),
        name="mem_kv",
    )(mem, wk, wv)


def _mixer_in_kernel(x_ref, csq_ref, csk_ref, w_in_ref, vng_ref, ws_ref, bs_ref, qng_ref,
                     wuq_ref, kvng_ref, wk_ref, wv_ref, convw_ref, poolw_ref, pscale_ref,
                     q_ref, k_ref, v_ref, y_ref,
                     conv_buf, pool_a, pool_b, pool_carry, *, tm):
    si = pl.program_id(1)

    @pl.when(si == 0)
    def _():
        pool_carry[...] = jnp.zeros((POOL_HALO, D_WIDTH), F32)
        conv_buf[0:SUBLANES, :] = jnp.zeros((SUBLANES, C_WIDTH), F32)

    xb = x_ref[0].astype(BF16)

    za = _dot(xb, w_in_ref[:, OFF_U:OFF_CQ])
    zb = _dot(xb, w_in_ref[:, OFF_CQ:OFF_BG])
    zcd = _dot(xb, w_in_ref[:, OFF_BG:P_EXT])

    za = jax.nn.gelu(za)
    u = za[:, :A_WIDTH]
    v = _rms(za[:, A_WIDTH:], vng_ref[...]).astype(BF16)
    cq = _rms(zb[:, :Q_LORA], qng_ref[...]).astype(BF16)
    ckv = _rms(zb[:, Q_LORA:Q_LORA + KV_LORA], kvng_ref[...]).astype(BF16)

    row = lax.broadcasted_iota(jnp.int32, (CHUNK, CHUNK), 0)
    col = lax.broadcasted_iota(jnp.int32, (CHUNK, CHUNK), 1)
    w_causal = [jnp.where(row >= col, ws_ref[h], jnp.zeros((), BF16)) for h in range(A_HEADS)]
    lane_head = lax.broadcasted_iota(jnp.int32, (CHUNK, A_WIDTH), 1) // A_HEAD_DIM
    for c in range(tm // CHUNK):
        rows = slice(c * CHUNK, (c + 1) * CHUNK)
        vc = v[rows]
        mixed = _dot(w_causal[0], vc)
        for h in range(1, A_HEADS):
            mixed = jnp.where(lane_head == h, _dot(w_causal[h], vc), mixed)
        y_ref[0, rows, 0:A_WIDTH] = (u[rows] * (mixed + bs_ref[...])).astype(BF16)

    qt = _dot_nt(wuq_ref[...], cq)
    kk = _dot(ckv, wk_ref[...])
    vt = _dot_nt(wv_ref[...], ckv)

    zd = zcd[:, 3 * C_WIDTH:]
    base = SUBLANES
    n = tm + POOL_HALO
    pool_a[0:base, :] = jnp.zeros((base, D_WIDTH), F32)
    pool_b[0:base, :] = jnp.zeros((base, D_WIDTH), F32)
    pool_a[base:base + POOL_HALO, :] = pool_carry[...]
    pool_a[base + POOL_HALO:base + n, :] = zd
    pool_carry[...] = zd[tm - POOL_HALO:, :]
    lane_d = lax.broadcasted_iota(jnp.int32, (n, D_WIDTH), 1)
    pool_b[base:base + n, :] = pool_a[base:base + n, :] + pool_a[base - 1:base - 1 + n, :]
    pool_a[base:base + n, :] = pool_b[base:base + n, :] + jnp.where(
        lane_d >= D_GROUP, pool_b[base - 2:base - 2 + n, :], 0.0)
    pool_b[base:base + n, :] = pool_a[base:base + n, :] + jnp.where(
        lane_d >= 2 * D_GROUP, pool_a[base - 4:base - 4 + n, :], 0.0)
    t0 = base + POOL_HALO
    lane_t = lax.broadcasted_iota(jnp.int32, (tm, D_WIDTH), 1)
    win_sum = pool_b[t0:t0 + tm, :] + jnp.where(
        lane_t >= 3 * D_GROUP, pool_b[t0 - 8:t0 - 8 + tm, :], 0.0)
    pos1 = si * tm + lax.broadcasted_iota(jnp.int32, (tm, D_WIDTH), 0) + 1
    window = jnp.left_shift(2, lane_t // D_GROUP)
    count = jnp.minimum(pos1, window).astype(F32)
    pooled = (win_sum / count - zd).astype(BF16)
    yd = _dot(pooled, poolw_ref[...]) * pscale_ref[...]
    y_ref[0, :, A_WIDTH + C_WIDTH:] = yd.astype(BF16)

    gh = zcd[:, C_WIDTH:2 * C_WIDTH] * zcd[:, 2 * C_WIDTH:3 * C_WIDTH]
    conv_buf[SUBLANES:SUBLANES + tm, :] = gh
    conv = (convw_ref[2:3, :] * gh
            + convw_ref[1:2, :] * conv_buf[SUBLANES - 1:SUBLANES - 1 + tm, :]
            + convw_ref[0:1, :] * conv_buf[SUBLANES - 2:SUBLANES - 2 + tm, :])
    y_ref[0, :, A_WIDTH:A_WIDTH + C_WIDTH] = (zcd[:, :C_WIDTH] * conv).astype(BF16)
    conv_buf[0:SUBLANES, :] = conv_buf[tm:tm + SUBLANES, :]

    csq = csq_ref[0, 0]
    for h in range(MLA_HEADS):
        q_ref[0, h, 0] = (qt[h * QK_DIM:(h + 1) * QK_DIM, :] * csq).astype(BF16)
    r4 = zb[:, Q_LORA + KV_LORA:] * csk_ref[0]
    kr = r4 + pltpu.roll(r4, MLA_ROPE, 1)
    lane = lax.broadcasted_iota(jnp.int32, (tm, QK_DIM), 1)
    kr = jnp.where(lane >= MLA_NOPE, kr, 0.0)
    for h in range(MLA_HEADS):
        k_ref[0, h] = (kk[:, h * QK_DIM:(h + 1) * QK_DIM] + kr).astype(BF16)
    ones_row = jnp.where(lax.broadcasted_iota(jnp.int32, (LANES, tm), 0) == MLA_V, 1.0, 0.0)
    for h in range(MLA_HEADS):
        v_ref[0, h, 0] = (vt[h * LANES:(h + 1) * LANES, :] + ones_row).astype(BF16)


def _mixer_in(x, csq, csk, w_in, vng, ws, bs, qng, wuq, kvng, wk, wv, convw, poolw, pscale,
              *, layer, tm):
    bsz, seq, d = x.shape
    tok = lambda w: pl.BlockSpec((1, tm, w), lambda b, s: (b, s, 0))
    head = lambda n, w: pl.BlockSpec((1, n, tm, w), lambda b, s: (b, 0, s, 0))
    head_t = pl.BlockSpec((1, MLA_HEADS, 1, LANES, tm), lambda b, s: (b, 0, s, 0, 0))
    shape_t = jax.ShapeDtypeStruct((bsz, MLA_HEADS, seq // tm, LANES, tm), BF16)
    return pl.pallas_call(
        functools.partial(_mixer_in_kernel, tm=tm),
        grid=(bsz, seq // tm),
        in_specs=[tok(d), pl.BlockSpec((1, 1, QK_DIM, tm), lambda b, s: (b, s, 0, 0)), tok(QK_DIM),
                  *[_layer_spec(w, layer) for w in (w_in, vng, ws, bs, qng, wuq, kvng, wk, wv,
                                                     convw, poolw, pscale)]],
        out_specs=[head_t, head(MLA_HEADS, QK_DIM), head_t, tok(A_WIDTH + C_WIDTH + D_WIDTH)],
        out_shape=[shape_t,
                   jax.ShapeDtypeStruct((bsz, MLA_HEADS, seq, QK_DIM), BF16),
                   shape_t,
                   jax.ShapeDtypeStruct((bsz, seq, A_WIDTH + C_WIDTH + D_WIDTH), BF16)],
        scratch_shapes=[pltpu.VMEM((tm + SUBLANES, C_WIDTH), F32),
                        pltpu.VMEM((tm + POOL_HALO + SUBLANES, D_WIDTH), F32),
                        pltpu.VMEM((tm + POOL_HALO + SUBLANES, D_WIDTH), F32),
                        pltpu.VMEM((POOL_HALO, D_WIDTH), F32)],
        compiler_params=_params(2),
        name="mixer_in",
    )(x, csq, csk, w_in, vng, ws, bs, qng, wuq, kvng, wk, wv, convw, poolw, pscale)


def _mla_attn_kernel(qt_ref, k_ref, vt_ref, o_ref, m_ref, acc_ref, *, tq, hg):
    qi = pl.program_id(2)
    half = tq // 2
    m_ref[...] = jnp.full(m_ref.shape, NEG_BIG, F32)
    acc_ref[...] = jnp.zeros(acc_ref.shape, F32)

    def step(j, diagonal):
        base = pl.multiple_of(j * tq, tq)
        pieces = ([(hh, 0, 0) for hh in range(hg)]
                  + [(hh, half, half if diagonal else 0) for hh in range(hg)])

        def scores(piece):
            hh, k0, q0 = piece
            return _dot(k_ref[0, hh, pl.ds(base + k0, half), :], qt_ref[0, hh, 0, :, q0:])

        def softmax(piece, st):
            hh, k0, q0 = piece
            if diagonal:
                key = lax.broadcasted_iota(jnp.int32, st.shape, 0)
                qry = lax.broadcasted_iota(jnp.int32, st.shape, 1)
                st = jnp.where(key <= qry, st, NEG_BIG)
            m_old = m_ref[hh, :, q0:]
            m_new = jnp.maximum(m_old, jnp.max(st, axis=0, keepdims=True))
            m_ref[hh, :, q0:] = m_new
            return piece, jnp.exp2(st - m_new).astype(BF16), jnp.exp2(m_old - m_new)

        def weighted_values(piece, probs, rescale):
            hh, k0, q0 = piece
            pv = _dot(vt_ref[0, hh, j, :, k0:k0 + half], probs)
            acc_ref[hh, :, q0:] = rescale * acc_ref[hh, :, q0:] + pv

        st_next = scores(pieces[0])
        pending = None
        for i, piece in enumerate(pieces):
            st = st_next
            if i + 1 < len(pieces):
                st_next = scores(pieces[i + 1])
            if pending is not None:
                weighted_values(*pending)
            pending = softmax(piece, st)
        weighted_values(*pending)

    def body(j, carry):
        step(j, False)
        return carry

    lax.fori_loop(0, qi, body, 0)
    step(qi, True)
    for pr in range(hg // 2):
        halves = []
        for hh in (2 * pr, 2 * pr + 1):
            acc = acc_ref[hh]
            halves.append(acc[:MLA_V] / acc[MLA_V:MLA_V + 1])
        o_ref[0, :, pr * LANES:(pr + 1) * LANES] = jnp.concatenate(halves, axis=0).T.astype(BF16)


def _mla_attn(qt, k, vt, *, hg):
    bsz, heads, n_tiles, _, tq = qt.shape
    seq = n_tiles * tq
    return pl.pallas_call(
        functools.partial(_mla_attn_kernel, tq=tq, hg=hg),
        grid=(bsz, heads // hg, n_tiles),
        in_specs=[pl.BlockSpec((1, hg, 1, QK_DIM, tq), lambda b, g, i: (b, g, i, 0, 0)),
                  pl.BlockSpec((1, hg, seq, QK_DIM), lambda b, g, i: (b, g, 0, 0)),
                  pl.BlockSpec((1, hg, n_tiles, LANES, tq), lambda b, g, i: (b, g, 0, 0, 0))],
        out_specs=pl.BlockSpec((1, tq, hg * MLA_V), lambda b, g, i: (b, i, g)),
        out_shape=jax.ShapeDtypeStruct((bsz, seq, MLA_WIDTH), BF16),
        scratch_shapes=[pltpu.VMEM((hg, 1, tq), F32), pltpu.VMEM((hg, LANES, tq), F32)],
        compiler_params=_params(3),
        name="mla_attn",
    )(qt, k, vt)


def _route_t(logits_t, bias_t):
    scores = jax.nn.sigmoid(logits_t)
    sel_all = scores + bias_t
    sel = [sel_all[e:e + 1] for e in range(N_EXPERTS)]
    group_score = []
    for g in range(N_GROUPS):
        v = sel[g * EXPERTS_PER_GROUP:(g + 1) * EXPERTS_PER_GROUP]
        best = None
        for i in range(EXPERTS_PER_GROUP):
            for j in range(i + 1, EXPERTS_PER_GROUP):
                best = v[i] + v[j] if best is None else jnp.maximum(best, v[i] + v[j])
        group_score.append(best)
    top, top_idx = group_score[0], jnp.zeros(group_score[0].shape, jnp.int32)
    for g in range(1, N_GROUPS):
        better = group_score[g] > top
        top = jnp.where(better, group_score[g], top)
        top_idx = jnp.where(better, g, top_idx)
    member = [jnp.where(top_idx == g, 1.0, 0.0) for g in range(N_GROUPS)]
    weights = []
    for e in range(N_EXPERTS):
        g = e // EXPERTS_PER_GROUP
        rank = jnp.zeros(sel[e].shape, jnp.int32)
        for j in range(g * EXPERTS_PER_GROUP, (g + 1) * EXPERTS_PER_GROUP):
            if j != e:
                rank = rank + jnp.where(sel[j] > sel[e], 1,
                                        jnp.where(sel[j] == sel[e], 1 if j < e else 0, 0))
        weights.append(jnp.where(rank < TOP_K, member[g], 0.0) * scores[e:e + 1])
    total = weights[0]
    for w in weights[1:]:
        total = total + w
    return [w / total for w in weights], member


def _post_attn_kernel(yacd_ref, yb_ref, x_ref, wo_acd_ref, wo_b_ref, g1_ref, b1_ref, wq_ref,
                      km_ref, vm_ref, wo_ref, g2_ref, b2_ref, rw_ref,
                      x2_ref, logits_ref, *, tm, alpha, n_chains):
    def chain(rows):
        h = _dot(yacd_ref[0, rows], wo_acd_ref[...]) + _dot(yb_ref[0, rows], wo_b_ref[...])
        yield
        x1 = _layer_norm(alpha * x_ref[0, rows] + h, g1_ref[...], b1_ref[...])
        q = _dot(x1.astype(BF16), wq_ref[...])
        yield
        heads = []
        for hd in range(X_HEADS):
            cols = slice(hd * X_HEAD_DIM, (hd + 1) * X_HEAD_DIM)
            s = _dot_nt(q[:, cols].astype(BF16), km_ref[0, :, cols])
            e = jnp.exp(s - jnp.max(s, axis=-1, keepdims=True))
            p = e / jnp.sum(e, axis=-1, keepdims=True)
            heads.append(_dot(p.astype(BF16), vm_ref[0, :, cols]).astype(BF16))
            yield
        h2 = _dot(jnp.concatenate(heads, axis=1), wo_ref[...])
        yield
        x2 = _layer_norm(alpha * x1 + h2, g2_ref[...], b2_ref[...])
        x2_ref[0, rows] = x2
        x_hi = x2.astype(BF16)
        x_lo = (x2 - x_hi.astype(F32)).astype(BF16)
        hl = _dot_nt(rw_ref[...], x_hi)
        lo = _dot_nt(rw_ref[0:N_EXPERTS, :], x_lo)
        logits_ref[0, :, rows] = hl[:N_EXPERTS] + hl[N_EXPERTS:] + lo

    rows_per = tm // n_chains
    _run_interleaved([chain(slice(c * rows_per, (c + 1) * rows_per)) for c in range(n_chains)])


def _post_attn(yacd, yb, x, wo_acd, wo_b, g1, b1, wq, km, vm, wo, g2, b2, rw,
               *, layer, tm, alpha):
    bsz, seq, d = x.shape
    tok = lambda w: pl.BlockSpec((1, tm, w), lambda b, s: (b, s, 0))
    memspec = pl.BlockSpec((None, 1) + km.shape[2:], lambda b, s: (layer, b, 0, 0))
    per_layer = lambda w: _layer_spec(w, layer)
    return pl.pallas_call(
        functools.partial(_post_attn_kernel, tm=tm, alpha=alpha, n_chains=2),
        grid=(bsz, seq // tm),
        in_specs=[tok(yacd.shape[-1]), tok(yb.shape[-1]), tok(d),
                  per_layer(wo_acd), per_layer(wo_b), per_layer(g1), per_layer(b1), per_layer(wq),
                  memspec, memspec, per_layer(wo), per_layer(g2), per_layer(b2),
                  _const_spec(rw.shape)],
        out_specs=[tok(d), pl.BlockSpec((1, N_EXPERTS, tm), lambda b, s: (b, 0, s))],
        out_shape=[jax.ShapeDtypeStruct((bsz, seq, d), F32),
                   jax.ShapeDtypeStruct((bsz, N_EXPERTS, seq), F32)],
        compiler_params=_params(2),
        name="post_attn",
    )(yacd, yb, x, wo_acd, wo_b, g1, b1, wq, km, vm, wo, g2, b2, rw)


def _dot_tn(a, b):
    return lax.dot_general(a, b, (((0,), (0,)), ((), ())), preferred_element_type=F32)


def _moe_kernel(x_ref, logits_ref, rb_ref, before_ref, wg_ref, wu_ref, wd_ref, g3_ref, b3_ref, o_ref,
                xb_ref, pos_ref, memb_ref, gsplit_ref, acc_ref, *, alpha, tm, rows):
    group_w = EXPERTS_PER_GROUP * D_FF
    x = x_ref[0]
    xb_ref[...] = x.astype(BF16)
    gates, member = _route_t(logits_ref[0], rb_ref[...])
    memb = jnp.concatenate(member + [jnp.zeros((ROUTE_ROWS - N_GROUPS, tm), F32)], axis=0)
    memb_ref[...] = memb
    pos_ref[...] = _dot(memb.astype(BF16), before_ref[...])
    gate_rows = jnp.concatenate(gates, axis=0)
    g_hi = gate_rows.astype(BF16)
    gsplit_ref[...] = jnp.concatenate([g_hi, (gate_rows - g_hi.astype(F32)).astype(BF16)], axis=0)

    def expert_chunk(g, ci):
        slot = (lax.broadcasted_iota(jnp.int32, (rows, tm), 0) + ci * rows).astype(F32)
        p = jnp.where(pos_ref[g:g + 1, :] == slot, memb_ref[g:g + 1, :], 0.0).astype(BF16)
        xg = _dot(p, xb_ref[...]).astype(BF16)
        gg = _dot_nt(p, gsplit_ref[...])
        hs = []
        for j in range(EXPERTS_PER_GROUP):
            e = g * EXPERTS_PER_GROUP + j
            cols = slice(e * D_FF, (e + 1) * D_FF)
            gate = gg[:, e:e + 1] + gg[:, N_EXPERTS + e:N_EXPERTS + e + 1]
            hg = _dot(xg, wg_ref[:, cols])
            hu = _dot(xg, wu_ref[:, cols])
            hs.append((jax.nn.silu(hg) * hu * gate).astype(BF16))
        y = _dot(jnp.concatenate(hs, axis=1), wd_ref[g * group_w:(g + 1) * group_w, :])
        return p, y.astype(BF16)

    first = [expert_chunk(g, 0) for g in range(N_GROUPS)]
    acc_ref[...] = _dot_tn(jnp.concatenate([p for p, _ in first], axis=0),
                           jnp.concatenate([y for _, y in first], axis=0))
    largest = jnp.max(jnp.sum(memb_ref[0:SUBLANES, :], axis=1, keepdims=True))

    @pl.when(largest > rows)
    def _():
        for g in range(N_GROUPS):
            n_tok = jnp.sum(memb_ref[g:g + 1, :]).astype(jnp.int32)

            def extra(ci, carry, g=g):
                p, y = expert_chunk(g, ci)
                acc_ref[...] += _dot_tn(p, y)
                return carry

            lax.fori_loop(1, (n_tok + rows - 1) // rows, extra, 0)

    o_ref[0] = _layer_norm(alpha * x_ref[0] + acc_ref[...], g3_ref[...], b3_ref[...])


def _moe(x, logits_t, rb_t, before, wg, wu, wd, g3, b3, *, layer, tm, alpha):
    bsz, seq, d = x.shape
    tok = lambda w: pl.BlockSpec((1, tm, w), lambda b, s: (b, s, 0))
    resident = lambda w: _layer_spec(w, layer, single_buffer=True)
    return pl.pallas_call(
        functools.partial(_moe_kernel, alpha=alpha, tm=tm, rows=MOE_ROWS),
        grid=(bsz, seq // tm),
        in_specs=[tok(d), pl.BlockSpec((1, N_EXPERTS, tm), lambda b, s: (b, 0, s)),
                  _const_spec(rb_t.shape), _const_spec(before.shape),
                  resident(wg), resident(wu), resident(wd),
                  _layer_spec(g3, layer), _layer_spec(b3, layer)],
        out_specs=tok(d),
        out_shape=jax.ShapeDtypeStruct((bsz, seq, d), F32),
        scratch_shapes=[pltpu.VMEM((tm, d), BF16),
                        pltpu.VMEM((ROUTE_ROWS, tm), F32),
                        pltpu.VMEM((ROUTE_ROWS, tm), F32),
                        pltpu.VMEM((2 * N_EXPERTS, tm), BF16),
                        pltpu.VMEM((tm, d), F32)],
        compiler_params=_params(2),
        name="moe",
    )(x, logits_t, rb_t, before, wg, wu, wd, g3, b3)


def _rot_cols(w):
    half = w.shape[-1] // 2
    return jnp.concatenate([-w[..., half:], w[..., :half]], axis=-1)


def _prep_w_in(w_in):
    u, v = w_in[..., 0:256], w_in[..., 256:512]
    cq, ckv, kr = w_in[..., 512:768], w_in[..., 768:896], w_in[..., 896:928]
    rest = w_in[..., 928:]
    rope4 = jnp.concatenate([kr, _rot_cols(kr), kr, _rot_cols(kr)], axis=-1)
    return jnp.concatenate([u, v, cq, ckv, rope4, rest], axis=-1).astype(BF16)


def _prep_w_uq(w_uq):
    depth = w_uq.shape[0]
    w = w_uq.reshape(depth, Q_LORA, MLA_HEADS, MLA_NOPE + MLA_ROPE)
    nope, rope = w[..., :MLA_NOPE], w[..., MLA_NOPE:]
    ext = jnp.concatenate([nope, rope, _rot_cols(rope)], axis=-1)
    return ext.reshape(depth, Q_LORA, MLA_HEADS * QK_DIM).astype(BF16)


def _prep_w_ukv(w_ukv):
    depth = w_ukv.shape[0]
    w = w_ukv.reshape(depth, KV_LORA, MLA_HEADS, MLA_NOPE + MLA_V)
    k_nope, v = w[..., :MLA_NOPE], w[..., MLA_NOPE:]
    wk = jnp.concatenate([k_nope, jnp.zeros_like(k_nope)], axis=-1)
    wv = jnp.concatenate([v, jnp.zeros_like(v)], axis=-1)
    return (wk.reshape(depth, KV_LORA, MLA_HEADS * QK_DIM).astype(BF16),
            wv.reshape(depth, KV_LORA, MLA_HEADS * LANES).astype(BF16))


def _prep_pool_w(pool_w):
    depth, groups = pool_w.shape[:2]
    eye = jnp.eye(groups, dtype=pool_w.dtype)
    bd = jnp.einsum('lgcd,gh->lgchd', pool_w, eye)
    return bd.reshape(depth, D_WIDTH, D_WIDTH).astype(BF16)


def kernel(x, mem, positions, w_in, gmlp_v_norm_g, gmlp_w_s, gmlp_b_s, mla_q_norm_g, mla_w_uq,
           mla_kv_norm_g, mla_w_ukv, conv_w, pool_w, pool_scale, w_out, ln1_g, ln1_b,
           xattn_wq, xattn_wk, xattn_wv, xattn_wo, ln2_g, ln2_b, router_w, router_b,
           moe_w_gate, moe_w_up, moe_w_down, ln3_g, ln3_b):
    depth = w_in.shape[0]
    alpha = (2 * depth) ** 0.25
    tm = 512
    bsz, seq, _ = x.shape

    inv_freq = ROPE_BASE ** (-jnp.arange(0, MLA_ROPE, 2, dtype=F32) / MLA_ROPE)
    ang = positions.astype(F32)[..., None] * inv_freq
    cos2 = jnp.tile(jnp.cos(ang), (1, 1, 2))
    sin2 = jnp.tile(jnp.sin(ang), (1, 1, 2))
    scale = (MLA_NOPE + MLA_ROPE) ** -0.5 * LOG2_E
    csq = scale * jnp.concatenate([jnp.ones(cos2.shape[:2] + (MLA_NOPE,), F32), cos2, sin2], -1)
    csq = jnp.swapaxes(csq.reshape(bsz, seq // tm, tm, QK_DIM), 2, 3)
    csk = jnp.concatenate([cos2, sin2, cos2, sin2], axis=-1)

    row = lambda a: a[:, None, :]
    w_in_e = _prep_w_in(w_in)
    w_uq_e = jnp.swapaxes(_prep_w_uq(mla_w_uq), 1, 2)
    w_k_e, w_v_e = _prep_w_ukv(mla_w_ukv)
    w_v_e = jnp.swapaxes(w_v_e, 1, 2)
    ws = gmlp_w_s.astype(BF16)
    bs = jnp.repeat(jnp.swapaxes(gmlp_b_s, 1, 2), A_HEAD_DIM, axis=2)
    pool_bd = _prep_pool_w(pool_w)
    wo_acd = jnp.concatenate([w_out[:, :A_WIDTH], w_out[:, A_WIDTH + MLA_WIDTH:]], axis=1).astype(BF16)
    wo_b = w_out[:, A_WIDTH:A_WIDTH + MLA_WIDTH].astype(BF16)
    wq = (xattn_wq * (X_HEAD_DIM ** -0.5)).astype(BF16)
    wo = xattn_wo.astype(BF16)
    rw_t = router_w.T
    rw_hi = rw_t.astype(BF16)
    rw = jnp.concatenate([rw_hi, (rw_t - rw_hi.astype(F32)).astype(BF16)], axis=0)
    rb_t = jnp.broadcast_to(router_b[:, None], (N_EXPERTS, tm))
    before = jnp.triu(jnp.ones((tm, tm), BF16), 1)
    cat_experts = lambda w: jnp.swapaxes(w, 1, 2).reshape(depth, D_MODEL, N_EXPERTS * D_FF)
    wg = cat_experts(moe_w_gate).astype(BF16)
    wu = cat_experts(moe_w_up).astype(BF16)
    wd = moe_w_down.reshape(depth, N_EXPERTS * D_FF, D_MODEL).astype(BF16)

    km, vm = _mem_kv(mem, xattn_wk.astype(BF16), xattn_wv.astype(BF16))

    for l in range(depth):
        q, k, v, yacd = _mixer_in(
            x, csq, csk, w_in_e, row(gmlp_v_norm_g), ws, bs, row(mla_q_norm_g),
            w_uq_e, row(mla_kv_norm_g), w_k_e, w_v_e, conv_w, pool_bd,
            row(pool_scale), layer=l, tm=tm)
        yb = _mla_attn(q, k, v, hg=MLA_HEADS)
        x2, logits = _post_attn(
            yacd, yb, x, wo_acd, wo_b, row(ln1_g), row(ln1_b), wq, km, vm,
            wo, row(ln2_g), row(ln2_b), rw, layer=l, tm=tm, alpha=alpha)
        x = _moe(x2, logits, rb_t, before, wg, wu, wd, row(ln3_g), row(ln3_b),
                 layer=l, tm=tm, alpha=alpha)
    return x
```

```python
import functools

import jax
import jax.numpy as jnp
from jax import lax
from jax.experimental import pallas as pl
from jax.experimental.pallas import tpu as pltpu

F32 = jnp.float32
BF16 = jnp.bfloat16

D_MODEL = 1024
A_HEADS, A_HEAD_DIM, CHUNK = 4, 64, 128
A_WIDTH = A_HEADS * A_HEAD_DIM
MLA_HEADS, MLA_NOPE, MLA_ROPE, MLA_V = 8, 64, 32, 64
Q_LORA, KV_LORA = 256, 128
MLA_WIDTH = MLA_HEADS * MLA_V
ROPE_BASE = 10000.0
C_WIDTH, CONV_W = 256, 3
D_WIDTH = 256
POOL_WINDOWS = (2, 4, 8, 16)
D_GROUP = D_WIDTH // len(POOL_WINDOWS)
X_HEADS = 4
X_HEAD_DIM = D_MODEL // X_HEADS
N_EXPERTS, N_GROUPS, TOP_K = 16, 4, 2
EXPERTS_PER_GROUP = N_EXPERTS // N_GROUPS
D_FF = 256
EPS = 1e-6

LANES = 128
SUBLANES = 8
VMEM_LIMIT = 56 * 1024 * 1024

OFF_U, OFF_CQ, OFF_BG, P_EXT = 0, 512, 1024, 2048
QK_DIM = 128
POOL_HALO = 16
MOE_ROWS = 160
ROUTE_ROWS = 16
NEG_BIG = -1e30
LOG2_E = 1.4426950408889634


def _dot(a, b):
    return jnp.dot(a, b, preferred_element_type=F32)


def _dot_nt(a, b):
    return lax.dot_general(a, b, (((1,), (1,)), ((), ())), preferred_element_type=F32)


def _rms(x, g):
    return x * lax.rsqrt(jnp.mean(x * x, axis=-1, keepdims=True) + EPS) * g


def _layer_norm(x, g, b):
    mu = jnp.mean(x, axis=-1, keepdims=True)
    xc = x - mu
    var = jnp.mean(xc * xc, axis=-1, keepdims=True)
    return xc * lax.rsqrt(var + EPS) * g + b


def _params(n_grid):
    return pltpu.CompilerParams(dimension_semantics=("arbitrary",) * n_grid,
                                vmem_limit_bytes=VMEM_LIMIT)


def _run_interleaved(chains):
    live = list(chains)
    while live:
        for c in list(live):
            try:
                next(c)
            except StopIteration:
                live.remove(c)


def _const_spec(shape):
    zeros = (0,) * len(shape)
    return pl.BlockSpec(shape, lambda *_: zeros)


def _layer_spec(stacked, layer, single_buffer=False):
    index = (layer,) + (0,) * (stacked.ndim - 1)
    mode = dict(pipeline_mode=pl.Buffered(1)) if single_buffer else {}
    return pl.BlockSpec((None,) + stacked.shape[1:], lambda *_: index, **mode)


def _mem_kv_kernel(mem_ref, wk_ref, wv_ref, k_ref, v_ref):
    m = mem_ref[0].astype(BF16)
    k_ref[0, 0] = _dot(m, wk_ref[0]).astype(BF16)
    v_ref[0, 0] = _dot(m, wv_ref[0]).astype(BF16)


def _mem_kv(mem, wk, wv):
    depth = wk.shape[0]
    bsz, mlen, d = mem.shape
    out = jax.ShapeDtypeStruct((depth, bsz, mlen, d), BF16)
    return pl.pallas_call(
        _mem_kv_kernel,
        grid=(depth, bsz),
        in_specs=[pl.BlockSpec((1, mlen, d), lambda l, b: (b, 0, 0)),
                  pl.BlockSpec((1, d, d), lambda l, b: (l, 0, 0)),
                  pl.BlockSpec((1, d, d), lambda l, b: (l, 0, 0))],
        out_specs=[pl.BlockSpec((1, 1, mlen, d), lambda l, b: (l, b, 0, 0)),
                   pl.BlockSpec((1, 1, mlen, d), lambda l, b: (l, b, 0, 0))],
        out_shape=[out, out],
        compiler_params=_params(2),
        name="mem_kv",
    )(mem, wk, wv)


def _mixer_in_kernel(x_ref, csq_ref, csk_ref, w_in_ref, vng_ref, ws_ref, bs_ref, qng_ref,
                     wuq_ref, kvng_ref, wk_ref, wv_ref, convw_ref, poolw_ref, pscale_ref,
                     q_ref, k_ref, v_ref, y_ref,
                     conv_buf, pool_a, pool_b, pool_carry, *, tm):
    si = pl.program_id(1)

    @pl.when(si == 0)
    def _():
        pool_carry[...] = jnp.zeros((POOL_HALO, D_WIDTH), F32)
        conv_buf[0:SUBLANES, :] = jnp.zeros((SUBLANES, C_WIDTH), F32)

    xb = x_ref[0].astype(BF16)

    za = _dot(xb, w_in_ref[:, OFF_U:OFF_CQ])
    zb = _dot(xb, w_in_ref[:, OFF_CQ:OFF_BG])
    zcd = _dot(xb, w_in_ref[:, OFF_BG:P_EXT])

    za = jax.nn.gelu(za)
    u = za[:, :A_WIDTH]
    v = _rms(za[:, A_WIDTH:], vng_ref[...]).astype(BF16)
    cq = _rms(zb[:, :Q_LORA], qng_ref[...]).astype(BF16)
    ckv = _rms(zb[:, Q_LORA:Q_LORA + KV_LORA], kvng_ref[...]).astype(BF16)

    row = lax.broadcasted_iota(jnp.int32, (CHUNK, CHUNK), 0)
    col = lax.broadcasted_iota(jnp.int32, (CHUNK, CHUNK), 1)
    w_causal = [jnp.where(row >= col, ws_ref[h], jnp.zeros((), BF16)) for h in range(A_HEADS)]
    lane_head = lax.broadcasted_iota(jnp.int32, (CHUNK, A_WIDTH), 1) // A_HEAD_DIM
    for c in range(tm // CHUNK):
        rows = slice(c * CHUNK, (c + 1) * CHUNK)
        vc = v[rows]
        mixed = _dot(w_causal[0], vc)
        for h in range(1, A_HEADS):
            mixed = jnp.where(lane_head == h, _dot(w_causal[h], vc), mixed)
        y_ref[0, rows, 0:A_WIDTH] = (u[rows] * (mixed + bs_ref[...])).astype(BF16)

    qt = _dot_nt(wuq_ref[...], cq)
    kk = _dot(ckv, wk_ref[...])
    vt = _dot_nt(wv_ref[...], ckv)

    zd = zcd[:, 3 * C_WIDTH:]
    base = SUBLANES
    n = tm + POOL_HALO
    pool_a[0:base, :] = jnp.zeros((base, D_WIDTH), F32)
    pool_b[0:base, :] = jnp.zeros((base, D_WIDTH), F32)
    pool_a[base:base + POOL_HALO, :] = pool_carry[...]
    pool_a[base + POOL_HALO:base + n, :] = zd
    pool_carry[...] = zd[tm - POOL_HALO:, :]
    lane_d = lax.broadcasted_iota(jnp.int32, (n, D_WIDTH), 1)
    pool_b[base:base + n, :] = pool_a[base:base + n, :] + pool_a[base - 1:base - 1 + n, :]
    pool_a[base:base + n, :] = pool_b[base:base + n, :] + jnp.where(
        lane_d >= D_GROUP, pool_b[base - 2:base - 2 + n, :], 0.0)
    pool_b[base:base + n, :] = pool_a[base:base + n, :] + jnp.where(
        lane_d >= 2 * D_GROUP, pool_a[base - 4:base - 4 + n, :], 0.0)
    t0 = base + POOL_HALO
    lane_t = lax.broadcasted_iota(jnp.int32, (tm, D_WIDTH), 1)
    win_sum = pool_b[t0:t0 + tm, :] + jnp.where(
        lane_t >= 3 * D_GROUP, pool_b[t0 - 8:t0 - 8 + tm, :], 0.0)
    pos1 = si * tm + lax.broadcasted_iota(jnp.int32, (tm, D_WIDTH), 0) + 1
    window = jnp.left_shift(2, lane_t // D_GROUP)
    count = jnp.minimum(pos1, window).astype(F32)
    pooled = (win_sum / count - zd).astype(BF16)
    yd = _dot(pooled, poolw_ref[...]) * pscale_ref[...]
    y_ref[0, :, A_WIDTH + C_WIDTH:] = yd.astype(BF16)

    gh = zcd[:, C_WIDTH:2 * C_WIDTH] * zcd[:, 2 * C_WIDTH:3 * C_WIDTH]
    conv_buf[SUBLANES:SUBLANES + tm, :] = gh
    conv = (convw_ref[2:3, :] * gh
            + convw_ref[1:2, :] * conv_buf[SUBLANES - 1:SUBLANES - 1 + tm, :]
            + convw_ref[0:1, :] * conv_buf[SUBLANES - 2:SUBLANES - 2 + tm, :])
    y_ref[0, :, A_WIDTH:A_WIDTH + C_WIDTH] = (zcd[:, :C_WIDTH] * conv).astype(BF16)
    conv_buf[0:SUBLANES, :] = conv_buf[tm:tm + SUBLANES, :]

    csq = csq_ref[0, 0]
    for h in range(MLA_HEADS):
        q_ref[0, h, 0] = (qt[h * QK_DIM:(h + 1) * QK_DIM, :] * csq).astype(BF16)
    r4 = zb[:, Q_LORA + KV_LORA:] * csk_ref[0]
    kr = r4 + pltpu.roll(r4, MLA_ROPE, 1)
    lane = lax.broadcasted_iota(jnp.int32, (tm, QK_DIM), 1)
    kr = jnp.where(lane >= MLA_NOPE, kr, 0.0)
    for h in range(MLA_HEADS):
        k_ref[0, h] = (kk[:, h * QK_DIM:(h + 1) * QK_DIM] + kr).astype(BF16)
    ones_row = jnp.where(lax.broadcasted_iota(jnp.int32, (LANES, tm), 0) == MLA_V, 1.0, 0.0)
    for h in range(MLA_HEADS):
        v_ref[0, h, 0] = (vt[h * LANES:(h + 1) * LANES, :] + ones_row).astype(BF16)


def _mixer_in(x, csq, csk, w_in, vng, ws, bs, qng, wuq, kvng, wk, wv, convw, poolw, pscale,
              *, layer, tm):
    bsz, seq, d = x.shape
    tok = lambda w: pl.BlockSpec((1, tm, w), lambda b, s: (b, s, 0))
    head = lambda n, w: pl.BlockSpec((1, n, tm, w), lambda b, s: (b, 0, s, 0))
    head_t = pl.BlockSpec((1, MLA_HEADS, 1, LANES, tm), lambda b, s: (b, 0, s, 0, 0))
    shape_t = jax.ShapeDtypeStruct((bsz, MLA_HEADS, seq // tm, LANES, tm), BF16)
    return pl.pallas_call(
        functools.partial(_mixer_in_kernel, tm=tm),
        grid=(bsz, seq // tm),
        in_specs=[tok(d), pl.BlockSpec((1, 1, QK_DIM, tm), lambda b, s: (b, s, 0, 0)), tok(QK_DIM),
                  *[_layer_spec(w, layer) for w in (w_in, vng, ws, bs, qng, wuq, kvng, wk, wv,
                                                     convw, poolw, pscale)]],
        out_specs=[head_t, head(MLA_HEADS, QK_DIM), head_t, tok(A_WIDTH + C_WIDTH + D_WIDTH)],
        out_shape=[shape_t,
                   jax.ShapeDtypeStruct((bsz, MLA_HEADS, seq, QK_DIM), BF16),
                   shape_t,
                   jax.ShapeDtypeStruct((bsz, seq, A_WIDTH + C_WIDTH + D_WIDTH), BF16)],
        scratch_shapes=[pltpu.VMEM((tm + SUBLANES, C_WIDTH), F32),
                        pltpu.VMEM((tm + POOL_HALO + SUBLANES, D_WIDTH), F32),
                        pltpu.VMEM((tm + POOL_HALO + SUBLANES, D_WIDTH), F32),
                        pltpu.VMEM((POOL_HALO, D_WIDTH), F32)],
        compiler_params=_params(2),
        name="mixer_in",
    )(x, csq, csk, w_in, vng, ws, bs, qng, wuq, kvng, wk, wv, convw, poolw, pscale)


def _mla_attn_kernel(qt_ref, k_ref, vt_ref, o_ref, m_ref, acc_ref, *, tq, hg):
    qi = pl.program_id(2)
    half = tq // 2
    m_ref[...] = jnp.full(m_ref.shape, NEG_BIG, F32)
    acc_ref[...] = jnp.zeros(acc_ref.shape, F32)

    def step(j, diagonal):
        base = pl.multiple_of(j * tq, tq)
        pieces = ([(hh, 0, 0) for hh in range(hg)]
                  + [(hh, half, half if diagonal else 0) for hh in range(hg)])

        def scores(piece):
            hh, k0, q0 = piece
            return _dot(k_ref[0, hh, pl.ds(base + k0, half), :], qt_ref[0, hh, 0, :, q0:])

        def softmax(piece, st):
            hh, k0, q0 = piece
            if diagonal:
                key = lax.broadcasted_iota(jnp.int32, st.shape, 0)
                qry = lax.broadcasted_iota(jnp.int32, st.shape, 1)
                st = jnp.where(key <= qry, st, NEG_BIG)
            m_old = m_ref[hh, :, q0:]
            m_new = jnp.maximum(m_old, jnp.max(st, axis=0, keepdims=True))
            m_ref[hh, :, q0:] = m_new
            return piece, jnp.exp2(st - m_new).astype(BF16), jnp.exp2(m_old - m_new)

        def weighted_values(piece, probs, rescale):
            hh, k0, q0 = piece
            pv = _dot(vt_ref[0, hh, j, :, k0:k0 + half], probs)
            acc_ref[hh, :, q0:] = rescale * acc_ref[hh, :, q0:] + pv

        st_next = scores(pieces[0])
        pending = None
        for i, piece in enumerate(pieces):
            st = st_next
            if i + 1 < len(pieces):
                st_next = scores(pieces[i + 1])
            if pending is not None:
                weighted_values(*pending)
            pending = softmax(piece, st)
        weighted_values(*pending)

    def body(j, carry):
        step(j, False)
        return carry

    lax.fori_loop(0, qi, body, 0)
    step(qi, True)
    for pr in range(hg // 2):
        halves = []
        for hh in (2 * pr, 2 * pr + 1):
            acc = acc_ref[hh]
            halves.append(acc[:MLA_V] / acc[MLA_V:MLA_V + 1])
        o_ref[0, :, pr * LANES:(pr + 1) * LANES] = jnp.concatenate(halves, axis=0).T.astype(BF16)


def _mla_attn(qt, k, vt, *, hg):
    bsz, heads, n_tiles, _, tq = qt.shape
    seq = n_tiles * tq
    return pl.pallas_call(
        functools.partial(_mla_attn_kernel, tq=tq, hg=hg),
        grid=(bsz, heads // hg, n_tiles),
        in_specs=[pl.BlockSpec((1, hg, 1, QK_DIM, tq), lambda b, g, i: (b, g, i, 0, 0)),
                  pl.BlockSpec((1, hg, seq, QK_DIM), lambda b, g, i: (b, g, 0, 0)),
                  pl.BlockSpec((1, hg, n_tiles, LANES, tq), lambda b, g, i: (b, g, 0, 0, 0))],
        out_specs=pl.BlockSpec((1, tq, hg * MLA_V), lambda b, g, i: (b, i, g)),
        out_shape=jax.ShapeDtypeStruct((bsz, seq, MLA_WIDTH), BF16),
        scratch_shapes=[pltpu.VMEM((hg, 1, tq), F32), pltpu.VMEM((hg, LANES, tq), F32)],
        compiler_params=_params(3),
        name="mla_attn",
    )(qt, k, vt)


def _route_t(logits_t, bias_t):
    scores = jax.nn.sigmoid(logits_t)
    sel_all = scores + bias_t
    sel = [sel_all[e:e + 1] for e in range(N_EXPERTS)]
    group_score = []
    for g in range(N_GROUPS):
        v = sel[g * EXPERTS_PER_GROUP:(g + 1) * EXPERTS_PER_GROUP]
        best = None
        for i in range(EXPERTS_PER_GROUP):
            for j in range(i + 1, EXPERTS_PER_GROUP):
                best = v[i] + v[j] if best is None else jnp.maximum(best, v[i] + v[j])
        group_score.append(best)
    top, top_idx = group_score[0], jnp.zeros(group_score[0].shape, jnp.int32)
    for g in range(1, N_GROUPS):
        better = group_score[g] > top
        top = jnp.where(better, group_score[g], top)
        top_idx = jnp.where(better, g, top_idx)
    member = [jnp.where(top_idx == g, 1.0, 0.0) for g in range(N_GROUPS)]
    weights = []
    for e in range(N_EXPERTS):
        g = e // EXPERTS_PER_GROUP
        rank = jnp.zeros(sel[e].shape, jnp.int32)
        for j in range(g * EXPERTS_PER_GROUP, (g + 1) * EXPERTS_PER_GROUP):
            if j != e:
                rank = rank + jnp.where(sel[j] > sel[e], 1,
                                        jnp.where(sel[j] == sel[e], 1 if j < e else 0, 0))
        weights.append(jnp.where(rank < TOP_K, member[g], 0.0) * scores[e:e + 1])
    total = weights[0]
    for w in weights[1:]:
        total = total + w
    return [w / total for w in weights], member


def _post_attn_kernel(yacd_ref, yb_ref, x_ref, wo_acd_ref, wo_b_ref, g1_ref, b1_ref, wq_ref,
                      km_ref, vm_ref, wo_ref, g2_ref, b2_ref, rw_ref,
                      x2_ref, logits_ref, *, tm, alpha, n_chains):
    def chain(rows):
        h = _dot(yacd_ref[0, rows], wo_acd_ref[...]) + _dot(yb_ref[0, rows], wo_b_ref[...])
        yield
        x1 = _layer_norm(alpha * x_ref[0, rows] + h, g1_ref[...], b1_ref[...])
        q = _dot(x1.astype(BF16), wq_ref[...])
        yield
        heads = []
        for hd in range(X_HEADS):
            cols = slice(hd * X_HEAD_DIM, (hd + 1) * X_HEAD_DIM)
            s = _dot_nt(q[:, cols].astype(BF16), km_ref[0, :, cols])
            e = jnp.exp2(s - jnp.max(s, axis=-1, keepdims=True))
            o = _dot(e.astype(BF16), vm_ref[0, :, cols]) / jnp.sum(e, axis=-1, keepdims=True)
            heads.append(o.astype(BF16))
            yield
        h2 = _dot(jnp.concatenate(heads, axis=1), wo_ref[...])
        yield
        x2 = _layer_norm(alpha * x1 + h2, g2_ref[...], b2_ref[...])
        x2_ref[0, rows] = x2
        x_hi = x2.astype(BF16)
        x_lo = (x2 - x_hi.astype(F32)).astype(BF16)
        hl = _dot_nt(rw_ref[...], x_hi)
        lo = _dot_nt(rw_ref[0:N_EXPERTS, :], x_lo)
        logits_ref[0, :, rows] = hl[:N_EXPERTS] + hl[N_EXPERTS:] + lo

    rows_per = tm // n_chains
    _run_interleaved([chain(slice(c * rows_per, (c + 1) * rows_per)) for c in range(n_chains)])


def _post_attn(yacd, yb, x, wo_acd, wo_b, g1, b1, wq, km, vm, wo, g2, b2, rw,
               *, layer, tm, alpha):
    bsz, seq, d = x.shape
    tok = lambda w: pl.BlockSpec((1, tm, w), lambda b, s: (b, s, 0))
    memspec = pl.BlockSpec((None, 1) + km.shape[2:], lambda b, s: (layer, b, 0, 0))
    per_layer = lambda w: _layer_spec(w, layer)
    return pl.pallas_call(
        functools.partial(_post_attn_kernel, tm=tm, alpha=alpha, n_chains=2),
        grid=(bsz, seq // tm),
        in_specs=[tok(yacd.shape[-1]), tok(yb.shape[-1]), tok(d),
                  per_layer(wo_acd), per_layer(wo_b), per_layer(g1), per_layer(b1), per_layer(wq),
                  memspec, memspec, per_layer(wo), per_layer(g2), per_layer(b2),
                  _const_spec(rw.shape)],
        out_specs=[tok(d), pl.BlockSpec((1, N_EXPERTS, tm), lambda b, s: (b, 0, s))],
        out_shape=[jax.ShapeDtypeStruct((bsz, seq, d), F32),
                   jax.ShapeDtypeStruct((bsz, N_EXPERTS, seq), F32)],
        compiler_params=_params(2),
        name="post_attn",
    )(yacd, yb, x, wo_acd, wo_b, g1, b1, wq, km, vm, wo, g2, b2, rw)


def _dot_tn(a, b):
    return lax.dot_general(a, b, (((0,), (0,)), ((), ())), preferred_element_type=F32)


def _moe_kernel(x_ref, logits_ref, rb_ref, before_ref, wg_ref, wu_ref, wd_ref, g3_ref, b3_ref, o_ref,
                xb_ref, pos_ref, memb_ref, gsplit_ref, acc_ref, *, alpha, tm, rows):
    group_w = EXPERTS_PER_GROUP * D_FF
    x = x_ref[0]
    xb_ref[...] = x.astype(BF16)
    gates, member = _route_t(logits_ref[0], rb_ref[...])
    memb = jnp.concatenate(member + [jnp.zeros((ROUTE_ROWS - N_GROUPS, tm), F32)], axis=0)
    memb_ref[...] = memb
    pos_ref[...] = _dot(memb.astype(BF16), before_ref[...])
    gate_rows = jnp.concatenate(gates, axis=0)
    g_hi = gate_rows.astype(BF16)
    gsplit_ref[...] = jnp.concatenate([g_hi, (gate_rows - g_hi.astype(F32)).astype(BF16)], axis=0)

    def expert_chunk(g, ci):
        slot = (lax.broadcasted_iota(jnp.int32, (rows, tm), 0) + ci * rows).astype(F32)
        p = jnp.where(pos_ref[g:g + 1, :] == slot, memb_ref[g:g + 1, :], 0.0).astype(BF16)
        xg = _dot(p, xb_ref[...]).astype(BF16)
        gg = _dot_nt(p, gsplit_ref[...])
        hs = []
        for j in range(EXPERTS_PER_GROUP):
            e = g * EXPERTS_PER_GROUP + j
            cols = slice(e * D_FF, (e + 1) * D_FF)
            gate = gg[:, e:e + 1] + gg[:, N_EXPERTS + e:N_EXPERTS + e + 1]
            hg = _dot(xg, wg_ref[:, cols])
            hu = _dot(xg, wu_ref[:, cols])
            hs.append((jax.nn.silu(hg) * hu * gate).astype(BF16))
        y = _dot(jnp.concatenate(hs, axis=1), wd_ref[g * group_w:(g + 1) * group_w, :])
        return p, y.astype(BF16)

    first = [expert_chunk(g, 0) for g in range(N_GROUPS)]
    acc_ref[...] = _dot_tn(jnp.concatenate([p for p, _ in first], axis=0),
                           jnp.concatenate([y for _, y in first], axis=0))
    largest = jnp.max(jnp.sum(memb_ref[0:SUBLANES, :], axis=1, keepdims=True))

    @pl.when(largest > rows)
    def _():
        for g in range(N_GROUPS):
            n_tok = jnp.sum(memb_ref[g:g + 1, :]).astype(jnp.int32)

            def extra(ci, carry, g=g):
                p, y = expert_chunk(g, ci)
                acc_ref[...] += _dot_tn(p, y)
                return carry

            lax.fori_loop(1, (n_tok + rows - 1) // rows, extra, 0)

    o_ref[0] = _layer_norm(alpha * x_ref[0] + acc_ref[...], g3_ref[...], b3_ref[...])


def _moe(x, logits_t, rb_t, before, wg, wu, wd, g3, b3, *, layer, tm, alpha):
    bsz, seq, d = x.shape
    tok = lambda w: pl.BlockSpec((1, tm, w), lambda b, s: (b, s, 0))
    resident = lambda w: _layer_spec(w, layer, single_buffer=True)
    return pl.pallas_call(
        functools.partial(_moe_kernel, alpha=alpha, tm=tm, rows=MOE_ROWS),
        grid=(bsz, seq // tm),
        in_specs=[tok(d), pl.BlockSpec((1, N_EXPERTS, tm), lambda b, s: (b, 0, s)),
                  _const_spec(rb_t.shape), _const_spec(before.shape),
                  resident(wg), resident(wu), resident(wd),
                  _layer_spec(g3, layer), _layer_spec(b3, layer)],
        out_specs=tok(d),
        out_shape=jax.ShapeDtypeStruct((bsz, seq, d), F32),
        scratch_shapes=[pltpu.VMEM((tm, d), BF16),
                        pltpu.VMEM((ROUTE_ROWS, tm), F32),
                        pltpu.VMEM((ROUTE_ROWS, tm), F32),
                        pltpu.VMEM((2 * N_EXPERTS, tm), BF16),
                        pltpu.VMEM((tm, d), F32)],
        compiler_params=_params(2),
        name="moe",
    )(x, logits_t, rb_t, before, wg, wu, wd, g3, b3)


def _rot_cols(w):
    half = w.shape[-1] // 2
    return jnp.concatenate([-w[..., half:], w[..., :half]], axis=-1)


def _prep_w_in(w_in):
    u, v = w_in[..., 0:256], w_in[..., 256:512]
    cq, ckv, kr = w_in[..., 512:768], w_in[..., 768:896], w_in[..., 896:928]
    rest = w_in[..., 928:]
    rope4 = jnp.concatenate([kr, _rot_cols(kr), kr, _rot_cols(kr)], axis=-1)
    return jnp.concatenate([u, v, cq, ckv, rope4, rest], axis=-1).astype(BF16)


def _prep_w_uq(w_uq):
    depth = w_uq.shape[0]
    w = w_uq.reshape(depth, Q_LORA, MLA_HEADS, MLA_NOPE + MLA_ROPE)
    nope, rope = w[..., :MLA_NOPE], w[..., MLA_NOPE:]
    ext = jnp.concatenate([nope, rope, _rot_cols(rope)], axis=-1)
    return ext.reshape(depth, Q_LORA, MLA_HEADS * QK_DIM).astype(BF16)


def _prep_w_ukv(w_ukv):
    depth = w_ukv.shape[0]
    w = w_ukv.reshape(depth, KV_LORA, MLA_HEADS, MLA_NOPE + MLA_V)
    k_nope, v = w[..., :MLA_NOPE], w[..., MLA_NOPE:]
    wk = jnp.concatenate([k_nope, jnp.zeros_like(k_nope)], axis=-1)
    wv = jnp.concatenate([v, jnp.zeros_like(v)], axis=-1)
    return (wk.reshape(depth, KV_LORA, MLA_HEADS * QK_DIM).astype(BF16),
            wv.reshape(depth, KV_LORA, MLA_HEADS * LANES).astype(BF16))


def _prep_pool_w(pool_w):
    depth, groups = pool_w.shape[:2]
    eye = jnp.eye(groups, dtype=pool_w.dtype)
    bd = jnp.einsum('lgcd,gh->lgchd', pool_w, eye)
    return bd.reshape(depth, D_WIDTH, D_WIDTH).astype(BF16)


def kernel(x, mem, positions, w_in, gmlp_v_norm_g, gmlp_w_s, gmlp_b_s, mla_q_norm_g, mla_w_uq,
           mla_kv_norm_g, mla_w_ukv, conv_w, pool_w, pool_scale, w_out, ln1_g, ln1_b,
           xattn_wq, xattn_wk, xattn_wv, xattn_wo, ln2_g, ln2_b, router_w, router_b,
           moe_w_gate, moe_w_up, moe_w_down, ln3_g, ln3_b):
    depth = w_in.shape[0]
    alpha = (2 * depth) ** 0.25
    tm = 512
    bsz, seq, _ = x.shape

    inv_freq = ROPE_BASE ** (-jnp.arange(0, MLA_ROPE, 2, dtype=F32) / MLA_ROPE)
    ang = positions.astype(F32)[..., None] * inv_freq
    cos2 = jnp.tile(jnp.cos(ang), (1, 1, 2))
    sin2 = jnp.tile(jnp.sin(ang), (1, 1, 2))
    scale = (MLA_NOPE + MLA_ROPE) ** -0.5 * LOG2_E
    csq = scale * jnp.concatenate([jnp.ones(cos2.shape[:2] + (MLA_NOPE,), F32), cos2, sin2], -1)
    csq = jnp.swapaxes(csq.reshape(bsz, seq // tm, tm, QK_DIM), 2, 3)
    csk = jnp.concatenate([cos2, sin2, cos2, sin2], axis=-1)

    row = lambda a: a[:, None, :]
    w_in_e = _prep_w_in(w_in)
    w_uq_e = jnp.swapaxes(_prep_w_uq(mla_w_uq), 1, 2)
    w_k_e, w_v_e = _prep_w_ukv(mla_w_ukv)
    w_v_e = jnp.swapaxes(w_v_e, 1, 2)
    ws = gmlp_w_s.astype(BF16)
    bs = jnp.repeat(jnp.swapaxes(gmlp_b_s, 1, 2), A_HEAD_DIM, axis=2)
    pool_bd = _prep_pool_w(pool_w)
    wo_acd = jnp.concatenate([w_out[:, :A_WIDTH], w_out[:, A_WIDTH + MLA_WIDTH:]], axis=1).astype(BF16)
    wo_b = w_out[:, A_WIDTH:A_WIDTH + MLA_WIDTH].astype(BF16)
    wq = (xattn_wq * (X_HEAD_DIM ** -0.5 * LOG2_E)).astype(BF16)
    wo = xattn_wo.astype(BF16)
    rw_t = router_w.T
    rw_hi = rw_t.astype(BF16)
    rw = jnp.concatenate([rw_hi, (rw_t - rw_hi.astype(F32)).astype(BF16)], axis=0)
    rb_t = jnp.broadcast_to(router_b[:, None], (N_EXPERTS, tm))
    before = jnp.triu(jnp.ones((tm, tm), BF16), 1)
    cat_experts = lambda w: jnp.swapaxes(w, 1, 2).reshape(depth, D_MODEL, N_EXPERTS * D_FF)
    wg = cat_experts(moe_w_gate).astype(BF16)
    wu = cat_experts(moe_w_up).astype(BF16)
    wd = moe_w_down.reshape(depth, N_EXPERTS * D_FF, D_MODEL).astype(BF16)

    km, vm = _mem_kv(mem, xattn_wk.astype(BF16), xattn_wv.astype(BF16))

    for l in range(depth):
        q, k, v, yacd = _mixer_in(
            x, csq, csk, w_in_e, row(gmlp_v_norm_g), ws, bs, row(mla_q_norm_g),
            w_uq_e, row(mla_kv_norm_g), w_k_e, w_v_e, conv_w, pool_bd,
            row(pool_scale), layer=l, tm=tm)
        yb = _mla_attn(q, k, v, hg=MLA_HEADS)
        x2, logits = _post_attn(
            yacd, yb, x, wo_acd, wo_b, row(ln1_g), row(ln1_b), wq, km, vm,
            wo, row(ln2_g), row(ln2_b), rw, layer=l, tm=tm, alpha=alpha)
        x = _moe(x2, logits, rb_t, before, wg, wu, wd, row(ln3_g), row(ln3_b),
                 layer=l, tm=tm, alpha=alpha)
    return x
```

```python
import functools

import jax
import jax.numpy as jnp
from jax import lax
from jax.experimental import pallas as pl
from jax.experimental.pallas import tpu as pltpu

F32 = jnp.float32
BF16 = jnp.bfloat16

D_MODEL = 1024
A_HEADS, A_HEAD_DIM, CHUNK = 4, 64, 128
A_WIDTH = A_HEADS * A_HEAD_DIM
MLA_HEADS, MLA_NOPE, MLA_ROPE, MLA_V = 8, 64, 32, 64
Q_LORA, KV_LORA = 256, 128
MLA_WIDTH = MLA_HEADS * MLA_V
ROPE_BASE = 10000.0
C_WIDTH, CONV_W = 256, 3
D_WIDTH = 256
POOL_WINDOWS = (2, 4, 8, 16)
D_GROUP = D_WIDTH // len(POOL_WINDOWS)
X_HEADS = 4
X_HEAD_DIM = D_MODEL // X_HEADS
N_EXPERTS, N_GROUPS, TOP_K = 16, 4, 2
EXPERTS_PER_GROUP = N_EXPERTS // N_GROUPS
D_FF = 256
EPS = 1e-6

LANES = 128
SUBLANES = 8
VMEM_LIMIT = 56 * 1024 * 1024

OFF_U, OFF_CQ, OFF_BG, P_EXT = 0, 512, 1024, 2048
QK_DIM = 128
POOL_HALO = 16
MOE_ROWS = 160
ROUTE_ROWS = 16
NEG_BIG = -1e30
LOG2_E = 1.4426950408889634


def _dot(a, b):
    return jnp.dot(a, b, preferred_element_type=F32)


def _dot_nt(a, b):
    return lax.dot_general(a, b, (((1,), (1,)), ((), ())), preferred_element_type=F32)


def _rms(x, g):
    return x * lax.rsqrt(jnp.mean(x * x, axis=-1, keepdims=True) + EPS) * g


def _layer_norm(x, g, b):
    mu = jnp.mean(x, axis=-1, keepdims=True)
    xc = x - mu
    var = jnp.mean(xc * xc, axis=-1, keepdims=True)
    return xc * lax.rsqrt(var + EPS) * g + b


def _params(n_grid):
    return pltpu.CompilerParams(dimension_semantics=("arbitrary",) * n_grid,
                                vmem_limit_bytes=VMEM_LIMIT)


def _run_interleaved(chains):
    live = list(chains)
    while live:
        for c in list(live):
            try:
                next(c)
            except StopIteration:
                live.remove(c)


def _const_spec(shape):
    zeros = (0,) * len(shape)
    return pl.BlockSpec(shape, lambda *_: zeros)


def _layer_spec(stacked, layer, single_buffer=False):
    index = (layer,) + (0,) * (stacked.ndim - 1)
    mode = dict(pipeline_mode=pl.Buffered(1)) if single_buffer else {}
    return pl.BlockSpec((None,) + stacked.shape[1:], lambda *_: index, **mode)


def _mem_kv_kernel(mem_ref, wk_ref, wv_ref, k_ref, v_ref):
    m = mem_ref[0].astype(BF16)
    k_ref[0, 0] = _dot(m, wk_ref[0]).astype(BF16)
    v_ref[0, 0] = _dot(m, wv_ref[0]).astype(BF16)


def _mem_kv(mem, wk, wv):
    depth = wk.shape[0]
    bsz, mlen, d = mem.shape
    out = jax.ShapeDtypeStruct((depth, bsz, mlen, d), BF16)
    return pl.pallas_call(
        _mem_kv_kernel,
        grid=(depth, bsz),
        in_specs=[pl.BlockSpec((1, mlen, d), lambda l, b: (b, 0, 0)),
                  pl.BlockSpec((1, d, d), lambda l, b: (l, 0, 0)),
                  pl.BlockSpec((1, d, d), lambda l, b: (l, 0, 0))],
        out_specs=[pl.BlockSpec((1, 1, mlen, d), lambda l, b: (l, b, 0, 0)),
                   pl.BlockSpec((1, 1, mlen, d), lambda l, b: (l, b, 0, 0))],
        out_shape=[out, out],
        compiler_params=_params(2),
        name="mem_kv",
    )(mem, wk, wv)


def _mixer_in_kernel(x_ref, csq_ref, csk_ref, w_in_ref, vng_ref, ws_ref, bs_ref, qng_ref,
                     wuq_ref, kvng_ref, wk_ref, wv_ref, convw_ref, poolw_ref, pscale_ref,
                     q_ref, k_ref, v_ref, y_ref,
                     conv_buf, pool_a, pool_b, pool_carry, *, tm):
    si = pl.program_id(1)

    @pl.when(si == 0)
    def _():
        pool_carry[...] = jnp.zeros((POOL_HALO, D_WIDTH), F32)
        conv_buf[0:SUBLANES, :] = jnp.zeros((SUBLANES, C_WIDTH), F32)

    xb = x_ref[0].astype(BF16)

    za = _dot(xb, w_in_ref[:, OFF_U:OFF_CQ])
    zb = _dot(xb, w_in_ref[:, OFF_CQ:OFF_BG])
    zcd = _dot(xb, w_in_ref[:, OFF_BG:P_EXT])

    za = jax.nn.gelu(za)
    u = za[:, :A_WIDTH]
    v = _rms(za[:, A_WIDTH:], vng_ref[...]).astype(BF16)
    cq = _rms(zb[:, :Q_LORA], qng_ref[...]).astype(BF16)
    ckv = _rms(zb[:, Q_LORA:Q_LORA + KV_LORA], kvng_ref[...]).astype(BF16)

    row = lax.broadcasted_iota(jnp.int32, (CHUNK, CHUNK), 0)
    col = lax.broadcasted_iota(jnp.int32, (CHUNK, CHUNK), 1)
    w_causal = jnp.concatenate(
        [jnp.where(row >= col, ws_ref[h], jnp.zeros((), BF16)) for h in range(A_HEADS)], axis=1)
    lane_head = lax.broadcasted_iota(jnp.int32, (CHUNK, A_WIDTH), 1) // A_HEAD_DIM
    for c in range(tm // CHUNK):
        rows = slice(c * CHUNK, (c + 1) * CHUNK)
        vc = v[rows]
        v_heads = jnp.concatenate(
            [jnp.where(lane_head == h, vc, jnp.zeros((), BF16)) for h in range(A_HEADS)], axis=0)
        mixed = _dot(w_causal, v_heads)
        y_ref[0, rows, 0:A_WIDTH] = (u[rows] * (mixed + bs_ref[...])).astype(BF16)

    qt = _dot_nt(wuq_ref[...], cq)
    kk = _dot(ckv, wk_ref[...])
    vt = _dot_nt(wv_ref[...], ckv)

    zd = zcd[:, 3 * C_WIDTH:]
    base = SUBLANES
    n = tm + POOL_HALO
    pool_a[0:base, :] = jnp.zeros((base, D_WIDTH), F32)
    pool_b[0:base, :] = jnp.zeros((base, D_WIDTH), F32)
    pool_a[base:base + POOL_HALO, :] = pool_carry[...]
    pool_a[base + POOL_HALO:base + n, :] = zd
    pool_carry[...] = zd[tm - POOL_HALO:, :]
    lane_d = lax.broadcasted_iota(jnp.int32, (n, D_WIDTH), 1)
    pool_b[base:base + n, :] = pool_a[base:base + n, :] + pool_a[base - 1:base - 1 + n, :]
    pool_a[base:base + n, :] = pool_b[base:base + n, :] + jnp.where(
        lane_d >= D_GROUP, pool_b[base - 2:base - 2 + n, :], 0.0)
    pool_b[base:base + n, :] = pool_a[base:base + n, :] + jnp.where(
        lane_d >= 2 * D_GROUP, pool_a[base - 4:base - 4 + n, :], 0.0)
    t0 = base + POOL_HALO
    lane_t = lax.broadcasted_iota(jnp.int32, (tm, D_WIDTH), 1)
    win_sum = pool_b[t0:t0 + tm, :] + jnp.where(
        lane_t >= 3 * D_GROUP, pool_b[t0 - 8:t0 - 8 + tm, :], 0.0)
    pos1 = si * tm + lax.broadcasted_iota(jnp.int32, (tm, D_WIDTH), 0) + 1
    window = jnp.left_shift(2, lane_t // D_GROUP)
    count = jnp.minimum(pos1, window).astype(F32)
    pooled = (win_sum / count - zd).astype(BF16)
    yd = _dot(pooled, poolw_ref[...]) * pscale_ref[...]
    y_ref[0, :, A_WIDTH + C_WIDTH:] = yd.astype(BF16)

    gh = zcd[:, C_WIDTH:2 * C_WIDTH] * zcd[:, 2 * C_WIDTH:3 * C_WIDTH]
    conv_buf[SUBLANES:SUBLANES + tm, :] = gh
    conv = (convw_ref[2:3, :] * gh
            + convw_ref[1:2, :] * conv_buf[SUBLANES - 1:SUBLANES - 1 + tm, :]
            + convw_ref[0:1, :] * conv_buf[SUBLANES - 2:SUBLANES - 2 + tm, :])
    y_ref[0, :, A_WIDTH:A_WIDTH + C_WIDTH] = (zcd[:, :C_WIDTH] * conv).astype(BF16)
    conv_buf[0:SUBLANES, :] = conv_buf[tm:tm + SUBLANES, :]

    csq = csq_ref[0, 0]
    for h in range(MLA_HEADS):
        q_ref[0, h, 0] = (qt[h * QK_DIM:(h + 1) * QK_DIM, :] * csq).astype(BF16)
    r4 = zb[:, Q_LORA + KV_LORA:] * csk_ref[0]
    kr = r4 + pltpu.roll(r4, MLA_ROPE, 1)
    lane = lax.broadcasted_iota(jnp.int32, (tm, QK_DIM), 1)
    kr = jnp.where(lane >= MLA_NOPE, kr, 0.0)
    for h in range(MLA_HEADS):
        k_ref[0, h] = (kk[:, h * QK_DIM:(h + 1) * QK_DIM] + kr).astype(BF16)
    ones_then_zeros = jnp.where(
        lax.broadcasted_iota(jnp.int32, (LANES - MLA_V, tm), 0) == 0, 1.0, 0.0).astype(BF16)
    for h in range(MLA_HEADS):
        v_ref[0, h, 0, 0:MLA_V, :] = vt[h * MLA_V:(h + 1) * MLA_V, :].astype(BF16)
        v_ref[0, h, 0, MLA_V:, :] = ones_then_zeros


def _mixer_in(x, csq, csk, w_in, vng, ws, bs, qng, wuq, kvng, wk, wv, convw, poolw, pscale,
              *, layer, tm):
    bsz, seq, d = x.shape
    tok = lambda w: pl.BlockSpec((1, tm, w), lambda b, s: (b, s, 0))
    head = lambda n, w: pl.BlockSpec((1, n, tm, w), lambda b, s: (b, 0, s, 0))
    head_t = pl.BlockSpec((1, MLA_HEADS, 1, LANES, tm), lambda b, s: (b, 0, s, 0, 0))
    shape_t = jax.ShapeDtypeStruct((bsz, MLA_HEADS, seq // tm, LANES, tm), BF16)
    return pl.pallas_call(
        functools.partial(_mixer_in_kernel, tm=tm),
        grid=(bsz, seq // tm),
        in_specs=[tok(d), pl.BlockSpec((1, 1, QK_DIM, tm), lambda b, s: (b, s, 0, 0)), tok(QK_DIM),
                  *[_layer_spec(w, layer) for w in (w_in, vng, ws, bs, qng, wuq, kvng, wk, wv,
                                                     convw, poolw, pscale)]],
        out_specs=[head_t, head(MLA_HEADS, QK_DIM), head_t, tok(A_WIDTH + C_WIDTH + D_WIDTH)],
        out_shape=[shape_t,
                   jax.ShapeDtypeStruct((bsz, MLA_HEADS, seq, QK_DIM), BF16),
                   shape_t,
                   jax.ShapeDtypeStruct((bsz, seq, A_WIDTH + C_WIDTH + D_WIDTH), BF16)],
        scratch_shapes=[pltpu.VMEM((tm + SUBLANES, C_WIDTH), F32),
                        pltpu.VMEM((tm + POOL_HALO + SUBLANES, D_WIDTH), F32),
                        pltpu.VMEM((tm + POOL_HALO + SUBLANES, D_WIDTH), F32),
                        pltpu.VMEM((POOL_HALO, D_WIDTH), F32)],
        compiler_params=_params(2),
        name="mixer_in",
    )(x, csq, csk, w_in, vng, ws, bs, qng, wuq, kvng, wk, wv, convw, poolw, pscale)


def _mla_attn_kernel(qt_ref, k_ref, vt_ref, o_ref, m_ref, acc_ref, *, tq, hg):
    qi = pl.program_id(2)
    half = tq // 2
    m_ref[...] = jnp.full(m_ref.shape, NEG_BIG, F32)
    acc_ref[...] = jnp.zeros(acc_ref.shape, F32)

    def step(j, diagonal):
        base = pl.multiple_of(j * tq, tq)
        pieces = ([(hh, 0, 0) for hh in range(hg)]
                  + [(hh, half, half if diagonal else 0) for hh in range(hg)])

        def scores(piece):
            hh, k0, q0 = piece
            return _dot(k_ref[0, hh, pl.ds(base + k0, half), :], qt_ref[0, hh, 0, :, q0:])

        def softmax(piece, st):
            hh, k0, q0 = piece
            if diagonal:
                key = lax.broadcasted_iota(jnp.int32, st.shape, 0)
                qry = lax.broadcasted_iota(jnp.int32, st.shape, 1)
                st = jnp.where(key <= qry, st, NEG_BIG)
            m_old = m_ref[hh, :, q0:]
            m_new = jnp.maximum(m_old, jnp.max(st, axis=0, keepdims=True))
            m_ref[hh, :, q0:] = m_new
            return piece, jnp.exp2(st - m_new).astype(BF16), jnp.exp2(m_old - m_new)

        def weighted_values(piece, probs, rescale):
            hh, k0, q0 = piece
            pv = _dot(vt_ref[0, hh, j, :, k0:k0 + half], probs)
            acc_ref[hh, :, q0:] = rescale * acc_ref[hh, :, q0:] + pv

        st_next = scores(pieces[0])
        pending = None
        for i, piece in enumerate(pieces):
            st = st_next
            if i + 1 < len(pieces):
                st_next = scores(pieces[i + 1])
            if pending is not None:
                weighted_values(*pending)
            pending = softmax(piece, st)
        weighted_values(*pending)

    def body(j, carry):
        step(j, False)
        return carry

    lax.fori_loop(0, qi, body, 0)
    step(qi, True)
    for pr in range(hg // 2):
        halves = []
        for hh in (2 * pr, 2 * pr + 1):
            acc = acc_ref[hh]
            halves.append(acc[:MLA_V] / acc[MLA_V:MLA_V + 1])
        o_ref[0, :, pr * LANES:(pr + 1) * LANES] = jnp.concatenate(halves, axis=0).T.astype(BF16)


def _mla_attn(qt, k, vt, *, hg):
    bsz, heads, n_tiles, _, tq = qt.shape
    seq = n_tiles * tq
    return pl.pallas_call(
        functools.partial(_mla_attn_kernel, tq=tq, hg=hg),
        grid=(bsz, heads // hg, n_tiles),
        in_specs=[pl.BlockSpec((1, hg, 1, QK_DIM, tq), lambda b, g, i: (b, g, i, 0, 0)),
                  pl.BlockSpec((1, hg, seq, QK_DIM), lambda b, g, i: (b, g, 0, 0)),
                  pl.BlockSpec((1, hg, n_tiles, LANES, tq), lambda b, g, i: (b, g, 0, 0, 0))],
        out_specs=pl.BlockSpec((1, tq, hg * MLA_V), lambda b, g, i: (b, i, g)),
        out_shape=jax.ShapeDtypeStruct((bsz, seq, MLA_WIDTH), BF16),
        scratch_shapes=[pltpu.VMEM((hg, 1, tq), F32), pltpu.VMEM((hg, LANES, tq), F32)],
        compiler_params=_params(3),
        name="mla_attn",
    )(qt, k, vt)


def _route_t(logits_t, bias_t):
    scores = jax.nn.sigmoid(logits_t)
    sel_all = scores + bias_t
    sel = [sel_all[e:e + 1] for e in range(N_EXPERTS)]
    group_score = []
    for g in range(N_GROUPS):
        v = sel[g * EXPERTS_PER_GROUP:(g + 1) * EXPERTS_PER_GROUP]
        best = None
        for i in range(EXPERTS_PER_GROUP):
            for j in range(i + 1, EXPERTS_PER_GROUP):
                best = v[i] + v[j] if best is None else jnp.maximum(best, v[i] + v[j])
        group_score.append(best)
    top, top_idx = group_score[0], jnp.zeros(group_score[0].shape, jnp.int32)
    for g in range(1, N_GROUPS):
        better = group_score[g] > top
        top = jnp.where(better, group_score[g], top)
        top_idx = jnp.where(better, g, top_idx)
    member = [jnp.where(top_idx == g, 1.0, 0.0) for g in range(N_GROUPS)]
    weights = []
    for e in range(N_EXPERTS):
        g = e // EXPERTS_PER_GROUP
        rank = jnp.zeros(sel[e].shape, jnp.int32)
        for j in range(g * EXPERTS_PER_GROUP, (g + 1) * EXPERTS_PER_GROUP):
            if j != e:
                rank = rank + jnp.where(sel[j] > sel[e], 1,
                                        jnp.where(sel[j] == sel[e], 1 if j < e else 0, 0))
        weights.append(jnp.where(rank < TOP_K, member[g], 0.0) * scores[e:e + 1])
    total = weights[0]
    for w in weights[1:]:
        total = total + w
    return [w / total for w in weights], member


def _post_attn_kernel(yacd_ref, yb_ref, x_ref, wo_acd_ref, wo_b_ref, g1_ref, b1_ref, wq_ref,
                      km_ref, vm_ref, wo_ref, g2_ref, b2_ref, rw_ref,
                      x2_ref, logits_ref, *, tm, alpha, n_chains):
    def chain(rows):
        h = _dot(yacd_ref[0, rows], wo_acd_ref[...]) + _dot(yb_ref[0, rows], wo_b_ref[...])
        yield
        x1 = _layer_norm(alpha * x_ref[0, rows] + h, g1_ref[...], b1_ref[...])
        q = _dot(x1.astype(BF16), wq_ref[...])
        yield
        heads = []
        for hd in range(X_HEADS):
            cols = slice(hd * X_HEAD_DIM, (hd + 1) * X_HEAD_DIM)
            s = _dot_nt(q[:, cols].astype(BF16), km_ref[0, :, cols])
            e = jnp.exp2(s - jnp.max(s, axis=-1, keepdims=True))
            o = _dot(e.astype(BF16), vm_ref[0, :, cols]) / jnp.sum(e, axis=-1, keepdims=True)
            heads.append(o.astype(BF16))
            yield
        h2 = _dot(jnp.concatenate(heads, axis=1), wo_ref[...])
        yield
        x2 = _layer_norm(alpha * x1 + h2, g2_ref[...], b2_ref[...])
        x2_ref[0, rows] = x2
        x_hi = x2.astype(BF16)
        x_lo = (x2 - x_hi.astype(F32)).astype(BF16)
        hl = _dot_nt(rw_ref[...], x_hi)
        lo = _dot_nt(rw_ref[0:N_EXPERTS, :], x_lo)
        logits_ref[0, :, rows] = hl[:N_EXPERTS] + hl[N_EXPERTS:] + lo

    rows_per = tm // n_chains
    _run_interleaved([chain(slice(c * rows_per, (c + 1) * rows_per)) for c in range(n_chains)])


def _post_attn(yacd, yb, x, wo_acd, wo_b, g1, b1, wq, km, vm, wo, g2, b2, rw,
               *, layer, tm, alpha):
    bsz, seq, d = x.shape
    tok = lambda w: pl.BlockSpec((1, tm, w), lambda b, s: (b, s, 0))
    memspec = pl.BlockSpec((None, 1) + km.shape[2:], lambda b, s: (layer, b, 0, 0))
    per_layer = lambda w: _layer_spec(w, layer)
    return pl.pallas_call(
        functools.partial(_post_attn_kernel, tm=tm, alpha=alpha, n_chains=2),
        grid=(bsz, seq // tm),
        in_specs=[tok(yacd.shape[-1]), tok(yb.shape[-1]), tok(d),
                  per_layer(wo_acd), per_layer(wo_b), per_layer(g1), per_layer(b1), per_layer(wq),
                  memspec, memspec, per_layer(wo), per_layer(g2), per_layer(b2),
                  _const_spec(rw.shape)],
        out_specs=[tok(d), pl.BlockSpec((1, N_EXPERTS, tm), lambda b, s: (b, 0, s))],
        out_shape=[jax.ShapeDtypeStruct((bsz, seq, d), F32),
                   jax.ShapeDtypeStruct((bsz, N_EXPERTS, seq), F32)],
        compiler_params=_params(2),
        name="post_attn",
    )(yacd, yb, x, wo_acd, wo_b, g1, b1, wq, km, vm, wo, g2, b2, rw)


def _dot_tn(a, b):
    return lax.dot_general(a, b, (((0,), (0,)), ((), ())), preferred_element_type=F32)


def _moe_kernel(x_ref, logits_ref, rb_ref, before_ref, wg_ref, wu_ref, wd_ref, g3_ref, b3_ref, o_ref,
                xb_ref, pos_ref, memb_ref, gsplit_ref, acc_ref, *, alpha, tm, rows):
    group_w = EXPERTS_PER_GROUP * D_FF
    x = x_ref[0]
    xb_ref[...] = x.astype(BF16)
    gates, member = _route_t(logits_ref[0], rb_ref[...])
    memb = jnp.concatenate(member + [jnp.zeros((ROUTE_ROWS - N_GROUPS, tm), F32)], axis=0)
    memb_ref[...] = memb
    pos_ref[...] = _dot(memb.astype(BF16), before_ref[...])
    gate_rows = jnp.concatenate(gates, axis=0)
    g_hi = gate_rows.astype(BF16)
    gsplit_ref[...] = jnp.concatenate([g_hi, (gate_rows - g_hi.astype(F32)).astype(BF16)], axis=0)

    def expert_chunk(g, ci):
        slot = (lax.broadcasted_iota(jnp.int32, (rows, tm), 0) + ci * rows).astype(F32)
        p = jnp.where(pos_ref[g:g + 1, :] == slot, memb_ref[g:g + 1, :], 0.0).astype(BF16)
        xg = _dot(p, xb_ref[...]).astype(BF16)
        gg = _dot_nt(p, gsplit_ref[...])
        hs = []
        for j in range(EXPERTS_PER_GROUP):
            e = g * EXPERTS_PER_GROUP + j
            gate = gg[:, e:e + 1] + gg[:, N_EXPERTS + e:N_EXPERTS + e + 1]
            hg = _dot(xg, wg_ref[e])
            hu = _dot(xg, wu_ref[e])
            hs.append((jax.nn.silu(hg) * hu * gate).astype(BF16))
        y = _dot(jnp.concatenate(hs, axis=1), wd_ref[g * group_w:(g + 1) * group_w, :])
        return p, y.astype(BF16)

    first = [expert_chunk(g, 0) for g in range(N_GROUPS)]
    acc_ref[...] = _dot_tn(jnp.concatenate([p for p, _ in first], axis=0),
                           jnp.concatenate([y for _, y in first], axis=0))
    largest = jnp.max(jnp.sum(memb_ref[0:SUBLANES, :], axis=1, keepdims=True))

    @pl.when(largest > rows)
    def _():
        for g in range(N_GROUPS):
            n_tok = jnp.sum(memb_ref[g:g + 1, :]).astype(jnp.int32)

            def extra(ci, carry, g=g):
                p, y = expert_chunk(g, ci)
                acc_ref[...] += _dot_tn(p, y)
                return carry

            lax.fori_loop(1, (n_tok + rows - 1) // rows, extra, 0)

    o_ref[0] = _layer_norm(alpha * x_ref[0] + acc_ref[...], g3_ref[...], b3_ref[...])


def _moe(x, logits_t, rb_t, before, wg, wu, wd, g3, b3, *, layer, tm, alpha):
    bsz, seq, d = x.shape
    tok = lambda w: pl.BlockSpec((1, tm, w), lambda b, s: (b, s, 0))
    resident = lambda w: _layer_spec(w, layer, single_buffer=True)
    return pl.pallas_call(
        functools.partial(_moe_kernel, alpha=alpha, tm=tm, rows=MOE_ROWS),
        grid=(bsz, seq // tm),
        in_specs=[tok(d), pl.BlockSpec((1, N_EXPERTS, tm), lambda b, s: (b, 0, s)),
                  _const_spec(rb_t.shape), _const_spec(before.shape),
                  resident(wg), resident(wu), resident(wd),
                  _layer_spec(g3, layer), _layer_spec(b3, layer)],
        out_specs=tok(d),
        out_shape=jax.ShapeDtypeStruct((bsz, seq, d), F32),
        scratch_shapes=[pltpu.VMEM((tm, d), BF16),
                        pltpu.VMEM((ROUTE_ROWS, tm), F32),
                        pltpu.VMEM((ROUTE_ROWS, tm), F32),
                        pltpu.VMEM((2 * N_EXPERTS, tm), BF16),
                        pltpu.VMEM((tm, d), F32)],
        compiler_params=_params(2),
        name="moe",
    )(x, logits_t, rb_t, before, wg, wu, wd, g3, b3)


def _rot_cols(w):
    half = w.shape[-1] // 2
    return jnp.concatenate([-w[..., half:], w[..., :half]], axis=-1)


def _prep_w_in(w_in):
    u, v = w_in[..., 0:256], w_in[..., 256:512]
    cq, ckv, kr = w_in[..., 512:768], w_in[..., 768:896], w_in[..., 896:928]
    rest = w_in[..., 928:]
    rope4 = jnp.concatenate([kr, _rot_cols(kr), kr, _rot_cols(kr)], axis=-1)
    return jnp.concatenate([u, v, cq, ckv, rope4, rest], axis=-1).astype(BF16)


def _prep_w_uq(w_uq):
    depth = w_uq.shape[0]
    w = w_uq.reshape(depth, Q_LORA, MLA_HEADS, MLA_NOPE + MLA_ROPE)
    nope, rope = w[..., :MLA_NOPE], w[..., MLA_NOPE:]
    ext = jnp.concatenate([nope, rope, _rot_cols(rope)], axis=-1)
    return ext.reshape(depth, Q_LORA, MLA_HEADS * QK_DIM).astype(BF16)


def _prep_w_ukv(w_ukv):
    depth = w_ukv.shape[0]
    w = w_ukv.reshape(depth, KV_LORA, MLA_HEADS, MLA_NOPE + MLA_V)
    k_nope, v = w[..., :MLA_NOPE], w[..., MLA_NOPE:]
    wk = jnp.concatenate([k_nope, jnp.zeros_like(k_nope)], axis=-1)
    return (wk.reshape(depth, KV_LORA, MLA_HEADS * QK_DIM).astype(BF16),
            v.reshape(depth, KV_LORA, MLA_WIDTH).astype(BF16))


def _prep_pool_w(pool_w):
    depth, groups = pool_w.shape[:2]
    eye = jnp.eye(groups, dtype=pool_w.dtype)
    bd = jnp.einsum('lgcd,gh->lgchd', pool_w, eye)
    return bd.reshape(depth, D_WIDTH, D_WIDTH).astype(BF16)


def kernel(x, mem, positions, w_in, gmlp_v_norm_g, gmlp_w_s, gmlp_b_s, mla_q_norm_g, mla_w_uq,
           mla_kv_norm_g, mla_w_ukv, conv_w, pool_w, pool_scale, w_out, ln1_g, ln1_b,
           xattn_wq, xattn_wk, xattn_wv, xattn_wo, ln2_g, ln2_b, router_w, router_b,
           moe_w_gate, moe_w_up, moe_w_down, ln3_g, ln3_b):
    depth = w_in.shape[0]
    alpha = (2 * depth) ** 0.25
    tm = 512
    bsz, seq, _ = x.shape

    inv_freq = ROPE_BASE ** (-jnp.arange(0, MLA_ROPE, 2, dtype=F32) / MLA_ROPE)
    ang = positions.astype(F32)[..., None] * inv_freq
    cos2 = jnp.tile(jnp.cos(ang), (1, 1, 2))
    sin2 = jnp.tile(jnp.sin(ang), (1, 1, 2))
    scale = (MLA_NOPE + MLA_ROPE) ** -0.5 * LOG2_E
    csq = scale * jnp.concatenate([jnp.ones(cos2.shape[:2] + (MLA_NOPE,), F32), cos2, sin2], -1)
    csq = jnp.swapaxes(csq.reshape(bsz, seq // tm, tm, QK_DIM), 2, 3)
    csk = jnp.concatenate([cos2, sin2, cos2, sin2], axis=-1)

    row = lambda a: a[:, None, :]
    w_in_e = _prep_w_in(w_in)
    w_uq_e = jnp.swapaxes(_prep_w_uq(mla_w_uq), 1, 2)
    w_k_e, w_v_e = _prep_w_ukv(mla_w_ukv)
    w_v_e = jnp.swapaxes(w_v_e, 1, 2)
    ws = gmlp_w_s.astype(BF16)
    bs = jnp.repeat(jnp.swapaxes(gmlp_b_s, 1, 2), A_HEAD_DIM, axis=2)
    pool_bd = _prep_pool_w(pool_w)
    wo_acd = jnp.concatenate([w_out[:, :A_WIDTH], w_out[:, A_WIDTH + MLA_WIDTH:]], axis=1).astype(BF16)
    wo_b = w_out[:, A_WIDTH:A_WIDTH + MLA_WIDTH].astype(BF16)
    wq = (xattn_wq * (X_HEAD_DIM ** -0.5 * LOG2_E)).astype(BF16)
    wo = xattn_wo.astype(BF16)
    rw_t = router_w.T
    rw_hi = rw_t.astype(BF16)
    rw = jnp.concatenate([rw_hi, (rw_t - rw_hi.astype(F32)).astype(BF16)], axis=0)
    rb_t = jnp.broadcast_to(router_b[:, None], (N_EXPERTS, tm))
    before = jnp.triu(jnp.ones((tm, tm), BF16), 1)
    wg = moe_w_gate.astype(BF16)
    wu = moe_w_up.astype(BF16)
    wd = moe_w_down.reshape(depth, N_EXPERTS * D_FF, D_MODEL).astype(BF16)

    km, vm = _mem_kv(mem, xattn_wk.astype(BF16), xattn_wv.astype(BF16))

    for l in range(depth):
        q, k, v, yacd = _mixer_in(
            x, csq, csk, w_in_e, row(gmlp_v_norm_g), ws, bs, row(mla_q_norm_g),
            w_uq_e, row(mla_kv_norm_g), w_k_e, w_v_e, conv_w, pool_bd,
            row(pool_scale), layer=l, tm=tm)
        yb = _mla_attn(q, k, v, hg=MLA_HEADS)
        x2, logits = _post_attn(
            yacd, yb, x, wo_acd, wo_b, row(ln1_g), row(ln1_b), wq, km, vm,
            wo, row(ln2_g), row(ln2_b), rw, layer=l, tm=tm, alpha=alpha)
        x = _moe(x2, logits, rb_t, before, wg, wu, wd, row(ln3_g), row(ln3_b),
                 layer=l, tm=tm, alpha=alpha)
    return x
```

```python
import functools

import jax
import jax.numpy as jnp
from jax import lax
from jax.experimental import pallas as pl
from jax.experimental.pallas import tpu as pltpu

F32 = jnp.float32
BF16 = jnp.bfloat16

D_MODEL = 1024
A_HEADS, A_HEAD_DIM, CHUNK = 4, 64, 128
A_WIDTH = A_HEADS * A_HEAD_DIM
MLA_HEADS, MLA_NOPE, MLA_ROPE, MLA_V = 8, 64, 32, 64
Q_LORA, KV_LORA = 256, 128
MLA_WIDTH = MLA_HEADS * MLA_V
ROPE_BASE = 10000.0
C_WIDTH, CONV_W = 256, 3
D_WIDTH = 256
POOL_WINDOWS = (2, 4, 8, 16)
D_GROUP = D_WIDTH // len(POOL_WINDOWS)
X_HEADS = 4
X_HEAD_DIM = D_MODEL // X_HEADS
N_EXPERTS, N_GROUPS, TOP_K = 16, 4, 2
EXPERTS_PER_GROUP = N_EXPERTS // N_GROUPS
D_FF = 256
EPS = 1e-6

LANES = 128
SUBLANES = 8
VMEM_LIMIT = 56 * 1024 * 1024

OFF_U, OFF_CQ, OFF_BG, P_EXT = 0, 512, 1024, 2048
QK_DIM = 128
V_ROWS = 80
POOL_HALO = 16
MOE_ROWS = 160
ROUTE_ROWS = 16
NEG_BIG = -1e30
LOG2_E = 1.4426950408889634


def _dot(a, b):
    return jnp.dot(a, b, preferred_element_type=F32)


def _dot_nt(a, b):
    return lax.dot_general(a, b, (((1,), (1,)), ((), ())), preferred_element_type=F32)


def _rms(x, g):
    return x * lax.rsqrt(jnp.mean(x * x, axis=-1, keepdims=True) + EPS) * g


def _layer_norm(x, g, b):
    mu = jnp.mean(x, axis=-1, keepdims=True)
    xc = x - mu
    var = jnp.mean(xc * xc, axis=-1, keepdims=True)
    return xc * lax.rsqrt(var + EPS) * g + b


def _params(n_grid):
    return pltpu.CompilerParams(dimension_semantics=("arbitrary",) * n_grid,
                                vmem_limit_bytes=VMEM_LIMIT)


def _run_interleaved(chains):
    live = list(chains)
    while live:
        for c in list(live):
            try:
                next(c)
            except StopIteration:
                live.remove(c)


def _const_spec(shape):
    zeros = (0,) * len(shape)
    return pl.BlockSpec(shape, lambda *_: zeros)


def _layer_spec(stacked, layer, single_buffer=False):
    index = (layer,) + (0,) * (stacked.ndim - 1)
    mode = dict(pipeline_mode=pl.Buffered(1)) if single_buffer else {}
    return pl.BlockSpec((None,) + stacked.shape[1:], lambda *_: index, **mode)


def _mem_kv_kernel(mem_ref, wk_ref, wv_ref, k_ref, v_ref):
    m = mem_ref[0].astype(BF16)
    k_ref[0, 0] = _dot(m, wk_ref[0]).astype(BF16)
    v_ref[0, 0] = _dot(m, wv_ref[0]).astype(BF16)


def _mem_kv(mem, wk, wv):
    depth = wk.shape[0]
    bsz, mlen, d = mem.shape
    out = jax.ShapeDtypeStruct((depth, bsz, mlen, d), BF16)
    return pl.pallas_call(
        _mem_kv_kernel,
        grid=(depth, bsz),
        in_specs=[pl.BlockSpec((1, mlen, d), lambda l, b: (b, 0, 0)),
                  pl.BlockSpec((1, d, d), lambda l, b: (l, 0, 0)),
                  pl.BlockSpec((1, d, d), lambda l, b: (l, 0, 0))],
        out_specs=[pl.BlockSpec((1, 1, mlen, d), lambda l, b: (l, b, 0, 0)),
                   pl.BlockSpec((1, 1, mlen, d), lambda l, b: (l, b, 0, 0))],
        out_shape=[out, out],
        compiler_params=_params(2),
        name="mem_kv",
    )(mem, wk, wv)


def _mixer_in_kernel(x_ref, csq_ref, csk_ref, w_in_ref, vng_ref, ws_ref, bs_ref, qng_ref,
                     wuq_ref, kvng_ref, wk_ref, wv_ref, convw_ref, poolw_ref, pscale_ref,
                     q_ref, k_ref, v_ref, y_ref,
                     conv_buf, pool_a, pool_b, pool_carry, *, tm):
    si = pl.program_id(1)

    @pl.when(si == 0)
    def _():
        pool_carry[...] = jnp.zeros((POOL_HALO, D_WIDTH), F32)
        conv_buf[0:SUBLANES, :] = jnp.zeros((SUBLANES, C_WIDTH), F32)

    xb = x_ref[0].astype(BF16)

    za = _dot(xb, w_in_ref[:, OFF_U:OFF_CQ])
    zb = _dot(xb, w_in_ref[:, OFF_CQ:OFF_BG])
    zcd = _dot(xb, w_in_ref[:, OFF_BG:P_EXT])

    za = jax.nn.gelu(za)
    u = za[:, :A_WIDTH]
    v = _rms(za[:, A_WIDTH:], vng_ref[...]).astype(BF16)
    cq = _rms(zb[:, :Q_LORA], qng_ref[...]).astype(BF16)
    ckv = _rms(zb[:, Q_LORA:Q_LORA + KV_LORA], kvng_ref[...]).astype(BF16)

    row = lax.broadcasted_iota(jnp.int32, (CHUNK, CHUNK), 0)
    col = lax.broadcasted_iota(jnp.int32, (CHUNK, CHUNK), 1)
    w_causal = jnp.concatenate(
        [jnp.where(row >= col, ws_ref[h], jnp.zeros((), BF16)) for h in range(A_HEADS)], axis=1)
    lane_head = lax.broadcasted_iota(jnp.int32, (CHUNK, A_WIDTH), 1) // A_HEAD_DIM
    for c in range(tm // CHUNK):
        rows = slice(c * CHUNK, (c + 1) * CHUNK)
        vc = v[rows]
        v_heads = jnp.concatenate(
            [jnp.where(lane_head == h, vc, jnp.zeros((), BF16)) for h in range(A_HEADS)], axis=0)
        mixed = _dot(w_causal, v_heads)
        y_ref[0, rows, 0:A_WIDTH] = (u[rows] * (mixed + bs_ref[...])).astype(BF16)

    qt = _dot_nt(wuq_ref[...], cq)
    kk = _dot(ckv, wk_ref[...])
    vt = _dot_nt(wv_ref[...], ckv)

    zd = zcd[:, 3 * C_WIDTH:]
    base = SUBLANES
    n = tm + POOL_HALO
    pool_a[0:base, :] = jnp.zeros((base, D_WIDTH), F32)
    pool_b[0:base, :] = jnp.zeros((base, D_WIDTH), F32)
    pool_a[base:base + POOL_HALO, :] = pool_carry[...]
    pool_a[base + POOL_HALO:base + n, :] = zd
    pool_carry[...] = zd[tm - POOL_HALO:, :]
    lane_d = lax.broadcasted_iota(jnp.int32, (n, D_WIDTH), 1)
    pool_b[base:base + n, :] = pool_a[base:base + n, :] + pool_a[base - 1:base - 1 + n, :]
    pool_a[base:base + n, :] = pool_b[base:base + n, :] + jnp.where(
        lane_d >= D_GROUP, pool_b[base - 2:base - 2 + n, :], 0.0)
    pool_b[base:base + n, :] = pool_a[base:base + n, :] + jnp.where(
        lane_d >= 2 * D_GROUP, pool_a[base - 4:base - 4 + n, :], 0.0)
    t0 = base + POOL_HALO
    lane_t = lax.broadcasted_iota(jnp.int32, (tm, D_WIDTH), 1)
    win_sum = pool_b[t0:t0 + tm, :] + jnp.where(
        lane_t >= 3 * D_GROUP, pool_b[t0 - 8:t0 - 8 + tm, :], 0.0)
    pos1 = si * tm + lax.broadcasted_iota(jnp.int32, (tm, D_WIDTH), 0) + 1
    window = jnp.left_shift(2, lane_t // D_GROUP)
    count = jnp.minimum(pos1, window).astype(F32)
    pooled = (win_sum / count - zd).astype(BF16)
    yd = _dot(pooled, poolw_ref[...]) * pscale_ref[...]
    y_ref[0, :, A_WIDTH + C_WIDTH:] = yd.astype(BF16)

    gh = zcd[:, C_WIDTH:2 * C_WIDTH] * zcd[:, 2 * C_WIDTH:3 * C_WIDTH]
    conv_buf[SUBLANES:SUBLANES + tm, :] = gh
    conv = (convw_ref[2:3, :] * gh
            + convw_ref[1:2, :] * conv_buf[SUBLANES - 1:SUBLANES - 1 + tm, :]
            + convw_ref[0:1, :] * conv_buf[SUBLANES - 2:SUBLANES - 2 + tm, :])
    y_ref[0, :, A_WIDTH:A_WIDTH + C_WIDTH] = (zcd[:, :C_WIDTH] * conv).astype(BF16)
    conv_buf[0:SUBLANES, :] = conv_buf[tm:tm + SUBLANES, :]

    csq = csq_ref[0, 0]
    for h in range(MLA_HEADS):
        q_ref[0, h, 0] = (qt[h * QK_DIM:(h + 1) * QK_DIM, :] * csq).astype(BF16)
    r4 = zb[:, Q_LORA + KV_LORA:] * csk_ref[0]
    kr = r4 + pltpu.roll(r4, MLA_ROPE, 1)
    lane = lax.broadcasted_iota(jnp.int32, (tm, QK_DIM), 1)
    kr = jnp.where(lane >= MLA_NOPE, kr, 0.0)
    for h in range(MLA_HEADS):
        k_ref[0, h] = (kk[:, h * QK_DIM:(h + 1) * QK_DIM] + kr).astype(BF16)
    ones_then_zeros = jnp.where(
        lax.broadcasted_iota(jnp.int32, (V_ROWS - MLA_V, tm), 0) == 0, 1.0, 0.0).astype(BF16)
    for h in range(MLA_HEADS):
        v_ref[0, h, 0, 0:MLA_V, :] = vt[h * MLA_V:(h + 1) * MLA_V, :].astype(BF16)
        v_ref[0, h, 0, MLA_V:, :] = ones_then_zeros


def _mixer_in(x, csq, csk, w_in, vng, ws, bs, qng, wuq, kvng, wk, wv, convw, poolw, pscale,
              *, layer, tm):
    bsz, seq, d = x.shape
    tok = lambda w: pl.BlockSpec((1, tm, w), lambda b, s: (b, s, 0))
    head = lambda n, w: pl.BlockSpec((1, n, tm, w), lambda b, s: (b, 0, s, 0))
    head_t = lambda r: pl.BlockSpec((1, MLA_HEADS, 1, r, tm), lambda b, s: (b, 0, s, 0, 0))
    shape_t = lambda r: jax.ShapeDtypeStruct((bsz, MLA_HEADS, seq // tm, r, tm), BF16)
    return pl.pallas_call(
        functools.partial(_mixer_in_kernel, tm=tm),
        grid=(bsz, seq // tm),
        in_specs=[tok(d), pl.BlockSpec((1, 1, QK_DIM, tm), lambda b, s: (b, s, 0, 0)), tok(QK_DIM),
                  *[_layer_spec(w, layer) for w in (w_in, vng, ws, bs, qng, wuq, kvng, wk, wv,
                                                     convw, poolw, pscale)]],
        out_specs=[head_t(QK_DIM), head(MLA_HEADS, QK_DIM), head_t(V_ROWS),
                   tok(A_WIDTH + C_WIDTH + D_WIDTH)],
        out_shape=[shape_t(QK_DIM),
                   jax.ShapeDtypeStruct((bsz, MLA_HEADS, seq, QK_DIM), BF16),
                   shape_t(V_ROWS),
                   jax.ShapeDtypeStruct((bsz, seq, A_WIDTH + C_WIDTH + D_WIDTH), BF16)],
        scratch_shapes=[pltpu.VMEM((tm + SUBLANES, C_WIDTH), F32),
                        pltpu.VMEM((tm + POOL_HALO + SUBLANES, D_WIDTH), F32),
                        pltpu.VMEM((tm + POOL_HALO + SUBLANES, D_WIDTH), F32),
                        pltpu.VMEM((POOL_HALO, D_WIDTH), F32)],
        compiler_params=_params(2),
        name="mixer_in",
    )(x, csq, csk, w_in, vng, ws, bs, qng, wuq, kvng, wk, wv, convw, poolw, pscale)


def _mla_attn_kernel(qt_ref, k_ref, vt_ref, o_ref, m_ref, acc_ref, *, tq, hg):
    qi = pl.program_id(2)
    half = tq // 2
    m_ref[...] = jnp.full(m_ref.shape, NEG_BIG, F32)
    acc_ref[...] = jnp.zeros(acc_ref.shape, F32)

    def step(j, diagonal):
        base = pl.multiple_of(j * tq, tq)
        pieces = ([(hh, 0, 0) for hh in range(hg)]
                  + [(hh, half, half if diagonal else 0) for hh in range(hg)])

        def scores(piece):
            hh, k0, q0 = piece
            return _dot(k_ref[0, hh, pl.ds(base + k0, half), :], qt_ref[0, hh, 0, :, q0:])

        def softmax(piece, st):
            hh, k0, q0 = piece
            if diagonal:
                key = lax.broadcasted_iota(jnp.int32, st.shape, 0)
                qry = lax.broadcasted_iota(jnp.int32, st.shape, 1)
                st = jnp.where(key <= qry, st, NEG_BIG)
            m_old = m_ref[hh, :, q0:]
            m_new = jnp.maximum(m_old, jnp.max(st, axis=0, keepdims=True))
            m_ref[hh, :, q0:] = m_new
            return piece, jnp.exp2(st - m_new).astype(BF16), jnp.exp2(m_old - m_new)

        def weighted_values(piece, probs, rescale):
            hh, k0, q0 = piece
            pv = _dot(vt_ref[0, hh, j, :, k0:k0 + half], probs)
            acc_ref[hh, :, q0:] = rescale * acc_ref[hh, :, q0:] + pv

        st_next = scores(pieces[0])
        pending = None
        for i, piece in enumerate(pieces):
            st = st_next
            if i + 1 < len(pieces):
                st_next = scores(pieces[i + 1])
            if pending is not None:
                weighted_values(*pending)
            pending = softmax(piece, st)
        weighted_values(*pending)

    def body(j, carry):
        step(j, False)
        return carry

    lax.fori_loop(0, qi, body, 0)
    step(qi, True)
    for pr in range(hg // 2):
        halves = []
        for hh in (2 * pr, 2 * pr + 1):
            acc = acc_ref[hh]
            halves.append(acc[:MLA_V] / acc[MLA_V:MLA_V + 1])
        o_ref[0, :, pr * LANES:(pr + 1) * LANES] = jnp.concatenate(halves, axis=0).T.astype(BF16)


def _mla_attn(qt, k, vt, *, hg):
    bsz, heads, n_tiles, _, tq = qt.shape
    seq = n_tiles * tq
    return pl.pallas_call(
        functools.partial(_mla_attn_kernel, tq=tq, hg=hg),
        grid=(bsz, heads // hg, n_tiles),
        in_specs=[pl.BlockSpec((1, hg, 1, QK_DIM, tq), lambda b, g, i: (b, g, i, 0, 0)),
                  pl.BlockSpec((1, hg, seq, QK_DIM), lambda b, g, i: (b, g, 0, 0)),
                  pl.BlockSpec((1, hg, n_tiles, V_ROWS, tq), lambda b, g, i: (b, g, 0, 0, 0))],
        out_specs=pl.BlockSpec((1, tq, hg * MLA_V), lambda b, g, i: (b, i, g)),
        out_shape=jax.ShapeDtypeStruct((bsz, seq, MLA_WIDTH), BF16),
        scratch_shapes=[pltpu.VMEM((hg, 1, tq), F32), pltpu.VMEM((hg, V_ROWS, tq), F32)],
        compiler_params=_params(3),
        name="mla_attn",
    )(qt, k, vt)


def _route_t(logits_t, bias_t):
    scores = jax.nn.sigmoid(logits_t)
    sel_all = scores + bias_t
    sel = [sel_all[e:e + 1] for e in range(N_EXPERTS)]
    group_score = []
    for g in range(N_GROUPS):
        v = sel[g * EXPERTS_PER_GROUP:(g + 1) * EXPERTS_PER_GROUP]
        best = None
        for i in range(EXPERTS_PER_GROUP):
            for j in range(i + 1, EXPERTS_PER_GROUP):
                best = v[i] + v[j] if best is None else jnp.maximum(best, v[i] + v[j])
        group_score.append(best)
    top, top_idx = group_score[0], jnp.zeros(group_score[0].shape, jnp.int32)
    for g in range(1, N_GROUPS):
        better = group_score[g] > top
        top = jnp.where(better, group_score[g], top)
        top_idx = jnp.where(better, g, top_idx)
    member = [jnp.where(top_idx == g, 1.0, 0.0) for g in range(N_GROUPS)]
    weights = []
    for e in range(N_EXPERTS):
        g = e // EXPERTS_PER_GROUP
        rank = jnp.zeros(sel[e].shape, jnp.int32)
        for j in range(g * EXPERTS_PER_GROUP, (g + 1) * EXPERTS_PER_GROUP):
            if j != e:
                rank = rank + jnp.where(sel[j] > sel[e], 1,
                                        jnp.where(sel[j] == sel[e], 1 if j < e else 0, 0))
        weights.append(jnp.where(rank < TOP_K, member[g], 0.0) * scores[e:e + 1])
    total = weights[0]
    for w in weights[1:]:
        total = total + w
    return [w / total for w in weights], member


def _post_attn_kernel(yacd_ref, yb_ref, x_ref, wo_acd_ref, wo_b_ref, g1_ref, b1_ref, wq_ref,
                      km_ref, vm_ref, wo_ref, g2_ref, b2_ref, rw_ref,
                      x2_ref, logits_ref, *, tm, alpha, n_chains):
    def chain(rows):
        h = _dot(yacd_ref[0, rows], wo_acd_ref[...]) + _dot(yb_ref[0, rows], wo_b_ref[...])
        yield
        x1 = _layer_norm(alpha * x_ref[0, rows] + h, g1_ref[...], b1_ref[...])
        q = _dot(x1.astype(BF16), wq_ref[...])
        yield
        heads = []
        for hd in range(X_HEADS):
            cols = slice(hd * X_HEAD_DIM, (hd + 1) * X_HEAD_DIM)
            s = _dot_nt(q[:, cols].astype(BF16), km_ref[0, :, cols])
            e = jnp.exp2(s - jnp.max(s, axis=-1, keepdims=True))
            o = _dot(e.astype(BF16), vm_ref[0, :, cols]) / jnp.sum(e, axis=-1, keepdims=True)
            heads.append(o.astype(BF16))
            yield
        h2 = _dot(jnp.concatenate(heads, axis=1), wo_ref[...])
        yield
        x2 = _layer_norm(alpha * x1 + h2, g2_ref[...], b2_ref[...])
        x2_ref[0, rows] = x2
        x_hi = x2.astype(BF16)
        x_lo = (x2 - x_hi.astype(F32)).astype(BF16)
        hl = _dot_nt(rw_ref[...], x_hi)
        lo = _dot_nt(rw_ref[0:N_EXPERTS, :], x_lo)
        logits_ref[0, :, rows] = hl[:N_EXPERTS] + hl[N_EXPERTS:] + lo

    rows_per = tm // n_chains
    _run_interleaved([chain(slice(c * rows_per, (c + 1) * rows_per)) for c in range(n_chains)])


def _post_attn(yacd, yb, x, wo_acd, wo_b, g1, b1, wq, km, vm, wo, g2, b2, rw,
               *, layer, tm, alpha):
    bsz, seq, d = x.shape
    tok = lambda w: pl.BlockSpec((1, tm, w), lambda b, s: (b, s, 0))
    memspec = pl.BlockSpec((None, 1) + km.shape[2:], lambda b, s: (layer, b, 0, 0))
    per_layer = lambda w: _layer_spec(w, layer)
    return pl.pallas_call(
        functools.partial(_post_attn_kernel, tm=tm, alpha=alpha, n_chains=2),
        grid=(bsz, seq // tm),
        in_specs=[tok(yacd.shape[-1]), tok(yb.shape[-1]), tok(d),
                  per_layer(wo_acd), per_layer(wo_b), per_layer(g1), per_layer(b1), per_layer(wq),
                  memspec, memspec, per_layer(wo), per_layer(g2), per_layer(b2),
                  _const_spec(rw.shape)],
        out_specs=[tok(d), pl.BlockSpec((1, N_EXPERTS, tm), lambda b, s: (b, 0, s))],
        out_shape=[jax.ShapeDtypeStruct((bsz, seq, d), F32),
                   jax.ShapeDtypeStruct((bsz, N_EXPERTS, seq), F32)],
        compiler_params=_params(2),
        name="post_attn",
    )(yacd, yb, x, wo_acd, wo_b, g1, b1, wq, km, vm, wo, g2, b2, rw)


def _dot_tn(a, b):
    return lax.dot_general(a, b, (((0,), (0,)), ((), ())), preferred_element_type=F32)


def _moe_kernel(x_ref, logits_ref, rb_ref, before_ref, wg_ref, wu_ref, wd_ref, g3_ref, b3_ref, o_ref,
                xb_ref, pos_ref, memb_ref, gsplit_ref, acc_ref, *, alpha, tm, rows):
    group_w = EXPERTS_PER_GROUP * D_FF
    x = x_ref[0]
    xb_ref[...] = x.astype(BF16)
    gates, member = _route_t(logits_ref[0], rb_ref[...])
    memb = jnp.concatenate(member + [jnp.zeros((ROUTE_ROWS - N_GROUPS, tm), F32)], axis=0)
    memb_ref[...] = memb
    pos_ref[...] = _dot(memb.astype(BF16), before_ref[...])
    gate_rows = jnp.concatenate(gates, axis=0)
    g_hi = gate_rows.astype(BF16)
    gsplit_ref[...] = jnp.concatenate([g_hi, (gate_rows - g_hi.astype(F32)).astype(BF16)], axis=0)

    def expert_chunk(g, ci):
        slot = (lax.broadcasted_iota(jnp.int32, (rows, tm), 0) + ci * rows).astype(F32)
        p = jnp.where(pos_ref[g:g + 1, :] == slot, memb_ref[g:g + 1, :], 0.0).astype(BF16)
        xg = _dot(p, xb_ref[...]).astype(BF16)
        gg = _dot_nt(p, gsplit_ref[...])
        hs = []
        for j in range(EXPERTS_PER_GROUP):
            e = g * EXPERTS_PER_GROUP + j
            gate = gg[:, e:e + 1] + gg[:, N_EXPERTS + e:N_EXPERTS + e + 1]
            hg = _dot(xg, wg_ref[e])
            hu = _dot(xg, wu_ref[e])
            hs.append((jax.nn.silu(hg) * hu * gate).astype(BF16))
        y = _dot(jnp.concatenate(hs, axis=1), wd_ref[g * group_w:(g + 1) * group_w, :])
        return p, y.astype(BF16)

    first = [expert_chunk(g, 0) for g in range(N_GROUPS)]
    acc_ref[...] = _dot_tn(jnp.concatenate([p for p, _ in first], axis=0),
                           jnp.concatenate([y for _, y in first], axis=0))
    largest = jnp.max(jnp.sum(memb_ref[0:SUBLANES, :], axis=1, keepdims=True))

    @pl.when(largest > rows)
    def _():
        for g in range(N_GROUPS):
            n_tok = jnp.sum(memb_ref[g:g + 1, :]).astype(jnp.int32)

            def extra(ci, carry, g=g):
                p, y = expert_chunk(g, ci)
                acc_ref[...] += _dot_tn(p, y)
                return carry

            lax.fori_loop(1, (n_tok + rows - 1) // rows, extra, 0)

    o_ref[0] = _layer_norm(alpha * x_ref[0] + acc_ref[...], g3_ref[...], b3_ref[...])


def _moe(x, logits_t, rb_t, before, wg, wu, wd, g3, b3, *, layer, tm, alpha):
    bsz, seq, d = x.shape
    tok = lambda w: pl.BlockSpec((1, tm, w), lambda b, s: (b, s, 0))
    resident = lambda w: _layer_spec(w, layer, single_buffer=True)
    return pl.pallas_call(
        functools.partial(_moe_kernel, alpha=alpha, tm=tm, rows=MOE_ROWS),
        grid=(bsz, seq // tm),
        in_specs=[tok(d), pl.BlockSpec((1, N_EXPERTS, tm), lambda b, s: (b, 0, s)),
                  _const_spec(rb_t.shape), _const_spec(before.shape),
                  resident(wg), resident(wu), resident(wd),
                  _layer_spec(g3, layer), _layer_spec(b3, layer)],
        out_specs=tok(d),
        out_shape=jax.ShapeDtypeStruct((bsz, seq, d), F32),
        scratch_shapes=[pltpu.VMEM((tm, d), BF16),
                        pltpu.VMEM((ROUTE_ROWS, tm), F32),
                        pltpu.VMEM((ROUTE_ROWS, tm), F32),
                        pltpu.VMEM((2 * N_EXPERTS, tm), BF16),
                        pltpu.VMEM((tm, d), F32)],
        compiler_params=_params(2),
        name="moe",
    )(x, logits_t, rb_t, before, wg, wu, wd, g3, b3)


def _rot_cols(w):
    half = w.shape[-1] // 2
    return jnp.concatenate([-w[..., half:], w[..., :half]], axis=-1)


def _prep_w_in(w_in):
    u, v = w_in[..., 0:256], w_in[..., 256:512]
    cq, ckv, kr = w_in[..., 512:768], w_in[..., 768:896], w_in[..., 896:928]
    rest = w_in[..., 928:]
    rope4 = jnp.concatenate([kr, _rot_cols(kr), kr, _rot_cols(kr)], axis=-1)
    return jnp.concatenate([u, v, cq, ckv, rope4, rest], axis=-1).astype(BF16)


def _prep_w_uq(w_uq):
    depth = w_uq.shape[0]
    w = w_uq.reshape(depth, Q_LORA, MLA_HEADS, MLA_NOPE + MLA_ROPE)
    nope, rope = w[..., :MLA_NOPE], w[..., MLA_NOPE:]
    ext = jnp.concatenate([nope, rope, _rot_cols(rope)], axis=-1)
    return ext.reshape(depth, Q_LORA, MLA_HEADS * QK_DIM).astype(BF16)


def _prep_w_ukv(w_ukv):
    depth = w_ukv.shape[0]
    w = w_ukv.reshape(depth, KV_LORA, MLA_HEADS, MLA_NOPE + MLA_V)
    k_nope, v = w[..., :MLA_NOPE], w[..., MLA_NOPE:]
    wk = jnp.concatenate([k_nope, jnp.zeros_like(k_nope)], axis=-1)
    return (wk.reshape(depth, KV_LORA, MLA_HEADS * QK_DIM).astype(BF16),
            v.reshape(depth, KV_LORA, MLA_WIDTH).astype(BF16))


def _prep_pool_w(pool_w):
    depth, groups = pool_w.shape[:2]
    eye = jnp.eye(groups, dtype=pool_w.dtype)
    bd = jnp.einsum('lgcd,gh->lgchd', pool_w, eye)
    return bd.reshape(depth, D_WIDTH, D_WIDTH).astype(BF16)


def kernel(x, mem, positions, w_in, gmlp_v_norm_g, gmlp_w_s, gmlp_b_s, mla_q_norm_g, mla_w_uq,
           mla_kv_norm_g, mla_w_ukv, conv_w, pool_w, pool_scale, w_out, ln1_g, ln1_b,
           xattn_wq, xattn_wk, xattn_wv, xattn_wo, ln2_g, ln2_b, router_w, router_b,
           moe_w_gate, moe_w_up, moe_w_down, ln3_g, ln3_b):
    depth = w_in.shape[0]
    alpha = (2 * depth) ** 0.25
    tm = 512
    bsz, seq, _ = x.shape

    inv_freq = ROPE_BASE ** (-jnp.arange(0, MLA_ROPE, 2, dtype=F32) / MLA_ROPE)
    ang = positions.astype(F32)[..., None] * inv_freq
    cos2 = jnp.tile(jnp.cos(ang), (1, 1, 2))
    sin2 = jnp.tile(jnp.sin(ang), (1, 1, 2))
    scale = (MLA_NOPE + MLA_ROPE) ** -0.5 * LOG2_E
    csq = scale * jnp.concatenate([jnp.ones(cos2.shape[:2] + (MLA_NOPE,), F32), cos2, sin2], -1)
    csq = jnp.swapaxes(csq.reshape(bsz, seq // tm, tm, QK_DIM), 2, 3)
    csk = jnp.concatenate([cos2, sin2, cos2, sin2], axis=-1)

    row = lambda a: a[:, None, :]
    w_in_e = _prep_w_in(w_in)
    w_uq_e = jnp.swapaxes(_prep_w_uq(mla_w_uq), 1, 2)
    w_k_e, w_v_e = _prep_w_ukv(mla_w_ukv)
    w_v_e = jnp.swapaxes(w_v_e, 1, 2)
    ws = gmlp_w_s.astype(BF16)
    bs = jnp.repeat(jnp.swapaxes(gmlp_b_s, 1, 2), A_HEAD_DIM, axis=2)
    pool_bd = _prep_pool_w(pool_w)
    wo_acd = jnp.concatenate([w_out[:, :A_WIDTH], w_out[:, A_WIDTH + MLA_WIDTH:]], axis=1).astype(BF16)
    wo_b = w_out[:, A_WIDTH:A_WIDTH + MLA_WIDTH].astype(BF16)
    wq = (xattn_wq * (X_HEAD_DIM ** -0.5 * LOG2_E)).astype(BF16)
    wo = xattn_wo.astype(BF16)
    rw_t = router_w.T
    rw_hi = rw_t.astype(BF16)
    rw = jnp.concatenate([rw_hi, (rw_t - rw_hi.astype(F32)).astype(BF16)], axis=0)
    rb_t = jnp.broadcast_to(router_b[:, None], (N_EXPERTS, tm))
    before = jnp.triu(jnp.ones((tm, tm), BF16), 1)
    wg = moe_w_gate.astype(BF16)
    wu = moe_w_up.astype(BF16)
    wd = moe_w_down.reshape(depth, N_EXPERTS * D_FF, D_MODEL).astype(BF16)

    km, vm = _mem_kv(mem, xattn_wk.astype(BF16), xattn_wv.astype(BF16))

    for l in range(depth):
        q, k, v, yacd = _mixer_in(
            x, csq, csk, w_in_e, row(gmlp_v_norm_g), ws, bs, row(mla_q_norm_g),
            w_uq_e, row(mla_kv_norm_g), w_k_e, w_v_e, conv_w, pool_bd,
            row(pool_scale), layer=l, tm=tm)
        yb = _mla_attn(q, k, v, hg=MLA_HEADS)
        x2, logits = _post_attn(
            yacd, yb, x, wo_acd, wo_b, row(ln1_g), row(ln1_b), wq, km, vm,
            wo, row(ln2_g), row(ln2_b), rw, layer=l, tm=tm, alpha=alpha)
        x = _moe(x2, logits, rb_t, before, wg, wu, wd, row(ln3_g), row(ln3_b),
                 layer=l, tm=tm, alpha=alpha)
    return x
```

```python
import functools

import jax
import jax.numpy as jnp
from jax import lax
from jax.experimental import pallas as pl
from jax.experimental.pallas import tpu as pltpu

F32 = jnp.float32
BF16 = jnp.bfloat16

D_MODEL = 1024
A_HEADS, A_HEAD_DIM, CHUNK = 4, 64, 128
A_WIDTH = A_HEADS * A_HEAD_DIM
MLA_HEADS, MLA_NOPE, MLA_ROPE, MLA_V = 8, 64, 32, 64
Q_LORA, KV_LORA = 256, 128
MLA_WIDTH = MLA_HEADS * MLA_V
ROPE_BASE = 10000.0
C_WIDTH, CONV_W = 256, 3
D_WIDTH = 256
POOL_WINDOWS = (2, 4, 8, 16)
D_GROUP = D_WIDTH // len(POOL_WINDOWS)
X_HEADS = 4
X_HEAD_DIM = D_MODEL // X_HEADS
N_EXPERTS, N_GROUPS, TOP_K = 16, 4, 2
EXPERTS_PER_GROUP = N_EXPERTS // N_GROUPS
D_FF = 256
EPS = 1e-6

LANES = 128
SUBLANES = 8
VMEM_LIMIT = 56 * 1024 * 1024

OFF_U, OFF_CQ, OFF_BG, P_EXT = 0, 512, 1024, 2048
QK_DIM = 128
V_ROWS = 80
POOL_HALO = 16
MOE_ROWS = 160
ROUTE_ROWS = 16
NEG_BIG = -1e30
LOG2_E = 1.4426950408889634


def _dot(a, b):
    return jnp.dot(a, b, preferred_element_type=F32)


def _dot_nt(a, b):
    return lax.dot_general(a, b, (((1,), (1,)), ((), ())), preferred_element_type=F32)


def _rms(x, g):
    return x * lax.rsqrt(jnp.mean(x * x, axis=-1, keepdims=True) + EPS) * g


def _layer_norm(x, g, b):
    mu = jnp.mean(x, axis=-1, keepdims=True)
    xc = x - mu
    var = jnp.mean(xc * xc, axis=-1, keepdims=True)
    return xc * lax.rsqrt(var + EPS) * g + b


def _params(n_grid):
    return pltpu.CompilerParams(dimension_semantics=("arbitrary",) * n_grid,
                                vmem_limit_bytes=VMEM_LIMIT)


def _run_interleaved(chains):
    live = list(chains)
    while live:
        for c in list(live):
            try:
                next(c)
            except StopIteration:
                live.remove(c)


def _const_spec(shape):
    zeros = (0,) * len(shape)
    return pl.BlockSpec(shape, lambda *_: zeros)


def _layer_spec(stacked, layer, single_buffer=False):
    index = (layer,) + (0,) * (stacked.ndim - 1)
    mode = dict(pipeline_mode=pl.Buffered(1)) if single_buffer else {}
    return pl.BlockSpec((None,) + stacked.shape[1:], lambda *_: index, **mode)


def _mem_kv_kernel(mem_ref, wk_ref, wv_ref, k_ref, v_ref):
    m = mem_ref[0].astype(BF16)
    k_ref[0, 0] = _dot(m, wk_ref[0]).astype(BF16)
    v_ref[0, 0] = _dot(m, wv_ref[0]).astype(BF16)


def _mem_kv(mem, wk, wv):
    depth = wk.shape[0]
    bsz, mlen, d = mem.shape
    out = jax.ShapeDtypeStruct((depth, bsz, mlen, d), BF16)
    return pl.pallas_call(
        _mem_kv_kernel,
        grid=(depth, bsz),
        in_specs=[pl.BlockSpec((1, mlen, d), lambda l, b: (b, 0, 0)),
                  pl.BlockSpec((1, d, d), lambda l, b: (l, 0, 0)),
                  pl.BlockSpec((1, d, d), lambda l, b: (l, 0, 0))],
        out_specs=[pl.BlockSpec((1, 1, mlen, d), lambda l, b: (l, b, 0, 0)),
                   pl.BlockSpec((1, 1, mlen, d), lambda l, b: (l, b, 0, 0))],
        out_shape=[out, out],
        compiler_params=_params(2),
        name="mem_kv",
    )(mem, wk, wv)


def _mixer_in_kernel(x_ref, csq_ref, csk_ref, w_in_ref, vng_ref, ws_ref, bs_ref, qng_ref,
                     wuq_ref, kvng_ref, wk_ref, wv_ref, convw_ref, poolw_ref, pscale_ref,
                     q_ref, k_ref, v_ref, y_ref,
                     conv_buf, pool_a, pool_b, pool_carry, *, tm):
    si = pl.program_id(1)

    @pl.when(si == 0)
    def _():
        pool_carry[...] = jnp.zeros((POOL_HALO, D_WIDTH), F32)
        conv_buf[0:SUBLANES, :] = jnp.zeros((SUBLANES, C_WIDTH), F32)

    xb = x_ref[0].astype(BF16)

    za = _dot(xb, w_in_ref[:, OFF_U:OFF_CQ])
    zb = _dot(xb, w_in_ref[:, OFF_CQ:OFF_BG])
    zcd = _dot(xb, w_in_ref[:, OFF_BG:P_EXT])

    za = jax.nn.gelu(za)
    u = za[:, :A_WIDTH]
    v = _rms(za[:, A_WIDTH:], vng_ref[...]).astype(BF16)
    cq = _rms(zb[:, :Q_LORA], qng_ref[...]).astype(BF16)
    ckv = _rms(zb[:, Q_LORA:Q_LORA + KV_LORA], kvng_ref[...]).astype(BF16)

    row = lax.broadcasted_iota(jnp.int32, (CHUNK, CHUNK), 0)
    col = lax.broadcasted_iota(jnp.int32, (CHUNK, CHUNK), 1)
    w_causal = jnp.concatenate(
        [jnp.where(row >= col, ws_ref[h], jnp.zeros((), BF16)) for h in range(A_HEADS)], axis=1)
    lane_head = lax.broadcasted_iota(jnp.int32, (CHUNK, A_WIDTH), 1) // A_HEAD_DIM
    for c in range(tm // CHUNK):
        rows = slice(c * CHUNK, (c + 1) * CHUNK)
        vc = v[rows]
        v_heads = jnp.concatenate(
            [jnp.where(lane_head == h, vc, jnp.zeros((), BF16)) for h in range(A_HEADS)], axis=0)
        mixed = _dot(w_causal, v_heads)
        y_ref[0, rows, 0:A_WIDTH] = (u[rows] * (mixed + bs_ref[...])).astype(BF16)

    qt = _dot_nt(wuq_ref[...], cq)
    kk = _dot(ckv, wk_ref[...])
    vt = _dot_nt(wv_ref[...], ckv)

    zd = zcd[:, 3 * C_WIDTH:]
    base = SUBLANES
    n = tm + POOL_HALO
    pool_a[0:base, :] = jnp.zeros((base, D_WIDTH), F32)
    pool_b[0:base, :] = jnp.zeros((base, D_WIDTH), F32)
    pool_a[base:base + POOL_HALO, :] = pool_carry[...]
    pool_a[base + POOL_HALO:base + n, :] = zd
    pool_carry[...] = zd[tm - POOL_HALO:, :]
    lane_d = lax.broadcasted_iota(jnp.int32, (n, D_WIDTH), 1)
    pool_b[base:base + n, :] = pool_a[base:base + n, :] + pool_a[base - 1:base - 1 + n, :]
    pool_a[base:base + n, :] = pool_b[base:base + n, :] + jnp.where(
        lane_d >= D_GROUP, pool_b[base - 2:base - 2 + n, :], 0.0)
    pool_b[base:base + n, :] = pool_a[base:base + n, :] + jnp.where(
        lane_d >= 2 * D_GROUP, pool_a[base - 4:base - 4 + n, :], 0.0)
    t0 = base + POOL_HALO
    lane_t = lax.broadcasted_iota(jnp.int32, (tm, D_WIDTH), 1)
    win_sum = pool_b[t0:t0 + tm, :] + jnp.where(
        lane_t >= 3 * D_GROUP, pool_b[t0 - 8:t0 - 8 + tm, :], 0.0)
    pos1 = si * tm + lax.broadcasted_iota(jnp.int32, (tm, D_WIDTH), 0) + 1
    window = jnp.left_shift(2, lane_t // D_GROUP)
    count = jnp.minimum(pos1, window).astype(F32)
    pooled = (win_sum / count - zd).astype(BF16)
    yd = _dot(pooled, poolw_ref[...]) * pscale_ref[...]
    y_ref[0, :, A_WIDTH + C_WIDTH:] = yd.astype(BF16)

    gh = zcd[:, C_WIDTH:2 * C_WIDTH] * zcd[:, 2 * C_WIDTH:3 * C_WIDTH]
    conv_buf[SUBLANES:SUBLANES + tm, :] = gh
    conv = (convw_ref[2:3, :] * gh
            + convw_ref[1:2, :] * conv_buf[SUBLANES - 1:SUBLANES - 1 + tm, :]
            + convw_ref[0:1, :] * conv_buf[SUBLANES - 2:SUBLANES - 2 + tm, :])
    y_ref[0, :, A_WIDTH:A_WIDTH + C_WIDTH] = (zcd[:, :C_WIDTH] * conv).astype(BF16)
    conv_buf[0:SUBLANES, :] = conv_buf[tm:tm + SUBLANES, :]

    csq = csq_ref[0, 0]
    for h in range(MLA_HEADS):
        q_ref[0, h, 0] = (qt[h * QK_DIM:(h + 1) * QK_DIM, :] * csq).astype(BF16)
    r4 = zb[:, Q_LORA + KV_LORA:] * csk_ref[0]
    kr = r4 + pltpu.roll(r4, MLA_ROPE, 1)
    lane = lax.broadcasted_iota(jnp.int32, (tm, QK_DIM), 1)
    kr = jnp.where(lane >= MLA_NOPE, kr, 0.0)
    for h in range(MLA_HEADS):
        k_ref[0, h] = (kk[:, h * QK_DIM:(h + 1) * QK_DIM] + kr).astype(BF16)
    ones_then_zeros = jnp.where(
        lax.broadcasted_iota(jnp.int32, (V_ROWS - MLA_V, tm), 0) == 0, 1.0, 0.0).astype(BF16)
    for h in range(MLA_HEADS):
        v_ref[0, h, 0, 0:MLA_V, :] = vt[h * MLA_V:(h + 1) * MLA_V, :].astype(BF16)
        v_ref[0, h, 0, MLA_V:, :] = ones_then_zeros


def _mixer_in(x, csq, csk, w_in, vng, ws, bs, qng, wuq, kvng, wk, wv, convw, poolw, pscale,
              *, layer, tm):
    bsz, seq, d = x.shape
    tok = lambda w: pl.BlockSpec((1, tm, w), lambda b, s: (b, s, 0))
    head = lambda n, w: pl.BlockSpec((1, n, tm, w), lambda b, s: (b, 0, s, 0))
    head_t = lambda r: pl.BlockSpec((1, MLA_HEADS, 1, r, tm), lambda b, s: (b, 0, s, 0, 0))
    shape_t = lambda r: jax.ShapeDtypeStruct((bsz, MLA_HEADS, seq // tm, r, tm), BF16)
    return pl.pallas_call(
        functools.partial(_mixer_in_kernel, tm=tm),
        grid=(bsz, seq // tm),
        in_specs=[tok(d), pl.BlockSpec((1, 1, QK_DIM, tm), lambda b, s: (b, s, 0, 0)), tok(QK_DIM),
                  *[_layer_spec(w, layer) for w in (w_in, vng, ws, bs, qng, wuq, kvng, wk, wv,
                                                     convw, poolw, pscale)]],
        out_specs=[head_t(QK_DIM), head(MLA_HEADS, QK_DIM), head_t(V_ROWS),
                   tok(A_WIDTH + C_WIDTH + D_WIDTH)],
        out_shape=[shape_t(QK_DIM),
                   jax.ShapeDtypeStruct((bsz, MLA_HEADS, seq, QK_DIM), BF16),
                   shape_t(V_ROWS),
                   jax.ShapeDtypeStruct((bsz, seq, A_WIDTH + C_WIDTH + D_WIDTH), BF16)],
        scratch_shapes=[pltpu.VMEM((tm + SUBLANES, C_WIDTH), F32),
                        pltpu.VMEM((tm + POOL_HALO + SUBLANES, D_WIDTH), F32),
                        pltpu.VMEM((tm + POOL_HALO + SUBLANES, D_WIDTH), F32),
                        pltpu.VMEM((POOL_HALO, D_WIDTH), F32)],
        compiler_params=_params(2),
        name="mixer_in",
    )(x, csq, csk, w_in, vng, ws, bs, qng, wuq, kvng, wk, wv, convw, poolw, pscale)


def _mla_attn_kernel(qt_ref, k_ref, vt_ref, o_ref, m_ref, acc_ref, *, tq, hg):
    qi = pl.program_id(2)
    half = tq // 2
    m_ref[...] = jnp.full(m_ref.shape, NEG_BIG, F32)
    acc_ref[...] = jnp.zeros(acc_ref.shape, F32)

    def step(j, diagonal):
        base = pl.multiple_of(j * tq, tq)
        pieces = ([(hh, 0, 0) for hh in range(hg)]
                  + [(hh, half, half if diagonal else 0) for hh in range(hg)])

        def scores(piece):
            hh, k0, q0 = piece
            return _dot(k_ref[0, hh, pl.ds(base + k0, half), :], qt_ref[0, hh, 0, :, q0:])

        def softmax(piece, st):
            hh, k0, q0 = piece
            if diagonal:
                key = lax.broadcasted_iota(jnp.int32, st.shape, 0)
                qry = lax.broadcasted_iota(jnp.int32, st.shape, 1)
                st = jnp.where(key <= qry, st, NEG_BIG)
            m_old = m_ref[hh, :, q0:]
            m_new = jnp.maximum(m_old, jnp.max(st, axis=0, keepdims=True))
            m_ref[hh, :, q0:] = m_new
            return piece, jnp.exp2(st - m_new).astype(BF16), jnp.exp2(m_old - m_new)

        def weighted_values(piece, probs, rescale):
            hh, k0, q0 = piece
            pv = _dot(vt_ref[0, hh, j, :, k0:k0 + half], probs)
            acc_ref[hh, :, q0:] = rescale * acc_ref[hh, :, q0:] + pv

        st_next = scores(pieces[0])
        pending = None
        for i, piece in enumerate(pieces):
            st = st_next
            if i + 1 < len(pieces):
                st_next = scores(pieces[i + 1])
            if pending is not None:
                weighted_values(*pending)
            pending = softmax(piece, st)
        weighted_values(*pending)

    def body(j, carry):
        step(j, False)
        return carry

    lax.fori_loop(0, qi, body, 0)
    step(qi, True)
    for pr in range(hg // 2):
        halves = []
        for hh in (2 * pr, 2 * pr + 1):
            acc = acc_ref[hh]
            halves.append(acc[:MLA_V] / acc[MLA_V:MLA_V + 1])
        o_ref[0, :, pr * LANES:(pr + 1) * LANES] = jnp.concatenate(halves, axis=0).T.astype(BF16)


def _mla_attn(qt, k, vt, *, hg):
    bsz, heads, n_tiles, _, tq = qt.shape
    seq = n_tiles * tq
    return pl.pallas_call(
        functools.partial(_mla_attn_kernel, tq=tq, hg=hg),
        grid=(bsz, heads // hg, n_tiles),
        in_specs=[pl.BlockSpec((1, hg, 1, QK_DIM, tq), lambda b, g, i: (b, g, i, 0, 0)),
                  pl.BlockSpec((1, hg, seq, QK_DIM), lambda b, g, i: (b, g, 0, 0)),
                  pl.BlockSpec((1, hg, n_tiles, V_ROWS, tq), lambda b, g, i: (b, g, 0, 0, 0))],
        out_specs=pl.BlockSpec((1, tq, hg * MLA_V), lambda b, g, i: (b, i, g)),
        out_shape=jax.ShapeDtypeStruct((bsz, seq, MLA_WIDTH), BF16),
        scratch_shapes=[pltpu.VMEM((hg, 1, tq), F32), pltpu.VMEM((hg, V_ROWS, tq), F32)],
        compiler_params=_params(3),
        name="mla_attn",
    )(qt, k, vt)


def _route_t(logits_t, bias_t):
    scores = jax.nn.sigmoid(logits_t)
    sel_all = scores + bias_t
    sel = [sel_all[e:e + 1] for e in range(N_EXPERTS)]
    group_score = []
    for g in range(N_GROUPS):
        v = sel[g * EXPERTS_PER_GROUP:(g + 1) * EXPERTS_PER_GROUP]
        best = None
        for i in range(EXPERTS_PER_GROUP):
            for j in range(i + 1, EXPERTS_PER_GROUP):
                best = v[i] + v[j] if best is None else jnp.maximum(best, v[i] + v[j])
        group_score.append(best)
    top, top_idx = group_score[0], jnp.zeros(group_score[0].shape, jnp.int32)
    for g in range(1, N_GROUPS):
        better = group_score[g] > top
        top = jnp.where(better, group_score[g], top)
        top_idx = jnp.where(better, g, top_idx)
    member = [jnp.where(top_idx == g, 1.0, 0.0) for g in range(N_GROUPS)]
    weights = []
    for e in range(N_EXPERTS):
        g = e // EXPERTS_PER_GROUP
        rank = jnp.zeros(sel[e].shape, jnp.int32)
        for j in range(g * EXPERTS_PER_GROUP, (g + 1) * EXPERTS_PER_GROUP):
            if j != e:
                rank = rank + jnp.where(sel[j] > sel[e], 1,
                                        jnp.where(sel[j] == sel[e], 1 if j < e else 0, 0))
        weights.append(jnp.where(rank < TOP_K, member[g], 0.0) * scores[e:e + 1])
    total = weights[0]
    for w in weights[1:]:
        total = total + w
    return [w / total for w in weights], member


def _post_attn_kernel(yacd_ref, yb_ref, x_ref, wo_acd_ref, wo_b_ref, g1_ref, b1_ref, wq_ref,
                      km_ref, vm_ref, wo_ref, g2_ref, b2_ref, rw_ref,
                      x2_ref, logits_ref, resid_ref, *, tm, alpha, n_chains):
    t = pl.program_id(0)
    cur = t % 2

    @pl.when(t == 0)
    def _():
        resid_ref[...] = jnp.zeros(resid_ref.shape, F32)

    def finish():
        x2 = _layer_norm(resid_ref[1 - cur], g2_ref[...], b2_ref[...])
        x2_ref[0] = x2
        x_hi = x2.astype(BF16)
        x_lo = (x2 - x_hi.astype(F32)).astype(BF16)
        yield
        hl = _dot_nt(rw_ref[...], x_hi)
        lo = _dot_nt(rw_ref[0:N_EXPERTS, :], x_lo)
        logits_ref[0] = hl[:N_EXPERTS] + hl[N_EXPERTS:] + lo

    def chain(rows):
        h = _dot(yacd_ref[0, rows], wo_acd_ref[...]) + _dot(yb_ref[0, rows], wo_b_ref[...])
        yield
        x1 = _layer_norm(alpha * x_ref[0, rows] + h, g1_ref[...], b1_ref[...])
        q = _dot(x1.astype(BF16), wq_ref[...])
        yield
        heads = []
        for hd in range(X_HEADS):
            cols = slice(hd * X_HEAD_DIM, (hd + 1) * X_HEAD_DIM)
            s = _dot_nt(q[:, cols].astype(BF16), km_ref[0, :, cols])
            e = jnp.exp2(s - jnp.max(s, axis=-1, keepdims=True))
            o = _dot(e.astype(BF16), vm_ref[0, :, cols]) / jnp.sum(e, axis=-1, keepdims=True)
            heads.append(o.astype(BF16))
            yield
        h2 = _dot(jnp.concatenate(heads, axis=1), wo_ref[...])
        resid_ref[cur, rows] = alpha * x1 + h2

    rows_per = tm // n_chains
    _run_interleaved([finish()]
                     + [chain(slice(c * rows_per, (c + 1) * rows_per)) for c in range(n_chains)])


def _post_attn(yacd, yb, x, wo_acd, wo_b, g1, b1, wq, km, vm, wo, g2, b2, rw,
               *, layer, tm, alpha):
    bsz, seq, d = x.shape
    n_tiles = seq // tm
    last = bsz * n_tiles - 1
    cur_b = lambda t: jnp.minimum(t, last) // n_tiles
    cur_s = lambda t: jnp.minimum(t, last) % n_tiles
    prev_b = lambda t: jnp.maximum(t - 1, 0) // n_tiles
    prev_s = lambda t: jnp.maximum(t - 1, 0) % n_tiles
    tok = lambda w: pl.BlockSpec((1, tm, w), lambda t: (cur_b(t), cur_s(t), 0))
    memspec = pl.BlockSpec((None, 1) + km.shape[2:], lambda t: (layer, cur_b(t), 0, 0))
    per_layer = lambda w: _layer_spec(w, layer)
    return pl.pallas_call(
        functools.partial(_post_attn_kernel, tm=tm, alpha=alpha, n_chains=2),
        grid=(bsz * n_tiles + 1,),
        in_specs=[tok(yacd.shape[-1]), tok(yb.shape[-1]), tok(d),
                  per_layer(wo_acd), per_layer(wo_b), per_layer(g1), per_layer(b1), per_layer(wq),
                  memspec, memspec, per_layer(wo), per_layer(g2), per_layer(b2),
                  _const_spec(rw.shape)],
        out_specs=[pl.BlockSpec((1, tm, d), lambda t: (prev_b(t), prev_s(t), 0)),
                   pl.BlockSpec((1, N_EXPERTS, tm), lambda t: (prev_b(t), 0, prev_s(t)))],
        out_shape=[jax.ShapeDtypeStruct((bsz, seq, d), F32),
                   jax.ShapeDtypeStruct((bsz, N_EXPERTS, seq), F32)],
        scratch_shapes=[pltpu.VMEM((2, tm, d), F32)],
        compiler_params=_params(1),
        name="post_attn",
    )(yacd, yb, x, wo_acd, wo_b, g1, b1, wq, km, vm, wo, g2, b2, rw)


def _dot_tn(a, b):
    return lax.dot_general(a, b, (((0,), (0,)), ((), ())), preferred_element_type=F32)


def _moe_kernel(x_ref, logits_ref, rb_ref, before_ref, wg_ref, wu_ref, wd_ref, g3_ref, b3_ref, o_ref,
                xb_ref, pos_ref, memb_ref, gsplit_ref, acc_ref, *, alpha, tm, rows):
    group_w = EXPERTS_PER_GROUP * D_FF
    x = x_ref[0]
    xb_ref[...] = x.astype(BF16)
    gates, member = _route_t(logits_ref[0], rb_ref[...])
    memb = jnp.concatenate(member + [jnp.zeros((ROUTE_ROWS - N_GROUPS, tm), F32)], axis=0)
    memb_ref[...] = memb
    pos_ref[...] = _dot(memb.astype(BF16), before_ref[...])
    gate_rows = jnp.concatenate(gates, axis=0)
    g_hi = gate_rows.astype(BF16)
    gsplit_ref[...] = jnp.concatenate([g_hi, (gate_rows - g_hi.astype(F32)).astype(BF16)], axis=0)

    def expert_chunk(g, ci):
        slot = (lax.broadcasted_iota(jnp.int32, (rows, tm), 0) + ci * rows).astype(F32)
        p = jnp.where(pos_ref[g:g + 1, :] == slot, memb_ref[g:g + 1, :], 0.0).astype(BF16)
        xg = _dot(p, xb_ref[...]).astype(BF16)
        gg = _dot_nt(p, gsplit_ref[...])
        hs = []
        for j in range(EXPERTS_PER_GROUP):
            e = g * EXPERTS_PER_GROUP + j
            gate = gg[:, e:e + 1] + gg[:, N_EXPERTS + e:N_EXPERTS + e + 1]
            hg = _dot(xg, wg_ref[e])
            hu = _dot(xg, wu_ref[e])
            hs.append((jax.nn.silu(hg) * hu * gate).astype(BF16))
        y = _dot(jnp.concatenate(hs, axis=1), wd_ref[g * group_w:(g + 1) * group_w, :])
        return p, y.astype(BF16)

    first = [expert_chunk(g, 0) for g in range(N_GROUPS)]
    acc_ref[...] = _dot_tn(jnp.concatenate([p for p, _ in first], axis=0),
                           jnp.concatenate([y for _, y in first], axis=0))
    largest = jnp.max(jnp.sum(memb_ref[0:SUBLANES, :], axis=1, keepdims=True))

    @pl.when(largest > rows)
    def _():
        for g in range(N_GROUPS):
            n_tok = jnp.sum(memb_ref[g:g + 1, :]).astype(jnp.int32)

            def extra(ci, carry, g=g):
                p, y = expert_chunk(g, ci)
                acc_ref[...] += _dot_tn(p, y)
                return carry

            lax.fori_loop(1, (n_tok + rows - 1) // rows, extra, 0)

    o_ref[0] = _layer_norm(alpha * x_ref[0] + acc_ref[...], g3_ref[...], b3_ref[...])


def _moe(x, logits_t, rb_t, before, wg, wu, wd, g3, b3, *, layer, tm, alpha):
    bsz, seq, d = x.shape
    tok = lambda w: pl.BlockSpec((1, tm, w), lambda b, s: (b, s, 0))
    resident = lambda w: _layer_spec(w, layer, single_buffer=True)
    return pl.pallas_call(
        functools.partial(_moe_kernel, alpha=alpha, tm=tm, rows=MOE_ROWS),
        grid=(bsz, seq // tm),
        in_specs=[tok(d), pl.BlockSpec((1, N_EXPERTS, tm), lambda b, s: (b, 0, s)),
                  _const_spec(rb_t.shape), _const_spec(before.shape),
                  resident(wg), resident(wu), resident(wd),
                  _layer_spec(g3, layer), _layer_spec(b3, layer)],
        out_specs=tok(d),
        out_shape=jax.ShapeDtypeStruct((bsz, seq, d), F32),
        scratch_shapes=[pltpu.VMEM((tm, d), BF16),
                        pltpu.VMEM((ROUTE_ROWS, tm), F32),
                        pltpu.VMEM((ROUTE_ROWS, tm), F32),
                        pltpu.VMEM((2 * N_EXPERTS, tm), BF16),
                        pltpu.VMEM((tm, d), F32)],
        compiler_params=_params(2),
        name="moe",
    )(x, logits_t, rb_t, before, wg, wu, wd, g3, b3)


def _rot_cols(w):
    half = w.shape[-1] // 2
    return jnp.concatenate([-w[..., half:], w[..., :half]], axis=-1)


def _prep_w_in(w_in):
    u, v = w_in[..., 0:256], w_in[..., 256:512]
    cq, ckv, kr = w_in[..., 512:768], w_in[..., 768:896], w_in[..., 896:928]
    rest = w_in[..., 928:]
    rope4 = jnp.concatenate([kr, _rot_cols(kr), kr, _rot_cols(kr)], axis=-1)
    return jnp.concatenate([u, v, cq, ckv, rope4, rest], axis=-1).astype(BF16)


def _prep_w_uq(w_uq):
    depth = w_uq.shape[0]
    w = w_uq.reshape(depth, Q_LORA, MLA_HEADS, MLA_NOPE + MLA_ROPE)
    nope, rope = w[..., :MLA_NOPE], w[..., MLA_NOPE:]
    ext = jnp.concatenate([nope, rope, _rot_cols(rope)], axis=-1)
    return ext.reshape(depth, Q_LORA, MLA_HEADS * QK_DIM).astype(BF16)


def _prep_w_ukv(w_ukv):
    depth = w_ukv.shape[0]
    w = w_ukv.reshape(depth, KV_LORA, MLA_HEADS, MLA_NOPE + MLA_V)
    k_nope, v = w[..., :MLA_NOPE], w[..., MLA_NOPE:]
    wk = jnp.concatenate([k_nope, jnp.zeros_like(k_nope)], axis=-1)
    return (wk.reshape(depth, KV_LORA, MLA_HEADS * QK_DIM).astype(BF16),
            v.reshape(depth, KV_LORA, MLA_WIDTH).astype(BF16))


def _prep_pool_w(pool_w):
    depth, groups = pool_w.shape[:2]
    eye = jnp.eye(groups, dtype=pool_w.dtype)
    bd = jnp.einsum('lgcd,gh->lgchd', pool_w, eye)
    return bd.reshape(depth, D_WIDTH, D_WIDTH).astype(BF16)


def kernel(x, mem, positions, w_in, gmlp_v_norm_g, gmlp_w_s, gmlp_b_s, mla_q_norm_g, mla_w_uq,
           mla_kv_norm_g, mla_w_ukv, conv_w, pool_w, pool_scale, w_out, ln1_g, ln1_b,
           xattn_wq, xattn_wk, xattn_wv, xattn_wo, ln2_g, ln2_b, router_w, router_b,
           moe_w_gate, moe_w_up, moe_w_down, ln3_g, ln3_b):
    depth = w_in.shape[0]
    alpha = (2 * depth) ** 0.25
    tm = 512
    bsz, seq, _ = x.shape

    inv_freq = ROPE_BASE ** (-jnp.arange(0, MLA_ROPE, 2, dtype=F32) / MLA_ROPE)
    ang = positions.astype(F32)[..., None] * inv_freq
    cos2 = jnp.tile(jnp.cos(ang), (1, 1, 2))
    sin2 = jnp.tile(jnp.sin(ang), (1, 1, 2))
    scale = (MLA_NOPE + MLA_ROPE) ** -0.5 * LOG2_E
    csq = scale * jnp.concatenate([jnp.ones(cos2.shape[:2] + (MLA_NOPE,), F32), cos2, sin2], -1)
    csq = jnp.swapaxes(csq.reshape(bsz, seq // tm, tm, QK_DIM), 2, 3)
    csk = jnp.concatenate([cos2, sin2, cos2, sin2], axis=-1)

    row = lambda a: a[:, None, :]
    w_in_e = _prep_w_in(w_in)
    w_uq_e = jnp.swapaxes(_prep_w_uq(mla_w_uq), 1, 2)
    w_k_e, w_v_e = _prep_w_ukv(mla_w_ukv)
    w_v_e = jnp.swapaxes(w_v_e, 1, 2)
    ws = gmlp_w_s.astype(BF16)
    bs = jnp.repeat(jnp.swapaxes(gmlp_b_s, 1, 2), A_HEAD_DIM, axis=2)
    pool_bd = _prep_pool_w(pool_w)
    wo_acd = jnp.concatenate([w_out[:, :A_WIDTH], w_out[:, A_WIDTH + MLA_WIDTH:]], axis=1).astype(BF16)
    wo_b = w_out[:, A_WIDTH:A_WIDTH + MLA_WIDTH].astype(BF16)
    wq = (xattn_wq * (X_HEAD_DIM ** -0.5 * LOG2_E)).astype(BF16)
    wo = xattn_wo.astype(BF16)
    rw_t = router_w.T
    rw_hi = rw_t.astype(BF16)
    rw = jnp.concatenate([rw_hi, (rw_t - rw_hi.astype(F32)).astype(BF16)], axis=0)
    rb_t = jnp.broadcast_to(router_b[:, None], (N_EXPERTS, tm))
    before = jnp.triu(jnp.ones((tm, tm), BF16), 1)
    wg = moe_w_gate.astype(BF16)
    wu = moe_w_up.astype(BF16)
    wd = moe_w_down.reshape(depth, N_EXPERTS * D_FF, D_MODEL).astype(BF16)

    km, vm = _mem_kv(mem, xattn_wk.astype(BF16), xattn_wv.astype(BF16))

    for l in range(depth):
        q, k, v, yacd = _mixer_in(
            x, csq, csk, w_in_e, row(gmlp_v_norm_g), ws, bs, row(mla_q_norm_g),
            w_uq_e, row(mla_kv_norm_g), w_k_e, w_v_e, conv_w, pool_bd,
            row(pool_scale), layer=l, tm=tm)
        yb = _mla_attn(q, k, v, hg=MLA_HEADS)
        x2, logits = _post_attn(
            yacd, yb, x, wo_acd, wo_b, row(ln1_g), row(ln1_b), wq, km, vm,
            wo, row(ln2_g), row(ln2_b), rw, layer=l, tm=tm, alpha=alpha)
        x = _moe(x2, logits, rb_t, before, wg, wu, wd, row(ln3_g), row(ln3_b),
                 layer=l, tm=tm, alpha=alpha)
    return x
```

```python
import functools

import jax
import jax.numpy as jnp
from jax import lax
from jax.experimental import pallas as pl
from jax.experimental.pallas import tpu as pltpu

F32 = jnp.float32
BF16 = jnp.bfloat16

D_MODEL = 1024
A_HEADS, A_HEAD_DIM, CHUNK = 4, 64, 128
A_WIDTH = A_HEADS * A_HEAD_DIM
MLA_HEADS, MLA_NOPE, MLA_ROPE, MLA_V = 8, 64, 32, 64
Q_LORA, KV_LORA = 256, 128
MLA_WIDTH = MLA_HEADS * MLA_V
ROPE_BASE = 10000.0
C_WIDTH, CONV_W = 256, 3
D_WIDTH = 256
POOL_WINDOWS = (2, 4, 8, 16)
D_GROUP = D_WIDTH // len(POOL_WINDOWS)
X_HEADS = 4
X_HEAD_DIM = D_MODEL // X_HEADS
N_EXPERTS, N_GROUPS, TOP_K = 16, 4, 2
EXPERTS_PER_GROUP = N_EXPERTS // N_GROUPS
D_FF = 256
EPS = 1e-6

LANES = 128
SUBLANES = 8
VMEM_LIMIT = 56 * 1024 * 1024

OFF_U = 0
OFF_CQ = OFF_U + 2 * A_WIDTH
OFF_BG = OFF_CQ + Q_LORA + KV_LORA + 4 * MLA_ROPE
P_EXT = OFF_BG + 3 * C_WIDTH + D_WIDTH
QK_DIM = 128
V_ROWS = 80
POOL_HALO = 16
ROUTE_ROWS = 16
NEG_BIG = -1e30
LOG2_E = 1.4426950408889634


def _dot(a, b):
    return jnp.dot(a, b, preferred_element_type=F32)


def _dot_nt(a, b):
    return lax.dot_general(a, b, (((1,), (1,)), ((), ())), preferred_element_type=F32)


def _rms(x, g):
    return x * lax.rsqrt(jnp.mean(x * x, axis=-1, keepdims=True) + EPS) * g


def _layer_norm(x, g, b):
    mu = jnp.mean(x, axis=-1, keepdims=True)
    xc = x - mu
    var = jnp.mean(xc * xc, axis=-1, keepdims=True)
    return xc * lax.rsqrt(var + EPS) * g + b


def _params(n_grid):
    return pltpu.CompilerParams(dimension_semantics=("arbitrary",) * n_grid,
                                vmem_limit_bytes=VMEM_LIMIT)


def _run_interleaved(chains):
    live = list(chains)
    while live:
        for c in list(live):
            try:
                next(c)
            except StopIteration:
                live.remove(c)


def _const_spec(shape):
    zeros = (0,) * len(shape)
    return pl.BlockSpec(shape, lambda *_: zeros)


def _layer_spec(stacked, layer, single_buffer=False):
    index = (layer,) + (0,) * (stacked.ndim - 1)
    mode = dict(pipeline_mode=pl.Buffered(1)) if single_buffer else {}
    return pl.BlockSpec((None,) + stacked.shape[1:], lambda *_: index, **mode)


def _mem_kv_kernel(mem_ref, wk_ref, wv_ref, k_ref, v_ref):
    m = mem_ref[0].astype(BF16)
    k_ref[0, 0] = _dot(m, wk_ref[0]).astype(BF16)
    v_ref[0, 0] = _dot(m, wv_ref[0]).astype(BF16)


def _mem_kv(mem, wk, wv):
    depth = wk.shape[0]
    bsz, mlen, d = mem.shape
    out = jax.ShapeDtypeStruct((depth, bsz, mlen, d), BF16)
    return pl.pallas_call(
        _mem_kv_kernel,
        grid=(depth, bsz),
        in_specs=[pl.BlockSpec((1, mlen, d), lambda l, b: (b, 0, 0)),
                  pl.BlockSpec((1, d, d), lambda l, b: (l, 0, 0)),
                  pl.BlockSpec((1, d, d), lambda l, b: (l, 0, 0))],
        out_specs=[pl.BlockSpec((1, 1, mlen, d), lambda l, b: (l, b, 0, 0)),
                   pl.BlockSpec((1, 1, mlen, d), lambda l, b: (l, b, 0, 0))],
        out_shape=[out, out],
        compiler_params=_params(2),
        name="mem_kv",
    )(mem, wk, wv)


def _mixer_in_kernel(x_ref, csq_ref, csk_ref, w_in_ref, vng_ref, ws_ref, bs_ref, qng_ref,
                     wuq_ref, kvng_ref, wk_ref, wv_ref, convw_ref, poolw_ref, pscale_ref,
                     q_ref, k_ref, v_ref, y_ref,
                     conv_buf, pool_a, pool_b, pool_carry, *, tm):
    si = pl.program_id(1)

    @pl.when(si == 0)
    def _():
        pool_carry[...] = jnp.zeros((POOL_HALO, D_WIDTH), F32)
        conv_buf[0:SUBLANES, :] = jnp.zeros((SUBLANES, C_WIDTH), F32)

    xb = x_ref[0].astype(BF16)

    za = _dot(xb, w_in_ref[:, OFF_U:OFF_CQ])
    zb = _dot(xb, w_in_ref[:, OFF_CQ:OFF_BG])
    zcd = _dot(xb, w_in_ref[:, OFF_BG:P_EXT])

    za = jax.nn.gelu(za)
    u = za[:, :A_WIDTH]
    v = _rms(za[:, A_WIDTH:], vng_ref[...]).astype(BF16)
    cq = _rms(zb[:, :Q_LORA], qng_ref[...]).astype(BF16)
    ckv = _rms(zb[:, Q_LORA:Q_LORA + KV_LORA], kvng_ref[...]).astype(BF16)

    row = lax.broadcasted_iota(jnp.int32, (CHUNK, CHUNK), 0)
    col = lax.broadcasted_iota(jnp.int32, (CHUNK, CHUNK), 1)
    w_causal = jnp.concatenate(
        [jnp.where(row >= col, ws_ref[h], jnp.zeros((), BF16)) for h in range(A_HEADS)], axis=1)
    lane_head = lax.broadcasted_iota(jnp.int32, (CHUNK, A_WIDTH), 1) // A_HEAD_DIM
    for c in range(tm // CHUNK):
        rows = slice(c * CHUNK, (c + 1) * CHUNK)
        vc = v[rows]
        v_heads = jnp.concatenate(
            [jnp.where(lane_head == h, vc, jnp.zeros((), BF16)) for h in range(A_HEADS)], axis=0)
        mixed = _dot(w_causal, v_heads)
        y_ref[0, rows, 0:A_WIDTH] = (u[rows] * (mixed + bs_ref[...])).astype(BF16)

    qt = _dot_nt(wuq_ref[...], cq)
    kk = _dot(ckv, wk_ref[...])
    vt = _dot_nt(wv_ref[...], ckv)

    zd = zcd[:, 3 * C_WIDTH:]
    base = SUBLANES
    n = tm + POOL_HALO
    pool_a[0:base, :] = jnp.zeros((base, D_WIDTH), F32)
    pool_b[0:base, :] = jnp.zeros((base, D_WIDTH), F32)
    pool_a[base:base + POOL_HALO, :] = pool_carry[...]
    pool_a[base + POOL_HALO:base + n, :] = zd
    pool_carry[...] = zd[tm - POOL_HALO:, :]
    lane_d = lax.broadcasted_iota(jnp.int32, (n, D_WIDTH), 1)
    pool_b[base:base + n, :] = pool_a[base:base + n, :] + pool_a[base - 1:base - 1 + n, :]
    pool_a[base:base + n, :] = pool_b[base:base + n, :] + jnp.where(
        lane_d >= D_GROUP, pool_b[base - 2:base - 2 + n, :], 0.0)
    pool_b[base:base + n, :] = pool_a[base:base + n, :] + jnp.where(
        lane_d >= 2 * D_GROUP, pool_a[base - 4:base - 4 + n, :], 0.0)
    t0 = base + POOL_HALO
    lane_t = lax.broadcasted_iota(jnp.int32, (tm, D_WIDTH), 1)
    win_sum = pool_b[t0:t0 + tm, :] + jnp.where(
        lane_t >= 3 * D_GROUP, pool_b[t0 - 8:t0 - 8 + tm, :], 0.0)
    pos1 = si * tm + lax.broadcasted_iota(jnp.int32, (tm, D_WIDTH), 0) + 1
    window = jnp.left_shift(2, lane_t // D_GROUP)
    count = jnp.minimum(pos1, window).astype(F32)
    pooled = (win_sum / count - zd).astype(BF16)
    yd = _dot(pooled, poolw_ref[...]) * pscale_ref[...]
    y_ref[0, :, A_WIDTH + C_WIDTH:] = yd.astype(BF16)

    gh = zcd[:, C_WIDTH:2 * C_WIDTH] * zcd[:, 2 * C_WIDTH:3 * C_WIDTH]
    conv_buf[SUBLANES:SUBLANES + tm, :] = gh
    conv = (convw_ref[2:3, :] * gh
            + convw_ref[1:2, :] * conv_buf[SUBLANES - 1:SUBLANES - 1 + tm, :]
            + convw_ref[0:1, :] * conv_buf[SUBLANES - 2:SUBLANES - 2 + tm, :])
    y_ref[0, :, A_WIDTH:A_WIDTH + C_WIDTH] = (zcd[:, :C_WIDTH] * conv).astype(BF16)
    conv_buf[0:SUBLANES, :] = conv_buf[tm:tm + SUBLANES, :]

    csq = csq_ref[0, 0]
    for h in range(MLA_HEADS):
        q_ref[0, h, 0] = (qt[h * QK_DIM:(h + 1) * QK_DIM, :] * csq).astype(BF16)
    r4 = zb[:, Q_LORA + KV_LORA:] * csk_ref[0]
    kr = r4 + pltpu.roll(r4, MLA_ROPE, 1)
    lane = lax.broadcasted_iota(jnp.int32, (tm, QK_DIM), 1)
    kr = jnp.where(lane >= MLA_NOPE, kr, 0.0)
    for h in range(MLA_HEADS):
        k_ref[0, h] = (kk[:, h * QK_DIM:(h + 1) * QK_DIM] + kr).astype(BF16)
    ones_then_zeros = jnp.where(
        lax.broadcasted_iota(jnp.int32, (V_ROWS - MLA_V, tm), 0) == 0, 1.0, 0.0).astype(BF16)
    for h in range(MLA_HEADS):
        v_ref[0, h, 0, 0:MLA_V, :] = vt[h * MLA_V:(h + 1) * MLA_V, :].astype(BF16)
        v_ref[0, h, 0, MLA_V:, :] = ones_then_zeros


def _mixer_in(x, csq, csk, w_in, vng, ws, bs, qng, wuq, kvng, wk, wv, convw, poolw, pscale,
              *, layer, tm):
    bsz, seq, d = x.shape
    tok = lambda w: pl.BlockSpec((1, tm, w), lambda b, s: (b, s, 0))
    head = lambda n, w: pl.BlockSpec((1, n, tm, w), lambda b, s: (b, 0, s, 0))
    head_t = lambda r: pl.BlockSpec((1, MLA_HEADS, 1, r, tm), lambda b, s: (b, 0, s, 0, 0))
    shape_t = lambda r: jax.ShapeDtypeStruct((bsz, MLA_HEADS, seq // tm, r, tm), BF16)
    return pl.pallas_call(
        functools.partial(_mixer_in_kernel, tm=tm),
        grid=(bsz, seq // tm),
        in_specs=[tok(d), pl.BlockSpec((1, 1, QK_DIM, tm), lambda b, s: (b, s, 0, 0)), tok(QK_DIM),
                  *[_layer_spec(w, layer) for w in (w_in, vng, ws, bs, qng, wuq, kvng, wk, wv,
                                                     convw, poolw, pscale)]],
        out_specs=[head_t(QK_DIM), head(MLA_HEADS, QK_DIM), head_t(V_ROWS),
                   tok(A_WIDTH + C_WIDTH + D_WIDTH)],
        out_shape=[shape_t(QK_DIM),
                   jax.ShapeDtypeStruct((bsz, MLA_HEADS, seq, QK_DIM), BF16),
                   shape_t(V_ROWS),
                   jax.ShapeDtypeStruct((bsz, seq, A_WIDTH + C_WIDTH + D_WIDTH), BF16)],
        scratch_shapes=[pltpu.VMEM((tm + SUBLANES, C_WIDTH), F32),
                        pltpu.VMEM((tm + POOL_HALO + SUBLANES, D_WIDTH), F32),
                        pltpu.VMEM((tm + POOL_HALO + SUBLANES, D_WIDTH), F32),
                        pltpu.VMEM((POOL_HALO, D_WIDTH), F32)],
        compiler_params=_params(2),
        name="mixer_in",
    )(x, csq, csk, w_in, vng, ws, bs, qng, wuq, kvng, wk, wv, convw, poolw, pscale)


def _mla_attn_kernel(qt_ref, k_ref, vt_ref, o_ref, m_ref, acc_ref, *, tq, hg):
    qi = pl.program_id(2)
    half = tq // 2
    m_ref[...] = jnp.full(m_ref.shape, NEG_BIG, F32)
    acc_ref[...] = jnp.zeros(acc_ref.shape, F32)

    def step(j, diagonal):
        base = pl.multiple_of(j * tq, tq)
        pieces = ([(hh, 0, 0) for hh in range(hg)]
                  + [(hh, half, half if diagonal else 0) for hh in range(hg)])

        def scores(piece):
            hh, k0, q0 = piece
            return _dot(k_ref[0, hh, pl.ds(base + k0, half), :], qt_ref[0, hh, 0, :, q0:])

        def softmax(piece, st):
            hh, k0, q0 = piece
            if diagonal:
                key = lax.broadcasted_iota(jnp.int32, st.shape, 0)
                qry = lax.broadcasted_iota(jnp.int32, st.shape, 1)
                st = jnp.where(key <= qry, st, NEG_BIG)
            m_old = m_ref[hh, :, q0:]
            m_new = jnp.maximum(m_old, jnp.max(st, axis=0, keepdims=True))
            m_ref[hh, :, q0:] = m_new
            return piece, jnp.exp2(st - m_new).astype(BF16), jnp.exp2(m_old - m_new)

        def weighted_values(piece, probs, rescale):
            hh, k0, q0 = piece
            pv = _dot(vt_ref[0, hh, j, :, k0:k0 + half], probs)
            acc_ref[hh, :, q0:] = rescale * acc_ref[hh, :, q0:] + pv

        st_next = scores(pieces[0])
        pending = None
        for i, piece in enumerate(pieces):
            st = st_next
            if i + 1 < len(pieces):
                st_next = scores(pieces[i + 1])
            if pending is not None:
                weighted_values(*pending)
            pending = softmax(piece, st)
        weighted_values(*pending)

    def body(j, carry):
        step(j, False)
        return carry

    lax.fori_loop(0, qi, body, 0)
    step(qi, True)
    for pr in range(hg // 2):
        halves = []
        for hh in (2 * pr, 2 * pr + 1):
            acc = acc_ref[hh]
            halves.append(acc[:MLA_V] / acc[MLA_V:MLA_V + 1])
        o_ref[0, :, pr * LANES:(pr + 1) * LANES] = jnp.concatenate(halves, axis=0).T.astype(BF16)


def _mla_attn(qt, k, vt, *, hg):
    bsz, heads, n_tiles, _, tq = qt.shape
    seq = n_tiles * tq
    return pl.pallas_call(
        functools.partial(_mla_attn_kernel, tq=tq, hg=hg),
        grid=(bsz, heads // hg, n_tiles),
        in_specs=[pl.BlockSpec((1, hg, 1, QK_DIM, tq), lambda b, g, i: (b, g, i, 0, 0)),
                  pl.BlockSpec((1, hg, seq, QK_DIM), lambda b, g, i: (b, g, 0, 0)),
                  pl.BlockSpec((1, hg, n_tiles, V_ROWS, tq), lambda b, g, i: (b, g, 0, 0, 0))],
        out_specs=pl.BlockSpec((1, tq, hg * MLA_V), lambda b, g, i: (b, i, g)),
        out_shape=jax.ShapeDtypeStruct((bsz, seq, MLA_WIDTH), BF16),
        scratch_shapes=[pltpu.VMEM((hg, 1, tq), F32), pltpu.VMEM((hg, V_ROWS, tq), F32)],
        compiler_params=_params(3),
        name="mla_attn",
    )(qt, k, vt)


def _route_t(logits_t, bias_t):
    scores = jax.nn.sigmoid(logits_t)
    sel_all = scores + bias_t
    sel = [sel_all[e:e + 1] for e in range(N_EXPERTS)]
    group_score = []
    for g in range(N_GROUPS):
        v = sel[g * EXPERTS_PER_GROUP:(g + 1) * EXPERTS_PER_GROUP]
        best = None
        for i in range(EXPERTS_PER_GROUP):
            for j in range(i + 1, EXPERTS_PER_GROUP):
                best = v[i] + v[j] if best is None else jnp.maximum(best, v[i] + v[j])
        group_score.append(best)
    top, top_idx = group_score[0], jnp.zeros(group_score[0].shape, jnp.int32)
    for g in range(1, N_GROUPS):
        better = group_score[g] > top
        top = jnp.where(better, group_score[g], top)
        top_idx = jnp.where(better, g, top_idx)
    member = [jnp.where(top_idx == g, 1.0, 0.0) for g in range(N_GROUPS)]
    weights = []
    for e in range(N_EXPERTS):
        g = e // EXPERTS_PER_GROUP
        rank = jnp.zeros(sel[e].shape, jnp.int32)
        for j in range(g * EXPERTS_PER_GROUP, (g + 1) * EXPERTS_PER_GROUP):
            if j != e:
                rank = rank + jnp.where(sel[j] > sel[e], 1,
                                        jnp.where(sel[j] == sel[e], 1 if j < e else 0, 0))
        weights.append(jnp.where(rank < TOP_K, member[g], 0.0) * scores[e:e + 1])
    total = weights[0]
    for w in weights[1:]:
        total = total + w
    return [w / total for w in weights], member


def _post_attn_kernel(yacd_ref, yb_ref, x_ref, wo_acd_ref, wo_b_ref, g1_ref, b1_ref, wq_ref,
                      km_ref, vm_ref, wo_ref, g2_ref, b2_ref, rw_ref,
                      x2_ref, logits_ref, *, tm, alpha, n_chains):
    def chain(rows):
        h = _dot(yacd_ref[0, rows], wo_acd_ref[...]) + _dot(yb_ref[0, rows], wo_b_ref[...])
        yield
        x1 = _layer_norm(alpha * x_ref[0, rows] + h, g1_ref[...], b1_ref[...])
        q = _dot(x1.astype(BF16), wq_ref[...])
        yield
        heads = []
        for hd in range(X_HEADS):
            cols = slice(hd * X_HEAD_DIM, (hd + 1) * X_HEAD_DIM)
            s = _dot_nt(q[:, cols].astype(BF16), km_ref[0, :, cols])
            e = jnp.exp2(s - jnp.max(s, axis=-1, keepdims=True))
            o = _dot(e.astype(BF16), vm_ref[0, :, cols]) / jnp.sum(e, axis=-1, keepdims=True)
            heads.append(o.astype(BF16))
            yield
        h2 = _dot(jnp.concatenate(heads, axis=1), wo_ref[...])
        yield
        x2 = _layer_norm(alpha * x1 + h2, g2_ref[...], b2_ref[...])
        x2_ref[0, rows] = x2
        x_hi = x2.astype(BF16)
        x_lo = (x2 - x_hi.astype(F32)).astype(BF16)
        hl = _dot_nt(rw_ref[...], x_hi)
        lo = _dot_nt(rw_ref[0:N_EXPERTS, :], x_lo)
        logits_ref[0, :, rows] = hl[:N_EXPERTS] + hl[N_EXPERTS:] + lo

    rows_per = tm // n_chains
    _run_interleaved([chain(slice(c * rows_per, (c + 1) * rows_per)) for c in range(n_chains)])


def _post_attn(yacd, yb, x, wo_acd, wo_b, g1, b1, wq, km, vm, wo, g2, b2, rw,
               *, layer, tm, alpha):
    bsz, seq, d = x.shape
    tok = lambda w: pl.BlockSpec((1, tm, w), lambda b, s: (b, s, 0))
    memspec = pl.BlockSpec((None, 1) + km.shape[2:], lambda b, s: (layer, b, 0, 0))
    per_layer = lambda w: _layer_spec(w, layer)
    return pl.pallas_call(
        functools.partial(_post_attn_kernel, tm=tm, alpha=alpha, n_chains=2),
        grid=(bsz, seq // tm),
        in_specs=[tok(yacd.shape[-1]), tok(yb.shape[-1]), tok(d),
                  per_layer(wo_acd), per_layer(wo_b), per_layer(g1), per_layer(b1), per_layer(wq),
                  memspec, memspec, per_layer(wo), per_layer(g2), per_layer(b2),
                  _const_spec(rw.shape)],
        out_specs=[tok(d), pl.BlockSpec((1, N_EXPERTS, tm), lambda b, s: (b, 0, s))],
        out_shape=[jax.ShapeDtypeStruct((bsz, seq, d), F32),
                   jax.ShapeDtypeStruct((bsz, N_EXPERTS, seq), F32)],
        compiler_params=_params(2),
        name="post_attn",
    )(yacd, yb, x, wo_acd, wo_b, g1, b1, wq, km, vm, wo, g2, b2, rw)


def _dot_tn(a, b):
    return lax.dot_general(a, b, (((0,), (0,)), ((), ())), preferred_element_type=F32)


def _moe_kernel(x_ref, logits_ref, rb_ref, before_ref, wg_ref, wu_ref, wd_ref, g3_ref, b3_ref, o_ref,
                xb_ref, pos_ref, memb_ref, gsplit_ref, acc_ref, *, alpha, tm, rows):
    group_w = EXPERTS_PER_GROUP * D_FF
    x = x_ref[0]
    xb_ref[...] = x.astype(BF16)
    gates, member = _route_t(logits_ref[0], rb_ref[...])
    memb = jnp.concatenate(member + [jnp.zeros((ROUTE_ROWS - N_GROUPS, tm), F32)], axis=0)
    memb_ref[...] = memb
    pos_ref[...] = _dot(memb.astype(BF16), before_ref[...])
    gate_rows = jnp.concatenate(gates, axis=0)
    g_hi = gate_rows.astype(BF16)
    gsplit_ref[...] = jnp.concatenate([g_hi, (gate_rows - g_hi.astype(F32)).astype(BF16)], axis=0)

    def expert_chunk(g, ci):
        slot = (lax.broadcasted_iota(jnp.int32, (rows, tm), 0) + ci * rows).astype(F32)
        p = jnp.where(pos_ref[g:g + 1, :] == slot, memb_ref[g:g + 1, :], 0.0).astype(BF16)
        xg = _dot(p, xb_ref[...]).astype(BF16)
        gg = _dot_nt(p, gsplit_ref[...])
        hs = []
        for j in range(EXPERTS_PER_GROUP):
            e = g * EXPERTS_PER_GROUP + j
            gate = gg[:, e:e + 1] + gg[:, N_EXPERTS + e:N_EXPERTS + e + 1]
            hg = _dot(xg, wg_ref[e])
            hu = _dot(xg, wu_ref[e])
            hs.append((jax.nn.silu(hg) * hu * gate).astype(BF16))
        y = _dot(jnp.concatenate(hs, axis=1), wd_ref[g * group_w:(g + 1) * group_w, :])
        return p, y.astype(BF16)

    first = [expert_chunk(g, 0) for g in range(N_GROUPS)]
    acc_ref[...] = _dot_tn(jnp.concatenate([p for p, _ in first], axis=0),
                           jnp.concatenate([y for _, y in first], axis=0))
    largest = jnp.max(jnp.sum(memb_ref[0:SUBLANES, :], axis=1, keepdims=True))

    @pl.when(largest > rows)
    def _():
        for g in range(N_GROUPS):
            n_tok = jnp.sum(memb_ref[g:g + 1, :]).astype(jnp.int32)

            def extra(ci, carry, g=g):
                p, y = expert_chunk(g, ci)
                acc_ref[...] += _dot_tn(p, y)
                return carry

            lax.fori_loop(1, (n_tok + rows - 1) // rows, extra, 0)

    o_ref[0] = _layer_norm(alpha * x_ref[0] + acc_ref[...], g3_ref[...], b3_ref[...])


def _moe(x, logits_t, rb_t, before, wg, wu, wd, g3, b3, *, layer, tm, alpha):
    bsz, seq, d = x.shape
    tok = lambda w: pl.BlockSpec((1, tm, w), lambda b, s: (b, s, 0))
    resident = lambda w: _layer_spec(w, layer, single_buffer=True)
    return pl.pallas_call(
        functools.partial(_moe_kernel, alpha=alpha, tm=tm, rows=tm // N_GROUPS * 5 // 4),
        grid=(bsz, seq // tm),
        in_specs=[tok(d), pl.BlockSpec((1, N_EXPERTS, tm), lambda b, s: (b, 0, s)),
                  _const_spec(rb_t.shape), _const_spec(before.shape),
                  resident(wg), resident(wu), resident(wd),
                  _layer_spec(g3, layer), _layer_spec(b3, layer)],
        out_specs=tok(d),
        out_shape=jax.ShapeDtypeStruct((bsz, seq, d), F32),
        scratch_shapes=[pltpu.VMEM((tm, d), BF16),
                        pltpu.VMEM((ROUTE_ROWS, tm), F32),
                        pltpu.VMEM((ROUTE_ROWS, tm), F32),
                        pltpu.VMEM((2 * N_EXPERTS, tm), BF16),
                        pltpu.VMEM((tm, d), F32)],
        compiler_params=_params(2),
        name="moe",
    )(x, logits_t, rb_t, before, wg, wu, wd, g3, b3)


def _rot_cols(w):
    half = w.shape[-1] // 2
    return jnp.concatenate([-w[..., half:], w[..., :half]], axis=-1)


def _prep_w_in(w_in):
    k_rope = 2 * A_WIDTH + Q_LORA + KV_LORA
    head = w_in[..., :k_rope]
    kr = w_in[..., k_rope:k_rope + MLA_ROPE]
    rest = w_in[..., k_rope + MLA_ROPE:]
    rope4 = jnp.concatenate([kr, _rot_cols(kr), kr, _rot_cols(kr)], axis=-1)
    return jnp.concatenate([head, rope4, rest], axis=-1).astype(BF16)


def _prep_w_uq(w_uq):
    depth = w_uq.shape[0]
    w = w_uq.reshape(depth, Q_LORA, MLA_HEADS, MLA_NOPE + MLA_ROPE)
    nope, rope = w[..., :MLA_NOPE], w[..., MLA_NOPE:]
    ext = jnp.concatenate([nope, rope, _rot_cols(rope)], axis=-1)
    return ext.reshape(depth, Q_LORA, MLA_HEADS * QK_DIM).astype(BF16)


def _prep_w_ukv(w_ukv):
    depth = w_ukv.shape[0]
    w = w_ukv.reshape(depth, KV_LORA, MLA_HEADS, MLA_NOPE + MLA_V)
    k_nope, v = w[..., :MLA_NOPE], w[..., MLA_NOPE:]
    wk = jnp.concatenate([k_nope, jnp.zeros_like(k_nope)], axis=-1)
    return (wk.reshape(depth, KV_LORA, MLA_HEADS * QK_DIM).astype(BF16),
            v.reshape(depth, KV_LORA, MLA_WIDTH).astype(BF16))


def _prep_pool_w(pool_w):
    depth, groups = pool_w.shape[:2]
    eye = jnp.eye(groups, dtype=pool_w.dtype)
    bd = jnp.einsum('lgcd,gh->lgchd', pool_w, eye)
    return bd.reshape(depth, D_WIDTH, D_WIDTH).astype(BF16)


def kernel(x, mem, positions, w_in, gmlp_v_norm_g, gmlp_w_s, gmlp_b_s, mla_q_norm_g, mla_w_uq,
           mla_kv_norm_g, mla_w_ukv, conv_w, pool_w, pool_scale, w_out, ln1_g, ln1_b,
           xattn_wq, xattn_wk, xattn_wv, xattn_wo, ln2_g, ln2_b, router_w, router_b,
           moe_w_gate, moe_w_up, moe_w_down, ln3_g, ln3_b):
    depth = w_in.shape[0]
    alpha = (2 * depth) ** 0.25
    tm = 512
    bsz, seq, _ = x.shape

    inv_freq = ROPE_BASE ** (-jnp.arange(0, MLA_ROPE, 2, dtype=F32) / MLA_ROPE)
    ang = positions.astype(F32)[..., None] * inv_freq
    cos2 = jnp.tile(jnp.cos(ang), (1, 1, 2))
    sin2 = jnp.tile(jnp.sin(ang), (1, 1, 2))
    scale = (MLA_NOPE + MLA_ROPE) ** -0.5 * LOG2_E
    csq = scale * jnp.concatenate([jnp.ones(cos2.shape[:2] + (MLA_NOPE,), F32), cos2, sin2], -1)
    csq = jnp.swapaxes(csq.reshape(bsz, seq // tm, tm, QK_DIM), 2, 3)
    csk = jnp.concatenate([cos2, sin2, cos2, sin2], axis=-1)

    row = lambda a: a[:, None, :]
    w_in_e = _prep_w_in(w_in)
    w_uq_e = jnp.swapaxes(_prep_w_uq(mla_w_uq), 1, 2)
    w_k_e, w_v_e = _prep_w_ukv(mla_w_ukv)
    w_v_e = jnp.swapaxes(w_v_e, 1, 2)
    ws = gmlp_w_s.astype(BF16)
    bs = jnp.repeat(jnp.swapaxes(gmlp_b_s, 1, 2), A_HEAD_DIM, axis=2)
    pool_bd = _prep_pool_w(pool_w)
    wo_acd = jnp.concatenate([w_out[:, :A_WIDTH], w_out[:, A_WIDTH + MLA_WIDTH:]], axis=1).astype(BF16)
    wo_b = w_out[:, A_WIDTH:A_WIDTH + MLA_WIDTH].astype(BF16)
    wq = (xattn_wq * (X_HEAD_DIM ** -0.5 * LOG2_E)).astype(BF16)
    wo = xattn_wo.astype(BF16)
    rw_t = router_w.T
    rw_hi = rw_t.astype(BF16)
    rw = jnp.concatenate([rw_hi, (rw_t - rw_hi.astype(F32)).astype(BF16)], axis=0)
    rb_t = jnp.broadcast_to(router_b[:, None], (N_EXPERTS, tm))
    before = jnp.triu(jnp.ones((tm, tm), BF16), 1)
    wg = moe_w_gate.astype(BF16)
    wu = moe_w_up.astype(BF16)
    wd = moe_w_down.reshape(depth, N_EXPERTS * D_FF, D_MODEL).astype(BF16)

    km, vm = _mem_kv(mem, xattn_wk.astype(BF16), xattn_wv.astype(BF16))

    for l in range(depth):
        q, k, v, yacd = _mixer_in(
            x, csq, csk, w_in_e, row(gmlp_v_norm_g), ws, bs, row(mla_q_norm_g),
            w_uq_e, row(mla_kv_norm_g), w_k_e, w_v_e, conv_w, pool_bd,
            row(pool_scale), layer=l, tm=tm)
        yb = _mla_attn(q, k, v, hg=MLA_HEADS)
        x2, logits = _post_attn(
            yacd, yb, x, wo_acd, wo_b, row(ln1_g), row(ln1_b), wq, km, vm,
            wo, row(ln2_g), row(ln2_b), rw, layer=l, tm=tm, alpha=alpha)
        x = _moe(x2, logits, rb_t, before, wg, wu, wd, row(ln3_g), row(ln3_b),
                 layer=l, tm=tm, alpha=alpha)
    return x
```

```python
import functools

import jax
import jax.numpy as jnp
from jax import lax
from jax.experimental import pallas as pl
from jax.experimental.pallas import tpu as pltpu

F32 = jnp.float32
BF16 = jnp.bfloat16

D_MODEL = 1024
A_HEADS, A_HEAD_DIM, CHUNK = 4, 64, 128
A_WIDTH = A_HEADS * A_HEAD_DIM
MLA_HEADS, MLA_NOPE, MLA_ROPE, MLA_V = 8, 64, 32, 64
Q_LORA, KV_LORA = 256, 128
MLA_WIDTH = MLA_HEADS * MLA_V
ROPE_BASE = 10000.0
C_WIDTH, CONV_W = 256, 3
D_WIDTH = 256
POOL_WINDOWS = (2, 4, 8, 16)
D_GROUP = D_WIDTH // len(POOL_WINDOWS)
X_HEADS = 4
X_HEAD_DIM = D_MODEL // X_HEADS
N_EXPERTS, N_GROUPS, TOP_K = 16, 4, 2
EXPERTS_PER_GROUP = N_EXPERTS // N_GROUPS
D_FF = 256
EPS = 1e-6

LANES = 128
SUBLANES = 8
VMEM_LIMIT = 56 * 1024 * 1024

OFF_U = 0
OFF_CQ = OFF_U + 2 * A_WIDTH
OFF_BG = OFF_CQ + Q_LORA + KV_LORA + 4 * MLA_ROPE
P_EXT = OFF_BG + 3 * C_WIDTH + D_WIDTH
QK_DIM = 128
V_ROWS = 80
POOL_HALO = 16
ROUTE_ROWS = 16
NEG_BIG = -1e30
LOG2_E = 1.4426950408889634


def _dot(a, b):
    return jnp.dot(a, b, preferred_element_type=F32)


def _dot_nt(a, b):
    return lax.dot_general(a, b, (((1,), (1,)), ((), ())), preferred_element_type=F32)


def _rms(x, g):
    return x * lax.rsqrt(jnp.mean(x * x, axis=-1, keepdims=True) + EPS) * g


def _layer_norm(x, g, b):
    mu = jnp.mean(x, axis=-1, keepdims=True)
    xc = x - mu
    var = jnp.mean(xc * xc, axis=-1, keepdims=True)
    return xc * lax.rsqrt(var + EPS) * g + b


def _params(n_grid):
    return pltpu.CompilerParams(dimension_semantics=("arbitrary",) * n_grid,
                                vmem_limit_bytes=VMEM_LIMIT)


def _run_interleaved(chains):
    live = list(chains)
    while live:
        for c in list(live):
            try:
                next(c)
            except StopIteration:
                live.remove(c)


def _const_spec(shape):
    zeros = (0,) * len(shape)
    return pl.BlockSpec(shape, lambda *_: zeros)


def _layer_spec(stacked, layer, single_buffer=False):
    index = (layer,) + (0,) * (stacked.ndim - 1)
    mode = dict(pipeline_mode=pl.Buffered(1)) if single_buffer else {}
    return pl.BlockSpec((None,) + stacked.shape[1:], lambda *_: index, **mode)


def _mem_kv_kernel(mem_ref, wk_ref, wv_ref, k_ref, v_ref):
    m = mem_ref[0].astype(BF16)
    k_ref[0, 0] = _dot(m, wk_ref[0]).astype(BF16)
    v_ref[0, 0] = _dot(m, wv_ref[0]).astype(BF16)


def _mem_kv(mem, wk, wv):
    depth = wk.shape[0]
    bsz, mlen, d = mem.shape
    out = jax.ShapeDtypeStruct((depth, bsz, mlen, d), BF16)
    return pl.pallas_call(
        _mem_kv_kernel,
        grid=(depth, bsz),
        in_specs=[pl.BlockSpec((1, mlen, d), lambda l, b: (b, 0, 0)),
                  pl.BlockSpec((1, d, d), lambda l, b: (l, 0, 0)),
                  pl.BlockSpec((1, d, d), lambda l, b: (l, 0, 0))],
        out_specs=[pl.BlockSpec((1, 1, mlen, d), lambda l, b: (l, b, 0, 0)),
                   pl.BlockSpec((1, 1, mlen, d), lambda l, b: (l, b, 0, 0))],
        out_shape=[out, out],
        compiler_params=_params(2),
        name="mem_kv",
    )(mem, wk, wv)


def _mixer_in_kernel(x_ref, csq_ref, csk_ref, w_in_ref, vng_ref, ws_ref, bs_ref, qng_ref,
                     wuq_ref, kvng_ref, wk_ref, wv_ref, convw_ref, poolw_ref, pscale_ref,
                     q_ref, k_ref, v_ref, y_ref,
                     conv_buf, pool_a, pool_b, pool_carry, *, tm):
    si = pl.program_id(1)

    @pl.when(si == 0)
    def _():
        pool_carry[...] = jnp.zeros((POOL_HALO, D_WIDTH), F32)
        conv_buf[0:SUBLANES, :] = jnp.zeros((SUBLANES, C_WIDTH), F32)

    xb = x_ref[0].astype(BF16)

    za = _dot(xb, w_in_ref[:, OFF_U:OFF_CQ])
    zb = _dot(xb, w_in_ref[:, OFF_CQ:OFF_BG])
    zcd = _dot(xb, w_in_ref[:, OFF_BG:P_EXT])

    za = jax.nn.gelu(za)
    u = za[:, :A_WIDTH]
    v = _rms(za[:, A_WIDTH:], vng_ref[...]).astype(BF16)
    cq = _rms(zb[:, :Q_LORA], qng_ref[...]).astype(BF16)
    ckv = _rms(zb[:, Q_LORA:Q_LORA + KV_LORA], kvng_ref[...]).astype(BF16)

    row = lax.broadcasted_iota(jnp.int32, (CHUNK, CHUNK), 0)
    col = lax.broadcasted_iota(jnp.int32, (CHUNK, CHUNK), 1)
    w_causal = jnp.concatenate(
        [jnp.where(row >= col, ws_ref[h], jnp.zeros((), BF16)) for h in range(A_HEADS)], axis=1)
    lane_head = lax.broadcasted_iota(jnp.int32, (CHUNK, A_WIDTH), 1) // A_HEAD_DIM
    for c in range(tm // CHUNK):
        rows = slice(c * CHUNK, (c + 1) * CHUNK)
        vc = v[rows]
        v_heads = jnp.concatenate(
            [jnp.where(lane_head == h, vc, jnp.zeros((), BF16)) for h in range(A_HEADS)], axis=0)
        mixed = _dot(w_causal, v_heads)
        y_ref[0, rows, 0:A_WIDTH] = (u[rows] * (mixed + bs_ref[...])).astype(BF16)

    qt = _dot_nt(wuq_ref[...], cq)
    kk = _dot(ckv, wk_ref[...])
    vt = _dot_nt(wv_ref[...], ckv)

    zd = zcd[:, 3 * C_WIDTH:]
    base = SUBLANES
    n = tm + POOL_HALO
    pool_a[0:base, :] = jnp.zeros((base, D_WIDTH), F32)
    pool_b[0:base, :] = jnp.zeros((base, D_WIDTH), F32)
    pool_a[base:base + POOL_HALO, :] = pool_carry[...]
    pool_a[base + POOL_HALO:base + n, :] = zd
    pool_carry[...] = zd[tm - POOL_HALO:, :]
    lane_d = lax.broadcasted_iota(jnp.int32, (n, D_WIDTH), 1)
    pool_b[base:base + n, :] = pool_a[base:base + n, :] + pool_a[base - 1:base - 1 + n, :]
    pool_a[base:base + n, :] = pool_b[base:base + n, :] + jnp.where(
        lane_d >= D_GROUP, pool_b[base - 2:base - 2 + n, :], 0.0)
    pool_b[base:base + n, :] = pool_a[base:base + n, :] + jnp.where(
        lane_d >= 2 * D_GROUP, pool_a[base - 4:base - 4 + n, :], 0.0)
    t0 = base + POOL_HALO
    lane_t = lax.broadcasted_iota(jnp.int32, (tm, D_WIDTH), 1)
    win_sum = pool_b[t0:t0 + tm, :] + jnp.where(
        lane_t >= 3 * D_GROUP, pool_b[t0 - 8:t0 - 8 + tm, :], 0.0)
    pos1 = si * tm + lax.broadcasted_iota(jnp.int32, (tm, D_WIDTH), 0) + 1
    window = jnp.left_shift(2, lane_t // D_GROUP)
    count = jnp.minimum(pos1, window).astype(F32)
    pooled = (win_sum / count - zd).astype(BF16)
    yd = _dot(pooled, poolw_ref[...]) * pscale_ref[...]
    y_ref[0, :, A_WIDTH + C_WIDTH:] = yd.astype(BF16)

    gh = zcd[:, C_WIDTH:2 * C_WIDTH] * zcd[:, 2 * C_WIDTH:3 * C_WIDTH]
    conv_buf[SUBLANES:SUBLANES + tm, :] = gh
    conv = (convw_ref[2:3, :] * gh
            + convw_ref[1:2, :] * conv_buf[SUBLANES - 1:SUBLANES - 1 + tm, :]
            + convw_ref[0:1, :] * conv_buf[SUBLANES - 2:SUBLANES - 2 + tm, :])
    y_ref[0, :, A_WIDTH:A_WIDTH + C_WIDTH] = (zcd[:, :C_WIDTH] * conv).astype(BF16)
    conv_buf[0:SUBLANES, :] = conv_buf[tm:tm + SUBLANES, :]

    csq = csq_ref[0, 0]
    for h in range(MLA_HEADS):
        q_ref[0, h, 0] = (qt[h * QK_DIM:(h + 1) * QK_DIM, :] * csq).astype(BF16)
    r4 = zb[:, Q_LORA + KV_LORA:] * csk_ref[0]
    kr = r4 + pltpu.roll(r4, MLA_ROPE, 1)
    lane = lax.broadcasted_iota(jnp.int32, (tm, QK_DIM), 1)
    kr = jnp.where(lane >= MLA_NOPE, kr, 0.0)
    for h in range(MLA_HEADS):
        k_ref[0, h] = (kk[:, h * QK_DIM:(h + 1) * QK_DIM] + kr).astype(BF16)
    ones_then_zeros = jnp.where(
        lax.broadcasted_iota(jnp.int32, (V_ROWS - MLA_V, tm), 0) == 0, 1.0, 0.0).astype(BF16)
    for h in range(MLA_HEADS):
        v_ref[0, h, 0, 0:MLA_V, :] = vt[h * MLA_V:(h + 1) * MLA_V, :].astype(BF16)
        v_ref[0, h, 0, MLA_V:, :] = ones_then_zeros


def _mixer_in(x, csq, csk, w_in, vng, ws, bs, qng, wuq, kvng, wk, wv, convw, poolw, pscale,
              *, layer, tm):
    bsz, seq, d = x.shape
    tok = lambda w: pl.BlockSpec((1, tm, w), lambda b, s: (b, s, 0))
    head = lambda n, w: pl.BlockSpec((1, n, tm, w), lambda b, s: (b, 0, s, 0))
    head_t = lambda r: pl.BlockSpec((1, MLA_HEADS, 1, r, tm), lambda b, s: (b, 0, s, 0, 0))
    shape_t = lambda r: jax.ShapeDtypeStruct((bsz, MLA_HEADS, seq // tm, r, tm), BF16)
    return pl.pallas_call(
        functools.partial(_mixer_in_kernel, tm=tm),
        grid=(bsz, seq // tm),
        in_specs=[tok(d), pl.BlockSpec((1, 1, QK_DIM, tm), lambda b, s: (b, s, 0, 0)), tok(QK_DIM),
                  *[_layer_spec(w, layer) for w in (w_in, vng, ws, bs, qng, wuq, kvng, wk, wv,
                                                     convw, poolw, pscale)]],
        out_specs=[head_t(QK_DIM), head(MLA_HEADS, QK_DIM), head_t(V_ROWS),
                   tok(A_WIDTH + C_WIDTH + D_WIDTH)],
        out_shape=[shape_t(QK_DIM),
                   jax.ShapeDtypeStruct((bsz, MLA_HEADS, seq, QK_DIM), BF16),
                   shape_t(V_ROWS),
                   jax.ShapeDtypeStruct((bsz, seq, A_WIDTH + C_WIDTH + D_WIDTH), BF16)],
        scratch_shapes=[pltpu.VMEM((tm + SUBLANES, C_WIDTH), F32),
                        pltpu.VMEM((tm + POOL_HALO + SUBLANES, D_WIDTH), F32),
                        pltpu.VMEM((tm + POOL_HALO + SUBLANES, D_WIDTH), F32),
                        pltpu.VMEM((POOL_HALO, D_WIDTH), F32)],
        compiler_params=_params(2),
        name="mixer_in",
    )(x, csq, csk, w_in, vng, ws, bs, qng, wuq, kvng, wk, wv, convw, poolw, pscale)


def _mla_attn_kernel(qt_ref, k_ref, vt_ref, o_ref, m_ref, acc_ref, *, tq, hg):
    qi = pl.program_id(2)
    half = tq // 2
    m_ref[...] = jnp.full(m_ref.shape, NEG_BIG, F32)
    acc_ref[...] = jnp.zeros(acc_ref.shape, F32)

    def run_blocks(blocks):
        pieces = []
        for j, diagonal in blocks:
            pieces += [(j, diagonal, hh, 0, 0) for hh in range(hg)]
            pieces += [(j, diagonal, hh, half, half if diagonal else 0) for hh in range(hg)]

        def scores(piece):
            j, _, hh, k0, q0 = piece
            base = pl.multiple_of(j * tq, tq)
            return _dot(k_ref[0, hh, pl.ds(base + k0, half), :], qt_ref[0, hh, 0, :, q0:])

        def softmax(piece, st):
            _, diagonal, hh, _, q0 = piece
            if diagonal:
                key = lax.broadcasted_iota(jnp.int32, st.shape, 0)
                qry = lax.broadcasted_iota(jnp.int32, st.shape, 1)
                st = jnp.where(key <= qry, st, NEG_BIG)
            m_old = m_ref[hh, :, q0:]
            m_new = jnp.maximum(m_old, jnp.max(st, axis=0, keepdims=True))
            m_ref[hh, :, q0:] = m_new
            return piece, jnp.exp2(st - m_new).astype(BF16), jnp.exp2(m_old - m_new)

        def weighted_values(piece, probs, rescale):
            j, _, hh, k0, q0 = piece
            pv = _dot(vt_ref[0, hh, j, :, k0:k0 + half], probs)
            acc_ref[hh, :, q0:] = rescale * acc_ref[hh, :, q0:] + pv

        st_next = scores(pieces[0])
        pending = None
        for i, piece in enumerate(pieces):
            st = st_next
            if i + 1 < len(pieces):
                st_next = scores(pieces[i + 1])
            if pending is not None:
                weighted_values(*pending)
            pending = softmax(piece, st)
        weighted_values(*pending)

    def pair(jj, carry):
        run_blocks([(2 * jj, False), (2 * jj + 1, False)])
        return carry

    lax.fori_loop(0, qi // 2, pair, 0)

    @pl.when(qi % 2 == 1)
    def _():
        run_blocks([(qi - 1, False), (qi, True)])

    @pl.when(qi % 2 == 0)
    def _():
        run_blocks([(qi, True)])

    for pr in range(hg // 2):
        halves = []
        for hh in (2 * pr, 2 * pr + 1):
            acc = acc_ref[hh]
            halves.append(acc[:MLA_V] / acc[MLA_V:MLA_V + 1])
        o_ref[0, :, pr * LANES:(pr + 1) * LANES] = jnp.concatenate(halves, axis=0).T.astype(BF16)


def _mla_attn(qt, k, vt, *, hg):
    bsz, heads, n_tiles, _, tq = qt.shape
    seq = n_tiles * tq
    return pl.pallas_call(
        functools.partial(_mla_attn_kernel, tq=tq, hg=hg),
        grid=(bsz, heads // hg, n_tiles),
        in_specs=[pl.BlockSpec((1, hg, 1, QK_DIM, tq), lambda b, g, i: (b, g, i, 0, 0)),
                  pl.BlockSpec((1, hg, seq, QK_DIM), lambda b, g, i: (b, g, 0, 0)),
                  pl.BlockSpec((1, hg, n_tiles, V_ROWS, tq), lambda b, g, i: (b, g, 0, 0, 0))],
        out_specs=pl.BlockSpec((1, tq, hg * MLA_V), lambda b, g, i: (b, i, g)),
        out_shape=jax.ShapeDtypeStruct((bsz, seq, MLA_WIDTH), BF16),
        scratch_shapes=[pltpu.VMEM((hg, 1, tq), F32), pltpu.VMEM((hg, V_ROWS, tq), F32)],
        compiler_params=_params(3),
        name="mla_attn",
    )(qt, k, vt)


def _route_t(logits_t, bias_t):
    scores = jax.nn.sigmoid(logits_t)
    sel_all = scores + bias_t
    sel = [sel_all[e:e + 1] for e in range(N_EXPERTS)]
    group_score = []
    for g in range(N_GROUPS):
        v = sel[g * EXPERTS_PER_GROUP:(g + 1) * EXPERTS_PER_GROUP]
        best = None
        for i in range(EXPERTS_PER_GROUP):
            for j in range(i + 1, EXPERTS_PER_GROUP):
                best = v[i] + v[j] if best is None else jnp.maximum(best, v[i] + v[j])
        group_score.append(best)
    top, top_idx = group_score[0], jnp.zeros(group_score[0].shape, jnp.int32)
    for g in range(1, N_GROUPS):
        better = group_score[g] > top
        top = jnp.where(better, group_score[g], top)
        top_idx = jnp.where(better, g, top_idx)
    member = [jnp.where(top_idx == g, 1.0, 0.0) for g in range(N_GROUPS)]
    weights = []
    for e in range(N_EXPERTS):
        g = e // EXPERTS_PER_GROUP
        rank = jnp.zeros(sel[e].shape, jnp.int32)
        for j in range(g * EXPERTS_PER_GROUP, (g + 1) * EXPERTS_PER_GROUP):
            if j != e:
                rank = rank + jnp.where(sel[j] > sel[e], 1,
                                        jnp.where(sel[j] == sel[e], 1 if j < e else 0, 0))
        weights.append(jnp.where(rank < TOP_K, member[g], 0.0) * scores[e:e + 1])
    total = weights[0]
    for w in weights[1:]:
        total = total + w
    return [w / total for w in weights], member


def _post_attn_kernel(yacd_ref, yb_ref, x_ref, wo_acd_ref, wo_b_ref, g1_ref, b1_ref, wq_ref,
                      km_ref, vm_ref, wo_ref, g2_ref, b2_ref, rw_ref,
                      x2_ref, logits_ref, *, tm, alpha, n_chains):
    def chain(rows):
        h = _dot(yacd_ref[0, rows], wo_acd_ref[...]) + _dot(yb_ref[0, rows], wo_b_ref[...])
        yield
        x1 = _layer_norm(alpha * x_ref[0, rows] + h, g1_ref[...], b1_ref[...])
        q = _dot(x1.astype(BF16), wq_ref[...])
        yield
        heads = []
        for hd in range(X_HEADS):
            cols = slice(hd * X_HEAD_DIM, (hd + 1) * X_HEAD_DIM)
            s = _dot_nt(q[:, cols].astype(BF16), km_ref[0, :, cols])
            e = jnp.exp2(s - jnp.max(s, axis=-1, keepdims=True))
            o = _dot(e.astype(BF16), vm_ref[0, :, cols]) / jnp.sum(e, axis=-1, keepdims=True)
            heads.append(o.astype(BF16))
            yield
        h2 = _dot(jnp.concatenate(heads, axis=1), wo_ref[...])
        yield
        x2 = _layer_norm(alpha * x1 + h2, g2_ref[...], b2_ref[...])
        x2_ref[0, rows] = x2
        x_hi = x2.astype(BF16)
        x_lo = (x2 - x_hi.astype(F32)).astype(BF16)
        hl = _dot_nt(rw_ref[...], x_hi)
        lo = _dot_nt(rw_ref[0:N_EXPERTS, :], x_lo)
        logits_ref[0, :, rows] = hl[:N_EXPERTS] + hl[N_EXPERTS:] + lo

    rows_per = tm // n_chains
    _run_interleaved([chain(slice(c * rows_per, (c + 1) * rows_per)) for c in range(n_chains)])


def _post_attn(yacd, yb, x, wo_acd, wo_b, g1, b1, wq, km, vm, wo, g2, b2, rw,
               *, layer, tm, alpha):
    bsz, seq, d = x.shape
    tok = lambda w: pl.BlockSpec((1, tm, w), lambda b, s: (b, s, 0))
    memspec = pl.BlockSpec((None, 1) + km.shape[2:], lambda b, s: (layer, b, 0, 0))
    per_layer = lambda w: _layer_spec(w, layer)
    return pl.pallas_call(
        functools.partial(_post_attn_kernel, tm=tm, alpha=alpha, n_chains=2),
        grid=(bsz, seq // tm),
        in_specs=[tok(yacd.shape[-1]), tok(yb.shape[-1]), tok(d),
                  per_layer(wo_acd), per_layer(wo_b), per_layer(g1), per_layer(b1), per_layer(wq),
                  memspec, memspec, per_layer(wo), per_layer(g2), per_layer(b2),
                  _const_spec(rw.shape)],
        out_specs=[tok(d), pl.BlockSpec((1, N_EXPERTS, tm), lambda b, s: (b, 0, s))],
        out_shape=[jax.ShapeDtypeStruct((bsz, seq, d), F32),
                   jax.ShapeDtypeStruct((bsz, N_EXPERTS, seq), F32)],
        compiler_params=_params(2),
        name="post_attn",
    )(yacd, yb, x, wo_acd, wo_b, g1, b1, wq, km, vm, wo, g2, b2, rw)


def _dot_tn(a, b):
    return lax.dot_general(a, b, (((0,), (0,)), ((), ())), preferred_element_type=F32)


def _moe_kernel(x_ref, logits_ref, rb_ref, before_ref, wg_ref, wu_ref, wd_ref, g3_ref, b3_ref, o_ref,
                xb_ref, pos_ref, memb_ref, gsplit_ref, acc_ref, *, alpha, tm, rows):
    group_w = EXPERTS_PER_GROUP * D_FF
    x = x_ref[0]
    xb_ref[...] = x.astype(BF16)
    gates, member = _route_t(logits_ref[0], rb_ref[...])
    memb = jnp.concatenate(member + [jnp.zeros((ROUTE_ROWS - N_GROUPS, tm), F32)], axis=0)
    memb_ref[...] = memb
    pos_ref[...] = _dot(memb.astype(BF16), before_ref[...])
    gate_rows = jnp.concatenate(gates, axis=0)
    g_hi = gate_rows.astype(BF16)
    gsplit_ref[...] = jnp.concatenate([g_hi, (gate_rows - g_hi.astype(F32)).astype(BF16)], axis=0)

    def expert_chunk(g, ci):
        slot = (lax.broadcasted_iota(jnp.int32, (rows, tm), 0) + ci * rows).astype(F32)
        p = jnp.where(pos_ref[g:g + 1, :] == slot, memb_ref[g:g + 1, :], 0.0).astype(BF16)
        xg = _dot(p, xb_ref[...]).astype(BF16)
        gg = _dot_nt(p, gsplit_ref[...])
        hs = []
        for j in range(EXPERTS_PER_GROUP):
            e = g * EXPERTS_PER_GROUP + j
            gate = gg[:, e:e + 1] + gg[:, N_EXPERTS + e:N_EXPERTS + e + 1]
            hg = _dot(xg, wg_ref[e])
            hu = _dot(xg, wu_ref[e])
            hs.append((jax.nn.silu(hg) * hu * gate).astype(BF16))
        y = _dot(jnp.concatenate(hs, axis=1), wd_ref[g * group_w:(g + 1) * group_w, :])
        return p, y.astype(BF16)

    first = [expert_chunk(g, 0) for g in range(N_GROUPS)]
    acc_ref[...] = _dot_tn(jnp.concatenate([p for p, _ in first], axis=0),
                           jnp.concatenate([y for _, y in first], axis=0))
    largest = jnp.max(jnp.sum(memb_ref[0:SUBLANES, :], axis=1, keepdims=True))

    @pl.when(largest > rows)
    def _():
        for g in range(N_GROUPS):
            n_tok = jnp.sum(memb_ref[g:g + 1, :]).astype(jnp.int32)

            def extra(ci, carry, g=g):
                p, y = expert_chunk(g, ci)
                acc_ref[...] += _dot_tn(p, y)
                return carry

            lax.fori_loop(1, (n_tok + rows - 1) // rows, extra, 0)

    o_ref[0] = _layer_norm(alpha * x_ref[0] + acc_ref[...], g3_ref[...], b3_ref[...])


def _moe(x, logits_t, rb_t, before, wg, wu, wd, g3, b3, *, layer, tm, alpha):
    bsz, seq, d = x.shape
    tok = lambda w: pl.BlockSpec((1, tm, w), lambda b, s: (b, s, 0))
    resident = lambda w: _layer_spec(w, layer, single_buffer=True)
    return pl.pallas_call(
        functools.partial(_moe_kernel, alpha=alpha, tm=tm, rows=tm // N_GROUPS * 5 // 4),
        grid=(bsz, seq // tm),
        in_specs=[tok(d), pl.BlockSpec((1, N_EXPERTS, tm), lambda b, s: (b, 0, s)),
                  _const_spec(rb_t.shape), _const_spec(before.shape),
                  resident(wg), resident(wu), resident(wd),
                  _layer_spec(g3, layer), _layer_spec(b3, layer)],
        out_specs=tok(d),
        out_shape=jax.ShapeDtypeStruct((bsz, seq, d), F32),
        scratch_shapes=[pltpu.VMEM((tm, d), BF16),
                        pltpu.VMEM((ROUTE_ROWS, tm), F32),
                        pltpu.VMEM((ROUTE_ROWS, tm), F32),
                        pltpu.VMEM((2 * N_EXPERTS, tm), BF16),
                        pltpu.VMEM((tm, d), F32)],
        compiler_params=_params(2),
        name="moe",
    )(x, logits_t, rb_t, before, wg, wu, wd, g3, b3)


def _rot_cols(w):
    half = w.shape[-1] // 2
    return jnp.concatenate([-w[..., half:], w[..., :half]], axis=-1)


def _prep_w_in(w_in):
    k_rope = 2 * A_WIDTH + Q_LORA + KV_LORA
    head = w_in[..., :k_rope]
    kr = w_in[..., k_rope:k_rope + MLA_ROPE]
    rest = w_in[..., k_rope + MLA_ROPE:]
    rope4 = jnp.concatenate([kr, _rot_cols(kr), kr, _rot_cols(kr)], axis=-1)
    return jnp.concatenate([head, rope4, rest], axis=-1).astype(BF16)


def _prep_w_uq(w_uq):
    depth = w_uq.shape[0]
    w = w_uq.reshape(depth, Q_LORA, MLA_HEADS, MLA_NOPE + MLA_ROPE)
    nope, rope = w[..., :MLA_NOPE], w[..., MLA_NOPE:]
    ext = jnp.concatenate([nope, rope, _rot_cols(rope)], axis=-1)
    return ext.reshape(depth, Q_LORA, MLA_HEADS * QK_DIM).astype(BF16)


def _prep_w_ukv(w_ukv):
    depth = w_ukv.shape[0]
    w = w_ukv.reshape(depth, KV_LORA, MLA_HEADS, MLA_NOPE + MLA_V)
    k_nope, v = w[..., :MLA_NOPE], w[..., MLA_NOPE:]
    wk = jnp.concatenate([k_nope, jnp.zeros_like(k_nope)], axis=-1)
    return (wk.reshape(depth, KV_LORA, MLA_HEADS * QK_DIM).astype(BF16),
            v.reshape(depth, KV_LORA, MLA_WIDTH).astype(BF16))


def _prep_pool_w(pool_w):
    depth, groups = pool_w.shape[:2]
    eye = jnp.eye(groups, dtype=pool_w.dtype)
    bd = jnp.einsum('lgcd,gh->lgchd', pool_w, eye)
    return bd.reshape(depth, D_WIDTH, D_WIDTH).astype(BF16)


def kernel(x, mem, positions, w_in, gmlp_v_norm_g, gmlp_w_s, gmlp_b_s, mla_q_norm_g, mla_w_uq,
           mla_kv_norm_g, mla_w_ukv, conv_w, pool_w, pool_scale, w_out, ln1_g, ln1_b,
           xattn_wq, xattn_wk, xattn_wv, xattn_wo, ln2_g, ln2_b, router_w, router_b,
           moe_w_gate, moe_w_up, moe_w_down, ln3_g, ln3_b):
    depth = w_in.shape[0]
    alpha = (2 * depth) ** 0.25
    tm = 512
    bsz, seq, _ = x.shape

    inv_freq = ROPE_BASE ** (-jnp.arange(0, MLA_ROPE, 2, dtype=F32) / MLA_ROPE)
    ang = positions.astype(F32)[..., None] * inv_freq
    cos2 = jnp.tile(jnp.cos(ang), (1, 1, 2))
    sin2 = jnp.tile(jnp.sin(ang), (1, 1, 2))
    scale = (MLA_NOPE + MLA_ROPE) ** -0.5 * LOG2_E
    csq = scale * jnp.concatenate([jnp.ones(cos2.shape[:2] + (MLA_NOPE,), F32), cos2, sin2], -1)
    csq = jnp.swapaxes(csq.reshape(bsz, seq // tm, tm, QK_DIM), 2, 3)
    csk = jnp.concatenate([cos2, sin2, cos2, sin2], axis=-1)

    row = lambda a: a[:, None, :]
    w_in_e = _prep_w_in(w_in)
    w_uq_e = jnp.swapaxes(_prep_w_uq(mla_w_uq), 1, 2)
    w_k_e, w_v_e = _prep_w_ukv(mla_w_ukv)
    w_v_e = jnp.swapaxes(w_v_e, 1, 2)
    ws = gmlp_w_s.astype(BF16)
    bs = jnp.repeat(jnp.swapaxes(gmlp_b_s, 1, 2), A_HEAD_DIM, axis=2)
    pool_bd = _prep_pool_w(pool_w)
    wo_acd = jnp.concatenate([w_out[:, :A_WIDTH], w_out[:, A_WIDTH + MLA_WIDTH:]], axis=1).astype(BF16)
    wo_b = w_out[:, A_WIDTH:A_WIDTH + MLA_WIDTH].astype(BF16)
    wq = (xattn_wq * (X_HEAD_DIM ** -0.5 * LOG2_E)).astype(BF16)
    wo = xattn_wo.astype(BF16)
    rw_t = router_w.T
    rw_hi = rw_t.astype(BF16)
    rw = jnp.concatenate([rw_hi, (rw_t - rw_hi.astype(F32)).astype(BF16)], axis=0)
    rb_t = jnp.broadcast_to(router_b[:, None], (N_EXPERTS, tm))
    before = jnp.triu(jnp.ones((tm, tm), BF16), 1)
    wg = moe_w_gate.astype(BF16)
    wu = moe_w_up.astype(BF16)
    wd = moe_w_down.reshape(depth, N_EXPERTS * D_FF, D_MODEL).astype(BF16)

    km, vm = _mem_kv(mem, xattn_wk.astype(BF16), xattn_wv.astype(BF16))

    for l in range(depth):
        q, k, v, yacd = _mixer_in(
            x, csq, csk, w_in_e, row(gmlp_v_norm_g), ws, bs, row(mla_q_norm_g),
            w_uq_e, row(mla_kv_norm_g), w_k_e, w_v_e, conv_w, pool_bd,
            row(pool_scale), layer=l, tm=tm)
        yb = _mla_attn(q, k, v, hg=MLA_HEADS)
        x2, logits = _post_attn(
            yacd, yb, x, wo_acd, wo_b, row(ln1_g), row(ln1_b), wq, km, vm,
            wo, row(ln2_g), row(ln2_b), rw, layer=l, tm=tm, alpha=alpha)
        x = _moe(x2, logits, rb_t, before, wg, wu, wd, row(ln3_g), row(ln3_b),
                 layer=l, tm=tm, alpha=alpha)
    return x
```

```python
import functools

import jax
import jax.numpy as jnp
from jax import lax
from jax.experimental import pallas as pl
from jax.experimental.pallas import tpu as pltpu

F32 = jnp.float32
BF16 = jnp.bfloat16

D_MODEL = 1024
A_HEADS, A_HEAD_DIM, CHUNK = 4, 64, 128
A_WIDTH = A_HEADS * A_HEAD_DIM
MLA_HEADS, MLA_NOPE, MLA_ROPE, MLA_V = 8, 64, 32, 64
Q_LORA, KV_LORA = 256, 128
MLA_WIDTH = MLA_HEADS * MLA_V
ROPE_BASE = 10000.0
C_WIDTH, CONV_W = 256, 3
D_WIDTH = 256
POOL_WINDOWS = (2, 4, 8, 16)
D_GROUP = D_WIDTH // len(POOL_WINDOWS)
X_HEADS = 4
X_HEAD_DIM = D_MODEL // X_HEADS
N_EXPERTS, N_GROUPS, TOP_K = 16, 4, 2
EXPERTS_PER_GROUP = N_EXPERTS // N_GROUPS
D_FF = 256
EPS = 1e-6

LANES = 128
SUBLANES = 8
VMEM_LIMIT = 56 * 1024 * 1024

OFF_U = 0
OFF_CQ = OFF_U + 2 * A_WIDTH
OFF_BG = OFF_CQ + Q_LORA + KV_LORA + 4 * MLA_ROPE
P_EXT = OFF_BG + 3 * C_WIDTH + D_WIDTH
QK_DIM = 128
V_ROWS = 80
POOL_HALO = 16
ROUTE_ROWS = 16
NEG_BIG = -1e30
LOG2_E = 1.4426950408889634


def _dot(a, b):
    return jnp.dot(a, b, preferred_element_type=F32)


def _dot_nt(a, b):
    return lax.dot_general(a, b, (((1,), (1,)), ((), ())), preferred_element_type=F32)


def _rms(x, g):
    return x * lax.rsqrt(jnp.mean(x * x, axis=-1, keepdims=True) + EPS) * g


def _layer_norm(x, g, b):
    mu = jnp.mean(x, axis=-1, keepdims=True)
    xc = x - mu
    var = jnp.mean(xc * xc, axis=-1, keepdims=True)
    return xc * lax.rsqrt(var + EPS) * g + b


def _params(n_grid):
    return pltpu.CompilerParams(dimension_semantics=("arbitrary",) * n_grid,
                                vmem_limit_bytes=VMEM_LIMIT)


def _run_interleaved(chains):
    live = list(chains)
    while live:
        for c in list(live):
            try:
                next(c)
            except StopIteration:
                live.remove(c)


def _const_spec(shape):
    zeros = (0,) * len(shape)
    return pl.BlockSpec(shape, lambda *_: zeros)


def _layer_spec(stacked, layer, single_buffer=False):
    index = (layer,) + (0,) * (stacked.ndim - 1)
    mode = dict(pipeline_mode=pl.Buffered(1)) if single_buffer else {}
    return pl.BlockSpec((None,) + stacked.shape[1:], lambda *_: index, **mode)


def _mem_kv_kernel(mem_ref, wk_ref, wv_ref, k_ref, v_ref):
    m = mem_ref[0].astype(BF16)
    k_ref[0, 0] = _dot(m, wk_ref[0]).astype(BF16)
    v_ref[0, 0] = _dot(m, wv_ref[0]).astype(BF16)


def _mem_kv(mem, wk, wv):
    depth = wk.shape[0]
    bsz, mlen, d = mem.shape
    out = jax.ShapeDtypeStruct((depth, bsz, mlen, d), BF16)
    return pl.pallas_call(
        _mem_kv_kernel,
        grid=(depth, bsz),
        in_specs=[pl.BlockSpec((1, mlen, d), lambda l, b: (b, 0, 0)),
                  pl.BlockSpec((1, d, d), lambda l, b: (l, 0, 0)),
                  pl.BlockSpec((1, d, d), lambda l, b: (l, 0, 0))],
        out_specs=[pl.BlockSpec((1, 1, mlen, d), lambda l, b: (l, b, 0, 0)),
                   pl.BlockSpec((1, 1, mlen, d), lambda l, b: (l, b, 0, 0))],
        out_shape=[out, out],
        compiler_params=_params(2),
        name="mem_kv",
    )(mem, wk, wv)


def _mixer_in_kernel(x_ref, csq_ref, csk_ref, w_in_ref, vng_ref, ws_ref, bs_ref, qng_ref,
                     wuq_ref, kvng_ref, wk_ref, wv_ref, convw_ref, poolw_ref, pscale_ref,
                     q_ref, k_ref, v_ref, y_ref,
                     conv_buf, pool_a, pool_b, pool_carry, *, tm):
    si = pl.program_id(1)

    @pl.when(si == 0)
    def _():
        pool_carry[...] = jnp.zeros((POOL_HALO, D_WIDTH), F32)
        conv_buf[0:SUBLANES, :] = jnp.zeros((SUBLANES, C_WIDTH), F32)

    xb = x_ref[0].astype(BF16)

    za = _dot(xb, w_in_ref[:, OFF_U:OFF_CQ])
    zb = _dot(xb, w_in_ref[:, OFF_CQ:OFF_BG])
    zcd = _dot(xb, w_in_ref[:, OFF_BG:P_EXT])

    za = jax.nn.gelu(za)
    u = za[:, :A_WIDTH]
    v = _rms(za[:, A_WIDTH:], vng_ref[...]).astype(BF16)
    cq = _rms(zb[:, :Q_LORA], qng_ref[...]).astype(BF16)
    ckv = _rms(zb[:, Q_LORA:Q_LORA + KV_LORA], kvng_ref[...]).astype(BF16)

    row = lax.broadcasted_iota(jnp.int32, (CHUNK, CHUNK), 0)
    col = lax.broadcasted_iota(jnp.int32, (CHUNK, CHUNK), 1)
    w_causal = jnp.concatenate(
        [jnp.where(row >= col, ws_ref[h], jnp.zeros((), BF16)) for h in range(A_HEADS)], axis=1)
    lane_head = lax.broadcasted_iota(jnp.int32, (CHUNK, A_WIDTH), 1) // A_HEAD_DIM
    for c in range(tm // CHUNK):
        rows = slice(c * CHUNK, (c + 1) * CHUNK)
        vc = v[rows]
        v_heads = jnp.concatenate(
            [jnp.where(lane_head == h, vc, jnp.zeros((), BF16)) for h in range(A_HEADS)], axis=0)
        mixed = _dot(w_causal, v_heads)
        y_ref[0, rows, 0:A_WIDTH] = (u[rows] * (mixed + bs_ref[...])).astype(BF16)

    qt = _dot_nt(wuq_ref[...], cq)
    kk = _dot(ckv, wk_ref[...])
    vt = _dot_nt(wv_ref[...], ckv)

    zd = zcd[:, 3 * C_WIDTH:]
    base = SUBLANES
    n = tm + POOL_HALO
    pool_a[0:base, :] = jnp.zeros((base, D_WIDTH), F32)
    pool_b[0:base, :] = jnp.zeros((base, D_WIDTH), F32)
    pool_a[base:base + POOL_HALO, :] = pool_carry[...]
    pool_a[base + POOL_HALO:base + n, :] = zd
    pool_carry[...] = zd[tm - POOL_HALO:, :]
    lane_d = lax.broadcasted_iota(jnp.int32, (n, D_WIDTH), 1)
    pool_b[base:base + n, :] = pool_a[base:base + n, :] + pool_a[base - 1:base - 1 + n, :]
    pool_a[base:base + n, :] = pool_b[base:base + n, :] + jnp.where(
        lane_d >= D_GROUP, pool_b[base - 2:base - 2 + n, :], 0.0)
    pool_b[base:base + n, :] = pool_a[base:base + n, :] + jnp.where(
        lane_d >= 2 * D_GROUP, pool_a[base - 4:base - 4 + n, :], 0.0)
    t0 = base + POOL_HALO
    lane_t = lax.broadcasted_iota(jnp.int32, (tm, D_WIDTH), 1)
    win_sum = pool_b[t0:t0 + tm, :] + jnp.where(
        lane_t >= 3 * D_GROUP, pool_b[t0 - 8:t0 - 8 + tm, :], 0.0)
    pos1 = si * tm + lax.broadcasted_iota(jnp.int32, (tm, D_WIDTH), 0) + 1
    window = jnp.left_shift(2, lane_t // D_GROUP)
    count = jnp.minimum(pos1, window).astype(F32)
    pooled = (win_sum / count - zd).astype(BF16)
    yd = _dot(pooled, poolw_ref[...]) * pscale_ref[...]
    y_ref[0, :, A_WIDTH + C_WIDTH:] = yd.astype(BF16)

    gh = zcd[:, C_WIDTH:2 * C_WIDTH] * zcd[:, 2 * C_WIDTH:3 * C_WIDTH]
    conv_buf[SUBLANES:SUBLANES + tm, :] = gh
    conv = (convw_ref[2:3, :] * gh
            + convw_ref[1:2, :] * conv_buf[SUBLANES - 1:SUBLANES - 1 + tm, :]
            + convw_ref[0:1, :] * conv_buf[SUBLANES - 2:SUBLANES - 2 + tm, :])
    y_ref[0, :, A_WIDTH:A_WIDTH + C_WIDTH] = (zcd[:, :C_WIDTH] * conv).astype(BF16)
    conv_buf[0:SUBLANES, :] = conv_buf[tm:tm + SUBLANES, :]

    csq = csq_ref[0, 0]
    for h in range(MLA_HEADS):
        q_ref[0, h, 0] = (qt[h * QK_DIM:(h + 1) * QK_DIM, :] * csq).astype(BF16)
    r4 = zb[:, Q_LORA + KV_LORA:] * csk_ref[0]
    kr = r4 + pltpu.roll(r4, MLA_ROPE, 1)
    lane = lax.broadcasted_iota(jnp.int32, (tm, QK_DIM), 1)
    kr = jnp.where(lane >= MLA_NOPE, kr, 0.0)
    for h in range(MLA_HEADS):
        k_ref[0, h] = (kk[:, h * QK_DIM:(h + 1) * QK_DIM] + kr).astype(BF16)
    ones_then_zeros = jnp.where(
        lax.broadcasted_iota(jnp.int32, (V_ROWS - MLA_V, tm), 0) == 0, 1.0, 0.0).astype(BF16)
    for h in range(MLA_HEADS):
        v_ref[0, h, 0, 0:MLA_V, :] = vt[h * MLA_V:(h + 1) * MLA_V, :].astype(BF16)
        v_ref[0, h, 0, MLA_V:, :] = ones_then_zeros


def _mixer_in(x, csq, csk, w_in, vng, ws, bs, qng, wuq, kvng, wk, wv, convw, poolw, pscale,
              *, layer, tm):
    bsz, seq, d = x.shape
    tok = lambda w: pl.BlockSpec((1, tm, w), lambda b, s: (b, s, 0))
    head = lambda n, w: pl.BlockSpec((1, n, tm, w), lambda b, s: (b, 0, s, 0))
    head_t = lambda r: pl.BlockSpec((1, MLA_HEADS, 1, r, tm), lambda b, s: (b, 0, s, 0, 0))
    shape_t = lambda r: jax.ShapeDtypeStruct((bsz, MLA_HEADS, seq // tm, r, tm), BF16)
    return pl.pallas_call(
        functools.partial(_mixer_in_kernel, tm=tm),
        grid=(bsz, seq // tm),
        in_specs=[tok(d), pl.BlockSpec((1, 1, QK_DIM, tm), lambda b, s: (b, s, 0, 0)), tok(QK_DIM),
                  *[_layer_spec(w, layer) for w in (w_in, vng, ws, bs, qng, wuq, kvng, wk, wv,
                                                     convw, poolw, pscale)]],
        out_specs=[head_t(QK_DIM), head(MLA_HEADS, QK_DIM), head_t(V_ROWS),
                   tok(A_WIDTH + C_WIDTH + D_WIDTH)],
        out_shape=[shape_t(QK_DIM),
                   jax.ShapeDtypeStruct((bsz, MLA_HEADS, seq, QK_DIM), BF16),
                   shape_t(V_ROWS),
                   jax.ShapeDtypeStruct((bsz, seq, A_WIDTH + C_WIDTH + D_WIDTH), BF16)],
        scratch_shapes=[pltpu.VMEM((tm + SUBLANES, C_WIDTH), F32),
                        pltpu.VMEM((tm + POOL_HALO + SUBLANES, D_WIDTH), F32),
                        pltpu.VMEM((tm + POOL_HALO + SUBLANES, D_WIDTH), F32),
                        pltpu.VMEM((POOL_HALO, D_WIDTH), F32)],
        compiler_params=_params(2),
        name="mixer_in",
    )(x, csq, csk, w_in, vng, ws, bs, qng, wuq, kvng, wk, wv, convw, poolw, pscale)


def _mla_attn_kernel(qt_ref, k_ref, vt_ref, o_ref, m_ref, acc_ref, *, tq, hg):
    qi = pl.program_id(2)
    half = tq // 2
    m_ref[...] = jnp.full(m_ref.shape, NEG_BIG, F32)
    acc_ref[...] = jnp.zeros(acc_ref.shape, F32)

    def run_blocks(blocks):
        pieces = []
        for j, diagonal in blocks:
            pieces += [(j, diagonal, hh, 0, 0) for hh in range(hg)]
            pieces += [(j, diagonal, hh, half, half if diagonal else 0) for hh in range(hg)]

        def scores(piece):
            j, _, hh, k0, q0 = piece
            base = pl.multiple_of(j * tq, tq)
            return _dot(k_ref[0, hh, pl.ds(base + k0, half), :], qt_ref[0, hh, 0, :, q0:])

        def softmax(piece, st):
            _, diagonal, hh, _, q0 = piece
            if diagonal:
                key = lax.broadcasted_iota(jnp.int32, st.shape, 0)
                qry = lax.broadcasted_iota(jnp.int32, st.shape, 1)
                st = jnp.where(key <= qry, st, NEG_BIG)
            m_old = m_ref[hh, :, q0:]
            m_new = jnp.maximum(m_old, jnp.max(st, axis=0, keepdims=True))
            m_ref[hh, :, q0:] = m_new
            return piece, jnp.exp2(st - m_new).astype(BF16), jnp.exp2(m_old - m_new)

        def weighted_values(piece, probs, rescale):
            j, _, hh, k0, q0 = piece
            pv = _dot(vt_ref[0, hh, j, :, k0:k0 + half], probs)
            acc_ref[hh, :, q0:] = rescale * acc_ref[hh, :, q0:] + pv

        st_next = scores(pieces[0])
        pending = None
        for i, piece in enumerate(pieces):
            st = st_next
            if i + 1 < len(pieces):
                st_next = scores(pieces[i + 1])
            if pending is not None:
                weighted_values(*pending)
            pending = softmax(piece, st)
        weighted_values(*pending)

    def pair(jj, carry):
        run_blocks([(2 * jj, False), (2 * jj + 1, False)])
        return carry

    lax.fori_loop(0, qi // 2, pair, 0)

    @pl.when(qi % 2 == 1)
    def _():
        run_blocks([(qi - 1, False), (qi, True)])

    @pl.when(qi % 2 == 0)
    def _():
        run_blocks([(qi, True)])

    for pr in range(hg // 2):
        halves = []
        for hh in (2 * pr, 2 * pr + 1):
            acc = acc_ref[hh]
            halves.append(acc[:MLA_V] / acc[MLA_V:MLA_V + 1])
        o_ref[0, :, pr * LANES:(pr + 1) * LANES] = jnp.concatenate(halves, axis=0).T.astype(BF16)


def _mla_attn(qt, k, vt, *, hg):
    bsz, heads, n_tiles, _, tq = qt.shape
    seq = n_tiles * tq
    return pl.pallas_call(
        functools.partial(_mla_attn_kernel, tq=tq, hg=hg),
        grid=(bsz, heads // hg, n_tiles),
        in_specs=[pl.BlockSpec((1, hg, 1, QK_DIM, tq), lambda b, g, i: (b, g, i, 0, 0)),
                  pl.BlockSpec((1, hg, seq, QK_DIM), lambda b, g, i: (b, g, 0, 0)),
                  pl.BlockSpec((1, hg, n_tiles, V_ROWS, tq), lambda b, g, i: (b, g, 0, 0, 0))],
        out_specs=pl.BlockSpec((1, tq, hg * MLA_V), lambda b, g, i: (b, i, g)),
        out_shape=jax.ShapeDtypeStruct((bsz, seq, MLA_WIDTH), BF16),
        scratch_shapes=[pltpu.VMEM((hg, 1, tq), F32), pltpu.VMEM((hg, V_ROWS, tq), F32)],
        compiler_params=_params(3),
        name="mla_attn",
    )(qt, k, vt)


def _route_t(logits_t, bias_t):
    scores = jax.nn.sigmoid(logits_t)
    sel_all = scores + bias_t
    sel = [sel_all[e:e + 1] for e in range(N_EXPERTS)]
    group_score = []
    for g in range(N_GROUPS):
        v = sel[g * EXPERTS_PER_GROUP:(g + 1) * EXPERTS_PER_GROUP]
        best = None
        for i in range(EXPERTS_PER_GROUP):
            for j in range(i + 1, EXPERTS_PER_GROUP):
                best = v[i] + v[j] if best is None else jnp.maximum(best, v[i] + v[j])
        group_score.append(best)
    top, top_idx = group_score[0], jnp.zeros(group_score[0].shape, jnp.int32)
    for g in range(1, N_GROUPS):
        better = group_score[g] > top
        top = jnp.where(better, group_score[g], top)
        top_idx = jnp.where(better, g, top_idx)
    member = [jnp.where(top_idx == g, 1.0, 0.0) for g in range(N_GROUPS)]
    weights = []
    for e in range(N_EXPERTS):
        g = e // EXPERTS_PER_GROUP
        rank = jnp.zeros(sel[e].shape, jnp.int32)
        for j in range(g * EXPERTS_PER_GROUP, (g + 1) * EXPERTS_PER_GROUP):
            if j != e:
                rank = rank + jnp.where(sel[j] > sel[e], 1,
                                        jnp.where(sel[j] == sel[e], 1 if j < e else 0, 0))
        weights.append(jnp.where(rank < TOP_K, member[g], 0.0) * scores[e:e + 1])
    total = weights[0]
    for w in weights[1:]:
        total = total + w
    return [w / total for w in weights], member


def _post_attn_kernel(yacd_ref, yb_ref, x_ref, wo_acd_ref, wo_b_ref, g1_ref, b1_ref, wq_ref,
                      km_ref, vm_ref, wo_ref, g2_ref, b2_ref, rw_ref,
                      x2_ref, logits_ref, *, tm, alpha, n_chains):
    def chain(rows):
        h = _dot(yacd_ref[0, rows], wo_acd_ref[...]) + _dot(yb_ref[0, rows], wo_b_ref[...])
        yield
        x1 = _layer_norm(alpha * x_ref[0, rows] + h, g1_ref[...], b1_ref[...])
        q = _dot(x1.astype(BF16), wq_ref[...])
        yield
        heads = []
        for hd in range(X_HEADS):
            cols = slice(hd * X_HEAD_DIM, (hd + 1) * X_HEAD_DIM)
            s = _dot_nt(q[:, cols].astype(BF16), km_ref[0, :, cols])
            e = jnp.exp2(s - jnp.max(s, axis=-1, keepdims=True))
            o = _dot(e.astype(BF16), vm_ref[0, :, cols]) / jnp.sum(e, axis=-1, keepdims=True)
            heads.append(o.astype(BF16))
            yield
        h2 = _dot(jnp.concatenate(heads, axis=1), wo_ref[...])
        yield
        x2 = _layer_norm(alpha * x1 + h2, g2_ref[...], b2_ref[...])
        x2_ref[0, rows] = x2
        x_hi = x2.astype(BF16)
        x_lo = (x2 - x_hi.astype(F32)).astype(BF16)
        hl = _dot_nt(rw_ref[...], x_hi)
        lo = _dot_nt(rw_ref[0:N_EXPERTS, :], x_lo)
        logits_ref[0, :, rows] = hl[:N_EXPERTS] + hl[N_EXPERTS:] + lo

    rows_per = tm // n_chains
    _run_interleaved([chain(slice(c * rows_per, (c + 1) * rows_per)) for c in range(n_chains)])


def _post_attn(yacd, yb, x, wo_acd, wo_b, g1, b1, wq, km, vm, wo, g2, b2, rw,
               *, layer, tm, alpha):
    bsz, seq, d = x.shape
    tok = lambda w: pl.BlockSpec((1, tm, w), lambda b, s: (b, s, 0))
    memspec = pl.BlockSpec((None, 1) + km.shape[2:], lambda b, s: (layer, b, 0, 0))
    per_layer = lambda w: _layer_spec(w, layer)
    return pl.pallas_call(
        functools.partial(_post_attn_kernel, tm=tm, alpha=alpha, n_chains=2),
        grid=(bsz, seq // tm),
        in_specs=[tok(yacd.shape[-1]), tok(yb.shape[-1]), tok(d),
                  per_layer(wo_acd), per_layer(wo_b), per_layer(g1), per_layer(b1), per_layer(wq),
                  memspec, memspec, per_layer(wo), per_layer(g2), per_layer(b2),
                  _const_spec(rw.shape)],
        out_specs=[tok(d), pl.BlockSpec((1, N_EXPERTS, tm), lambda b, s: (b, 0, s))],
        out_shape=[jax.ShapeDtypeStruct((bsz, seq, d), F32),
                   jax.ShapeDtypeStruct((bsz, N_EXPERTS, seq), F32)],
        compiler_params=_params(2),
        name="post_attn",
    )(yacd, yb, x, wo_acd, wo_b, g1, b1, wq, km, vm, wo, g2, b2, rw)


def _dot_tn(a, b):
    return lax.dot_general(a, b, (((0,), (0,)), ((), ())), preferred_element_type=F32)


def _moe_kernel(x_ref, logits_ref, rb_ref, before_ref, wg_ref, wu_ref, wd_ref, g3_ref, b3_ref, o_ref,
                xb_ref, pos_ref, memb_ref, gsplit_ref, acc_ref, *, alpha, tm, rows):
    group_w = EXPERTS_PER_GROUP * D_FF
    x = x_ref[0]
    xb_ref[...] = x.astype(BF16)
    gates, member = _route_t(logits_ref[0], rb_ref[...])
    memb = jnp.concatenate(member + [jnp.zeros((ROUTE_ROWS - N_GROUPS, tm), F32)], axis=0)
    memb_ref[...] = memb
    pos_ref[...] = _dot(memb.astype(BF16), before_ref[...])
    gate_rows = jnp.concatenate(gates, axis=0)
    g_hi = gate_rows.astype(BF16)
    gsplit_ref[...] = jnp.concatenate([g_hi, (gate_rows - g_hi.astype(F32)).astype(BF16)], axis=0)

    def expert_chunk(g, ci):
        slot = (lax.broadcasted_iota(jnp.int32, (rows, tm), 0) + ci * rows).astype(F32)
        p = jnp.where(pos_ref[g:g + 1, :] == slot, memb_ref[g:g + 1, :], 0.0).astype(BF16)
        xg = _dot(p, xb_ref[...]).astype(BF16)
        gg = _dot_nt(p, gsplit_ref[...])
        hs = []
        for j in range(EXPERTS_PER_GROUP):
            e = g * EXPERTS_PER_GROUP + j
            gate = gg[:, e:e + 1] + gg[:, N_EXPERTS + e:N_EXPERTS + e + 1]
            hg = _dot(xg, wg_ref[e])
            hu = _dot(xg, wu_ref[e])
            hs.append((jax.nn.silu(hg) * hu * gate).astype(BF16))
        y = _dot(jnp.concatenate(hs, axis=1), wd_ref[g * group_w:(g + 1) * group_w, :])
        return p, y.astype(BF16)

    first = [expert_chunk(g, 0) for g in range(N_GROUPS)]
    acc_ref[...] = _dot_tn(jnp.concatenate([p for p, _ in first], axis=0),
                           jnp.concatenate([y for _, y in first], axis=0))
    largest = jnp.max(jnp.sum(memb_ref[0:SUBLANES, :], axis=1, keepdims=True))

    @pl.when(largest > rows)
    def _():
        for g in range(N_GROUPS):
            n_tok = jnp.sum(memb_ref[g:g + 1, :]).astype(jnp.int32)

            def extra(ci, carry, g=g):
                p, y = expert_chunk(g, ci)
                acc_ref[...] += _dot_tn(p, y)
                return carry

            lax.fori_loop(1, (n_tok + rows - 1) // rows, extra, 0)

    o_ref[0] = _layer_norm(alpha * x_ref[0] + acc_ref[...], g3_ref[...], b3_ref[...])


def _moe(x, logits_t, rb_t, before, wg, wu, wd, g3, b3, *, layer, tm, alpha):
    bsz, seq, d = x.shape
    tok = lambda w: pl.BlockSpec((1, tm, w), lambda b, s: (b, s, 0))
    resident = lambda w: _layer_spec(w, layer, single_buffer=True)
    return pl.pallas_call(
        functools.partial(_moe_kernel, alpha=alpha, tm=tm, rows=tm // N_GROUPS * 5 // 4),
        grid=(bsz, seq // tm),
        in_specs=[tok(d), pl.BlockSpec((1, N_EXPERTS, tm), lambda b, s: (b, 0, s)),
                  _const_spec(rb_t.shape), _const_spec(before.shape),
                  resident(wg), resident(wu), resident(wd),
                  _layer_spec(g3, layer), _layer_spec(b3, layer)],
        out_specs=tok(d),
        out_shape=jax.ShapeDtypeStruct((bsz, seq, d), F32),
        scratch_shapes=[pltpu.VMEM((tm, d), BF16),
                        pltpu.VMEM((ROUTE_ROWS, tm), F32),
                        pltpu.VMEM((ROUTE_ROWS, tm), F32),
                        pltpu.VMEM((2 * N_EXPERTS, tm), BF16),
                        pltpu.VMEM((tm, d), F32)],
        compiler_params=_params(2),
        name="moe",
    )(x, logits_t, rb_t, before, wg, wu, wd, g3, b3)


def _rot_cols(w):
    half = w.shape[-1] // 2
    return jnp.concatenate([-w[..., half:], w[..., :half]], axis=-1)


def _prep_w_in(w_in):
    k_rope = 2 * A_WIDTH + Q_LORA + KV_LORA
    head = w_in[..., :k_rope]
    kr = w_in[..., k_rope:k_rope + MLA_ROPE]
    rest = w_in[..., k_rope + MLA_ROPE:]
    rope4 = jnp.concatenate([kr, _rot_cols(kr), kr, _rot_cols(kr)], axis=-1)
    return jnp.concatenate([head, rope4, rest], axis=-1).astype(BF16)


def _prep_w_uq(w_uq):
    depth = w_uq.shape[0]
    w = w_uq.reshape(depth, Q_LORA, MLA_HEADS, MLA_NOPE + MLA_ROPE)
    nope, rope = w[..., :MLA_NOPE], w[..., MLA_NOPE:]
    ext = jnp.concatenate([nope, rope, _rot_cols(rope)], axis=-1)
    return ext.reshape(depth, Q_LORA, MLA_HEADS * QK_DIM).astype(BF16)


def _prep_w_ukv(w_ukv):
    depth = w_ukv.shape[0]
    w = w_ukv.reshape(depth, KV_LORA, MLA_HEADS, MLA_NOPE + MLA_V)
    k_nope, v = w[..., :MLA_NOPE], w[..., MLA_NOPE:]
    wk = jnp.concatenate([k_nope, jnp.zeros_like(k_nope)], axis=-1)
    return (wk.reshape(depth, KV_LORA, MLA_HEADS * QK_DIM).astype(BF16),
            v.reshape(depth, KV_LORA, MLA_WIDTH).astype(BF16))


def _prep_pool_w(pool_w):
    depth, groups = pool_w.shape[:2]
    eye = jnp.eye(groups, dtype=pool_w.dtype)
    bd = jnp.einsum('lgcd,gh->lgchd', pool_w, eye)
    return bd.reshape(depth, D_WIDTH, D_WIDTH).astype(BF16)


def kernel(x, mem, positions, w_in, gmlp_v_norm_g, gmlp_w_s, gmlp_b_s, mla_q_norm_g, mla_w_uq,
           mla_kv_norm_g, mla_w_ukv, conv_w, pool_w, pool_scale, w_out, ln1_g, ln1_b,
           xattn_wq, xattn_wk, xattn_wv, xattn_wo, ln2_g, ln2_b, router_w, router_b,
           moe_w_gate, moe_w_up, moe_w_down, ln3_g, ln3_b):
    depth = w_in.shape[0]
    alpha = (2 * depth) ** 0.25
    tm = 512
    bsz, seq, _ = x.shape

    inv_freq = ROPE_BASE ** (-jnp.arange(0, MLA_ROPE, 2, dtype=F32) / MLA_ROPE)
    ang = positions.astype(F32)[..., None] * inv_freq
    cos2 = jnp.tile(jnp.cos(ang), (1, 1, 2))
    sin2 = jnp.tile(jnp.sin(ang), (1, 1, 2))
    scale = (MLA_NOPE + MLA_ROPE) ** -0.5 * LOG2_E
    csq = scale * jnp.concatenate([jnp.ones(cos2.shape[:2] + (MLA_NOPE,), F32), cos2, sin2], -1)
    csq = jnp.swapaxes(csq.reshape(bsz, seq // tm, tm, QK_DIM), 2, 3)
    csk = jnp.concatenate([cos2, sin2, cos2, sin2], axis=-1)

    row = lambda a: a[:, None, :]
    w_in_e = _prep_w_in(w_in)
    w_uq_e = jnp.swapaxes(_prep_w_uq(mla_w_uq), 1, 2)
    w_k_e, w_v_e = _prep_w_ukv(mla_w_ukv)
    w_v_e = jnp.swapaxes(w_v_e, 1, 2)
    ws = gmlp_w_s.astype(BF16)
    bs = jnp.repeat(jnp.swapaxes(gmlp_b_s, 1, 2), A_HEAD_DIM, axis=2)
    pool_bd = _prep_pool_w(pool_w)
    wo_acd = jnp.concatenate([w_out[:, :A_WIDTH], w_out[:, A_WIDTH + MLA_WIDTH:]], axis=1).astype(BF16)
    wo_b = w_out[:, A_WIDTH:A_WIDTH + MLA_WIDTH].astype(BF16)
    wq = (xattn_wq * (X_HEAD_DIM ** -0.5 * LOG2_E)).astype(BF16)
    wo = xattn_wo.astype(BF16)
    rw_t = router_w.T
    rw_hi = rw_t.astype(BF16)
    rw = jnp.concatenate([rw_hi, (rw_t - rw_hi.astype(F32)).astype(BF16)], axis=0)
    rb_t = jnp.broadcast_to(router_b[:, None], (N_EXPERTS, tm))
    before = jnp.triu(jnp.ones((tm, tm), BF16), 1)
    wg = moe_w_gate.astype(BF16)
    wu = moe_w_up.astype(BF16)
    wd = moe_w_down.reshape(depth, N_EXPERTS * D_FF, D_MODEL).astype(BF16)

    km, vm = _mem_kv(mem, xattn_wk.astype(BF16), xattn_wv.astype(BF16))

    for l in range(depth):
        q, k, v, yacd = _mixer_in(
            x, csq, csk, w_in_e, row(gmlp_v_norm_g), ws, bs, row(mla_q_norm_g),
            w_uq_e, row(mla_kv_norm_g), w_k_e, w_v_e, conv_w, pool_bd,
            row(pool_scale), layer=l, tm=tm)
        yb = _mla_attn(q, k, v, hg=MLA_HEADS)
        x2, logits = _post_attn(
            yacd, yb, x, wo_acd, wo_b, row(ln1_g), row(ln1_b), wq, km, vm,
            wo, row(ln2_g), row(ln2_b), rw, layer=l, tm=2 * tm, alpha=alpha)
        x = _moe(x2, logits, rb_t, before, wg, wu, wd, row(ln3_g), row(ln3_b),
                 layer=l, tm=tm, alpha=alpha)
    return x
```

```python
import functools

import jax
import jax.numpy as jnp
from jax import lax
from jax.experimental import pallas as pl
from jax.experimental.pallas import tpu as pltpu

F32 = jnp.float32
BF16 = jnp.bfloat16

D_MODEL = 1024
A_HEADS, A_HEAD_DIM, CHUNK = 4, 64, 128
A_WIDTH = A_HEADS * A_HEAD_DIM
MLA_HEADS, MLA_NOPE, MLA_ROPE, MLA_V = 8, 64, 32, 64
Q_LORA, KV_LORA = 256, 128
MLA_WIDTH = MLA_HEADS * MLA_V
ROPE_BASE = 10000.0
C_WIDTH, CONV_W = 256, 3
D_WIDTH = 256
POOL_WINDOWS = (2, 4, 8, 16)
D_GROUP = D_WIDTH // len(POOL_WINDOWS)
X_HEADS = 4
X_HEAD_DIM = D_MODEL // X_HEADS
N_EXPERTS, N_GROUPS, TOP_K = 16, 4, 2
EXPERTS_PER_GROUP = N_EXPERTS // N_GROUPS
D_FF = 256
EPS = 1e-6

LANES = 128
SUBLANES = 8
VMEM_LIMIT = 56 * 1024 * 1024

OFF_U = 0
OFF_CQ = OFF_U + 2 * A_WIDTH
OFF_BG = OFF_CQ + Q_LORA + KV_LORA + 4 * MLA_ROPE
P_EXT = OFF_BG + 3 * C_WIDTH + D_WIDTH
QK_DIM = 128
V_ROWS = 80
POOL_HALO = 16
ROUTE_ROWS = 16
NEG_BIG = -1e30
LOG2_E = 1.4426950408889634


def _dot(a, b):
    return jnp.dot(a, b, preferred_element_type=F32)


def _dot_nt(a, b):
    return lax.dot_general(a, b, (((1,), (1,)), ((), ())), preferred_element_type=F32)


def _rms(x, g):
    return x * lax.rsqrt(jnp.mean(x * x, axis=-1, keepdims=True) + EPS) * g


def _layer_norm(x, g, b):
    mu = jnp.mean(x, axis=-1, keepdims=True)
    xc = x - mu
    var = jnp.mean(xc * xc, axis=-1, keepdims=True)
    return xc * lax.rsqrt(var + EPS) * g + b


def _params(n_grid):
    return pltpu.CompilerParams(dimension_semantics=("arbitrary",) * n_grid,
                                vmem_limit_bytes=VMEM_LIMIT)


def _run_interleaved(chains):
    live = list(chains)
    while live:
        for c in list(live):
            try:
                next(c)
            except StopIteration:
                live.remove(c)


def _const_spec(shape):
    zeros = (0,) * len(shape)
    return pl.BlockSpec(shape, lambda *_: zeros)


def _layer_spec(stacked, layer, single_buffer=False):
    index = (layer,) + (0,) * (stacked.ndim - 1)
    mode = dict(pipeline_mode=pl.Buffered(1)) if single_buffer else {}
    return pl.BlockSpec((None,) + stacked.shape[1:], lambda *_: index, **mode)


def _mem_kv_kernel(mem_ref, wk_ref, wv_ref, k_ref, v_ref):
    m = mem_ref[0].astype(BF16)
    k_ref[0, 0] = _dot(m, wk_ref[0]).astype(BF16)
    v_ref[0, 0] = _dot(m, wv_ref[0]).astype(BF16)


def _mem_kv(mem, wk, wv):
    depth = wk.shape[0]
    bsz, mlen, d = mem.shape
    out = jax.ShapeDtypeStruct((depth, bsz, mlen, d), BF16)
    return pl.pallas_call(
        _mem_kv_kernel,
        grid=(depth, bsz),
        in_specs=[pl.BlockSpec((1, mlen, d), lambda l, b: (b, 0, 0)),
                  pl.BlockSpec((1, d, d), lambda l, b: (l, 0, 0)),
                  pl.BlockSpec((1, d, d), lambda l, b: (l, 0, 0))],
        out_specs=[pl.BlockSpec((1, 1, mlen, d), lambda l, b: (l, b, 0, 0)),
                   pl.BlockSpec((1, 1, mlen, d), lambda l, b: (l, b, 0, 0))],
        out_shape=[out, out],
        compiler_params=_params(2),
        name="mem_kv",
    )(mem, wk, wv)


def _mixer_in_kernel(x_ref, csq_ref, csk_ref, w_in_ref, vng_ref, ws_ref, bs_ref, qng_ref,
                     wuq_ref, kvng_ref, wk_ref, wv_ref, convw_ref, poolw_ref, pscale_ref,
                     q_ref, k_ref, v_ref, y_ref,
                     conv_buf, pool_a, pool_b, pool_carry, *, tm, ta):
    si = pl.program_id(1)

    @pl.when(si == 0)
    def _():
        pool_carry[...] = jnp.zeros((POOL_HALO, D_WIDTH), F32)
        conv_buf[0:SUBLANES, :] = jnp.zeros((SUBLANES, C_WIDTH), F32)

    xb = x_ref[0].astype(BF16)

    za = _dot(xb, w_in_ref[:, OFF_U:OFF_CQ])
    zb = _dot(xb, w_in_ref[:, OFF_CQ:OFF_BG])
    zcd = _dot(xb, w_in_ref[:, OFF_BG:P_EXT])

    za = jax.nn.gelu(za)
    u = za[:, :A_WIDTH]
    v = _rms(za[:, A_WIDTH:], vng_ref[...]).astype(BF16)
    cq = _rms(zb[:, :Q_LORA], qng_ref[...]).astype(BF16)
    ckv = _rms(zb[:, Q_LORA:Q_LORA + KV_LORA], kvng_ref[...]).astype(BF16)

    row = lax.broadcasted_iota(jnp.int32, (CHUNK, CHUNK), 0)
    col = lax.broadcasted_iota(jnp.int32, (CHUNK, CHUNK), 1)
    w_causal = jnp.concatenate(
        [jnp.where(row >= col, ws_ref[h], jnp.zeros((), BF16)) for h in range(A_HEADS)], axis=1)
    lane_head = lax.broadcasted_iota(jnp.int32, (CHUNK, A_WIDTH), 1) // A_HEAD_DIM
    for c in range(tm // CHUNK):
        rows = slice(c * CHUNK, (c + 1) * CHUNK)
        vc = v[rows]
        v_heads = jnp.concatenate(
            [jnp.where(lane_head == h, vc, jnp.zeros((), BF16)) for h in range(A_HEADS)], axis=0)
        mixed = _dot(w_causal, v_heads)
        y_ref[0, rows, 0:A_WIDTH] = (u[rows] * (mixed + bs_ref[...])).astype(BF16)

    qt = _dot_nt(wuq_ref[...], cq)
    kk = _dot(ckv, wk_ref[...])
    vt = _dot_nt(wv_ref[...], ckv)

    zd = zcd[:, 3 * C_WIDTH:]
    base = SUBLANES
    n = tm + POOL_HALO
    pool_a[0:base, :] = jnp.zeros((base, D_WIDTH), F32)
    pool_b[0:base, :] = jnp.zeros((base, D_WIDTH), F32)
    pool_a[base:base + POOL_HALO, :] = pool_carry[...]
    pool_a[base + POOL_HALO:base + n, :] = zd
    pool_carry[...] = zd[tm - POOL_HALO:, :]
    lane_d = lax.broadcasted_iota(jnp.int32, (n, D_WIDTH), 1)
    pool_b[base:base + n, :] = pool_a[base:base + n, :] + pool_a[base - 1:base - 1 + n, :]
    pool_a[base:base + n, :] = pool_b[base:base + n, :] + jnp.where(
        lane_d >= D_GROUP, pool_b[base - 2:base - 2 + n, :], 0.0)
    pool_b[base:base + n, :] = pool_a[base:base + n, :] + jnp.where(
        lane_d >= 2 * D_GROUP, pool_a[base - 4:base - 4 + n, :], 0.0)
    t0 = base + POOL_HALO
    lane_t = lax.broadcasted_iota(jnp.int32, (tm, D_WIDTH), 1)
    win_sum = pool_b[t0:t0 + tm, :] + jnp.where(
        lane_t >= 3 * D_GROUP, pool_b[t0 - 8:t0 - 8 + tm, :], 0.0)
    pos1 = si * tm + lax.broadcasted_iota(jnp.int32, (tm, D_WIDTH), 0) + 1
    window = jnp.left_shift(2, lane_t // D_GROUP)
    count = jnp.minimum(pos1, window).astype(F32)
    pooled = (win_sum / count - zd).astype(BF16)
    yd = _dot(pooled, poolw_ref[...]) * pscale_ref[...]
    y_ref[0, :, A_WIDTH + C_WIDTH:] = yd.astype(BF16)

    gh = zcd[:, C_WIDTH:2 * C_WIDTH] * zcd[:, 2 * C_WIDTH:3 * C_WIDTH]
    conv_buf[SUBLANES:SUBLANES + tm, :] = gh
    conv = (convw_ref[2:3, :] * gh
            + convw_ref[1:2, :] * conv_buf[SUBLANES - 1:SUBLANES - 1 + tm, :]
            + convw_ref[0:1, :] * conv_buf[SUBLANES - 2:SUBLANES - 2 + tm, :])
    y_ref[0, :, A_WIDTH:A_WIDTH + C_WIDTH] = (zcd[:, :C_WIDTH] * conv).astype(BF16)
    conv_buf[0:SUBLANES, :] = conv_buf[tm:tm + SUBLANES, :]

    for t in range(tm // ta):
        cols = slice(t * ta, (t + 1) * ta)
        csq = csq_ref[0, t]
        for h in range(MLA_HEADS):
            q_ref[0, h, t] = (qt[h * QK_DIM:(h + 1) * QK_DIM, cols] * csq).astype(BF16)
    r4 = zb[:, Q_LORA + KV_LORA:] * csk_ref[0]
    kr = r4 + pltpu.roll(r4, MLA_ROPE, 1)
    lane = lax.broadcasted_iota(jnp.int32, (tm, QK_DIM), 1)
    kr = jnp.where(lane >= MLA_NOPE, kr, 0.0)
    for h in range(MLA_HEADS):
        k_ref[0, h] = (kk[:, h * QK_DIM:(h + 1) * QK_DIM] + kr).astype(BF16)
    ones_then_zeros = jnp.where(
        lax.broadcasted_iota(jnp.int32, (V_ROWS - MLA_V, ta), 0) == 0, 1.0, 0.0).astype(BF16)
    for t in range(tm // ta):
        cols = slice(t * ta, (t + 1) * ta)
        for h in range(MLA_HEADS):
            v_ref[0, h, t, 0:MLA_V, :] = vt[h * MLA_V:(h + 1) * MLA_V, cols].astype(BF16)
            v_ref[0, h, t, MLA_V:, :] = ones_then_zeros


def _mixer_in(x, csq, csk, w_in, vng, ws, bs, qng, wuq, kvng, wk, wv, convw, poolw, pscale,
              *, layer, tm, ta):
    bsz, seq, d = x.shape
    tok = lambda w: pl.BlockSpec((1, tm, w), lambda b, s: (b, s, 0))
    head = lambda n, w: pl.BlockSpec((1, n, tm, w), lambda b, s: (b, 0, s, 0))
    per = tm // ta
    head_t = lambda r: pl.BlockSpec((1, MLA_HEADS, per, r, ta), lambda b, s: (b, 0, s, 0, 0))
    shape_t = lambda r: jax.ShapeDtypeStruct((bsz, MLA_HEADS, seq // ta, r, ta), BF16)
    return pl.pallas_call(
        functools.partial(_mixer_in_kernel, tm=tm, ta=ta),
        grid=(bsz, seq // tm),
        in_specs=[tok(d), pl.BlockSpec((1, per, QK_DIM, ta), lambda b, s: (b, s, 0, 0)), tok(QK_DIM),
                  *[_layer_spec(w, layer) for w in (w_in, vng, ws, bs, qng, wuq, kvng, wk, wv,
                                                     convw, poolw, pscale)]],
        out_specs=[head_t(QK_DIM), head(MLA_HEADS, QK_DIM), head_t(V_ROWS),
                   tok(A_WIDTH + C_WIDTH + D_WIDTH)],
        out_shape=[shape_t(QK_DIM),
                   jax.ShapeDtypeStruct((bsz, MLA_HEADS, seq, QK_DIM), BF16),
                   shape_t(V_ROWS),
                   jax.ShapeDtypeStruct((bsz, seq, A_WIDTH + C_WIDTH + D_WIDTH), BF16)],
        scratch_shapes=[pltpu.VMEM((tm + SUBLANES, C_WIDTH), F32),
                        pltpu.VMEM((tm + POOL_HALO + SUBLANES, D_WIDTH), F32),
                        pltpu.VMEM((tm + POOL_HALO + SUBLANES, D_WIDTH), F32),
                        pltpu.VMEM((POOL_HALO, D_WIDTH), F32)],
        compiler_params=_params(2),
        name="mixer_in",
    )(x, csq, csk, w_in, vng, ws, bs, qng, wuq, kvng, wk, wv, convw, poolw, pscale)


def _mla_attn_kernel(qt_ref, k_ref, vt_ref, o_ref, m_ref, acc_ref, *, tq, hg):
    qi = pl.program_id(2)
    half = tq // 2
    m_ref[...] = jnp.full(m_ref.shape, NEG_BIG, F32)
    acc_ref[...] = jnp.zeros(acc_ref.shape, F32)

    def run_blocks(blocks):
        pieces = []
        for j, diagonal in blocks:
            pieces += [(j, diagonal, hh, 0, 0) for hh in range(hg)]
            pieces += [(j, diagonal, hh, half, half if diagonal else 0) for hh in range(hg)]

        def scores(piece):
            j, _, hh, k0, q0 = piece
            base = pl.multiple_of(j * tq, tq)
            return _dot(k_ref[0, hh, pl.ds(base + k0, half), :], qt_ref[0, hh, 0, :, q0:])

        def softmax(piece, st):
            _, diagonal, hh, _, q0 = piece
            if diagonal:
                key = lax.broadcasted_iota(jnp.int32, st.shape, 0)
                qry = lax.broadcasted_iota(jnp.int32, st.shape, 1)
                st = jnp.where(key <= qry, st, NEG_BIG)
            m_old = m_ref[hh, :, q0:]
            m_new = jnp.maximum(m_old, jnp.max(st, axis=0, keepdims=True))
            m_ref[hh, :, q0:] = m_new
            return piece, jnp.exp2(st - m_new).astype(BF16), jnp.exp2(m_old - m_new)

        def weighted_values(piece, probs, rescale):
            j, _, hh, k0, q0 = piece
            pv = _dot(vt_ref[0, hh, j, :, k0:k0 + half], probs)
            acc_ref[hh, :, q0:] = rescale * acc_ref[hh, :, q0:] + pv

        st_next = scores(pieces[0])
        pending = None
        for i, piece in enumerate(pieces):
            st = st_next
            if i + 1 < len(pieces):
                st_next = scores(pieces[i + 1])
            if pending is not None:
                weighted_values(*pending)
            pending = softmax(piece, st)
        weighted_values(*pending)

    def pair(jj, carry):
        run_blocks([(2 * jj, False), (2 * jj + 1, False)])
        return carry

    lax.fori_loop(0, qi // 2, pair, 0)

    @pl.when(qi % 2 == 1)
    def _():
        run_blocks([(qi - 1, False), (qi, True)])

    @pl.when(qi % 2 == 0)
    def _():
        run_blocks([(qi, True)])

    for pr in range(hg // 2):
        halves = []
        for hh in (2 * pr, 2 * pr + 1):
            acc = acc_ref[hh]
            halves.append(acc[:MLA_V] / acc[MLA_V:MLA_V + 1])
        o_ref[0, :, pr * LANES:(pr + 1) * LANES] = jnp.concatenate(halves, axis=0).T.astype(BF16)


def _mla_attn(qt, k, vt, *, hg):
    bsz, heads, n_tiles, _, tq = qt.shape
    seq = n_tiles * tq
    return pl.pallas_call(
        functools.partial(_mla_attn_kernel, tq=tq, hg=hg),
        grid=(bsz, heads // hg, n_tiles),
        in_specs=[pl.BlockSpec((1, hg, 1, QK_DIM, tq), lambda b, g, i: (b, g, i, 0, 0)),
                  pl.BlockSpec((1, hg, seq, QK_DIM), lambda b, g, i: (b, g, 0, 0)),
                  pl.BlockSpec((1, hg, n_tiles, V_ROWS, tq), lambda b, g, i: (b, g, 0, 0, 0))],
        out_specs=pl.BlockSpec((1, tq, hg * MLA_V), lambda b, g, i: (b, i, g)),
        out_shape=jax.ShapeDtypeStruct((bsz, seq, MLA_WIDTH), BF16),
        scratch_shapes=[pltpu.VMEM((hg, 1, tq), F32), pltpu.VMEM((hg, V_ROWS, tq), F32)],
        compiler_params=_params(3),
        name="mla_attn",
    )(qt, k, vt)


def _route_t(logits_t, bias_t):
    scores = jax.nn.sigmoid(logits_t)
    sel_all = scores + bias_t
    sel = [sel_all[e:e + 1] for e in range(N_EXPERTS)]
    group_score = []
    for g in range(N_GROUPS):
        v = sel[g * EXPERTS_PER_GROUP:(g + 1) * EXPERTS_PER_GROUP]
        best = None
        for i in range(EXPERTS_PER_GROUP):
            for j in range(i + 1, EXPERTS_PER_GROUP):
                best = v[i] + v[j] if best is None else jnp.maximum(best, v[i] + v[j])
        group_score.append(best)
    top, top_idx = group_score[0], jnp.zeros(group_score[0].shape, jnp.int32)
    for g in range(1, N_GROUPS):
        better = group_score[g] > top
        top = jnp.where(better, group_score[g], top)
        top_idx = jnp.where(better, g, top_idx)
    member = [jnp.where(top_idx == g, 1.0, 0.0) for g in range(N_GROUPS)]
    weights = []
    for e in range(N_EXPERTS):
        g = e // EXPERTS_PER_GROUP
        rank = jnp.zeros(sel[e].shape, jnp.int32)
        for j in range(g * EXPERTS_PER_GROUP, (g + 1) * EXPERTS_PER_GROUP):
            if j != e:
                rank = rank + jnp.where(sel[j] > sel[e], 1,
                                        jnp.where(sel[j] == sel[e], 1 if j < e else 0, 0))
        weights.append(jnp.where(rank < TOP_K, member[g], 0.0) * scores[e:e + 1])
    total = weights[0]
    for w in weights[1:]:
        total = total + w
    return [w / total for w in weights], member


def _post_attn_kernel(yacd_ref, yb_ref, x_ref, wo_acd_ref, wo_b_ref, g1_ref, b1_ref, wq_ref,
                      km_ref, vm_ref, wo_ref, g2_ref, b2_ref, rw_ref,
                      x2_ref, logits_ref, *, tm, alpha, n_chains):
    def chain(rows):
        h = _dot(yacd_ref[0, rows], wo_acd_ref[...]) + _dot(yb_ref[0, rows], wo_b_ref[...])
        yield
        x1 = _layer_norm(alpha * x_ref[0, rows] + h, g1_ref[...], b1_ref[...])
        q = _dot(x1.astype(BF16), wq_ref[...])
        yield
        heads = []
        for hd in range(X_HEADS):
            cols = slice(hd * X_HEAD_DIM, (hd + 1) * X_HEAD_DIM)
            s = _dot_nt(q[:, cols].astype(BF16), km_ref[0, :, cols])
            e = jnp.exp2(s - jnp.max(s, axis=-1, keepdims=True))
            o = _dot(e.astype(BF16), vm_ref[0, :, cols]) / jnp.sum(e, axis=-1, keepdims=True)
            heads.append(o.astype(BF16))
            yield
        h2 = _dot(jnp.concatenate(heads, axis=1), wo_ref[...])
        yield
        x2 = _layer_norm(alpha * x1 + h2, g2_ref[...], b2_ref[...])
        x2_ref[0, rows] = x2
        x_hi = x2.astype(BF16)
        x_lo = (x2 - x_hi.astype(F32)).astype(BF16)
        hl = _dot_nt(rw_ref[...], x_hi)
        lo = _dot_nt(rw_ref[0:N_EXPERTS, :], x_lo)
        logits_ref[0, :, rows] = hl[:N_EXPERTS] + hl[N_EXPERTS:] + lo

    rows_per = tm // n_chains
    _run_interleaved([chain(slice(c * rows_per, (c + 1) * rows_per)) for c in range(n_chains)])


def _post_attn(yacd, yb, x, wo_acd, wo_b, g1, b1, wq, km, vm, wo, g2, b2, rw,
               *, layer, tm, alpha):
    bsz, seq, d = x.shape
    tok = lambda w: pl.BlockSpec((1, tm, w), lambda b, s: (b, s, 0))
    memspec = pl.BlockSpec((None, 1) + km.shape[2:], lambda b, s: (layer, b, 0, 0))
    per_layer = lambda w: _layer_spec(w, layer)
    return pl.pallas_call(
        functools.partial(_post_attn_kernel, tm=tm, alpha=alpha, n_chains=2),
        grid=(bsz, seq // tm),
        in_specs=[tok(yacd.shape[-1]), tok(yb.shape[-1]), tok(d),
                  per_layer(wo_acd), per_layer(wo_b), per_layer(g1), per_layer(b1), per_layer(wq),
                  memspec, memspec, per_layer(wo), per_layer(g2), per_layer(b2),
                  _const_spec(rw.shape)],
        out_specs=[tok(d), pl.BlockSpec((1, N_EXPERTS, tm), lambda b, s: (b, 0, s))],
        out_shape=[jax.ShapeDtypeStruct((bsz, seq, d), F32),
                   jax.ShapeDtypeStruct((bsz, N_EXPERTS, seq), F32)],
        compiler_params=_params(2),
        name="post_attn",
    )(yacd, yb, x, wo_acd, wo_b, g1, b1, wq, km, vm, wo, g2, b2, rw)


def _dot_tn(a, b):
    return lax.dot_general(a, b, (((0,), (0,)), ((), ())), preferred_element_type=F32)


def _moe_kernel(x_ref, logits_ref, rb_ref, before_ref, wg_ref, wu_ref, wd_ref, g3_ref, b3_ref, o_ref,
                xb_ref, pos_ref, memb_ref, gsplit_ref, acc_ref, *, alpha, tm, rows):
    group_w = EXPERTS_PER_GROUP * D_FF
    x = x_ref[0]
    xb_ref[...] = x.astype(BF16)
    gates, member = _route_t(logits_ref[0], rb_ref[...])
    memb = jnp.concatenate(member + [jnp.zeros((ROUTE_ROWS - N_GROUPS, tm), F32)], axis=0)
    memb_ref[...] = memb
    pos_ref[...] = _dot(memb.astype(BF16), before_ref[...])
    gate_rows = jnp.concatenate(gates, axis=0)
    g_hi = gate_rows.astype(BF16)
    gsplit_ref[...] = jnp.concatenate([g_hi, (gate_rows - g_hi.astype(F32)).astype(BF16)], axis=0)

    def expert_chunk(g, ci):
        slot = (lax.broadcasted_iota(jnp.int32, (rows, tm), 0) + ci * rows).astype(F32)
        p = jnp.where(pos_ref[g:g + 1, :] == slot, memb_ref[g:g + 1, :], 0.0).astype(BF16)
        xg = _dot(p, xb_ref[...]).astype(BF16)
        gg = _dot_nt(p, gsplit_ref[...])
        hs = []
        for j in range(EXPERTS_PER_GROUP):
            e = g * EXPERTS_PER_GROUP + j
            gate = gg[:, e:e + 1] + gg[:, N_EXPERTS + e:N_EXPERTS + e + 1]
            hg = _dot(xg, wg_ref[e])
            hu = _dot(xg, wu_ref[e])
            hs.append((jax.nn.silu(hg) * hu * gate).astype(BF16))
        y = _dot(jnp.concatenate(hs, axis=1), wd_ref[g * group_w:(g + 1) * group_w, :])
        return p, y.astype(BF16)

    first = [expert_chunk(g, 0) for g in range(N_GROUPS)]
    acc_ref[...] = _dot_tn(jnp.concatenate([p for p, _ in first], axis=0),
                           jnp.concatenate([y for _, y in first], axis=0))
    largest = jnp.max(jnp.sum(memb_ref[0:SUBLANES, :], axis=1, keepdims=True))

    @pl.when(largest > rows)
    def _():
        for g in range(N_GROUPS):
            n_tok = jnp.sum(memb_ref[g:g + 1, :]).astype(jnp.int32)

            def extra(ci, carry, g=g):
                p, y = expert_chunk(g, ci)
                acc_ref[...] += _dot_tn(p, y)
                return carry

            lax.fori_loop(1, (n_tok + rows - 1) // rows, extra, 0)

    o_ref[0] = _layer_norm(alpha * x_ref[0] + acc_ref[...], g3_ref[...], b3_ref[...])


def _moe(x, logits_t, rb_t, before, wg, wu, wd, g3, b3, *, layer, tm, alpha):
    bsz, seq, d = x.shape
    tok = lambda w: pl.BlockSpec((1, tm, w), lambda b, s: (b, s, 0))
    resident = lambda w: _layer_spec(w, layer, single_buffer=True)
    return pl.pallas_call(
        functools.partial(_moe_kernel, alpha=alpha, tm=tm, rows=tm // N_GROUPS * 5 // 4),
        grid=(bsz, seq // tm),
        in_specs=[tok(d), pl.BlockSpec((1, N_EXPERTS, tm), lambda b, s: (b, 0, s)),
                  _const_spec(rb_t.shape), _const_spec(before.shape),
                  resident(wg), resident(wu), resident(wd),
                  _layer_spec(g3, layer), _layer_spec(b3, layer)],
        out_specs=tok(d),
        out_shape=jax.ShapeDtypeStruct((bsz, seq, d), F32),
        scratch_shapes=[pltpu.VMEM((tm, d), BF16),
                        pltpu.VMEM((ROUTE_ROWS, tm), F32),
                        pltpu.VMEM((ROUTE_ROWS, tm), F32),
                        pltpu.VMEM((2 * N_EXPERTS, tm), BF16),
                        pltpu.VMEM((tm, d), F32)],
        compiler_params=_params(2),
        name="moe",
    )(x, logits_t, rb_t, before, wg, wu, wd, g3, b3)


def _rot_cols(w):
    half = w.shape[-1] // 2
    return jnp.concatenate([-w[..., half:], w[..., :half]], axis=-1)


def _prep_w_in(w_in):
    k_rope = 2 * A_WIDTH + Q_LORA + KV_LORA
    head = w_in[..., :k_rope]
    kr = w_in[..., k_rope:k_rope + MLA_ROPE]
    rest = w_in[..., k_rope + MLA_ROPE:]
    rope4 = jnp.concatenate([kr, _rot_cols(kr), kr, _rot_cols(kr)], axis=-1)
    return jnp.concatenate([head, rope4, rest], axis=-1).astype(BF16)


def _prep_w_uq(w_uq):
    depth = w_uq.shape[0]
    w = w_uq.reshape(depth, Q_LORA, MLA_HEADS, MLA_NOPE + MLA_ROPE)
    nope, rope = w[..., :MLA_NOPE], w[..., MLA_NOPE:]
    ext = jnp.concatenate([nope, rope, _rot_cols(rope)], axis=-1)
    return ext.reshape(depth, Q_LORA, MLA_HEADS * QK_DIM).astype(BF16)


def _prep_w_ukv(w_ukv):
    depth = w_ukv.shape[0]
    w = w_ukv.reshape(depth, KV_LORA, MLA_HEADS, MLA_NOPE + MLA_V)
    k_nope, v = w[..., :MLA_NOPE], w[..., MLA_NOPE:]
    wk = jnp.concatenate([k_nope, jnp.zeros_like(k_nope)], axis=-1)
    return (wk.reshape(depth, KV_LORA, MLA_HEADS * QK_DIM).astype(BF16),
            v.reshape(depth, KV_LORA, MLA_WIDTH).astype(BF16))


def _prep_pool_w(pool_w):
    depth, groups = pool_w.shape[:2]
    eye = jnp.eye(groups, dtype=pool_w.dtype)
    bd = jnp.einsum('lgcd,gh->lgchd', pool_w, eye)
    return bd.reshape(depth, D_WIDTH, D_WIDTH).astype(BF16)


def kernel(x, mem, positions, w_in, gmlp_v_norm_g, gmlp_w_s, gmlp_b_s, mla_q_norm_g, mla_w_uq,
           mla_kv_norm_g, mla_w_ukv, conv_w, pool_w, pool_scale, w_out, ln1_g, ln1_b,
           xattn_wq, xattn_wk, xattn_wv, xattn_wo, ln2_g, ln2_b, router_w, router_b,
           moe_w_gate, moe_w_up, moe_w_down, ln3_g, ln3_b):
    depth = w_in.shape[0]
    alpha = (2 * depth) ** 0.25
    tm = 512
    bsz, seq, _ = x.shape

    inv_freq = ROPE_BASE ** (-jnp.arange(0, MLA_ROPE, 2, dtype=F32) / MLA_ROPE)
    ang = positions.astype(F32)[..., None] * inv_freq
    cos2 = jnp.tile(jnp.cos(ang), (1, 1, 2))
    sin2 = jnp.tile(jnp.sin(ang), (1, 1, 2))
    scale = (MLA_NOPE + MLA_ROPE) ** -0.5 * LOG2_E
    csq = scale * jnp.concatenate([jnp.ones(cos2.shape[:2] + (MLA_NOPE,), F32), cos2, sin2], -1)
    csq = jnp.swapaxes(csq.reshape(bsz, seq // tm, tm, QK_DIM), 2, 3)
    csk = jnp.concatenate([cos2, sin2, cos2, sin2], axis=-1)

    row = lambda a: a[:, None, :]
    w_in_e = _prep_w_in(w_in)
    w_uq_e = jnp.swapaxes(_prep_w_uq(mla_w_uq), 1, 2)
    w_k_e, w_v_e = _prep_w_ukv(mla_w_ukv)
    w_v_e = jnp.swapaxes(w_v_e, 1, 2)
    ws = gmlp_w_s.astype(BF16)
    bs = jnp.repeat(jnp.swapaxes(gmlp_b_s, 1, 2), A_HEAD_DIM, axis=2)
    pool_bd = _prep_pool_w(pool_w)
    wo_acd = jnp.concatenate([w_out[:, :A_WIDTH], w_out[:, A_WIDTH + MLA_WIDTH:]], axis=1).astype(BF16)
    wo_b = w_out[:, A_WIDTH:A_WIDTH + MLA_WIDTH].astype(BF16)
    wq = (xattn_wq * (X_HEAD_DIM ** -0.5 * LOG2_E)).astype(BF16)
    wo = xattn_wo.astype(BF16)
    rw_t = router_w.T
    rw_hi = rw_t.astype(BF16)
    rw = jnp.concatenate([rw_hi, (rw_t - rw_hi.astype(F32)).astype(BF16)], axis=0)
    rb_t = jnp.broadcast_to(router_b[:, None], (N_EXPERTS, tm))
    before = jnp.triu(jnp.ones((tm, tm), BF16), 1)
    wg = moe_w_gate.astype(BF16)
    wu = moe_w_up.astype(BF16)
    wd = moe_w_down.reshape(depth, N_EXPERTS * D_FF, D_MODEL).astype(BF16)

    km, vm = _mem_kv(mem, xattn_wk.astype(BF16), xattn_wv.astype(BF16))

    for l in range(depth):
        q, k, v, yacd = _mixer_in(
            x, csq, csk, w_in_e, row(gmlp_v_norm_g), ws, bs, row(mla_q_norm_g),
            w_uq_e, row(mla_kv_norm_g), w_k_e, w_v_e, conv_w, pool_bd,
            row(pool_scale), layer=l, tm=2 * tm, ta=tm)
        yb = _mla_attn(q, k, v, hg=MLA_HEADS)
        x2, logits = _post_attn(
            yacd, yb, x, wo_acd, wo_b, row(ln1_g), row(ln1_b), wq, km, vm,
            wo, row(ln2_g), row(ln2_b), rw, layer=l, tm=2 * tm, alpha=alpha)
        x = _moe(x2, logits, rb_t, before, wg, wu, wd, row(ln3_g), row(ln3_b),
                 layer=l, tm=tm, alpha=alpha)
    return x
```

```python
import functools

import jax
import jax.numpy as jnp
from jax import lax
from jax.experimental import pallas as pl
from jax.experimental.pallas import tpu as pltpu

F32 = jnp.float32
BF16 = jnp.bfloat16

D_MODEL = 1024
A_HEADS, A_HEAD_DIM, CHUNK = 4, 64, 128
A_WIDTH = A_HEADS * A_HEAD_DIM
MLA_HEADS, MLA_NOPE, MLA_ROPE, MLA_V = 8, 64, 32, 64
Q_LORA, KV_LORA = 256, 128
MLA_WIDTH = MLA_HEADS * MLA_V
ROPE_BASE = 10000.0
C_WIDTH, CONV_W = 256, 3
D_WIDTH = 256
POOL_WINDOWS = (2, 4, 8, 16)
D_GROUP = D_WIDTH // len(POOL_WINDOWS)
X_HEADS = 4
X_HEAD_DIM = D_MODEL // X_HEADS
N_EXPERTS, N_GROUPS, TOP_K = 16, 4, 2
EXPERTS_PER_GROUP = N_EXPERTS // N_GROUPS
D_FF = 256
EPS = 1e-6

LANES = 128
SUBLANES = 8
VMEM_LIMIT = 56 * 1024 * 1024

OFF_U = 0
OFF_CQ = OFF_U + 2 * A_WIDTH
OFF_BG = OFF_CQ + Q_LORA + KV_LORA + 4 * MLA_ROPE
P_EXT = OFF_BG + 3 * C_WIDTH + D_WIDTH
QK_DIM = 128
V_ROWS = 80
POOL_HALO = 16
ROUTE_ROWS = 16
NEG_BIG = -1e30
LOG2_E = 1.4426950408889634


def _dot(a, b):
    return jnp.dot(a, b, preferred_element_type=F32)


def _dot_nt(a, b):
    return lax.dot_general(a, b, (((1,), (1,)), ((), ())), preferred_element_type=F32)


def _rms(x, g):
    return x * lax.rsqrt(jnp.mean(x * x, axis=-1, keepdims=True) + EPS) * g


def _layer_norm(x, g, b):
    mu = jnp.mean(x, axis=-1, keepdims=True)
    xc = x - mu
    var = jnp.mean(xc * xc, axis=-1, keepdims=True)
    return xc * lax.rsqrt(var + EPS) * g + b


def _params(n_grid):
    return pltpu.CompilerParams(dimension_semantics=("arbitrary",) * n_grid,
                                vmem_limit_bytes=VMEM_LIMIT)


def _run_interleaved(chains):
    live = list(chains)
    while live:
        for c in list(live):
            try:
                next(c)
            except StopIteration:
                live.remove(c)


def _const_spec(shape):
    zeros = (0,) * len(shape)
    return pl.BlockSpec(shape, lambda *_: zeros)


def _layer_spec(stacked, layer, single_buffer=False):
    index = (layer,) + (0,) * (stacked.ndim - 1)
    mode = dict(pipeline_mode=pl.Buffered(1)) if single_buffer else {}
    return pl.BlockSpec((None,) + stacked.shape[1:], lambda *_: index, **mode)


def _mem_kv_kernel(mem_ref, wk_ref, wv_ref, k_ref, v_ref):
    m = mem_ref[0].astype(BF16)
    k_ref[0, 0] = _dot(m, wk_ref[0]).astype(BF16)
    v_ref[0, 0] = _dot(m, wv_ref[0]).astype(BF16)


def _mem_kv(mem, wk, wv):
    depth = wk.shape[0]
    bsz, mlen, d = mem.shape
    out = jax.ShapeDtypeStruct((depth, bsz, mlen, d), BF16)
    return pl.pallas_call(
        _mem_kv_kernel,
        grid=(depth, bsz),
        in_specs=[pl.BlockSpec((1, mlen, d), lambda l, b: (b, 0, 0)),
                  pl.BlockSpec((1, d, d), lambda l, b: (l, 0, 0)),
                  pl.BlockSpec((1, d, d), lambda l, b: (l, 0, 0))],
        out_specs=[pl.BlockSpec((1, 1, mlen, d), lambda l, b: (l, b, 0, 0)),
                   pl.BlockSpec((1, 1, mlen, d), lambda l, b: (l, b, 0, 0))],
        out_shape=[out, out],
        compiler_params=_params(2),
        name="mem_kv",
    )(mem, wk, wv)


def _mixer_in_kernel(x_ref, csq_ref, csk_ref, w_in_ref, vng_ref, ws_ref, bs_ref, qng_ref,
                     wuq_ref, kvng_ref, wk_ref, wv_ref, convw_ref, poolw_ref, pscale_ref,
                     q_ref, k_ref, v_ref, y_ref,
                     conv_buf, pool_a, pool_b, pool_carry, *, tm, ta):
    si = pl.program_id(1)

    @pl.when(si == 0)
    def _():
        pool_carry[...] = jnp.zeros((POOL_HALO, D_WIDTH), F32)
        conv_buf[0:SUBLANES, :] = jnp.zeros((SUBLANES, C_WIDTH), F32)

    xb = x_ref[0].astype(BF16)

    za = _dot(xb, w_in_ref[:, OFF_U:OFF_CQ])
    zb = _dot(xb, w_in_ref[:, OFF_CQ:OFF_BG])
    zcd = _dot(xb, w_in_ref[:, OFF_BG:P_EXT])

    za = jax.nn.gelu(za)
    u = za[:, :A_WIDTH]
    v = _rms(za[:, A_WIDTH:], vng_ref[...]).astype(BF16)
    cq = _rms(zb[:, :Q_LORA], qng_ref[...]).astype(BF16)
    ckv = _rms(zb[:, Q_LORA:Q_LORA + KV_LORA], kvng_ref[...]).astype(BF16)

    row = lax.broadcasted_iota(jnp.int32, (CHUNK, CHUNK), 0)
    col = lax.broadcasted_iota(jnp.int32, (CHUNK, CHUNK), 1)
    w_causal = jnp.concatenate(
        [jnp.where(row >= col, ws_ref[h], jnp.zeros((), BF16)) for h in range(A_HEADS)], axis=1)
    lane_head = lax.broadcasted_iota(jnp.int32, (CHUNK, A_WIDTH), 1) // A_HEAD_DIM
    for c in range(tm // CHUNK):
        rows = slice(c * CHUNK, (c + 1) * CHUNK)
        vc = v[rows]
        v_heads = jnp.concatenate(
            [jnp.where(lane_head == h, vc, jnp.zeros((), BF16)) for h in range(A_HEADS)], axis=0)
        mixed = _dot(w_causal, v_heads)
        y_ref[0, rows, 0:A_WIDTH] = (u[rows] * (mixed + bs_ref[...])).astype(BF16)

    qt = _dot_nt(wuq_ref[...], cq)
    kk = _dot(ckv, wk_ref[...])
    vt = _dot_nt(wv_ref[...], ckv)

    zd = zcd[:, 3 * C_WIDTH:]
    base = SUBLANES
    n = tm + POOL_HALO
    pool_a[0:base, :] = jnp.zeros((base, D_WIDTH), F32)
    pool_b[0:base, :] = jnp.zeros((base, D_WIDTH), F32)
    pool_a[base:base + POOL_HALO, :] = pool_carry[...]
    pool_a[base + POOL_HALO:base + n, :] = zd
    pool_carry[...] = zd[tm - POOL_HALO:, :]
    lane_d = lax.broadcasted_iota(jnp.int32, (n, D_WIDTH), 1)
    pool_b[base:base + n, :] = pool_a[base:base + n, :] + pool_a[base - 1:base - 1 + n, :]
    pool_a[base:base + n, :] = pool_b[base:base + n, :] + jnp.where(
        lane_d >= D_GROUP, pool_b[base - 2:base - 2 + n, :], 0.0)
    pool_b[base:base + n, :] = pool_a[base:base + n, :] + jnp.where(
        lane_d >= 2 * D_GROUP, pool_a[base - 4:base - 4 + n, :], 0.0)
    t0 = base + POOL_HALO
    lane_t = lax.broadcasted_iota(jnp.int32, (tm, D_WIDTH), 1)
    win_sum = pool_b[t0:t0 + tm, :] + jnp.where(
        lane_t >= 3 * D_GROUP, pool_b[t0 - 8:t0 - 8 + tm, :], 0.0)
    pos1 = si * tm + lax.broadcasted_iota(jnp.int32, (tm, D_WIDTH), 0) + 1
    window = jnp.left_shift(2, lane_t // D_GROUP)
    count = jnp.minimum(pos1, window).astype(F32)
    pooled = (win_sum / count - zd).astype(BF16)
    yd = _dot(pooled, poolw_ref[...]) * pscale_ref[...]
    y_ref[0, :, A_WIDTH + C_WIDTH:] = yd.astype(BF16)

    gh = zcd[:, C_WIDTH:2 * C_WIDTH] * zcd[:, 2 * C_WIDTH:3 * C_WIDTH]
    conv_buf[SUBLANES:SUBLANES + tm, :] = gh
    conv = (convw_ref[2:3, :] * gh
            + convw_ref[1:2, :] * conv_buf[SUBLANES - 1:SUBLANES - 1 + tm, :]
            + convw_ref[0:1, :] * conv_buf[SUBLANES - 2:SUBLANES - 2 + tm, :])
    y_ref[0, :, A_WIDTH:A_WIDTH + C_WIDTH] = (zcd[:, :C_WIDTH] * conv).astype(BF16)
    conv_buf[0:SUBLANES, :] = conv_buf[tm:tm + SUBLANES, :]

    for t in range(tm // ta):
        cols = slice(t * ta, (t + 1) * ta)
        csq = csq_ref[0, t]
        for h in range(MLA_HEADS):
            q_ref[0, h, t] = (qt[h * QK_DIM:(h + 1) * QK_DIM, cols] * csq).astype(BF16)
    r4 = zb[:, Q_LORA + KV_LORA:] * csk_ref[0]
    kr = r4 + pltpu.roll(r4, MLA_ROPE, 1)
    lane = lax.broadcasted_iota(jnp.int32, (tm, QK_DIM), 1)
    kr = jnp.where(lane >= MLA_NOPE, kr, 0.0)
    for h in range(MLA_HEADS):
        k_ref[0, h] = (kk[:, h * QK_DIM:(h + 1) * QK_DIM] + kr).astype(BF16)
    ones_then_zeros = jnp.where(
        lax.broadcasted_iota(jnp.int32, (V_ROWS - MLA_V, ta), 0) == 0, 1.0, 0.0).astype(BF16)
    for t in range(tm // ta):
        cols = slice(t * ta, (t + 1) * ta)
        for h in range(MLA_HEADS):
            v_ref[0, h, t, 0:MLA_V, :] = vt[h * MLA_V:(h + 1) * MLA_V, cols].astype(BF16)
            v_ref[0, h, t, MLA_V:, :] = ones_then_zeros


def _mixer_in(x, csq, csk, w_in, vng, ws, bs, qng, wuq, kvng, wk, wv, convw, poolw, pscale,
              *, layer, tm, ta):
    bsz, seq, d = x.shape
    tok = lambda w: pl.BlockSpec((1, tm, w), lambda b, s: (b, s, 0))
    head = lambda n, w: pl.BlockSpec((1, n, tm, w), lambda b, s: (b, 0, s, 0))
    per = tm // ta
    head_t = lambda r: pl.BlockSpec((1, MLA_HEADS, per, r, ta), lambda b, s: (b, 0, s, 0, 0))
    shape_t = lambda r: jax.ShapeDtypeStruct((bsz, MLA_HEADS, seq // ta, r, ta), BF16)
    return pl.pallas_call(
        functools.partial(_mixer_in_kernel, tm=tm, ta=ta),
        grid=(bsz, seq // tm),
        in_specs=[tok(d), pl.BlockSpec((1, per, QK_DIM, ta), lambda b, s: (b, s, 0, 0)), tok(QK_DIM),
                  *[_layer_spec(w, layer) for w in (w_in, vng, ws, bs, qng, wuq, kvng, wk, wv,
                                                     convw, poolw, pscale)]],
        out_specs=[head_t(QK_DIM), head(MLA_HEADS, QK_DIM), head_t(V_ROWS),
                   tok(A_WIDTH + C_WIDTH + D_WIDTH)],
        out_shape=[shape_t(QK_DIM),
                   jax.ShapeDtypeStruct((bsz, MLA_HEADS, seq, QK_DIM), BF16),
                   shape_t(V_ROWS),
                   jax.ShapeDtypeStruct((bsz, seq, A_WIDTH + C_WIDTH + D_WIDTH), BF16)],
        scratch_shapes=[pltpu.VMEM((tm + SUBLANES, C_WIDTH), F32),
                        pltpu.VMEM((tm + POOL_HALO + SUBLANES, D_WIDTH), F32),
                        pltpu.VMEM((tm + POOL_HALO + SUBLANES, D_WIDTH), F32),
                        pltpu.VMEM((POOL_HALO, D_WIDTH), F32)],
        compiler_params=_params(2),
        name="mixer_in",
    )(x, csq, csk, w_in, vng, ws, bs, qng, wuq, kvng, wk, wv, convw, poolw, pscale)


def _mla_attn_kernel(qt_ref, k_ref, vt_ref, o_ref, m_ref, acc_ref, *, tq, hg):
    qi = pl.program_id(2)
    half = tq // 2
    m_ref[...] = jnp.full(m_ref.shape, NEG_BIG, F32)
    acc_ref[...] = jnp.zeros(acc_ref.shape, F32)

    def run_blocks(blocks):
        pieces = []
        for j, diagonal in blocks:
            pieces += [(j, diagonal, hh, 0, 0) for hh in range(hg)]
            pieces += [(j, diagonal, hh, half, half if diagonal else 0) for hh in range(hg)]

        def scores(piece):
            j, _, hh, k0, q0 = piece
            base = pl.multiple_of(j * tq, tq)
            return _dot(k_ref[0, hh, pl.ds(base + k0, half), :], qt_ref[0, hh, 0, :, q0:])

        def softmax(piece, st):
            _, diagonal, hh, _, q0 = piece
            if diagonal:
                key = lax.broadcasted_iota(jnp.int32, st.shape, 0)
                qry = lax.broadcasted_iota(jnp.int32, st.shape, 1)
                st = jnp.where(key <= qry, st, NEG_BIG)
            m_old = m_ref[hh, :, q0:]
            m_new = jnp.maximum(m_old, jnp.max(st, axis=0, keepdims=True))
            m_ref[hh, :, q0:] = m_new
            return piece, jnp.exp2(st - m_new).astype(BF16), jnp.exp2(m_old - m_new)

        def weighted_values(piece, probs, rescale):
            j, _, hh, k0, q0 = piece
            pv = _dot(vt_ref[0, hh, j, :, k0:k0 + half], probs)
            acc_ref[hh, :, q0:] = rescale * acc_ref[hh, :, q0:] + pv

        st_next = scores(pieces[0])
        pending = None
        for i, piece in enumerate(pieces):
            st = st_next
            if i + 1 < len(pieces):
                st_next = scores(pieces[i + 1])
            if pending is not None:
                weighted_values(*pending)
            pending = softmax(piece, st)
        weighted_values(*pending)

    def pair(jj, carry):
        run_blocks([(2 * jj, False), (2 * jj + 1, False)])
        return carry

    lax.fori_loop(0, qi // 2, pair, 0)

    @pl.when(qi % 2 == 1)
    def _():
        run_blocks([(qi - 1, False), (qi, True)])

    @pl.when(qi % 2 == 0)
    def _():
        run_blocks([(qi, True)])

    for pr in range(hg // 2):
        halves = []
        for hh in (2 * pr, 2 * pr + 1):
            acc = acc_ref[hh]
            halves.append(acc[:MLA_V] / acc[MLA_V:MLA_V + 1])
        o_ref[0, :, pr * LANES:(pr + 1) * LANES] = jnp.concatenate(halves, axis=0).T.astype(BF16)


def _mla_attn(qt, k, vt, *, hg):
    bsz, heads, n_tiles, _, tq = qt.shape
    seq = n_tiles * tq
    return pl.pallas_call(
        functools.partial(_mla_attn_kernel, tq=tq, hg=hg),
        grid=(bsz, heads // hg, n_tiles),
        in_specs=[pl.BlockSpec((1, hg, 1, QK_DIM, tq), lambda b, g, i: (b, g, i, 0, 0)),
                  pl.BlockSpec((1, hg, seq, QK_DIM), lambda b, g, i: (b, g, 0, 0)),
                  pl.BlockSpec((1, hg, n_tiles, V_ROWS, tq), lambda b, g, i: (b, g, 0, 0, 0))],
        out_specs=pl.BlockSpec((1, tq, hg * MLA_V), lambda b, g, i: (b, i, g)),
        out_shape=jax.ShapeDtypeStruct((bsz, seq, MLA_WIDTH), BF16),
        scratch_shapes=[pltpu.VMEM((hg, 1, tq), F32), pltpu.VMEM((hg, V_ROWS, tq), F32)],
        compiler_params=_params(3),
        name="mla_attn",
    )(qt, k, vt)


def _route_t(logits_t, bias_t):
    scores = jax.nn.sigmoid(logits_t)
    sel_all = scores + bias_t
    sel = [sel_all[e:e + 1] for e in range(N_EXPERTS)]
    group_score = []
    for g in range(N_GROUPS):
        v = sel[g * EXPERTS_PER_GROUP:(g + 1) * EXPERTS_PER_GROUP]
        best = None
        for i in range(EXPERTS_PER_GROUP):
            for j in range(i + 1, EXPERTS_PER_GROUP):
                best = v[i] + v[j] if best is None else jnp.maximum(best, v[i] + v[j])
        group_score.append(best)
    top, top_idx = group_score[0], jnp.zeros(group_score[0].shape, jnp.int32)
    for g in range(1, N_GROUPS):
        better = group_score[g] > top
        top = jnp.where(better, group_score[g], top)
        top_idx = jnp.where(better, g, top_idx)
    member = [jnp.where(top_idx == g, 1.0, 0.0) for g in range(N_GROUPS)]
    weights = []
    for e in range(N_EXPERTS):
        g = e // EXPERTS_PER_GROUP
        rank = jnp.zeros(sel[e].shape, jnp.int32)
        for j in range(g * EXPERTS_PER_GROUP, (g + 1) * EXPERTS_PER_GROUP):
            if j != e:
                rank = rank + jnp.where(sel[j] > sel[e], 1,
                                        jnp.where(sel[j] == sel[e], 1 if j < e else 0, 0))
        weights.append(jnp.where(rank < TOP_K, member[g], 0.0) * scores[e:e + 1])
    total = weights[0]
    for w in weights[1:]:
        total = total + w
    return [w / total for w in weights], member


def _post_attn_kernel(yacd_ref, yb_ref, x_ref, wo_acd_ref, wo_b_ref, g1_ref, b1_ref, wq_ref,
                      km_ref, vm_ref, wo_ref, g2_ref, b2_ref, rw_ref,
                      x2_ref, logits_ref, *, tm, alpha, n_chains):
    def chain(rows):
        h = _dot(yacd_ref[0, rows], wo_acd_ref[...]) + _dot(yb_ref[0, rows], wo_b_ref[...])
        yield
        x1 = _layer_norm(alpha * x_ref[0, rows] + h, g1_ref[...], b1_ref[...])
        q = _dot(x1.astype(BF16), wq_ref[...])
        yield
        heads = []
        for hd in range(X_HEADS):
            cols = slice(hd * X_HEAD_DIM, (hd + 1) * X_HEAD_DIM)
            s = _dot_nt(q[:, cols].astype(BF16), km_ref[0, :, cols])
            e = jnp.exp2(s - jnp.max(s, axis=-1, keepdims=True))
            o = _dot(e.astype(BF16), vm_ref[0, :, cols]) / jnp.sum(e, axis=-1, keepdims=True)
            heads.append(o.astype(BF16))
            yield
        h2 = _dot(jnp.concatenate(heads, axis=1), wo_ref[...])
        yield
        x2 = _layer_norm(alpha * x1 + h2, g2_ref[...], b2_ref[...])
        x2_ref[0, rows] = x2
        x_hi = x2.astype(BF16)
        x_lo = (x2 - x_hi.astype(F32)).astype(BF16)
        hl = _dot_nt(rw_ref[...], x_hi)
        lo = _dot_nt(rw_ref[0:N_EXPERTS, :], x_lo)
        logits_ref[0, :, rows] = hl[:N_EXPERTS] + hl[N_EXPERTS:] + lo

    rows_per = tm // n_chains
    _run_interleaved([chain(slice(c * rows_per, (c + 1) * rows_per)) for c in range(n_chains)])


def _post_attn(yacd, yb, x, wo_acd, wo_b, g1, b1, wq, km, vm, wo, g2, b2, rw,
               *, layer, tm, alpha):
    bsz, seq, d = x.shape
    tok = lambda w: pl.BlockSpec((1, tm, w), lambda b, s: (b, s, 0))
    memspec = pl.BlockSpec((None, 1) + km.shape[2:], lambda b, s: (layer, b, 0, 0))
    per_layer = lambda w: _layer_spec(w, layer)
    return pl.pallas_call(
        functools.partial(_post_attn_kernel, tm=tm, alpha=alpha, n_chains=2),
        grid=(bsz, seq // tm),
        in_specs=[tok(yacd.shape[-1]), tok(yb.shape[-1]), tok(d),
                  per_layer(wo_acd), per_layer(wo_b), per_layer(g1), per_layer(b1), per_layer(wq),
                  memspec, memspec, per_layer(wo), per_layer(g2), per_layer(b2),
                  _const_spec(rw.shape)],
        out_specs=[tok(d), pl.BlockSpec((1, N_EXPERTS, tm), lambda b, s: (b, 0, s))],
        out_shape=[jax.ShapeDtypeStruct((bsz, seq, d), F32),
                   jax.ShapeDtypeStruct((bsz, N_EXPERTS, seq), F32)],
        compiler_params=_params(2),
        name="post_attn",
    )(yacd, yb, x, wo_acd, wo_b, g1, b1, wq, km, vm, wo, g2, b2, rw)


def _dot_tn(a, b):
    return lax.dot_general(a, b, (((0,), (0,)), ((), ())), preferred_element_type=F32)


def _moe_kernel(x_ref, logits_ref, rb_ref, before_ref, wg_ref, wu_ref, wd_ref, g3_ref, b3_ref, o_ref,
                xb_ref, pos_ref, memb_ref, gsplit_ref, acc_ref, *, alpha, tm, rows):
    group_w = EXPERTS_PER_GROUP * D_FF
    x = x_ref[0]
    xb_ref[...] = x.astype(BF16)
    gates, member = _route_t(logits_ref[0], rb_ref[...])
    memb = jnp.concatenate(member + [jnp.zeros((ROUTE_ROWS - N_GROUPS, tm), F32)], axis=0)
    memb_ref[...] = memb
    pos_ref[...] = _dot(memb.astype(BF16), before_ref[...])
    gate_rows = jnp.concatenate(gates, axis=0)
    g_hi = gate_rows.astype(BF16)
    gsplit_ref[...] = jnp.concatenate([g_hi, (gate_rows - g_hi.astype(F32)).astype(BF16)], axis=0)

    def expert_chunk(g, ci):
        slot = (lax.broadcasted_iota(jnp.int32, (rows, tm), 0) + ci * rows).astype(F32)
        p = jnp.where(pos_ref[g:g + 1, :] == slot, memb_ref[g:g + 1, :], 0.0).astype(BF16)
        xg = _dot(p, xb_ref[...]).astype(BF16)
        gg = _dot_nt(p, gsplit_ref[...])
        hs = []
        for j in range(EXPERTS_PER_GROUP):
            e = g * EXPERTS_PER_GROUP + j
            gate = gg[:, e:e + 1] + gg[:, N_EXPERTS + e:N_EXPERTS + e + 1]
            hg = _dot(xg, wg_ref[e])
            hu = _dot(xg, wu_ref[e])
            hs.append((jax.nn.silu(hg) * hu * gate).astype(BF16))
        y = _dot(jnp.concatenate(hs, axis=1), wd_ref[g * group_w:(g + 1) * group_w, :])
        return p, y.astype(BF16)

    first = [expert_chunk(g, 0) for g in range(N_GROUPS)]
    acc_ref[...] = _dot_tn(jnp.concatenate([p for p, _ in first], axis=0),
                           jnp.concatenate([y for _, y in first], axis=0))
    largest = jnp.max(jnp.sum(memb_ref[0:SUBLANES, :], axis=1, keepdims=True))

    @pl.when(largest > rows)
    def _():
        for g in range(N_GROUPS):
            n_tok = jnp.sum(memb_ref[g:g + 1, :]).astype(jnp.int32)

            def extra(ci, carry, g=g):
                p, y = expert_chunk(g, ci)
                acc_ref[...] += _dot_tn(p, y)
                return carry

            lax.fori_loop(1, (n_tok + rows - 1) // rows, extra, 0)

    o_ref[0] = _layer_norm(alpha * x_ref[0] + acc_ref[...], g3_ref[...], b3_ref[...])


def _moe(x, logits_t, rb_t, before, wg, wu, wd, g3, b3, *, layer, tm, alpha):
    bsz, seq, d = x.shape
    tok = lambda w: pl.BlockSpec((1, tm, w), lambda b, s: (b, s, 0))
    resident = lambda w: _layer_spec(w, layer, single_buffer=True)
    return pl.pallas_call(
        functools.partial(_moe_kernel, alpha=alpha, tm=tm, rows=tm // N_GROUPS * 5 // 4),
        grid=(bsz, seq // tm),
        in_specs=[tok(d), pl.BlockSpec((1, N_EXPERTS, tm), lambda b, s: (b, 0, s)),
                  _const_spec(rb_t.shape), _const_spec(before.shape),
                  resident(wg), resident(wu), resident(wd),
                  _layer_spec(g3, layer), _layer_spec(b3, layer)],
        out_specs=tok(d),
        out_shape=jax.ShapeDtypeStruct((bsz, seq, d), F32),
        scratch_shapes=[pltpu.VMEM((tm, d), BF16),
                        pltpu.VMEM((ROUTE_ROWS, tm), F32),
                        pltpu.VMEM((ROUTE_ROWS, tm), F32),
                        pltpu.VMEM((2 * N_EXPERTS, tm), BF16),
                        pltpu.VMEM((tm, d), F32)],
        compiler_params=_params(2),
        name="moe",
    )(x, logits_t, rb_t, before, wg, wu, wd, g3, b3)


def _rot_cols(w):
    half = w.shape[-1] // 2
    return jnp.concatenate([-w[..., half:], w[..., :half]], axis=-1)


def _prep_w_in(w_in):
    k_rope = 2 * A_WIDTH + Q_LORA + KV_LORA
    head = w_in[..., :k_rope]
    kr = w_in[..., k_rope:k_rope + MLA_ROPE]
    rest = w_in[..., k_rope + MLA_ROPE:]
    rope4 = jnp.concatenate([kr, _rot_cols(kr), kr, _rot_cols(kr)], axis=-1)
    return jnp.concatenate([head, rope4, rest], axis=-1)


def _prep_w_uq(w_uq):
    depth = w_uq.shape[0]
    w = w_uq.reshape(depth, Q_LORA, MLA_HEADS, MLA_NOPE + MLA_ROPE)
    nope, rope = w[..., :MLA_NOPE], w[..., MLA_NOPE:]
    ext = jnp.concatenate([nope, rope, _rot_cols(rope)], axis=-1)
    return ext.reshape(depth, Q_LORA, MLA_HEADS * QK_DIM)


def _prep_w_ukv(w_ukv):
    depth = w_ukv.shape[0]
    w = w_ukv.reshape(depth, KV_LORA, MLA_HEADS, MLA_NOPE + MLA_V)
    k_nope, v = w[..., :MLA_NOPE], w[..., MLA_NOPE:]
    wk = jnp.concatenate([k_nope, jnp.zeros_like(k_nope)], axis=-1)
    return (wk.reshape(depth, KV_LORA, MLA_HEADS * QK_DIM),
            v.reshape(depth, KV_LORA, MLA_WIDTH))


def _prep_pool_w(pool_w):
    depth, groups = pool_w.shape[:2]
    eye = jnp.eye(groups, dtype=pool_w.dtype)
    bd = jnp.einsum('lgcd,gh->lgchd', pool_w, eye)
    return bd.reshape(depth, D_WIDTH, D_WIDTH)


def kernel(x, mem, positions, w_in, gmlp_v_norm_g, gmlp_w_s, gmlp_b_s, mla_q_norm_g, mla_w_uq,
           mla_kv_norm_g, mla_w_ukv, conv_w, pool_w, pool_scale, w_out, ln1_g, ln1_b,
           xattn_wq, xattn_wk, xattn_wv, xattn_wo, ln2_g, ln2_b, router_w, router_b,
           moe_w_gate, moe_w_up, moe_w_down, ln3_g, ln3_b):
    depth = w_in.shape[0]
    alpha = (2 * depth) ** 0.25
    tm = 512
    bsz, seq, _ = x.shape

    inv_freq = ROPE_BASE ** (-jnp.arange(0, MLA_ROPE, 2, dtype=F32) / MLA_ROPE)
    ang = positions.astype(F32)[..., None] * inv_freq
    cos2 = jnp.tile(jnp.cos(ang), (1, 1, 2))
    sin2 = jnp.tile(jnp.sin(ang), (1, 1, 2))
    scale = (MLA_NOPE + MLA_ROPE) ** -0.5 * LOG2_E
    csq = scale * jnp.concatenate([jnp.ones(cos2.shape[:2] + (MLA_NOPE,), F32), cos2, sin2], -1)
    csq = jnp.swapaxes(csq.reshape(bsz, seq // tm, tm, QK_DIM), 2, 3)
    csk = jnp.concatenate([cos2, sin2, cos2, sin2], axis=-1)

    row = lambda a: a[:, None, :]
    w_in_e = _prep_w_in(w_in.astype(BF16))
    w_uq_e = jnp.swapaxes(_prep_w_uq(mla_w_uq.astype(BF16)), 1, 2)
    w_k_e, w_v_e = _prep_w_ukv(mla_w_ukv.astype(BF16))
    w_v_e = jnp.swapaxes(w_v_e, 1, 2)
    ws = gmlp_w_s.astype(BF16)
    bs = jnp.repeat(jnp.swapaxes(gmlp_b_s, 1, 2), A_HEAD_DIM, axis=2)
    pool_bd = _prep_pool_w(pool_w.astype(BF16))
    w_out = w_out.astype(BF16)
    wo_acd = jnp.concatenate([w_out[:, :A_WIDTH], w_out[:, A_WIDTH + MLA_WIDTH:]], axis=1)
    wo_b = w_out[:, A_WIDTH:A_WIDTH + MLA_WIDTH]
    wq = (xattn_wq * (X_HEAD_DIM ** -0.5 * LOG2_E)).astype(BF16)
    wo = xattn_wo.astype(BF16)
    rw_t = router_w.T
    rw_hi = rw_t.astype(BF16)
    rw = jnp.concatenate([rw_hi, (rw_t - rw_hi.astype(F32)).astype(BF16)], axis=0)
    rb_t = jnp.broadcast_to(router_b[:, None], (N_EXPERTS, tm))
    before = jnp.triu(jnp.ones((tm, tm), BF16), 1)
    wg = moe_w_gate.astype(BF16)
    wu = moe_w_up.astype(BF16)
    wd = moe_w_down.reshape(depth, N_EXPERTS * D_FF, D_MODEL).astype(BF16)

    km, vm = _mem_kv(mem, xattn_wk.astype(BF16), xattn_wv.astype(BF16))

    for l in range(depth):
        q, k, v, yacd = _mixer_in(
            x, csq, csk, w_in_e, row(gmlp_v_norm_g), ws, bs, row(mla_q_norm_g),
            w_uq_e, row(mla_kv_norm_g), w_k_e, w_v_e, conv_w, pool_bd,
            row(pool_scale), layer=l, tm=2 * tm, ta=tm)
        yb = _mla_attn(q, k, v, hg=MLA_HEADS)
        x2, logits = _post_attn(
            yacd, yb, x, wo_acd, wo_b, row(ln1_g), row(ln1_b), wq, km, vm,
            wo, row(ln2_g), row(ln2_b), rw, layer=l, tm=2 * tm, alpha=alpha)
        x = _moe(x2, logits, rb_t, before, wg, wu, wd, row(ln3_g), row(ln3_b),
                 layer=l, tm=tm, alpha=alpha)
    return x
```

```python
import functools

import jax
import jax.numpy as jnp
from jax import lax
from jax.experimental import pallas as pl
from jax.experimental.pallas import tpu as pltpu

F32 = jnp.float32
BF16 = jnp.bfloat16

D_MODEL = 1024
A_HEADS, A_HEAD_DIM, CHUNK = 4, 64, 128
A_WIDTH = A_HEADS * A_HEAD_DIM
MLA_HEADS, MLA_NOPE, MLA_ROPE, MLA_V = 8, 64, 32, 64
Q_LORA, KV_LORA = 256, 128
MLA_WIDTH = MLA_HEADS * MLA_V
ROPE_BASE = 10000.0
C_WIDTH, CONV_W = 256, 3
D_WIDTH = 256
POOL_WINDOWS = (2, 4, 8, 16)
D_GROUP = D_WIDTH // len(POOL_WINDOWS)
X_HEADS = 4
X_HEAD_DIM = D_MODEL // X_HEADS
N_EXPERTS, N_GROUPS, TOP_K = 16, 4, 2
EXPERTS_PER_GROUP = N_EXPERTS // N_GROUPS
D_FF = 256
EPS = 1e-6

LANES = 128
SUBLANES = 8
VMEM_LIMIT = 56 * 1024 * 1024

OFF_U = 0
OFF_CQ = OFF_U + 2 * A_WIDTH
OFF_BG = OFF_CQ + Q_LORA + KV_LORA + 4 * MLA_ROPE
P_EXT = OFF_BG + 3 * C_WIDTH + D_WIDTH
QK_DIM = 128
V_ROWS = 80
POOL_HALO = 16
ROUTE_ROWS = 16
NEG_BIG = -1e30
LOG2_E = 1.4426950408889634


def _dot(a, b):
    return jnp.dot(a, b, preferred_element_type=F32)


def _dot_nt(a, b):
    return lax.dot_general(a, b, (((1,), (1,)), ((), ())), preferred_element_type=F32)


def _rms(x, g):
    return x * lax.rsqrt(jnp.mean(x * x, axis=-1, keepdims=True) + EPS) * g


def _layer_norm(x, g, b):
    mu = jnp.mean(x, axis=-1, keepdims=True)
    xc = x - mu
    var = jnp.mean(xc * xc, axis=-1, keepdims=True)
    return xc * lax.rsqrt(var + EPS) * g + b


def _params(n_grid):
    return pltpu.CompilerParams(dimension_semantics=("arbitrary",) * n_grid,
                                vmem_limit_bytes=VMEM_LIMIT)


def _run_interleaved(chains):
    live = list(chains)
    while live:
        for c in list(live):
            try:
                next(c)
            except StopIteration:
                live.remove(c)


def _const_spec(shape):
    zeros = (0,) * len(shape)
    return pl.BlockSpec(shape, lambda *_: zeros)


def _layer_spec(stacked, layer, single_buffer=False):
    index = (layer,) + (0,) * (stacked.ndim - 1)
    mode = dict(pipeline_mode=pl.Buffered(1)) if single_buffer else {}
    return pl.BlockSpec((None,) + stacked.shape[1:], lambda *_: index, **mode)


def _mem_kv_kernel(mem_ref, wk_ref, wv_ref, k_ref, v_ref):
    m = mem_ref[0].astype(BF16)
    k_ref[0, 0] = _dot(m, wk_ref[0]).astype(BF16)
    v_ref[0, 0] = _dot(m, wv_ref[0]).astype(BF16)


def _mem_kv(mem, wk, wv):
    depth = wk.shape[0]
    bsz, mlen, d = mem.shape
    out = jax.ShapeDtypeStruct((depth, bsz, mlen, d), BF16)
    return pl.pallas_call(
        _mem_kv_kernel,
        grid=(depth, bsz),
        in_specs=[pl.BlockSpec((1, mlen, d), lambda l, b: (b, 0, 0)),
                  pl.BlockSpec((1, d, d), lambda l, b: (l, 0, 0)),
                  pl.BlockSpec((1, d, d), lambda l, b: (l, 0, 0))],
        out_specs=[pl.BlockSpec((1, 1, mlen, d), lambda l, b: (l, b, 0, 0)),
                   pl.BlockSpec((1, 1, mlen, d), lambda l, b: (l, b, 0, 0))],
        out_shape=[out, out],
        compiler_params=_params(2),
        name="mem_kv",
    )(mem, wk, wv)


def _mixer_in_kernel(x_ref, csq_ref, csk_ref, w_in_ref, vng_ref, ws_ref, bs_ref, qng_ref,
                     wuq_ref, kvng_ref, wk_ref, wv_ref, convw_ref, poolw_ref, pscale_ref,
                     q_ref, k_ref, v_ref, y_ref,
                     conv_buf, pool_a, pool_b, pool_carry, *, tm, ta):
    si = pl.program_id(1)

    @pl.when(si == 0)
    def _():
        pool_carry[...] = jnp.zeros((POOL_HALO, D_WIDTH), F32)
        conv_buf[0:SUBLANES, :] = jnp.zeros((SUBLANES, C_WIDTH), F32)

    xb = x_ref[0].astype(BF16)

    za = _dot(xb, w_in_ref[:, OFF_U:OFF_CQ])
    zb = _dot(xb, w_in_ref[:, OFF_CQ:OFF_BG])
    zcd = _dot(xb, w_in_ref[:, OFF_BG:P_EXT])

    za = jax.nn.gelu(za)
    u = za[:, :A_WIDTH]
    v = _rms(za[:, A_WIDTH:], vng_ref[...]).astype(BF16)
    cq = _rms(zb[:, :Q_LORA], qng_ref[...]).astype(BF16)
    ckv = _rms(zb[:, Q_LORA:Q_LORA + KV_LORA], kvng_ref[...]).astype(BF16)

    row = lax.broadcasted_iota(jnp.int32, (CHUNK, CHUNK), 0)
    col = lax.broadcasted_iota(jnp.int32, (CHUNK, CHUNK), 1)
    w_causal = jnp.concatenate(
        [jnp.where(row >= col, ws_ref[h], jnp.zeros((), BF16)) for h in range(A_HEADS)], axis=1)
    lane_head = lax.broadcasted_iota(jnp.int32, (CHUNK, A_WIDTH), 1) // A_HEAD_DIM
    for c in range(tm // CHUNK):
        rows = slice(c * CHUNK, (c + 1) * CHUNK)
        vc = v[rows]
        v_heads = jnp.concatenate(
            [jnp.where(lane_head == h, vc, jnp.zeros((), BF16)) for h in range(A_HEADS)], axis=0)
        mixed = _dot(w_causal, v_heads)
        y_ref[0, rows, 0:A_WIDTH] = (u[rows] * (mixed + bs_ref[...])).astype(BF16)

    qt = _dot_nt(wuq_ref[...], cq)
    kk = _dot(ckv, wk_ref[...])
    vt = _dot_nt(wv_ref[...], ckv)

    zd = zcd[:, 3 * C_WIDTH:]
    base = SUBLANES
    n = tm + POOL_HALO
    pool_a[0:base, :] = jnp.zeros((base, D_WIDTH), F32)
    pool_b[0:base, :] = jnp.zeros((base, D_WIDTH), F32)
    pool_a[base:base + POOL_HALO, :] = pool_carry[...]
    pool_a[base + POOL_HALO:base + n, :] = zd
    pool_carry[...] = zd[tm - POOL_HALO:, :]
    lane_d = lax.broadcasted_iota(jnp.int32, (n, D_WIDTH), 1)
    pool_b[base:base + n, :] = pool_a[base:base + n, :] + pool_a[base - 1:base - 1 + n, :]
    pool_a[base:base + n, :] = pool_b[base:base + n, :] + jnp.where(
        lane_d >= D_GROUP, pool_b[base - 2:base - 2 + n, :], 0.0)
    pool_b[base:base + n, :] = pool_a[base:base + n, :] + jnp.where(
        lane_d >= 2 * D_GROUP, pool_a[base - 4:base - 4 + n, :], 0.0)
    t0 = base + POOL_HALO
    lane_t = lax.broadcasted_iota(jnp.int32, (tm, D_WIDTH), 1)
    win_sum = pool_b[t0:t0 + tm, :] + jnp.where(
        lane_t >= 3 * D_GROUP, pool_b[t0 - 8:t0 - 8 + tm, :], 0.0)
    pos1 = si * tm + lax.broadcasted_iota(jnp.int32, (tm, D_WIDTH), 0) + 1
    window = jnp.left_shift(2, lane_t // D_GROUP)
    count = jnp.minimum(pos1, window).astype(F32)
    pooled = (win_sum / count - zd).astype(BF16)
    yd = _dot(pooled, poolw_ref[...]) * pscale_ref[...]
    y_ref[0, :, A_WIDTH + C_WIDTH:] = yd.astype(BF16)

    gh = zcd[:, C_WIDTH:2 * C_WIDTH] * zcd[:, 2 * C_WIDTH:3 * C_WIDTH]
    conv_buf[SUBLANES:SUBLANES + tm, :] = gh
    conv = (convw_ref[2:3, :] * gh
            + convw_ref[1:2, :] * conv_buf[SUBLANES - 1:SUBLANES - 1 + tm, :]
            + convw_ref[0:1, :] * conv_buf[SUBLANES - 2:SUBLANES - 2 + tm, :])
    y_ref[0, :, A_WIDTH:A_WIDTH + C_WIDTH] = (zcd[:, :C_WIDTH] * conv).astype(BF16)
    conv_buf[0:SUBLANES, :] = conv_buf[tm:tm + SUBLANES, :]

    for t in range(tm // ta):
        cols = slice(t * ta, (t + 1) * ta)
        csq = csq_ref[0, t]
        for h in range(MLA_HEADS):
            q_ref[0, h, t] = (qt[h * QK_DIM:(h + 1) * QK_DIM, cols] * csq).astype(BF16)
    r4 = zb[:, Q_LORA + KV_LORA:] * csk_ref[0]
    kr = r4 + pltpu.roll(r4, MLA_ROPE, 1)
    lane = lax.broadcasted_iota(jnp.int32, (tm, QK_DIM), 1)
    kr = jnp.where(lane >= MLA_NOPE, kr, 0.0)
    for h in range(MLA_HEADS):
        k_ref[0, h] = (kk[:, h * QK_DIM:(h + 1) * QK_DIM] + kr).astype(BF16)
    ones_then_zeros = jnp.where(
        lax.broadcasted_iota(jnp.int32, (V_ROWS - MLA_V, ta), 0) == 0, 1.0, 0.0).astype(BF16)
    for t in range(tm // ta):
        cols = slice(t * ta, (t + 1) * ta)
        for h in range(MLA_HEADS):
            v_ref[0, h, t, 0:MLA_V, :] = vt[h * MLA_V:(h + 1) * MLA_V, cols].astype(BF16)
            v_ref[0, h, t, MLA_V:, :] = ones_then_zeros


def _mixer_in(x, csq, csk, w_in, vng, ws, bs, qng, wuq, kvng, wk, wv, convw, poolw, pscale,
              *, layer, tm, ta):
    bsz, seq, d = x.shape
    tok = lambda w: pl.BlockSpec((1, tm, w), lambda b, s: (b, s, 0))
    head = lambda n, w: pl.BlockSpec((1, n, tm, w), lambda b, s: (b, 0, s, 0))
    per = tm // ta
    head_t = lambda r: pl.BlockSpec((1, MLA_HEADS, per, r, ta), lambda b, s: (b, 0, s, 0, 0))
    shape_t = lambda r: jax.ShapeDtypeStruct((bsz, MLA_HEADS, seq // ta, r, ta), BF16)
    return pl.pallas_call(
        functools.partial(_mixer_in_kernel, tm=tm, ta=ta),
        grid=(bsz, seq // tm),
        in_specs=[tok(d), pl.BlockSpec((1, per, QK_DIM, ta), lambda b, s: (b, s, 0, 0)), tok(QK_DIM),
                  *[_layer_spec(w, layer) for w in (w_in, vng, ws, bs, qng, wuq, kvng, wk, wv,
                                                     convw, poolw, pscale)]],
        out_specs=[head_t(QK_DIM), head(MLA_HEADS, QK_DIM), head_t(V_ROWS),
                   tok(A_WIDTH + C_WIDTH + D_WIDTH)],
        out_shape=[shape_t(QK_DIM),
                   jax.ShapeDtypeStruct((bsz, MLA_HEADS, seq, QK_DIM), BF16),
                   shape_t(V_ROWS),
                   jax.ShapeDtypeStruct((bsz, seq, A_WIDTH + C_WIDTH + D_WIDTH), BF16)],
        scratch_shapes=[pltpu.VMEM((tm + SUBLANES, C_WIDTH), F32),
                        pltpu.VMEM((tm + POOL_HALO + SUBLANES, D_WIDTH), F32),
                        pltpu.VMEM((tm + POOL_HALO + SUBLANES, D_WIDTH), F32),
                        pltpu.VMEM((POOL_HALO, D_WIDTH), F32)],
        compiler_params=_params(2),
        name="mixer_in",
    )(x, csq, csk, w_in, vng, ws, bs, qng, wuq, kvng, wk, wv, convw, poolw, pscale)


def _mla_attn_kernel(qt_ref, k_ref, vt_ref, o_ref, m_ref, acc_ref, *, tq, hg):
    qi = pl.program_id(2)
    half = tq // 2
    m_ref[...] = jnp.full(m_ref.shape, NEG_BIG, F32)
    acc_ref[...] = jnp.zeros(acc_ref.shape, F32)

    def run_blocks(blocks):
        pieces = []
        for j, diagonal in blocks:
            pieces += [(j, diagonal, hh, 0, 0) for hh in range(hg)]
            pieces += [(j, diagonal, hh, half, half if diagonal else 0) for hh in range(hg)]

        def scores(piece):
            j, _, hh, k0, q0 = piece
            base = pl.multiple_of(j * tq, tq)
            return _dot(k_ref[0, hh, pl.ds(base + k0, half), :], qt_ref[0, hh, 0, :, q0:])

        def softmax(piece, st):
            _, diagonal, hh, _, q0 = piece
            if diagonal:
                key = lax.broadcasted_iota(jnp.int32, st.shape, 0)
                qry = lax.broadcasted_iota(jnp.int32, st.shape, 1)
                st = jnp.where(key <= qry, st, NEG_BIG)
            m_old = m_ref[hh, :, q0:]
            m_new = jnp.maximum(m_old, jnp.max(st, axis=0, keepdims=True))
            m_ref[hh, :, q0:] = m_new
            return piece, jnp.exp2(st - m_new).astype(BF16), jnp.exp2(m_old - m_new)

        def weighted_values(piece, probs, rescale):
            j, _, hh, k0, q0 = piece
            pv = _dot(vt_ref[0, hh, j, :, k0:k0 + half], probs)
            acc_ref[hh, :, q0:] = rescale * acc_ref[hh, :, q0:] + pv

        st_next = scores(pieces[0])
        pending = None
        for i, piece in enumerate(pieces):
            st = st_next
            if i + 1 < len(pieces):
                st_next = scores(pieces[i + 1])
            if pending is not None:
                weighted_values(*pending)
            pending = softmax(piece, st)
        weighted_values(*pending)

    def pair(jj, carry):
        run_blocks([(2 * jj, False), (2 * jj + 1, False)])
        return carry

    lax.fori_loop(0, qi // 2, pair, 0)

    @pl.when(qi % 2 == 1)
    def _():
        run_blocks([(qi - 1, False), (qi, True)])

    @pl.when(qi % 2 == 0)
    def _():
        run_blocks([(qi, True)])

    for pr in range(hg // 2):
        halves = []
        for hh in (2 * pr, 2 * pr + 1):
            acc = acc_ref[hh]
            halves.append(acc[:MLA_V] / acc[MLA_V:MLA_V + 1])
        o_ref[0, :, pr * LANES:(pr + 1) * LANES] = jnp.concatenate(halves, axis=0).T.astype(BF16)


def _mla_attn(qt, k, vt, *, hg):
    bsz, heads, n_tiles, _, tq = qt.shape
    seq = n_tiles * tq
    return pl.pallas_call(
        functools.partial(_mla_attn_kernel, tq=tq, hg=hg),
        grid=(bsz, heads // hg, n_tiles),
        in_specs=[pl.BlockSpec((1, hg, 1, QK_DIM, tq), lambda b, g, i: (b, g, i, 0, 0)),
                  pl.BlockSpec((1, hg, seq, QK_DIM), lambda b, g, i: (b, g, 0, 0)),
                  pl.BlockSpec((1, hg, n_tiles, V_ROWS, tq), lambda b, g, i: (b, g, 0, 0, 0))],
        out_specs=pl.BlockSpec((1, tq, hg * MLA_V), lambda b, g, i: (b, i, g)),
        out_shape=jax.ShapeDtypeStruct((bsz, seq, MLA_WIDTH), BF16),
        scratch_shapes=[pltpu.VMEM((hg, 1, tq), F32), pltpu.VMEM((hg, V_ROWS, tq), F32)],
        compiler_params=_params(3),
        name="mla_attn",
    )(qt, k, vt)


def _route_t(logits_t, bias_t):
    scores = jax.nn.sigmoid(logits_t)
    sel_all = scores + bias_t
    sel = [sel_all[e:e + 1] for e in range(N_EXPERTS)]
    group_score = []
    for g in range(N_GROUPS):
        v = sel[g * EXPERTS_PER_GROUP:(g + 1) * EXPERTS_PER_GROUP]
        best = None
        for i in range(EXPERTS_PER_GROUP):
            for j in range(i + 1, EXPERTS_PER_GROUP):
                best = v[i] + v[j] if best is None else jnp.maximum(best, v[i] + v[j])
        group_score.append(best)
    top, top_idx = group_score[0], jnp.zeros(group_score[0].shape, jnp.int32)
    for g in range(1, N_GROUPS):
        better = group_score[g] > top
        top = jnp.where(better, group_score[g], top)
        top_idx = jnp.where(better, g, top_idx)
    member = [jnp.where(top_idx == g, 1.0, 0.0) for g in range(N_GROUPS)]
    weights = []
    for e in range(N_EXPERTS):
        g = e // EXPERTS_PER_GROUP
        rank = jnp.zeros(sel[e].shape, jnp.int32)
        for j in range(g * EXPERTS_PER_GROUP, (g + 1) * EXPERTS_PER_GROUP):
            if j != e:
                rank = rank + jnp.where(sel[j] > sel[e], 1,
                                        jnp.where(sel[j] == sel[e], 1 if j < e else 0, 0))
        weights.append(jnp.where(rank < TOP_K, member[g], 0.0) * scores[e:e + 1])
    total = weights[0]
    for w in weights[1:]:
        total = total + w
    return [w / total for w in weights], member


def _post_attn_kernel(yacd_ref, yb_ref, x_ref, wo_acd_ref, wo_b_ref, g1_ref, b1_ref, wq_ref,
                      km_ref, vm_ref, wo_ref, g2_ref, b2_ref, rw_ref,
                      x2_ref, logits_ref, *, tm, alpha, n_chains):
    def chain(rows):
        h = _dot(yacd_ref[0, rows], wo_acd_ref[...]) + _dot(yb_ref[0, rows], wo_b_ref[...])
        yield
        x1 = _layer_norm(alpha * x_ref[0, rows] + h, g1_ref[...], b1_ref[...])
        q = _dot(x1.astype(BF16), wq_ref[...])
        yield
        heads = []
        for hd in range(X_HEADS):
            cols = slice(hd * X_HEAD_DIM, (hd + 1) * X_HEAD_DIM)
            s = _dot_nt(q[:, cols].astype(BF16), km_ref[0, :, cols])
            e = jnp.exp2(s - jnp.max(s, axis=-1, keepdims=True))
            o = _dot(e.astype(BF16), vm_ref[0, :, cols]) / jnp.sum(e, axis=-1, keepdims=True)
            heads.append(o.astype(BF16))
            yield
        h2 = _dot(jnp.concatenate(heads, axis=1), wo_ref[...])
        yield
        x2 = _layer_norm(alpha * x1 + h2, g2_ref[...], b2_ref[...])
        x2_ref[0, rows] = x2
        x_hi = x2.astype(BF16)
        x_lo = (x2 - x_hi.astype(F32)).astype(BF16)
        hl = _dot_nt(rw_ref[...], x_hi)
        lo = _dot_nt(rw_ref[0:N_EXPERTS, :], x_lo)
        logits_ref[0, :, rows] = hl[:N_EXPERTS] + hl[N_EXPERTS:] + lo

    rows_per = tm // n_chains
    _run_interleaved([chain(slice(c * rows_per, (c + 1) * rows_per)) for c in range(n_chains)])


def _post_attn(yacd, yb, x, wo_acd, wo_b, g1, b1, wq, km, vm, wo, g2, b2, rw,
               *, layer, tm, alpha):
    bsz, seq, d = x.shape
    tok = lambda w: pl.BlockSpec((1, tm, w), lambda b, s: (b, s, 0))
    memspec = pl.BlockSpec((None, 1) + km.shape[2:], lambda b, s: (layer, b, 0, 0))
    per_layer = lambda w: _layer_spec(w, layer)
    return pl.pallas_call(
        functools.partial(_post_attn_kernel, tm=tm, alpha=alpha, n_chains=2),
        grid=(bsz, seq // tm),
        in_specs=[tok(yacd.shape[-1]), tok(yb.shape[-1]), tok(d),
                  per_layer(wo_acd), per_layer(wo_b), per_layer(g1), per_layer(b1), per_layer(wq),
                  memspec, memspec, per_layer(wo), per_layer(g2), per_layer(b2),
                  _const_spec(rw.shape)],
        out_specs=[tok(d), pl.BlockSpec((1, N_EXPERTS, tm), lambda b, s: (b, 0, s))],
        out_shape=[jax.ShapeDtypeStruct((bsz, seq, d), F32),
                   jax.ShapeDtypeStruct((bsz, N_EXPERTS, seq), F32)],
        compiler_params=_params(2),
        name="post_attn",
    )(yacd, yb, x, wo_acd, wo_b, g1, b1, wq, km, vm, wo, g2, b2, rw)


def _dot_tn(a, b):
    return lax.dot_general(a, b, (((0,), (0,)), ((), ())), preferred_element_type=F32)


def _moe_kernel(x_ref, logits_ref, rb_ref, before_ref, wg_ref, wu_ref, wd_ref, g3_ref, b3_ref, o_ref,
                xb_ref, pos_ref, memb_ref, gsplit_ref, acc_ref, *, alpha, ts, n_sub, rows):
    group_w = EXPERTS_PER_GROUP * D_FF

    def span(u):
        return slice(u * ts, (u + 1) * ts)

    def route(u):
        xb_ref[u] = x_ref[0, span(u)].astype(BF16)
        gates, member = _route_t(logits_ref[0, :, span(u)], rb_ref[...])
        memb = jnp.concatenate(member + [jnp.zeros((ROUTE_ROWS - N_GROUPS, ts), F32)], axis=0)
        memb_ref[u] = memb
        pos_ref[u] = _dot(memb.astype(BF16), before_ref[...])
        gate_rows = jnp.concatenate(gates, axis=0)
        g_hi = gate_rows.astype(BF16)
        gsplit_ref[u] = jnp.concatenate([g_hi, (gate_rows - g_hi.astype(F32)).astype(BF16)], axis=0)

    def expert_chunk(u, g, ci):
        slot = (lax.broadcasted_iota(jnp.int32, (rows, ts), 0) + ci * rows).astype(F32)
        p = jnp.where(pos_ref[u, g:g + 1, :] == slot, memb_ref[u, g:g + 1, :], 0.0).astype(BF16)
        xg = _dot(p, xb_ref[u]).astype(BF16)
        gg = _dot_nt(p, gsplit_ref[u])
        hs = []
        for j in range(EXPERTS_PER_GROUP):
            e = g * EXPERTS_PER_GROUP + j
            gate = gg[:, e:e + 1] + gg[:, N_EXPERTS + e:N_EXPERTS + e + 1]
            hg = _dot(xg, wg_ref[e])
            hu = _dot(xg, wu_ref[e])
            hs.append((jax.nn.silu(hg) * hu * gate).astype(BF16))
        y = _dot(jnp.concatenate(hs, axis=1), wd_ref[g * group_w:(g + 1) * group_w, :])
        return p, y.astype(BF16)

    def finish(u):
        o_ref[0, span(u)] = _layer_norm(alpha * x_ref[0, span(u)] + acc_ref[u],
                                        g3_ref[...], b3_ref[...])

    for u in range(n_sub):
        route(u)
        first = [expert_chunk(u, g, 0) for g in range(N_GROUPS)]
        acc_ref[u] = _dot_tn(jnp.concatenate([p for p, _ in first], axis=0),
                             jnp.concatenate([y for _, y in first], axis=0))
        finish(u)
    for u in range(n_sub):
        largest = jnp.max(jnp.sum(memb_ref[u, 0:SUBLANES, :], axis=1, keepdims=True))

        @pl.when(largest > rows)
        def _(u=u):
            for g in range(N_GROUPS):
                n_tok = jnp.sum(memb_ref[u, g:g + 1, :]).astype(jnp.int32)

                def extra(ci, carry, g=g):
                    p, y = expert_chunk(u, g, ci)
                    acc_ref[u] += _dot_tn(p, y)
                    return carry

                lax.fori_loop(1, (n_tok + rows - 1) // rows, extra, 0)
            finish(u)


def _moe(x, logits_t, rb_t, before, wg, wu, wd, g3, b3, *, layer, ts, n_sub, alpha):
    bsz, seq, d = x.shape
    tm = ts * n_sub
    tok = lambda w: pl.BlockSpec((1, tm, w), lambda b, s: (b, s, 0))
    resident = lambda w: _layer_spec(w, layer, single_buffer=True)
    return pl.pallas_call(
        functools.partial(_moe_kernel, alpha=alpha, ts=ts, n_sub=n_sub,
                          rows=ts // N_GROUPS * 5 // 4),
        grid=(bsz, seq // tm),
        in_specs=[tok(d), pl.BlockSpec((1, N_EXPERTS, tm), lambda b, s: (b, 0, s)),
                  _const_spec(rb_t.shape), _const_spec(before.shape),
                  resident(wg), resident(wu), resident(wd),
                  _layer_spec(g3, layer), _layer_spec(b3, layer)],
        out_specs=tok(d),
        out_shape=jax.ShapeDtypeStruct((bsz, seq, d), F32),
        scratch_shapes=[pltpu.VMEM((n_sub, ts, d), BF16),
                        pltpu.VMEM((n_sub, ROUTE_ROWS, ts), F32),
                        pltpu.VMEM((n_sub, ROUTE_ROWS, ts), F32),
                        pltpu.VMEM((n_sub, 2 * N_EXPERTS, ts), BF16),
                        pltpu.VMEM((n_sub, ts, d), F32)],
        compiler_params=_params(2),
        name="moe",
    )(x, logits_t, rb_t, before, wg, wu, wd, g3, b3)


def _rot_cols(w):
    half = w.shape[-1] // 2
    return jnp.concatenate([-w[..., half:], w[..., :half]], axis=-1)


def _prep_w_in(w_in):
    k_rope = 2 * A_WIDTH + Q_LORA + KV_LORA
    head = w_in[..., :k_rope]
    kr = w_in[..., k_rope:k_rope + MLA_ROPE]
    rest = w_in[..., k_rope + MLA_ROPE:]
    rope4 = jnp.concatenate([kr, _rot_cols(kr), kr, _rot_cols(kr)], axis=-1)
    return jnp.concatenate([head, rope4, rest], axis=-1).astype(BF16)


def _prep_w_uq(w_uq):
    depth = w_uq.shape[0]
    w = w_uq.reshape(depth, Q_LORA, MLA_HEADS, MLA_NOPE + MLA_ROPE)
    nope, rope = w[..., :MLA_NOPE], w[..., MLA_NOPE:]
    ext = jnp.concatenate([nope, rope, _rot_cols(rope)], axis=-1)
    return ext.reshape(depth, Q_LORA, MLA_HEADS * QK_DIM).astype(BF16)


def _prep_w_ukv(w_ukv):
    depth = w_ukv.shape[0]
    w = w_ukv.reshape(depth, KV_LORA, MLA_HEADS, MLA_NOPE + MLA_V)
    k_nope, v = w[..., :MLA_NOPE], w[..., MLA_NOPE:]
    wk = jnp.concatenate([k_nope, jnp.zeros_like(k_nope)], axis=-1)
    return (wk.reshape(depth, KV_LORA, MLA_HEADS * QK_DIM).astype(BF16),
            v.reshape(depth, KV_LORA, MLA_WIDTH).astype(BF16))


def _prep_pool_w(pool_w):
    depth, groups = pool_w.shape[:2]
    eye = jnp.eye(groups, dtype=pool_w.dtype)
    bd = jnp.einsum('lgcd,gh->lgchd', pool_w, eye)
    return bd.reshape(depth, D_WIDTH, D_WIDTH).astype(BF16)


def kernel(x, mem, positions, w_in, gmlp_v_norm_g, gmlp_w_s, gmlp_b_s, mla_q_norm_g, mla_w_uq,
           mla_kv_norm_g, mla_w_ukv, conv_w, pool_w, pool_scale, w_out, ln1_g, ln1_b,
           xattn_wq, xattn_wk, xattn_wv, xattn_wo, ln2_g, ln2_b, router_w, router_b,
           moe_w_gate, moe_w_up, moe_w_down, ln3_g, ln3_b):
    depth = w_in.shape[0]
    alpha = (2 * depth) ** 0.25
    tm = 512
    bsz, seq, _ = x.shape

    inv_freq = ROPE_BASE ** (-jnp.arange(0, MLA_ROPE, 2, dtype=F32) / MLA_ROPE)
    ang = positions.astype(F32)[..., None] * inv_freq
    cos2 = jnp.tile(jnp.cos(ang), (1, 1, 2))
    sin2 = jnp.tile(jnp.sin(ang), (1, 1, 2))
    scale = (MLA_NOPE + MLA_ROPE) ** -0.5 * LOG2_E
    csq = scale * jnp.concatenate([jnp.ones(cos2.shape[:2] + (MLA_NOPE,), F32), cos2, sin2], -1)
    csq = jnp.swapaxes(csq.reshape(bsz, seq // tm, tm, QK_DIM), 2, 3)
    csk = jnp.concatenate([cos2, sin2, cos2, sin2], axis=-1)

    row = lambda a: a[:, None, :]
    w_in_e = _prep_w_in(w_in)
    w_uq_e = jnp.swapaxes(_prep_w_uq(mla_w_uq), 1, 2)
    w_k_e, w_v_e = _prep_w_ukv(mla_w_ukv)
    w_v_e = jnp.swapaxes(w_v_e, 1, 2)
    ws = gmlp_w_s.astype(BF16)
    bs = jnp.repeat(jnp.swapaxes(gmlp_b_s, 1, 2), A_HEAD_DIM, axis=2)
    pool_bd = _prep_pool_w(pool_w)
    wo_acd = jnp.concatenate([w_out[:, :A_WIDTH], w_out[:, A_WIDTH + MLA_WIDTH:]], axis=1).astype(BF16)
    wo_b = w_out[:, A_WIDTH:A_WIDTH + MLA_WIDTH].astype(BF16)
    wq = (xattn_wq * (X_HEAD_DIM ** -0.5 * LOG2_E)).astype(BF16)
    wo = xattn_wo.astype(BF16)
    rw_t = router_w.T
    rw_hi = rw_t.astype(BF16)
    rw = jnp.concatenate([rw_hi, (rw_t - rw_hi.astype(F32)).astype(BF16)], axis=0)
    rb_t = jnp.broadcast_to(router_b[:, None], (N_EXPERTS, tm))
    before = jnp.triu(jnp.ones((tm, tm), BF16), 1)
    wg = moe_w_gate.astype(BF16)
    wu = moe_w_up.astype(BF16)
    wd = moe_w_down.reshape(depth, N_EXPERTS * D_FF, D_MODEL).astype(BF16)

    km, vm = _mem_kv(mem, xattn_wk.astype(BF16), xattn_wv.astype(BF16))

    for l in range(depth):
        q, k, v, yacd = _mixer_in(
            x, csq, csk, w_in_e, row(gmlp_v_norm_g), ws, bs, row(mla_q_norm_g),
            w_uq_e, row(mla_kv_norm_g), w_k_e, w_v_e, conv_w, pool_bd,
            row(pool_scale), layer=l, tm=2 * tm, ta=tm)
        yb = _mla_attn(q, k, v, hg=MLA_HEADS)
        x2, logits = _post_attn(
            yacd, yb, x, wo_acd, wo_b, row(ln1_g), row(ln1_b), wq, km, vm,
            wo, row(ln2_g), row(ln2_b), rw, layer=l, tm=2 * tm, alpha=alpha)
        x = _moe(x2, logits, rb_t, before, wg, wu, wd, row(ln3_g), row(ln3_b),
                 layer=l, ts=tm, n_sub=2, alpha=alpha)
    return x
```

```python
import functools

import jax
import jax.numpy as jnp
from jax import lax
from jax.experimental import pallas as pl
from jax.experimental.pallas import tpu as pltpu

F32 = jnp.float32
BF16 = jnp.bfloat16

D_MODEL = 1024
A_HEADS, A_HEAD_DIM, CHUNK = 4, 64, 128
A_WIDTH = A_HEADS * A_HEAD_DIM
MLA_HEADS, MLA_NOPE, MLA_ROPE, MLA_V = 8, 64, 32, 64
Q_LORA, KV_LORA = 256, 128
MLA_WIDTH = MLA_HEADS * MLA_V
ROPE_BASE = 10000.0
C_WIDTH, CONV_W = 256, 3
D_WIDTH = 256
POOL_WINDOWS = (2, 4, 8, 16)
D_GROUP = D_WIDTH // len(POOL_WINDOWS)
X_HEADS = 4
X_HEAD_DIM = D_MODEL // X_HEADS
N_EXPERTS, N_GROUPS, TOP_K = 16, 4, 2
EXPERTS_PER_GROUP = N_EXPERTS // N_GROUPS
D_FF = 256
EPS = 1e-6

LANES = 128
SUBLANES = 8
VMEM_LIMIT = 56 * 1024 * 1024

OFF_U = 0
OFF_CQ = OFF_U + 2 * A_WIDTH
OFF_BG = OFF_CQ + Q_LORA + KV_LORA + 4 * MLA_ROPE
P_EXT = OFF_BG + 3 * C_WIDTH + D_WIDTH
QK_DIM = 128
V_ROWS = 80
POOL_HALO = 16
ROUTE_ROWS = 16
NEG_BIG = -1e30
LOG2_E = 1.4426950408889634


def _dot(a, b):
    return jnp.dot(a, b, preferred_element_type=F32)


def _dot_nt(a, b):
    return lax.dot_general(a, b, (((1,), (1,)), ((), ())), preferred_element_type=F32)


def _rms(x, g):
    return x * lax.rsqrt(jnp.mean(x * x, axis=-1, keepdims=True) + EPS) * g


def _layer_norm(x, g, b):
    mu = jnp.mean(x, axis=-1, keepdims=True)
    xc = x - mu
    var = jnp.mean(xc * xc, axis=-1, keepdims=True)
    return xc * lax.rsqrt(var + EPS) * g + b


def _params(n_grid):
    return pltpu.CompilerParams(dimension_semantics=("arbitrary",) * n_grid,
                                vmem_limit_bytes=VMEM_LIMIT)


def _run_interleaved(chains):
    live = list(chains)
    while live:
        for c in list(live):
            try:
                next(c)
            except StopIteration:
                live.remove(c)


def _const_spec(shape):
    zeros = (0,) * len(shape)
    return pl.BlockSpec(shape, lambda *_: zeros)


def _layer_spec(stacked, layer, single_buffer=False):
    index = (layer,) + (0,) * (stacked.ndim - 1)
    mode = dict(pipeline_mode=pl.Buffered(1)) if single_buffer else {}
    return pl.BlockSpec((None,) + stacked.shape[1:], lambda *_: index, **mode)


def _mem_kv_kernel(mem_ref, wk_ref, wv_ref, k_ref, v_ref):
    m = mem_ref[0].astype(BF16)
    k_ref[0, 0] = _dot(m, wk_ref[0]).astype(BF16)
    v_ref[0, 0] = _dot(m, wv_ref[0]).astype(BF16)


def _mem_kv(mem, wk, wv):
    depth = wk.shape[0]
    bsz, mlen, d = mem.shape
    out = jax.ShapeDtypeStruct((depth, bsz, mlen, d), BF16)
    return pl.pallas_call(
        _mem_kv_kernel,
        grid=(depth, bsz),
        in_specs=[pl.BlockSpec((1, mlen, d), lambda l, b: (b, 0, 0)),
                  pl.BlockSpec((1, d, d), lambda l, b: (l, 0, 0)),
                  pl.BlockSpec((1, d, d), lambda l, b: (l, 0, 0))],
        out_specs=[pl.BlockSpec((1, 1, mlen, d), lambda l, b: (l, b, 0, 0)),
                   pl.BlockSpec((1, 1, mlen, d), lambda l, b: (l, b, 0, 0))],
        out_shape=[out, out],
        compiler_params=_params(2),
        name="mem_kv",
    )(mem, wk, wv)


def _mixer_in_kernel(x_ref, csq_ref, csk_ref, w_in_ref, vng_ref, ws_ref, bs_ref, qng_ref,
                     wuq_ref, kvng_ref, wk_ref, wv_ref, convw_ref, poolw_ref, pscale_ref,
                     q_ref, k_ref, v_ref, y_ref,
                     conv_buf, pool_a, pool_b, pool_carry, *, tm, ta):
    si = pl.program_id(1)

    @pl.when(si == 0)
    def _():
        pool_carry[...] = jnp.zeros((POOL_HALO, D_WIDTH), F32)
        conv_buf[0:SUBLANES, :] = jnp.zeros((SUBLANES, C_WIDTH), F32)

    xb = x_ref[0].astype(BF16)

    za = _dot(xb, w_in_ref[:, OFF_U:OFF_CQ])
    zb = _dot(xb, w_in_ref[:, OFF_CQ:OFF_BG])
    zcd = _dot(xb, w_in_ref[:, OFF_BG:P_EXT])

    za = jax.nn.gelu(za)
    u = za[:, :A_WIDTH]
    v = _rms(za[:, A_WIDTH:], vng_ref[...]).astype(BF16)
    cq = _rms(zb[:, :Q_LORA], qng_ref[...]).astype(BF16)
    ckv = _rms(zb[:, Q_LORA:Q_LORA + KV_LORA], kvng_ref[...]).astype(BF16)

    row = lax.broadcasted_iota(jnp.int32, (CHUNK, CHUNK), 0)
    col = lax.broadcasted_iota(jnp.int32, (CHUNK, CHUNK), 1)
    w_causal = jnp.concatenate(
        [jnp.where(row >= col, ws_ref[h], jnp.zeros((), BF16)) for h in range(A_HEADS)], axis=1)
    lane_head = lax.broadcasted_iota(jnp.int32, (CHUNK, A_WIDTH), 1) // A_HEAD_DIM
    for c in range(tm // CHUNK):
        rows = slice(c * CHUNK, (c + 1) * CHUNK)
        vc = v[rows]
        v_heads = jnp.concatenate(
            [jnp.where(lane_head == h, vc, jnp.zeros((), BF16)) for h in range(A_HEADS)], axis=0)
        mixed = _dot(w_causal, v_heads)
        y_ref[0, rows, 0:A_WIDTH] = (u[rows] * (mixed + bs_ref[...])).astype(BF16)

    qt = _dot_nt(wuq_ref[...], cq)
    kk = _dot(ckv, wk_ref[...])
    vt = _dot_nt(wv_ref[...], ckv)

    zd = zcd[:, 3 * C_WIDTH:]
    base = SUBLANES
    n = tm + POOL_HALO
    pool_a[0:base, :] = jnp.zeros((base, D_WIDTH), F32)
    pool_b[0:base, :] = jnp.zeros((base, D_WIDTH), F32)
    pool_a[base:base + POOL_HALO, :] = pool_carry[...]
    pool_a[base + POOL_HALO:base + n, :] = zd
    pool_carry[...] = zd[tm - POOL_HALO:, :]
    lane_d = lax.broadcasted_iota(jnp.int32, (n, D_WIDTH), 1)
    pool_b[base:base + n, :] = pool_a[base:base + n, :] + pool_a[base - 1:base - 1 + n, :]
    pool_a[base:base + n, :] = pool_b[base:base + n, :] + jnp.where(
        lane_d >= D_GROUP, pool_b[base - 2:base - 2 + n, :], 0.0)
    pool_b[base:base + n, :] = pool_a[base:base + n, :] + jnp.where(
        lane_d >= 2 * D_GROUP, pool_a[base - 4:base - 4 + n, :], 0.0)
    t0 = base + POOL_HALO
    lane_t = lax.broadcasted_iota(jnp.int32, (tm, D_WIDTH), 1)
    win_sum = pool_b[t0:t0 + tm, :] + jnp.where(
        lane_t >= 3 * D_GROUP, pool_b[t0 - 8:t0 - 8 + tm, :], 0.0)
    pos1 = si * tm + lax.broadcasted_iota(jnp.int32, (tm, D_WIDTH), 0) + 1
    window = jnp.left_shift(2, lane_t // D_GROUP)
    count = jnp.minimum(pos1, window).astype(F32)
    pooled = (win_sum / count - zd).astype(BF16)
    yd = _dot(pooled, poolw_ref[...]) * pscale_ref[...]
    y_ref[0, :, A_WIDTH + C_WIDTH:] = yd.astype(BF16)

    gh = zcd[:, C_WIDTH:2 * C_WIDTH] * zcd[:, 2 * C_WIDTH:3 * C_WIDTH]
    conv_buf[SUBLANES:SUBLANES + tm, :] = gh
    conv = (convw_ref[2:3, :] * gh
            + convw_ref[1:2, :] * conv_buf[SUBLANES - 1:SUBLANES - 1 + tm, :]
            + convw_ref[0:1, :] * conv_buf[SUBLANES - 2:SUBLANES - 2 + tm, :])
    y_ref[0, :, A_WIDTH:A_WIDTH + C_WIDTH] = (zcd[:, :C_WIDTH] * conv).astype(BF16)
    conv_buf[0:SUBLANES, :] = conv_buf[tm:tm + SUBLANES, :]

    for t in range(tm // ta):
        cols = slice(t * ta, (t + 1) * ta)
        csq = csq_ref[0, t]
        for h in range(MLA_HEADS):
            q_ref[0, h, t] = (qt[h * QK_DIM:(h + 1) * QK_DIM, cols] * csq).astype(BF16)
    r4 = zb[:, Q_LORA + KV_LORA:] * csk_ref[0]
    kr = r4 + pltpu.roll(r4, MLA_ROPE, 1)
    lane = lax.broadcasted_iota(jnp.int32, (tm, QK_DIM), 1)
    kr = jnp.where(lane >= MLA_NOPE, kr, 0.0)
    for h in range(MLA_HEADS):
        k_ref[0, h] = (kk[:, h * QK_DIM:(h + 1) * QK_DIM] + kr).astype(BF16)
    ones_then_zeros = jnp.where(
        lax.broadcasted_iota(jnp.int32, (V_ROWS - MLA_V, ta), 0) == 0, 1.0, 0.0).astype(BF16)
    for t in range(tm // ta):
        cols = slice(t * ta, (t + 1) * ta)
        for h in range(MLA_HEADS):
            v_ref[0, h, t, 0:MLA_V, :] = vt[h * MLA_V:(h + 1) * MLA_V, cols].astype(BF16)
            v_ref[0, h, t, MLA_V:, :] = ones_then_zeros


def _mixer_in(x, csq, csk, w_in, vng, ws, bs, qng, wuq, kvng, wk, wv, convw, poolw, pscale,
              *, layer, tm, ta):
    bsz, seq, d = x.shape
    tok = lambda w: pl.BlockSpec((1, tm, w), lambda b, s: (b, s, 0))
    head = lambda n, w: pl.BlockSpec((1, n, tm, w), lambda b, s: (b, 0, s, 0))
    per = tm // ta
    head_t = lambda r: pl.BlockSpec((1, MLA_HEADS, per, r, ta), lambda b, s: (b, 0, s, 0, 0))
    shape_t = lambda r: jax.ShapeDtypeStruct((bsz, MLA_HEADS, seq // ta, r, ta), BF16)
    return pl.pallas_call(
        functools.partial(_mixer_in_kernel, tm=tm, ta=ta),
        grid=(bsz, seq // tm),
        in_specs=[tok(d), pl.BlockSpec((1, per, QK_DIM, ta), lambda b, s: (b, s, 0, 0)), tok(QK_DIM),
                  *[_layer_spec(w, layer) for w in (w_in, vng, ws, bs, qng, wuq, kvng, wk, wv,
                                                     convw, poolw, pscale)]],
        out_specs=[head_t(QK_DIM), head(MLA_HEADS, QK_DIM), head_t(V_ROWS),
                   tok(A_WIDTH + C_WIDTH + D_WIDTH)],
        out_shape=[shape_t(QK_DIM),
                   jax.ShapeDtypeStruct((bsz, MLA_HEADS, seq, QK_DIM), BF16),
                   shape_t(V_ROWS),
                   jax.ShapeDtypeStruct((bsz, seq, A_WIDTH + C_WIDTH + D_WIDTH), BF16)],
        scratch_shapes=[pltpu.VMEM((tm + SUBLANES, C_WIDTH), F32),
                        pltpu.VMEM((tm + POOL_HALO + SUBLANES, D_WIDTH), F32),
                        pltpu.VMEM((tm + POOL_HALO + SUBLANES, D_WIDTH), F32),
                        pltpu.VMEM((POOL_HALO, D_WIDTH), F32)],
        compiler_params=_params(2),
        name="mixer_in",
    )(x, csq, csk, w_in, vng, ws, bs, qng, wuq, kvng, wk, wv, convw, poolw, pscale)


def _mla_attn_kernel(qt_ref, k_ref, vt_ref, o_ref, m_ref, acc_ref, *, tq, hg):
    qi = pl.program_id(2)
    half = tq // 2
    m_ref[...] = jnp.full(m_ref.shape, NEG_BIG, F32)
    acc_ref[...] = jnp.zeros(acc_ref.shape, F32)

    def run_blocks(blocks):
        pieces = []
        for j, diagonal in blocks:
            pieces += [(j, diagonal, hh, 0, 0) for hh in range(hg)]
            pieces += [(j, diagonal, hh, half, half if diagonal else 0) for hh in range(hg)]

        def scores(piece):
            j, _, hh, k0, q0 = piece
            base = pl.multiple_of(j * tq, tq)
            return _dot(k_ref[0, hh, pl.ds(base + k0, half), :], qt_ref[0, hh, 0, :, q0:])

        def softmax(piece, st):
            _, diagonal, hh, _, q0 = piece
            if diagonal:
                key = lax.broadcasted_iota(jnp.int32, st.shape, 0)
                qry = lax.broadcasted_iota(jnp.int32, st.shape, 1)
                st = jnp.where(key <= qry, st, NEG_BIG)
            m_old = m_ref[hh, :, q0:]
            m_new = jnp.maximum(m_old, jnp.max(st, axis=0, keepdims=True))
            m_ref[hh, :, q0:] = m_new
            return piece, jnp.exp2(st - m_new).astype(BF16), jnp.exp2(m_old - m_new)

        def weighted_values(piece, probs, rescale):
            j, _, hh, k0, q0 = piece
            pv = _dot(vt_ref[0, hh, j, :, k0:k0 + half], probs)
            acc_ref[hh, :, q0:] = rescale * acc_ref[hh, :, q0:] + pv

        st_next = scores(pieces[0])
        pending = None
        for i, piece in enumerate(pieces):
            st = st_next
            if i + 1 < len(pieces):
                st_next = scores(pieces[i + 1])
            if pending is not None:
                weighted_values(*pending)
            pending = softmax(piece, st)
        weighted_values(*pending)

    group = 4

    def many(jj, carry):
        run_blocks([(group * jj + i, False) for i in range(group)])
        return carry

    lax.fori_loop(0, qi // group, many, 0)
    for r in range(group):
        @pl.when(qi % group == r)
        def _(r=r):
            run_blocks([(qi - r + i, False) for i in range(r)] + [(qi, True)])

    for pr in range(hg // 2):
        halves = []
        for hh in (2 * pr, 2 * pr + 1):
            acc = acc_ref[hh]
            halves.append(acc[:MLA_V] / acc[MLA_V:MLA_V + 1])
        o_ref[0, :, pr * LANES:(pr + 1) * LANES] = jnp.concatenate(halves, axis=0).T.astype(BF16)


def _mla_attn(qt, k, vt, *, hg):
    bsz, heads, n_tiles, _, tq = qt.shape
    seq = n_tiles * tq
    return pl.pallas_call(
        functools.partial(_mla_attn_kernel, tq=tq, hg=hg),
        grid=(bsz, heads // hg, n_tiles),
        in_specs=[pl.BlockSpec((1, hg, 1, QK_DIM, tq), lambda b, g, i: (b, g, i, 0, 0)),
                  pl.BlockSpec((1, hg, seq, QK_DIM), lambda b, g, i: (b, g, 0, 0)),
                  pl.BlockSpec((1, hg, n_tiles, V_ROWS, tq), lambda b, g, i: (b, g, 0, 0, 0))],
        out_specs=pl.BlockSpec((1, tq, hg * MLA_V), lambda b, g, i: (b, i, g)),
        out_shape=jax.ShapeDtypeStruct((bsz, seq, MLA_WIDTH), BF16),
        scratch_shapes=[pltpu.VMEM((hg, 1, tq), F32), pltpu.VMEM((hg, V_ROWS, tq), F32)],
        compiler_params=_params(3),
        name="mla_attn",
    )(qt, k, vt)


def _route_t(logits_t, bias_t):
    scores = jax.nn.sigmoid(logits_t)
    sel_all = scores + bias_t
    sel = [sel_all[e:e + 1] for e in range(N_EXPERTS)]
    group_score = []
    for g in range(N_GROUPS):
        v = sel[g * EXPERTS_PER_GROUP:(g + 1) * EXPERTS_PER_GROUP]
        best = None
        for i in range(EXPERTS_PER_GROUP):
            for j in range(i + 1, EXPERTS_PER_GROUP):
                best = v[i] + v[j] if best is None else jnp.maximum(best, v[i] + v[j])
        group_score.append(best)
    top, top_idx = group_score[0], jnp.zeros(group_score[0].shape, jnp.int32)
    for g in range(1, N_GROUPS):
        better = group_score[g] > top
        top = jnp.where(better, group_score[g], top)
        top_idx = jnp.where(better, g, top_idx)
    member = [jnp.where(top_idx == g, 1.0, 0.0) for g in range(N_GROUPS)]
    weights = []
    for e in range(N_EXPERTS):
        g = e // EXPERTS_PER_GROUP
        rank = jnp.zeros(sel[e].shape, jnp.int32)
        for j in range(g * EXPERTS_PER_GROUP, (g + 1) * EXPERTS_PER_GROUP):
            if j != e:
                rank = rank + jnp.where(sel[j] > sel[e], 1,
                                        jnp.where(sel[j] == sel[e], 1 if j < e else 0, 0))
        weights.append(jnp.where(rank < TOP_K, member[g], 0.0) * scores[e:e + 1])
    total = weights[0]
    for w in weights[1:]:
        total = total + w
    return [w / total for w in weights], member


def _post_attn_kernel(yacd_ref, yb_ref, x_ref, wo_acd_ref, wo_b_ref, g1_ref, b1_ref, wq_ref,
                      km_ref, vm_ref, wo_ref, g2_ref, b2_ref, rw_ref,
                      x2_ref, logits_ref, *, tm, alpha, n_chains):
    def chain(rows):
        h = _dot(yacd_ref[0, rows], wo_acd_ref[...]) + _dot(yb_ref[0, rows], wo_b_ref[...])
        yield
        x1 = _layer_norm(alpha * x_ref[0, rows] + h, g1_ref[...], b1_ref[...])
        q = _dot(x1.astype(BF16), wq_ref[...])
        yield
        heads = []
        for hd in range(X_HEADS):
            cols = slice(hd * X_HEAD_DIM, (hd + 1) * X_HEAD_DIM)
            s = _dot_nt(q[:, cols].astype(BF16), km_ref[0, :, cols])
            e = jnp.exp2(s - jnp.max(s, axis=-1, keepdims=True))
            o = _dot(e.astype(BF16), vm_ref[0, :, cols]) / jnp.sum(e, axis=-1, keepdims=True)
            heads.append(o.astype(BF16))
            yield
        h2 = _dot(jnp.concatenate(heads, axis=1), wo_ref[...])
        yield
        x2 = _layer_norm(alpha * x1 + h2, g2_ref[...], b2_ref[...])
        x2_ref[0, rows] = x2
        x_hi = x2.astype(BF16)
        x_lo = (x2 - x_hi.astype(F32)).astype(BF16)
        hl = _dot_nt(rw_ref[...], x_hi)
        lo = _dot_nt(rw_ref[0:N_EXPERTS, :], x_lo)
        logits_ref[0, :, rows] = hl[:N_EXPERTS] + hl[N_EXPERTS:] + lo

    rows_per = tm // n_chains
    _run_interleaved([chain(slice(c * rows_per, (c + 1) * rows_per)) for c in range(n_chains)])


def _post_attn(yacd, yb, x, wo_acd, wo_b, g1, b1, wq, km, vm, wo, g2, b2, rw,
               *, layer, tm, alpha):
    bsz, seq, d = x.shape
    tok = lambda w: pl.BlockSpec((1, tm, w), lambda b, s: (b, s, 0))
    memspec = pl.BlockSpec((None, 1) + km.shape[2:], lambda b, s: (layer, b, 0, 0))
    per_layer = lambda w: _layer_spec(w, layer)
    return pl.pallas_call(
        functools.partial(_post_attn_kernel, tm=tm, alpha=alpha, n_chains=2),
        grid=(bsz, seq // tm),
        in_specs=[tok(yacd.shape[-1]), tok(yb.shape[-1]), tok(d),
                  per_layer(wo_acd), per_layer(wo_b), per_layer(g1), per_layer(b1), per_layer(wq),
                  memspec, memspec, per_layer(wo), per_layer(g2), per_layer(b2),
                  _const_spec(rw.shape)],
        out_specs=[tok(d), pl.BlockSpec((1, N_EXPERTS, tm), lambda b, s: (b, 0, s))],
        out_shape=[jax.ShapeDtypeStruct((bsz, seq, d), F32),
                   jax.ShapeDtypeStruct((bsz, N_EXPERTS, seq), F32)],
        compiler_params=_params(2),
        name="post_attn",
    )(yacd, yb, x, wo_acd, wo_b, g1, b1, wq, km, vm, wo, g2, b2, rw)


def _dot_tn(a, b):
    return lax.dot_general(a, b, (((0,), (0,)), ((), ())), preferred_element_type=F32)


def _moe_kernel(x_ref, logits_ref, rb_ref, before_ref, wg_ref, wu_ref, wd_ref, g3_ref, b3_ref, o_ref,
                xb_ref, pos_ref, memb_ref, gsplit_ref, acc_ref, *, alpha, tm, rows):
    group_w = EXPERTS_PER_GROUP * D_FF
    x = x_ref[0]
    xb_ref[...] = x.astype(BF16)
    gates, member = _route_t(logits_ref[0], rb_ref[...])
    memb = jnp.concatenate(member + [jnp.zeros((ROUTE_ROWS - N_GROUPS, tm), F32)], axis=0)
    memb_ref[...] = memb
    pos_ref[...] = _dot(memb.astype(BF16), before_ref[...])
    gate_rows = jnp.concatenate(gates, axis=0)
    g_hi = gate_rows.astype(BF16)
    gsplit_ref[...] = jnp.concatenate([g_hi, (gate_rows - g_hi.astype(F32)).astype(BF16)], axis=0)

    def expert_chunk(g, ci):
        slot = (lax.broadcasted_iota(jnp.int32, (rows, tm), 0) + ci * rows).astype(F32)
        p = jnp.where(pos_ref[g:g + 1, :] == slot, memb_ref[g:g + 1, :], 0.0).astype(BF16)
        xg = _dot(p, xb_ref[...]).astype(BF16)
        gg = _dot_nt(p, gsplit_ref[...])
        hs = []
        for j in range(EXPERTS_PER_GROUP):
            e = g * EXPERTS_PER_GROUP + j
            gate = gg[:, e:e + 1] + gg[:, N_EXPERTS + e:N_EXPERTS + e + 1]
            hg = _dot(xg, wg_ref[e])
            hu = _dot(xg, wu_ref[e])
            hs.append((jax.nn.silu(hg) * hu * gate).astype(BF16))
        y = _dot(jnp.concatenate(hs, axis=1), wd_ref[g * group_w:(g + 1) * group_w, :])
        return p, y.astype(BF16)

    first = [expert_chunk(g, 0) for g in range(N_GROUPS)]
    acc_ref[...] = _dot_tn(jnp.concatenate([p for p, _ in first], axis=0),
                           jnp.concatenate([y for _, y in first], axis=0))
    largest = jnp.max(jnp.sum(memb_ref[0:SUBLANES, :], axis=1, keepdims=True))

    @pl.when(largest > rows)
    def _():
        for g in range(N_GROUPS):
            n_tok = jnp.sum(memb_ref[g:g + 1, :]).astype(jnp.int32)

            def extra(ci, carry, g=g):
                p, y = expert_chunk(g, ci)
                acc_ref[...] += _dot_tn(p, y)
                return carry

            lax.fori_loop(1, (n_tok + rows - 1) // rows, extra, 0)

    o_ref[0] = _layer_norm(alpha * x_ref[0] + acc_ref[...], g3_ref[...], b3_ref[...])


def _moe(x, logits_t, rb_t, before, wg, wu, wd, g3, b3, *, layer, tm, alpha):
    bsz, seq, d = x.shape
    tok = lambda w: pl.BlockSpec((1, tm, w), lambda b, s: (b, s, 0))
    resident = lambda w: _layer_spec(w, layer, single_buffer=True)
    return pl.pallas_call(
        functools.partial(_moe_kernel, alpha=alpha, tm=tm, rows=tm // N_GROUPS * 5 // 4),
        grid=(bsz, seq // tm),
        in_specs=[tok(d), pl.BlockSpec((1, N_EXPERTS, tm), lambda b, s: (b, 0, s)),
                  _const_spec(rb_t.shape), _const_spec(before.shape),
                  resident(wg), resident(wu), resident(wd),
                  _layer_spec(g3, layer), _layer_spec(b3, layer)],
        out_specs=tok(d),
        out_shape=jax.ShapeDtypeStruct((bsz, seq, d), F32),
        scratch_shapes=[pltpu.VMEM((tm, d), BF16),
                        pltpu.VMEM((ROUTE_ROWS, tm), F32),
                        pltpu.VMEM((ROUTE_ROWS, tm), F32),
                        pltpu.VMEM((2 * N_EXPERTS, tm), BF16),
                        pltpu.VMEM((tm, d), F32)],
        compiler_params=_params(2),
        name="moe",
    )(x, logits_t, rb_t, before, wg, wu, wd, g3, b3)


def _rot_cols(w):
    half = w.shape[-1] // 2
    return jnp.concatenate([-w[..., half:], w[..., :half]], axis=-1)


def _prep_w_in(w_in):
    k_rope = 2 * A_WIDTH + Q_LORA + KV_LORA
    head = w_in[..., :k_rope]
    kr = w_in[..., k_rope:k_rope + MLA_ROPE]
    rest = w_in[..., k_rope + MLA_ROPE:]
    rope4 = jnp.concatenate([kr, _rot_cols(kr), kr, _rot_cols(kr)], axis=-1)
    return jnp.concatenate([head, rope4, rest], axis=-1).astype(BF16)


def _prep_w_uq(w_uq):
    depth = w_uq.shape[0]
    w = w_uq.reshape(depth, Q_LORA, MLA_HEADS, MLA_NOPE + MLA_ROPE)
    nope, rope = w[..., :MLA_NOPE], w[..., MLA_NOPE:]
    ext = jnp.concatenate([nope, rope, _rot_cols(rope)], axis=-1)
    return ext.reshape(depth, Q_LORA, MLA_HEADS * QK_DIM).astype(BF16)


def _prep_w_ukv(w_ukv):
    depth = w_ukv.shape[0]
    w = w_ukv.reshape(depth, KV_LORA, MLA_HEADS, MLA_NOPE + MLA_V)
    k_nope, v = w[..., :MLA_NOPE], w[..., MLA_NOPE:]
    wk = jnp.concatenate([k_nope, jnp.zeros_like(k_nope)], axis=-1)
    return (wk.reshape(depth, KV_LORA, MLA_HEADS * QK_DIM).astype(BF16),
            v.reshape(depth, KV_LORA, MLA_WIDTH).astype(BF16))


def _prep_pool_w(pool_w):
    depth, groups = pool_w.shape[:2]
    eye = jnp.eye(groups, dtype=pool_w.dtype)
    bd = jnp.einsum('lgcd,gh->lgchd', pool_w, eye)
    return bd.reshape(depth, D_WIDTH, D_WIDTH).astype(BF16)


def kernel(x, mem, positions, w_in, gmlp_v_norm_g, gmlp_w_s, gmlp_b_s, mla_q_norm_g, mla_w_uq,
           mla_kv_norm_g, mla_w_ukv, conv_w, pool_w, pool_scale, w_out, ln1_g, ln1_b,
           xattn_wq, xattn_wk, xattn_wv, xattn_wo, ln2_g, ln2_b, router_w, router_b,
           moe_w_gate, moe_w_up, moe_w_down, ln3_g, ln3_b):
    depth = w_in.shape[0]
    alpha = (2 * depth) ** 0.25
    tm = 512
    bsz, seq, _ = x.shape

    inv_freq = ROPE_BASE ** (-jnp.arange(0, MLA_ROPE, 2, dtype=F32) / MLA_ROPE)
    ang = positions.astype(F32)[..., None] * inv_freq
    cos2 = jnp.tile(jnp.cos(ang), (1, 1, 2))
    sin2 = jnp.tile(jnp.sin(ang), (1, 1, 2))
    scale = (MLA_NOPE + MLA_ROPE) ** -0.5 * LOG2_E
    csq = scale * jnp.concatenate([jnp.ones(cos2.shape[:2] + (MLA_NOPE,), F32), cos2, sin2], -1)
    csq = jnp.swapaxes(csq.reshape(bsz, seq // tm, tm, QK_DIM), 2, 3)
    csk = jnp.concatenate([cos2, sin2, cos2, sin2], axis=-1)

    row = lambda a: a[:, None, :]
    w_in_e = _prep_w_in(w_in)
    w_uq_e = jnp.swapaxes(_prep_w_uq(mla_w_uq), 1, 2)
    w_k_e, w_v_e = _prep_w_ukv(mla_w_ukv)
    w_v_e = jnp.swapaxes(w_v_e, 1, 2)
    ws = gmlp_w_s.astype(BF16)
    bs = jnp.repeat(jnp.swapaxes(gmlp_b_s, 1, 2), A_HEAD_DIM, axis=2)
    pool_bd = _prep_pool_w(pool_w)
    wo_acd = jnp.concatenate([w_out[:, :A_WIDTH], w_out[:, A_WIDTH + MLA_WIDTH:]], axis=1).astype(BF16)
    wo_b = w_out[:, A_WIDTH:A_WIDTH + MLA_WIDTH].astype(BF16)
    wq = (xattn_wq * (X_HEAD_DIM ** -0.5 * LOG2_E)).astype(BF16)
    wo = xattn_wo.astype(BF16)
    rw_t = router_w.T
    rw_hi = rw_t.astype(BF16)
    rw = jnp.concatenate([rw_hi, (rw_t - rw_hi.astype(F32)).astype(BF16)], axis=0)
    rb_t = jnp.broadcast_to(router_b[:, None], (N_EXPERTS, tm))
    before = jnp.triu(jnp.ones((tm, tm), BF16), 1)
    wg = moe_w_gate.astype(BF16)
    wu = moe_w_up.astype(BF16)
    wd = moe_w_down.reshape(depth, N_EXPERTS * D_FF, D_MODEL).astype(BF16)

    km, vm = _mem_kv(mem, xattn_wk.astype(BF16), xattn_wv.astype(BF16))

    for l in range(depth):
        q, k, v, yacd = _mixer_in(
            x, csq, csk, w_in_e, row(gmlp_v_norm_g), ws, bs, row(mla_q_norm_g),
            w_uq_e, row(mla_kv_norm_g), w_k_e, w_v_e, conv_w, pool_bd,
            row(pool_scale), layer=l, tm=2 * tm, ta=tm)
        yb = _mla_attn(q, k, v, hg=MLA_HEADS)
        x2, logits = _post_attn(
            yacd, yb, x, wo_acd, wo_b, row(ln1_g), row(ln1_b), wq, km, vm,
            wo, row(ln2_g), row(ln2_b), rw, layer=l, tm=2 * tm, alpha=alpha)
        x = _moe(x2, logits, rb_t, before, wg, wu, wd, row(ln3_g), row(ln3_b),
                 layer=l, tm=tm, alpha=alpha)
    return x
```
